```python
import math
import jax, jax.numpy as jnp
from jax import lax
import numpy as np


D_MODEL = 1024
BATCH = 32
SEQ = 256
DEPTH = 4
DEC_BATCH = 2
DEC_SEQ = 4096
PAST_LEN = 256

GRID_W = 64
N_MIXERS = 4
N_MLA = (DEPTH + 3) // 4
N_DIFF = (DEPTH + 2) // 4
N_GQA = (DEPTH + 1) // 4
N_LRU = DEPTH // 4
EPS = 1e-6
ROPE_THETA = 10000.0
Q_BLOCK = 128
FFN_HIDDEN = -(-8 * D_MODEL // (3 * 256)) * 256
MLA_HEADS = 8
MLA_NOPE = 128
MLA_ROPE = 64
MLA_V = 128
MLA_Q_RANK = 384
MLA_KV_RANK = 256
DIFF_HD = 64
DIFF_HEADS = D_MODEL // (2 * DIFF_HD)
GQA_HD = 128
GQA_Q_HEADS = D_MODEL // GQA_HD
GQA_KV_HEADS = 2
D_RNN = D_MODEL
LRU_BLOCKS = 8
LRU_BLK = D_RNN // LRU_BLOCKS
CONV_W = 4
CONV_LEFT = 1
LRU_C = 8.0

kernel_name = "hybrid_diffusion_mla_diff_gqa_rglru_step"


def rmsnorm(x, g):
    xf = x.astype(jnp.float32)
    y = xf * lax.rsqrt(jnp.mean(xf * xf, axis=-1, keepdims=True) + EPS)
    return (y * g.astype(jnp.float32)).astype(x.dtype)


def axial_rope(n_tok, rot_dim):
    rows = n_tok // GRID_W
    row = jnp.repeat(jnp.arange(rows), GRID_W).astype(jnp.float32)
    col = jnp.tile(jnp.arange(GRID_W), rows).astype(jnp.float32)
    n_freq = rot_dim // 4
    inv = ROPE_THETA ** (-jnp.arange(n_freq, dtype=jnp.float32) / n_freq)
    ang = jnp.concatenate([row[:, None] * inv, col[:, None] * inv], axis=-1)
    return jnp.cos(ang), jnp.sin(ang)


def apply_rope(x, cos, sin):
    half = x.shape[-1] // 2
    shape = (1, x.shape[1]) + (1,) * (x.ndim - 3) + (half,)
    cos, sin = cos.reshape(shape), sin.reshape(shape)
    xf = x.astype(jnp.float32)
    x1, x2 = xf[..., :half], xf[..., half:]
    return jnp.concatenate([x1 * cos - x2 * sin, x1 * sin + x2 * cos], axis=-1).astype(x.dtype)


def blocked_attention(q, k, v):
    b, s, hq, dq = q.shape
    hkv, dv = k.shape[2], v.shape[-1]
    g = hq // hkv
    nb = s // Q_BLOCK
    scale = dq ** -0.5
    qb = jnp.moveaxis(q.reshape(b, nb, Q_BLOCK, hkv, g, dq), 1, 0)

    def attend(qblk):
        sc = jnp.einsum('bqhgd,bkhd->bhgqk', qblk, k).astype(jnp.float32) * scale
        p = jax.nn.softmax(sc, axis=-1).astype(v.dtype)
        return jnp.einsum('bhgqk,bkhd->bqhgd', p, v)

    o = lax.map(attend, qb)
    return jnp.moveaxis(o, 0, 1).reshape(b, s, hq, dv)


def mla_mixer(h, w_in, g_cq, g_ckv, w_uq, w_ukv, g_qk, w_o, ctx):
    b, s, _ = h.shape
    proj = h @ w_in
    cq = rmsnorm(proj[..., :MLA_Q_RANK], g_cq)
    ckv = rmsnorm(proj[..., MLA_Q_RANK:MLA_Q_RANK + MLA_KV_RANK], g_ckv)
    kr = rmsnorm(proj[..., MLA_Q_RANK + MLA_KV_RANK:], g_qk[1, MLA_NOPE:])
    q = (cq @ w_uq).reshape(b, s, MLA_HEADS, MLA_NOPE + MLA_ROPE)
    q_nope = rmsnorm(q[..., :MLA_NOPE], g_qk[0, :MLA_NOPE])
    q_rope = rmsnorm(q[..., MLA_NOPE:], g_qk[0, MLA_NOPE:])
    keys_ckv, keys_kr = ckv, kr
    if ctx is not None:
        cos, sin = axial_rope(s, MLA_ROPE)
        q_rope = apply_rope(q_rope, cos, sin)
        keys_ckv = jnp.concatenate([ctx[0], ckv], axis=1)
        keys_kr = jnp.concatenate([ctx[1], apply_rope(kr, cos, sin)], axis=1)
    l = keys_ckv.shape[1]
    kv = (keys_ckv @ w_ukv).reshape(b, l, MLA_HEADS, MLA_NOPE + MLA_V)
    k_nope = rmsnorm(kv[..., :MLA_NOPE], g_qk[1, :MLA_NOPE])
    k = jnp.concatenate([k_nope, jnp.broadcast_to(keys_kr[:, :, None, :], (b, l, MLA_HEADS, MLA_ROPE))], axis=-1)
    o = blocked_attention(jnp.concatenate([q_nope, q_rope], axis=-1), k, kv[..., MLA_NOPE:])
    return o.reshape(b, s, MLA_HEADS * MLA_V) @ w_o, (ckv, kr)


def diff_mixer(h, w_in, g_qk, lam, g_sub, w_o, lambda_init, ctx):
    b, s, _ = h.shape
    q, k, v = jnp.split(h @ w_in, 3, axis=-1)
    q = rmsnorm(q.reshape(b, s, DIFF_HEADS, 2, DIFF_HD), g_qk[0])
    k = rmsnorm(k.reshape(b, s, DIFF_HEADS, 2, DIFF_HD), g_qk[1])
    v = v.reshape(b, s, DIFF_HEADS, 2 * DIFF_HD)
    keys, vals = k, v
    if ctx is not None:
        cos, sin = axial_rope(s, DIFF_HD)
        q = apply_rope(q, cos, sin)
        keys = jnp.concatenate([ctx[0], apply_rope(k, cos, sin)], axis=1)
        vals = jnp.concatenate([ctx[1], v], axis=1)
    lam = lam.astype(jnp.float32)
    lam_full = jnp.exp(jnp.sum(lam[0] * lam[1])) - jnp.exp(jnp.sum(lam[2] * lam[3])) + lambda_init
    o1 = blocked_attention(q[:, :, :, 0], keys[:, :, :, 0], vals)
    o2 = blocked_attention(q[:, :, :, 1], keys[:, :, :, 1], vals)
    o = o1.astype(jnp.float32) - lam_full * o2.astype(jnp.float32)
    o = (rmsnorm(o, g_sub) * (1.0 - lambda_init)).astype(h.dtype)
    return o.reshape(b, s, DIFF_HEADS * 2 * DIFF_HD) @ w_o, (k, v)


def gqa_mixer(h, w_in, g_qk, w_o, ctx):
    b, s, _ = h.shape
    nq, nk = GQA_Q_HEADS * GQA_HD, GQA_KV_HEADS * GQA_HD
    proj = h @ w_in
    q = rmsnorm(proj[..., :nq].reshape(b, s, GQA_Q_HEADS, GQA_HD), g_qk[0])
    k = rmsnorm(proj[..., nq:nq + nk].reshape(b, s, GQA_KV_HEADS, GQA_HD), g_qk[1])
    v = proj[..., nq + nk:].reshape(b, s, GQA_KV_HEADS, GQA_HD)
    keys, vals = k, v
    if ctx is not None:
        cos, sin = axial_rope(s, GQA_HD)
        q = apply_rope(q, cos, sin)
        keys = jnp.concatenate([ctx[0], apply_rope(k, cos, sin)], axis=1)
        vals = jnp.concatenate([ctx[1], v], axis=1)
    o = blocked_attention(q, keys, vals)
    return o.reshape(b, s, nq) @ w_o, (k, v)


def centred_conv(x, w, bias):
    s = x.shape[1]
    xp = jnp.pad(x, ((0, 0), (CONV_LEFT, CONV_W - 1 - CONV_LEFT), (0, 0)))
    y = xp[:, 0:s] * w[0]
    for t in range(1, CONV_W):
        y = y + xp[:, t:t + s] * w[t]
    return y + bias


def _linear_combine(e1, e2):
    a1, b1 = e1
    a2, b2 = e2
    return a1 * a2, a2 * b1 + b2


def rglru_scan(x, w_gate, b_gate, lam, h0):
    b, s, _ = x.shape
    xb = x.reshape(b, s, LRU_BLOCKS, LRU_BLK)
    g = jnp.einsum('bsnk,gnkj->gbsnj', xb, w_gate).reshape(2, b, s, D_RNN)
    g = g.astype(jnp.float32) + b_gate.astype(jnp.float32)[:, None, None, :]
    r, i = jax.nn.sigmoid(g[0]), jax.nn.sigmoid(g[1])
    log_a = -LRU_C * r * jax.nn.softplus(-lam.astype(jnp.float32))
    a = jnp.exp(log_a)
    u = jnp.sqrt(-jnp.expm1(2.0 * log_a)) * (i * x.astype(jnp.float32))
    if h0 is not None:
        u = u.at[:, 0].add(a[:, 0] * h0)
    _, hs = lax.associative_scan(_linear_combine, (a, u), axis=1)
    return hs


def lru_mixer(h, w_in, conv_w, conv_b, w_gate, b_gate, lam, w_out, ctx):
    gate_branch, xr = jnp.split(h @ w_in, 2, axis=-1)
    xr = centred_conv(xr, conv_w, conv_b)
    h0f = None if ctx is None else ctx[0][:, 0].astype(jnp.float32)
    h0b = None if ctx is None else ctx[0][:, 1].astype(jnp.float32)
    hf = rglru_scan(xr, w_gate[0], b_gate[0], lam[0], h0f)
    hb = jnp.flip(rglru_scan(jnp.flip(xr, axis=1), w_gate[1], b_gate[1], lam[1], h0b), axis=1)
    y = (hf + hb).astype(h.dtype) * jax.nn.gelu(gate_branch)
    state = jnp.stack([hf[:, -1], hb[:, 0]], axis=1)
    return y @ w_out, (state,)


def swiglu(h, w_in, w_out):
    gte, up = jnp.split(h @ w_in, 2, axis=-1)
    return (jax.nn.silu(gte) * up) @ w_out


def setup_inputs(seed: int = 0) -> dict:
    key = jax.random.key(seed)
    ks = iter(jax.random.split(key, 48))
    D = D_MODEL

    def nrm(shape, scale=1.0):
        return jax.random.normal(next(ks), shape, jnp.float32) * scale

    def gain(shape):
        return 1.0 + nrm(shape, 0.02)

    u = jax.random.uniform(next(ks), (N_LRU, 2, D_RNN), jnp.float32, 0.9, 0.999)
    a0 = u ** (1.0 / LRU_C)
    lru_lambda = jnp.log(a0 / (1.0 - a0))
    return {
        "x_prompt": nrm((BATCH, SEQ, D)),
        "x_sample": nrm((DEC_BATCH, DEC_SEQ, D)),
        "cache_mla_ckv": nrm((DEC_BATCH, N_MLA, PAST_LEN, MLA_KV_RANK)),
        "cache_mla_krope": nrm((DEC_BATCH, N_MLA, PAST_LEN, MLA_ROPE)),
        "cache_diff_k": nrm((DEC_BATCH, N_DIFF, PAST_LEN, DIFF_HEADS, 2, DIFF_HD)),
        "cache_diff_v": nrm((DEC_BATCH, N_DIFF, PAST_LEN, DIFF_HEADS, 2 * DIFF_HD)),
        "cache_gqa_k": nrm((DEC_BATCH, N_GQA, PAST_LEN, GQA_KV_HEADS, GQA_HD)),
        "cache_gqa_v": nrm((DEC_BATCH, N_GQA, PAST_LEN, GQA_KV_HEADS, GQA_HD)),
        "state_lru_h": nrm((DEC_BATCH, N_LRU, 2, D_RNN), 0.5),
        "c": nrm((DEC_BATCH, D)),
        "c_ctx": nrm((D,)),
        "w_mod": nrm((DEPTH, D, 6 * D), 0.5 * D ** -0.5),
        "b_mod": nrm((DEPTH, 6 * D), 0.01),
        "g_norm1": gain((DEPTH, D)),
        "g_norm2": gain((DEPTH, D)),
        "w_ffn_in": nrm((DEPTH, D, 2 * FFN_HIDDEN), D ** -0.5),
        "w_ffn_out": nrm((DEPTH, FFN_HIDDEN, D), FFN_HIDDEN ** -0.5),
        "mla_w_in": nrm((N_MLA, D, MLA_Q_RANK + MLA_KV_RANK + MLA_ROPE), D ** -0.5),
        "mla_g_cq": gain((N_MLA, MLA_Q_RANK)),
        "mla_g_ckv": gain((N_MLA, MLA_KV_RANK)),
        "mla_w_uq": nrm((N_MLA, MLA_Q_RANK, MLA_HEADS * (MLA_NOPE + MLA_ROPE)), MLA_Q_RANK ** -0.5),
        "mla_w_ukv": nrm((N_MLA, MLA_KV_RANK, MLA_HEADS * (MLA_NOPE + MLA_V)), MLA_KV_RANK ** -0.5),
        "mla_g_qk": gain((N_MLA, 2, MLA_NOPE + MLA_ROPE)),
        "mla_w_o": nrm((N_MLA, MLA_HEADS * MLA_V, D), (MLA_HEADS * MLA_V) ** -0.5),
        "diff_w_in": nrm((N_DIFF, D, 3 * DIFF_HEADS * 2 * DIFF_HD), D ** -0.5),
        "diff_g_qk": gain((N_DIFF, 2, DIFF_HD)),
        "diff_lambda": nrm((N_DIFF, 4, DIFF_HD), 0.1),
        "diff_g_sub": gain((N_DIFF, 2 * DIFF_HD)),
        "diff_w_o": nrm((N_DIFF, DIFF_HEADS * 2 * DIFF_HD, D), (DIFF_HEADS * 2 * DIFF_HD) ** -0.5),
        "gqa_w_in": nrm((N_GQA, D, (GQA_Q_HEADS + 2 * GQA_KV_HEADS) * GQA_HD), D ** -0.5),
        "gqa_g_qk": gain((N_GQA, 2, GQA_HD)),
        "gqa_w_o": nrm((N_GQA, GQA_Q_HEADS * GQA_HD, D), (GQA_Q_HEADS * GQA_HD) ** -0.5),
        "lru_w_in": nrm((N_LRU, D, 2 * D_RNN), D ** -0.5),
        "lru_conv_w": nrm((N_LRU, CONV_W, D_RNN), CONV_W ** -0.5),
        "lru_conv_b": nrm((N_LRU, D_RNN), 0.01),
        "lru_w_gate": nrm((N_LRU, 2, 2, LRU_BLOCKS, LRU_BLK, LRU_BLK), LRU_BLK ** -0.5),
        "lru_b_gate": nrm((N_LRU, 2, 2, D_RNN), 0.01),
        "lru_lambda": lru_lambda,
        "lru_w_out": nrm((N_LRU, D_RNN, D), D_RNN ** -0.5),
    }


def reference(x_prompt, x_sample, cache_mla_ckv, cache_mla_krope, cache_diff_k, cache_diff_v,
              cache_gqa_k, cache_gqa_v, state_lru_h, c, c_ctx, w_mod, b_mod, g_norm1, g_norm2,
              w_ffn_in, w_ffn_out, mla_w_in, mla_g_cq, mla_g_ckv, mla_w_uq, mla_w_ukv, mla_g_qk,
              mla_w_o, diff_w_in, diff_g_qk, diff_lambda, diff_g_sub, diff_w_o, gqa_w_in, gqa_g_qk,
              gqa_w_o, lru_w_in, lru_conv_w, lru_conv_b, lru_w_gate, lru_b_gate, lru_lambda, lru_w_out):

    def token_mixer(l, h, ctx):
        kind, j = l % N_MIXERS, l // N_MIXERS
        if kind == 0:
            return mla_mixer(h, mla_w_in[j], mla_g_cq[j], mla_g_ckv[j], mla_w_uq[j], mla_w_ukv[j],
                             mla_g_qk[j], mla_w_o[j], ctx)
        if kind == 1:
            lambda_init = 0.8 - 0.6 * math.exp(-0.3 * l)
            return diff_mixer(h, diff_w_in[j], diff_g_qk[j], diff_lambda[j], diff_g_sub[j], diff_w_o[j],
                              lambda_init, ctx)
        if kind == 2:
            return gqa_mixer(h, gqa_w_in[j], gqa_g_qk[j], gqa_w_o[j], ctx)
        return lru_mixer(h, lru_w_in[j], lru_conv_w[j], lru_conv_b[j], lru_w_gate[j], lru_b_gate[j],
                         lru_lambda[j], lru_w_out[j], ctx)

    def layer_cache(l):
        kind, j = l % N_MIXERS, l // N_MIXERS
        if kind == 0:
            return (cache_mla_ckv[:, j], cache_mla_krope[:, j])
        if kind == 1:
            return (cache_diff_k[:, j], cache_diff_v[:, j])
        if kind == 2:
            return (cache_gqa_k[:, j], cache_gqa_v[:, j])
        return (state_lru_h[:, j],)

    def trunk_layer(l, x, cond, ctx):
        mod = (jax.nn.silu(cond) @ w_mod[l] + b_mod[l]).reshape(cond.shape[0], 6, D_MODEL)[:, :, None, :]
        h = rmsnorm(x, g_norm1[l]) * (1.0 + mod[:, 1]) + mod[:, 0]
        out, ctx_tensors = token_mixer(l, h, ctx)
        x = x + mod[:, 2] * out
        h = rmsnorm(x, g_norm2[l]) * (1.0 + mod[:, 4]) + mod[:, 3]
        x = x + mod[:, 5] * swiglu(h, w_ffn_in[l], w_ffn_out[l])
        return x, ctx_tensors

    x = x_prompt
    ctx_states = []
    for l in range(DEPTH):
        x, st = trunk_layer(l, x, c_ctx[None, :], None)
        ctx_states.append(st)
    y_prompt = x

    def collect(kind, idx):
        return jnp.stack([ctx_states[l][idx] for l in range(kind, DEPTH, N_MIXERS)], axis=1)

    new_mla_ckv = collect(0, 0)
    new_mla_krope = collect(0, 1)
    new_diff_k = collect(1, 0)
    new_diff_v = collect(1, 1)
    new_gqa_k = collect(2, 0)
    new_gqa_v = collect(2, 1)
    new_lru_h = collect(3, 0)

    x = x_sample
    for l in range(DEPTH):
        x, _ = trunk_layer(l, x, c, layer_cache(l))
    y_sample = x

    return (y_prompt, y_sample, new_mla_ckv, new_mla_krope, new_diff_k, new_diff_v, new_gqa_k, new_gqa_v, new_lru_h)
```

```python
import functools
import math

import jax
import jax.numpy as jnp
from jax import lax
from jax.experimental import pallas as pl
from jax.experimental.pallas import tpu as pltpu

F32 = jnp.float32
BF16 = jnp.bfloat16

D = 1024
BATCH = 32
SEQ = 256
DEPTH = 4
DEC_BATCH = 2
DEC_SEQ = 4096
PAST = 256
GRID_W = 64
EPS = 1e-6
ROPE_THETA = 10000.0
FFN_H = 2816
N_CTX = BATCH * SEQ
N_LAT = DEC_BATCH * DEC_SEQ
T = N_CTX + N_LAT
N_GROUPS = 1 + DEC_BATCH

MLA_HEADS = 8
MLA_NOPE = 128
MLA_ROPE = 64
MLA_V = 128
MLA_Q_RANK = 384
MLA_KV_RANK = 256
MLA_DQ = 256
DIFF_HD = 64
DIFF_HEADS = 8
GQA_HD = 128
GQA_Q_HEADS = 8
GQA_KV_HEADS = 2
LRU_BLOCKS = 8
LRU_BLK = 128
LRU_C = 8.0

LANES = 128
SUBLANES = 8
VMEM_LIMIT = 56 * 1024 * 1024

TM = 512
FFN_CHUNK = 256
LRU_TB = 256


def _cparams(sem):
    return pltpu.CompilerParams(dimension_semantics=sem, vmem_limit_bytes=VMEM_LIMIT)


def _group_of_block(i, rows_per_block):
    n_ctx_blocks = N_CTX // rows_per_block
    per = DEC_SEQ // rows_per_block
    return jnp.maximum(i - (n_ctx_blocks - per), 0) // per


def _rms(x, gain):
    y = x * lax.rsqrt(jnp.mean(x * x, axis=-1, keepdims=True) + EPS)
    return y * gain


def _rms_halves(x, gain):
    lane = lax.broadcasted_iota(jnp.int32, x.shape, 1)
    low = lane < 64
    sq = x * x
    s_low = jnp.sum(jnp.where(low, sq, 0.0), axis=-1, keepdims=True)
    s_all = jnp.sum(sq, axis=-1, keepdims=True)
    ms = jnp.where(low, s_low, s_all - s_low) * (1.0 / 64.0)
    return x * lax.rsqrt(ms + EPS) * gain


def _rope128(x, c, s):
    return x * c + pltpu.roll(x, 64, 1) * s


def _rope64x2(x, c, s):
    lane = lax.broadcasted_iota(jnp.int32, x.shape, 1)
    low = (lane % 64) < 32
    partner = jnp.where(low, pltpu.roll(x, 96, 1), pltpu.roll(x, 32, 1))
    return x * c + partner * s


def _norm_mod(x_ref, g_ref, mod_ref, shift_row):
    x = x_ref[...]
    y = _rms(x, g_ref[...])
    shift = mod_ref[shift_row:shift_row + 1, :]
    scale = mod_ref[shift_row + 1:shift_row + 2, :]
    return (y * (1.0 + scale) + shift).astype(BF16)


def _mod_kernel(cond_ref, w_ref, b_ref, o_ref):
    s = jax.nn.silu(cond_ref[...]).astype(BF16)
    w = w_ref[...].astype(BF16)
    o_ref[...] = jnp.dot(s, w, preferred_element_type=F32) + b_ref[...]


def _modulation(cond, w_mod, b_mod):
    tn = 1536
    return pl.pallas_call(
        _mod_kernel,
        grid=(DEPTH, 6 * D // tn),
        in_specs=[
            pl.BlockSpec((SUBLANES, D), lambda l, j: (0, 0)),
            pl.BlockSpec((None, D, tn), lambda l, j: (l, 0, j)),
            pl.BlockSpec((None, 1, tn), lambda l, j: (l, 0, j)),
        ],
        out_specs=pl.BlockSpec((None, SUBLANES, tn), lambda l, j: (l, 0, j)),
        out_shape=jax.ShapeDtypeStruct((DEPTH, SUBLANES, 6 * D), F32),
        compiler_params=_cparams(("arbitrary", "arbitrary")),
        name="adaln_mod",
    )(cond, w_mod, b_mod.reshape(DEPTH, 1, 6 * D))


def _row_spec(width):
    return pl.BlockSpec((TM, width), lambda i: (i, 0))


def _full_spec(shape):
    return pl.BlockSpec(shape, lambda i: (0,) * len(shape))


def _mod_spec():
    return pl.BlockSpec((None, 6, D), lambda i: (_group_of_block(i, TM), 0, 0))


def _heads_spec(heads, width):
    return pl.BlockSpec((heads, TM, width), lambda i: (0, i, 0))


def _gqa_in_kernel(x_ref, mod_ref, g_ref, w_ref, gqk_ref, cos_ref, sin_ref,
                   q_ref, k_ref, v_ref, ks_ref, vs_ref):
    h = _norm_mod(x_ref, g_ref, mod_ref, 0)
    proj = jnp.dot(h, w_ref[...], preferred_element_type=F32)
    c = cos_ref[...]
    s = sin_ref[...]
    scale = GQA_HD ** -0.5
    for hd in range(GQA_Q_HEADS):
        q = _rms(proj[:, hd * GQA_HD:(hd + 1) * GQA_HD], gqk_ref[0:1, :])
        q_ref[hd] = (_rope128(q, c, s) * scale).astype(BF16)
    for hd in range(GQA_KV_HEADS):
        lo = (GQA_Q_HEADS + hd) * GQA_HD
        k = _rms(proj[:, lo:lo + GQA_HD], gqk_ref[1:2, :])
        ks_ref[:, hd * GQA_HD:(hd + 1) * GQA_HD] = k
        k_ref[hd] = _rope128(k, c, s).astype(BF16)
        lo = (GQA_Q_HEADS + GQA_KV_HEADS + hd) * GQA_HD
        v = proj[:, lo:lo + GQA_HD]
        vs_ref[:, hd * GQA_HD:(hd + 1) * GQA_HD] = v
        v_ref[hd] = v.astype(BF16)


def _gqa_in(x, mod, g1, w_in, g_qk, cos, sin):
    n_kv = GQA_KV_HEADS * GQA_HD
    return pl.pallas_call(
        _gqa_in_kernel,
        grid=(T // TM,),
        in_specs=[_row_spec(D), _mod_spec(), _full_spec((1, D)), _full_spec(w_in.shape),
                  _full_spec((2, GQA_HD)), _row_spec(LANES), _row_spec(LANES)],
        out_specs=[_heads_spec(GQA_Q_HEADS, GQA_HD), _heads_spec(GQA_KV_HEADS, GQA_HD),
                   _heads_spec(GQA_KV_HEADS, GQA_HD), _row_spec(n_kv), _row_spec(n_kv)],
        out_shape=[jax.ShapeDtypeStruct((GQA_Q_HEADS, T, GQA_HD), BF16),
                   jax.ShapeDtypeStruct((GQA_KV_HEADS, T, GQA_HD), BF16),
                   jax.ShapeDtypeStruct((GQA_KV_HEADS, T, GQA_HD), BF16),
                   jax.ShapeDtypeStruct((T, n_kv), F32),
                   jax.ShapeDtypeStruct((T, n_kv), F32)],
        compiler_params=_cparams(("arbitrary",)),
        name="gqa_in",
    )(x, mod, g1, w_in, g_qk, cos, sin)


def _diff_in_kernel(x_ref, mod_ref, g_ref, w_ref, gqk_ref, cos_ref, sin_ref,
                    q_ref, k_ref, v_ref, ks_ref, vs_ref):
    h = _norm_mod(x_ref, g_ref, mod_ref, 0)
    proj = jnp.dot(h, w_ref[...], preferred_element_type=F32)
    c = cos_ref[...]
    s = sin_ref[...]
    scale = DIFF_HD ** -0.5
    lane = lax.broadcasted_iota(jnp.int32, (TM, LANES), 1)
    low = lane < DIFF_HD
    n = DIFF_HEADS * 2 * DIFF_HD
    for hd in range(DIFF_HEADS):
        sl = slice(hd * LANES, (hd + 1) * LANES)
        q = _rope64x2(_rms_halves(proj[:, sl], gqk_ref[0:1, :]), c, s) * scale
        q_ref[2 * hd] = jnp.where(low, q, 0.0).astype(BF16)
        q_ref[2 * hd + 1] = jnp.where(low, 0.0, q).astype(BF16)
        k = _rms_halves(proj[:, n + hd * LANES:n + (hd + 1) * LANES], gqk_ref[1:2, :])
        ks_ref[:, sl] = k
        k_ref[hd] = _rope64x2(k, c, s).astype(BF16)
        v = proj[:, 2 * n + hd * LANES:2 * n + (hd + 1) * LANES]
        vs_ref[:, sl] = v
        v_ref[hd] = v.astype(BF16)


def _diff_in(x, mod, g1, w_in, g_qk2, cos, sin):
    n = DIFF_HEADS * 2 * DIFF_HD
    return pl.pallas_call(
        _diff_in_kernel,
        grid=(T // TM,),
        in_specs=[_row_spec(D), _mod_spec(), _full_spec((1, D)), _full_spec(w_in.shape),
                  _full_spec((2, LANES)), _row_spec(LANES), _row_spec(LANES)],
        out_specs=[_heads_spec(2 * DIFF_HEADS, LANES), _heads_spec(DIFF_HEADS, LANES),
                   _heads_spec(DIFF_HEADS, LANES), _row_spec(n), _row_spec(n)],
        out_shape=[jax.ShapeDtypeStruct((2 * DIFF_HEADS, T, LANES), BF16),
                   jax.ShapeDtypeStruct((DIFF_HEADS, T, LANES), BF16),
                   jax.ShapeDtypeStruct((DIFF_HEADS, T, LANES), BF16),
                   jax.ShapeDtypeStruct((T, n), F32),
                   jax.ShapeDtypeStruct((T, n), F32)],
        compiler_params=_cparams(("arbitrary",)),
        name="diff_in",
    )(x, mod, g1, w_in, g_qk2, cos, sin)


def _mla_keys_values(ckv_bf16, kr2, w_ukv_ref, gk_ref, k_ref, v_ref):
    kv = jnp.dot(ckv_bf16, w_ukv_ref[...], preferred_element_type=F32)
    lane = lax.broadcasted_iota(jnp.int32, kr2.shape, 1)
    low = lane < MLA_ROPE
    kr_low = jnp.where(low, kr2, 0.0).astype(BF16)
    kr_high = jnp.where(low, 0.0, kr2).astype(BF16)
    width = MLA_NOPE + MLA_V
    for hd in range(MLA_HEADS):
        k_nope = _rms(kv[:, hd * width:hd * width + MLA_NOPE], gk_ref[...])
        k_ref[hd, :, 0:MLA_NOPE] = k_nope.astype(BF16)
        k_ref[hd, :, MLA_NOPE:MLA_DQ] = kr_low if hd % 2 == 0 else kr_high
        v_ref[hd] = kv[:, hd * width + MLA_NOPE:(hd + 1) * width].astype(BF16)


def _mla_in_kernel(x_ref, mod_ref, g_ref, w_ref, gcq_ref, gckv_ref, gqn_ref, gqr_ref, gkn_ref,
                   gkr_ref, wqn_ref, wqr_ref, wukv_ref, cos_ref, sin_ref,
                   q_ref, k_ref, v_ref, ckv_ref, kr_ref):
    h = _norm_mod(x_ref, g_ref, mod_ref, 0)
    proj = jnp.dot(h, w_ref[...], preferred_element_type=F32)
    c = cos_ref[...]
    s = sin_ref[...]
    cq = _rms(proj[:, 0:MLA_Q_RANK], gcq_ref[...]).astype(BF16)
    ckv = _rms(proj[:, MLA_Q_RANK:MLA_Q_RANK + MLA_KV_RANK], gckv_ref[...])
    ckv_ref[...] = ckv
    kr2 = _rms_halves(proj[:, MLA_Q_RANK + MLA_KV_RANK:], gkr_ref[...])
    kr_ref[...] = kr2[:, 0:MLA_ROPE]
    _mla_keys_values(ckv.astype(BF16), _rope64x2(kr2, c, s), wukv_ref, gkn_ref, k_ref, v_ref)

    scale = (MLA_NOPE + MLA_ROPE) ** -0.5
    qn = jnp.dot(cq, wqn_ref[...], preferred_element_type=F32)
    qr = jnp.dot(cq, wqr_ref[...], preferred_element_type=F32)
    lane = lax.broadcasted_iota(jnp.int32, (TM, LANES), 1)
    low = lane < MLA_ROPE
    for pair in range(MLA_HEADS // 2):
        r = _rope64x2(_rms_halves(qr[:, pair * LANES:(pair + 1) * LANES], gqr_ref[...]), c, s) * scale
        q_ref[2 * pair, :, MLA_NOPE:MLA_DQ] = jnp.where(low, r, 0.0).astype(BF16)
        q_ref[2 * pair + 1, :, MLA_NOPE:MLA_DQ] = jnp.where(low, 0.0, r).astype(BF16)
    for hd in range(MLA_HEADS):
        q = _rms(qn[:, hd * MLA_NOPE:(hd + 1) * MLA_NOPE], gqn_ref[...]) * scale
        q_ref[hd, :, 0:MLA_NOPE] = q.astype(BF16)


def _mla_in(x, mod, g1, w_in, g_cq, g_ckv, gqn, gqr, gkn, gkr, wqn, wqr, wukv, cos, sin):
    return pl.pallas_call(
        _mla_in_kernel,
        grid=(T // TM,),
        in_specs=[_row_spec(D), _mod_spec(), _full_spec((1, D)), _full_spec(w_in.shape),
                  _full_spec((1, MLA_Q_RANK)), _full_spec((1, MLA_KV_RANK)),
                  _full_spec((1, LANES)), _full_spec((1, LANES)), _full_spec((1, LANES)),
                  _full_spec((1, LANES)), _full_spec(wqn.shape), _full_spec(wqr.shape),
                  _full_spec(wukv.shape), _row_spec(LANES), _row_spec(LANES)],
        out_specs=[_heads_spec(MLA_HEADS, MLA_DQ), _heads_spec(MLA_HEADS, MLA_DQ),
                   _heads_spec(MLA_HEADS, MLA_V), _row_spec(MLA_KV_RANK), _row_spec(MLA_ROPE)],
        out_shape=[jax.ShapeDtypeStruct((MLA_HEADS, T, MLA_DQ), BF16),
                   jax.ShapeDtypeStruct((MLA_HEADS, T, MLA_DQ), BF16),
                   jax.ShapeDtypeStruct((MLA_HEADS, T, MLA_V), BF16),
                   jax.ShapeDtypeStruct((T, MLA_KV_RANK), F32),
                   jax.ShapeDtypeStruct((T, MLA_ROPE), F32)],
        compiler_params=_cparams(("arbitrary",)),
        name="mla_in",
    )(x, mod, g1, w_in, g_cq, g_ckv, gqn, gqr, gkn, gkr, wqn, wqr, wukv, cos, sin)


def _mla_cache_kernel(ckv_ref, kr_ref, wukv_ref, gkn_ref, k_ref, v_ref):
    kr = kr_ref[...]
    kr2 = jnp.concatenate([kr, kr], axis=-1)
    _mla_keys_values(ckv_ref[...].astype(BF16), kr2, wukv_ref, gkn_ref, k_ref, v_ref)


def _mla_cache(ckv, kr, wukv, gkn):
    rows = ckv.shape[0]
    return pl.pallas_call(
        _mla_cache_kernel,
        grid=(1,),
        in_specs=[_full_spec(ckv.shape), _full_spec(kr.shape), _full_spec(wukv.shape),
                  _full_spec((1, LANES))],
        out_specs=[_full_spec((MLA_HEADS, rows, MLA_DQ)), _full_spec((MLA_HEADS, rows, MLA_V))],
        out_shape=[jax.ShapeDtypeStruct((MLA_HEADS, rows, MLA_DQ), BF16),
                   jax.ShapeDtypeStruct((MLA_HEADS, rows, MLA_V), BF16)],
        compiler_params=_cparams(("arbitrary",)),
        name="mla_cache_kv",
    )(ckv, kr, wukv, gkn)


def _lru_in_kernel(x_ref, mod_ref, g_ref, w_ref, gate_ref, xr_ref):
    h = _norm_mod(x_ref, g_ref, mod_ref, 0)
    proj = jnp.dot(h, w_ref[...], preferred_element_type=F32)
    gate_ref[...] = proj[:, 0:D]
    xr_ref[...] = proj[:, D:2 * D]


def _lru_in(x, mod, g1, w_in):
    return pl.pallas_call(
        _lru_in_kernel,
        grid=(T // TM,),
        in_specs=[_row_spec(D), _mod_spec(), _full_spec((1, D)), _full_spec(w_in.shape)],
        out_specs=[_row_spec(D), _row_spec(D)],
        out_shape=[jax.ShapeDtypeStruct((T, D), F32), jax.ShapeDtypeStruct((T, D), F32)],
        compiler_params=_cparams(("arbitrary",)),
        name="lru_in",
    )(x, mod, g1, w_in)


def _attn_kernel(*refs, heads, group, has_cache, diff, dv, lambda_init):
    it = iter(refs)
    q_ref, kn_ref, vn_ref = next(it), next(it), next(it)
    kc_ref = vc_ref = lam_ref = gsub_ref = None
    if has_cache:
        kc_ref, vc_ref = next(it), next(it)
    if diff:
        lam_ref, gsub_ref = next(it), next(it)
    o_ref = next(it)

    nt = (((1,), (1,)), ((), ()))
    if diff:
        lam = lam_ref[...]
        lam_full = (jnp.exp(jnp.sum(lam[0:1] * lam[1:2], axis=-1, keepdims=True))
                    - jnp.exp(jnp.sum(lam[2:3] * lam[3:4], axis=-1, keepdims=True)) + lambda_init)
    outs = []
    for hd in range(heads):
        kv = hd // group
        q = q_ref[hd]
        kn = kn_ref[kv]
        s_n = lax.dot_general(q, kn, nt, preferred_element_type=F32)
        m = jnp.max(s_n, axis=-1, keepdims=True)
        if has_cache:
            kc = kc_ref[...].astype(BF16)
            s_c = lax.dot_general(q, kc, nt, preferred_element_type=F32)
            m = jnp.maximum(m, jnp.max(s_c, axis=-1, keepdims=True))
        p_n = jnp.exp(s_n - m)
        l = jnp.sum(p_n, axis=-1, keepdims=True)
        o = jnp.dot(p_n.astype(BF16), vn_ref[kv], preferred_element_type=F32)
        if has_cache:
            p_c = jnp.exp(s_c - m)
            l = l + jnp.sum(p_c, axis=-1, keepdims=True)
            o = o + jnp.dot(p_c.astype(BF16), vc_ref[...].astype(BF16), preferred_element_type=F32)
        o = o / l
        if diff:
            outs.append(o)
            if hd % 2 == 1:
                od = outs[-2] - lam_full * outs[-1]
                od = _rms(od, gsub_ref[...]) * (1.0 - lambda_init)
                o_ref[:, (hd // 2) * dv:(hd // 2 + 1) * dv] = od.astype(BF16)
        else:
            o_ref[:, hd * dv:(hd + 1) * dv] = o.astype(BF16)


def _attention(q, kn, vn, *, seq_len, n_seq, row0, tq, heads_per_step, group, cache=None,
               diff=None, lambda_init=0.0):
    hq, _, dq = q.shape
    hkv, _, dv = vn.shape
    kv_per_step = max(heads_per_step // group, 1)
    n_hblk = hq // heads_per_step
    n_qblk = seq_len // tq
    seq0 = row0 // seq_len
    qblk0 = row0 // tq

    def q_map(b, hb, qb):
        return (hb, qblk0 + b * n_qblk + qb, 0)

    def kv_map(b, hb, qb):
        return ((hb * heads_per_step) // (group * kv_per_step), seq0 + b, 0)

    in_specs = [pl.BlockSpec((heads_per_step, tq, dq), q_map),
                pl.BlockSpec((kv_per_step, seq_len, dq), kv_map),
                pl.BlockSpec((kv_per_step, seq_len, dv), kv_map)]
    args = [q, kn, vn]
    if cache is not None:
        kc, vc, kc_block, kc_map, vc_block, vc_map = cache
        in_specs += [pl.BlockSpec(kc_block, kc_map), pl.BlockSpec(vc_block, vc_map)]
        args += [kc, vc]
    out_heads = heads_per_step
    if diff is not None:
        lam, gsub = diff
        in_specs += [pl.BlockSpec(lam.shape, lambda b, hb, qb: (0, 0)),
                     pl.BlockSpec(gsub.shape, lambda b, hb, qb: (0, 0))]
        args += [lam, gsub]
        out_heads = heads_per_step // 2
    n_out = (hq // 2 if diff is not None else hq) * dv
    kernel = functools.partial(_attn_kernel, heads=heads_per_step, group=group,
                               has_cache=cache is not None, diff=diff is not None, dv=dv,
                               lambda_init=lambda_init)
    return pl.pallas_call(
        kernel,
        grid=(n_seq, n_hblk, n_qblk),
        in_specs=in_specs,
        out_specs=pl.BlockSpec((tq, out_heads * dv), lambda b, hb, qb: (b * n_qblk + qb, hb)),
        out_shape=jax.ShapeDtypeStruct((n_seq * seq_len, n_out), BF16),
        compiler_params=_cparams(("arbitrary", "arbitrary", "arbitrary")),
        name="attention",
    )(*args)


def _lru_seq_pos(i):
    n_ctx_blocks = N_CTX // LRU_TB
    per_ctx = SEQ // LRU_TB
    per_lat = DEC_SEQ // LRU_TB
    pos = jnp.where(i < n_ctx_blocks, i % per_ctx, (i - n_ctx_blocks) % per_lat)
    length = jnp.where(i < n_ctx_blocks, per_ctx, per_lat)
    return pos == 0, pos == length - 1


def _lru_seq_of_block(i):
    n_ctx_blocks = N_CTX // LRU_TB
    return jnp.where(i < n_ctx_blocks, i // (SEQ // LRU_TB),
                     BATCH + (i - n_ctx_blocks) // (DEC_SEQ // LRU_TB))


def _tile_scan(a, u, reverse):
    row = lax.broadcasted_iota(jnp.int32, a.shape, 0)
    for k in (1, 2, 4):
        if reverse:
            valid = row < SUBLANES - k
            shift = SUBLANES - k
        else:
            valid = row >= k
            shift = k
        a_s = jnp.where(valid, pltpu.roll(a, shift, 0), 1.0)
        u_s = jnp.where(valid, pltpu.roll(u, shift, 0), 0.0)
        u = a * u_s + u
        a = a * a_s
    return a, u


def _lru_scan_kernel(x_ref, prev_ref, next_ref, cw_ref, cb_ref, wg_ref, bg_ref, lam_ref, h0_ref,
                     *rest, reverse, combine):
    if combine:
        hf_ref, gate_ref, y_ref, st_ref, xpad, a_s, u_s, carry = rest
    else:
        y_ref, st_ref, xpad, a_s, u_s, carry = rest
    j = pl.program_id(0)
    i = (T // LRU_TB - 1 - j) if reverse else j
    first, last = _lru_seq_pos(i)
    starts = last if reverse else first
    ends = first if reverse else last

    xpad[0:SUBLANES, :] = jnp.where(first, 0.0, prev_ref[...])
    xpad[SUBLANES:SUBLANES + LRU_TB, :] = x_ref[...]
    xpad[SUBLANES + LRU_TB:, :] = jnp.where(last, 0.0, next_ref[...])
    xr = xpad[SUBLANES - 1:SUBLANES - 1 + LRU_TB, :] * cw_ref[0:1, :]
    for t in range(1, 4):
        xr = xr + xpad[SUBLANES - 1 + t:SUBLANES - 1 + t + LRU_TB, :] * cw_ref[t:t + 1, :]
    xr = xr + cb_ref[...]

    xr_b = xr.astype(BF16)
    lam = lam_ref[...]
    neg = -lam
    softplus = jnp.maximum(neg, 0.0) + jnp.log1p(jnp.exp(-jnp.abs(neg)))
    for n in range(LRU_BLOCKS):
        sl = slice(n * LRU_BLK, (n + 1) * LRU_BLK)
        g = jnp.dot(xr_b[:, sl], wg_ref[n], preferred_element_type=F32)
        r = jax.nn.sigmoid(g[:, 0:LRU_BLK] + bg_ref[0:1, sl])
        gi = jax.nn.sigmoid(g[:, LRU_BLK:] + bg_ref[1:2, sl])
        log_a = -LRU_C * r * softplus[:, sl]
        th = jnp.tanh(log_a)
        one_minus_a2 = -2.0 * th / (1.0 - th)
        a_s[:, sl] = jnp.exp(log_a)
        u_s[:, sl] = jnp.sqrt(one_minus_a2) * (gi * xr[:, sl])

    @pl.when(starts)
    def _():
        carry[...] = h0_ref[...]

    n_tiles = LRU_TB // SUBLANES

    def body(t, h):
        tt = (n_tiles - 1 - t) if reverse else t
        rows = pl.ds(pl.multiple_of(tt * SUBLANES, SUBLANES), SUBLANES)
        a_c, u_c = _tile_scan(a_s[rows, :], u_s[rows, :], reverse)
        hs = a_c * h + u_c
        if combine:
            gate = gate_ref[rows, :]
            y_ref[rows, :] = ((hf_ref[rows, :] + hs) * jax.nn.gelu(gate)).astype(BF16)
        else:
            y_ref[rows, :] = hs
        return hs[0:1, :] if reverse else hs[SUBLANES - 1:SUBLANES, :]

    h_end = lax.fori_loop(0, n_tiles, body, carry[...])
    carry[...] = h_end

    @pl.when(ends)
    def _():
        st_ref[...] = h_end


def _lru_scan(xr, conv_w, conv_b, w_gate, b_gate, lam, h0, *, reverse, hf=None, gate=None):
    nb = T // LRU_TB
    hb = LRU_TB // SUBLANES
    n_halo = T // SUBLANES

    def blk(j):
        return (nb - 1 - j) if reverse else j

    in_specs = [
        pl.BlockSpec((LRU_TB, D), lambda j: (blk(j), 0)),
        pl.BlockSpec((SUBLANES, D), lambda j: (jnp.maximum(blk(j) * hb - 1, 0), 0)),
        pl.BlockSpec((SUBLANES, D), lambda j: (jnp.minimum((blk(j) + 1) * hb, n_halo - 1), 0)),
        pl.BlockSpec((4, D), lambda j: (0, 0)),
        pl.BlockSpec((1, D), lambda j: (0, 0)),
        pl.BlockSpec((LRU_BLOCKS, LRU_BLK, 2 * LRU_BLK), lambda j: (0, 0, 0)),
        pl.BlockSpec((2, D), lambda j: (0, 0)),
        pl.BlockSpec((1, D), lambda j: (0, 0)),
        pl.BlockSpec((None, 1, D), lambda j: (_lru_seq_of_block(blk(j)), 0, 0)),
    ]
    args = [xr, xr, xr, conv_w, conv_b, w_gate, b_gate, lam, h0]
    combine = hf is not None
    if combine:
        in_specs += [pl.BlockSpec((LRU_TB, D), lambda j: (blk(j), 0)),
                     pl.BlockSpec((LRU_TB, D), lambda j: (blk(j), 0))]
        args += [hf, gate]
    n_seq = BATCH + DEC_BATCH
    return pl.pallas_call(
        functools.partial(_lru_scan_kernel, reverse=reverse, combine=combine),
        grid=(nb,),
        in_specs=in_specs,
        out_specs=[pl.BlockSpec((LRU_TB, D), lambda j: (blk(j), 0)),
                   pl.BlockSpec((None, 1, D), lambda j: (_lru_seq_of_block(blk(j)), 0, 0))],
        out_shape=[jax.ShapeDtypeStruct((T, D), BF16 if combine else F32),
                   jax.ShapeDtypeStruct((n_seq, 1, D), F32)],
        scratch_shapes=[pltpu.VMEM((LRU_TB + 2 * SUBLANES, D), F32),
                        pltpu.VMEM((LRU_TB, D), F32),
                        pltpu.VMEM((LRU_TB, D), F32),
                        pltpu.VMEM((1, D), F32)],
        compiler_params=_cparams(("arbitrary",)),
        name="lru_scan_bwd" if reverse else "lru_scan_fwd",
    )(*args)


def _post_kernel(x_ref, o_ref, mod_ref, g2_ref, wo_ref, win_ref, wout_ref, y_ref):
    x1 = x_ref[...] + mod_ref[2:3, :] * jnp.dot(o_ref[...], wo_ref[...], preferred_element_type=F32)
    h = _rms(x1, g2_ref[...])
    h = (h * (1.0 + mod_ref[4:5, :]) + mod_ref[3:4, :]).astype(BF16)
    acc = jnp.zeros((TM, D), F32)
    for c in range(FFN_H // FFN_CHUNK):
        lo = c * FFN_CHUNK
        g = jnp.dot(h, win_ref[:, lo:lo + FFN_CHUNK], preferred_element_type=F32)
        u = jnp.dot(h, win_ref[:, FFN_H + lo:FFN_H + lo + FFN_CHUNK], preferred_element_type=F32)
        a = (jax.nn.silu(g) * u).astype(BF16)
        acc = acc + jnp.dot(a, wout_ref[lo:lo + FFN_CHUNK, :], preferred_element_type=F32)
    y_ref[...] = x1 + mod_ref[5:6, :] * acc


def _resident_spec(shape):
    return pl.BlockSpec(shape, lambda i: (0,) * len(shape), pipeline_mode=pl.Buffered(1))


def _post(x, o, mod, g2, w_o, w_in, w_out):
    return pl.pallas_call(
        _post_kernel,
        grid=(T // TM,),
        in_specs=[_row_spec(D), _row_spec(D), _mod_spec(), _full_spec((1, D)),
                  _resident_spec(w_o.shape), _resident_spec(w_in.shape), _resident_spec(w_out.shape)],
        out_specs=_row_spec(D),
        out_shape=jax.ShapeDtypeStruct((T, D), F32),
        compiler_params=_cparams(("arbitrary",)),
        name="post_mixer_ffn",
    )(x, o, mod, g2, w_o, w_in, w_out)


def _axial_tables(rot_dim):
    row = jnp.repeat(jnp.arange(DEC_SEQ // GRID_W), GRID_W).astype(F32)
    col = jnp.tile(jnp.arange(GRID_W), DEC_SEQ // GRID_W).astype(F32)
    n_freq = rot_dim // 4
    inv = ROPE_THETA ** (-jnp.arange(n_freq, dtype=F32) / n_freq)
    ang = jnp.concatenate([row[:, None] * inv, col[:, None] * inv], axis=-1)
    cos, sin = jnp.cos(ang), jnp.sin(ang)
    reps = LANES // rot_dim
    cos_t = jnp.tile(jnp.concatenate([cos, cos], axis=-1), (DEC_BATCH, reps))
    sin_t = jnp.tile(jnp.concatenate([-sin, sin], axis=-1), (DEC_BATCH, reps))
    cos_t = jnp.concatenate([jnp.ones((N_CTX, LANES), F32), cos_t], axis=0)
    sin_t = jnp.concatenate([jnp.zeros((N_CTX, LANES), F32), sin_t], axis=0)
    return cos_t, sin_t


def kernel(x_prompt, x_sample, cache_mla_ckv, cache_mla_krope, cache_diff_k, cache_diff_v, cache_gqa_k, cache_gqa_v, state_lru_h, c, c_ctx, w_mod, b_mod, g_norm1, g_norm2, w_ffn_in, w_ffn_out, mla_w_in, mla_g_cq, mla_g_ckv, mla_w_uq, mla_w_ukv, mla_g_qk, mla_w_o, diff_w_in, diff_g_qk, diff_lambda, diff_g_sub, diff_w_o, gqa_w_in, gqa_g_qk, gqa_w_o, lru_w_in, lru_conv_w, lru_conv_b, lru_w_gate, lru_b_gate, lru_lambda, lru_w_out):
    x = jnp.concatenate([x_prompt.reshape(N_CTX, D), x_sample.reshape(N_LAT, D)], axis=0)
    cond = jnp.concatenate([c_ctx[None, :], c, jnp.zeros((SUBLANES - N_GROUPS, D), F32)], axis=0)
    mod_all = _modulation(cond, w_mod, b_mod)
    cos128, sin128 = _axial_tables(GQA_HD)
    cos64, sin64 = _axial_tables(DIFF_HD)

    def layer_mod(l):
        return mod_all[l, :N_GROUPS].reshape(N_GROUPS, 6, D)

    def post(l, x, o, w_o):
        return _post(x, o, layer_mod(l), g_norm2[l][None, :], w_o.astype(BF16),
                     w_ffn_in[l].astype(BF16), w_ffn_out[l].astype(BF16))

    def attend(q, k, v, *, hq, group, cache, diff, lambda_init, tq_lat, lat_heads):
        common = dict(group=group, diff=diff, lambda_init=lambda_init)
        o_ctx = _attention(q, k, v, seq_len=SEQ, n_seq=BATCH, row0=0, tq=SEQ,
                           heads_per_step=hq, cache=None, **common)
        o_lat = _attention(q, k, v, seq_len=DEC_SEQ, n_seq=DEC_BATCH, row0=N_CTX, tq=tq_lat,
                           heads_per_step=lat_heads, cache=cache, **common)
        return jnp.concatenate([o_ctx, o_lat], axis=0)

    l = 0
    w_in = mla_w_in[0]
    kr_cols = w_in[:, MLA_Q_RANK + MLA_KV_RANK:]
    w_in2 = jnp.concatenate([w_in, kr_cols], axis=1).astype(BF16)
    w_uq = mla_w_uq[0].reshape(MLA_Q_RANK, MLA_HEADS, MLA_NOPE + MLA_ROPE)
    wqn = w_uq[:, :, :MLA_NOPE].reshape(MLA_Q_RANK, MLA_HEADS * MLA_NOPE).astype(BF16)
    wqr = w_uq[:, :, MLA_NOPE:].reshape(MLA_Q_RANK, MLA_HEADS * MLA_ROPE).astype(BF16)
    wukv = mla_w_ukv[0].astype(BF16)
    gqk = mla_g_qk[0]
    gqn = gqk[0:1, :MLA_NOPE]
    gqr = jnp.tile(gqk[0:1, MLA_NOPE:], (1, 2))
    gkn = gqk[1:2, :MLA_NOPE]
    gkr = jnp.tile(gqk[1:2, MLA_NOPE:], (1, 2))
    q, k, v, ckv_new, kr_new = _mla_in(x, layer_mod(l), g_norm1[l][None, :], w_in2,
                                       mla_g_cq[0][None, :], mla_g_ckv[0][None, :],
                                       gqn, gqr, gkn, gkr, wqn, wqr, wukv, cos64, sin64)
    kc, vc = _mla_cache(cache_mla_ckv[:, 0].reshape(DEC_BATCH * PAST, MLA_KV_RANK),
                        cache_mla_krope[:, 0].reshape(DEC_BATCH * PAST, MLA_ROPE), wukv, gkn)
    cache = (kc, vc,
             (None, PAST, MLA_DQ), lambda b, hb, qb: (hb, b, 0),
             (None, PAST, MLA_V), lambda b, hb, qb: (hb, b, 0))
    o = attend(q, k, v, hq=MLA_HEADS, group=1, cache=cache, diff=None, lambda_init=0.0,
               tq_lat=512, lat_heads=1)
    x = post(l, x, o, mla_w_o[0])
    new_mla_ckv = ckv_new[:N_CTX].reshape(BATCH, 1, SEQ, MLA_KV_RANK)
    new_mla_krope = kr_new[:N_CTX].reshape(BATCH, 1, SEQ, MLA_ROPE)

    l = 1
    lambda_init = 0.8 - 0.6 * math.exp(-0.3 * l)
    q, k, v, k_new, v_new = _diff_in(x, layer_mod(l), g_norm1[l][None, :], diff_w_in[0].astype(BF16),
                                     jnp.tile(diff_g_qk[0], (1, 2)), cos64, sin64)
    n_diff = DIFF_HEADS * 2 * DIFF_HD
    cache = (cache_diff_k[:, 0].reshape(DEC_BATCH, PAST, n_diff),
             cache_diff_v[:, 0].reshape(DEC_BATCH, PAST, n_diff),
             (None, PAST, LANES), lambda b, hb, qb: (b, 0, hb),
             (None, PAST, LANES), lambda b, hb, qb: (b, 0, hb))
    o = attend(q, k, v, hq=2 * DIFF_HEADS, group=2, cache=cache,
               diff=(diff_lambda[0], diff_g_sub[0][None, :]), lambda_init=lambda_init,
               tq_lat=256, lat_heads=2)
    x = post(l, x, o, diff_w_o[0])
    new_diff_k = k_new[:N_CTX].reshape(BATCH, 1, SEQ, DIFF_HEADS, 2, DIFF_HD)
    new_diff_v = v_new[:N_CTX].reshape(BATCH, 1, SEQ, DIFF_HEADS, 2 * DIFF_HD)

    l = 2
    q, k, v, k_new, v_new = _gqa_in(x, layer_mod(l), g_norm1[l][None, :], gqa_w_in[0].astype(BF16),
                                    gqa_g_qk[0], cos128, sin128)
    n_kv = GQA_KV_HEADS * GQA_HD
    group = GQA_Q_HEADS // GQA_KV_HEADS
    cache = (cache_gqa_k[:, 0].reshape(DEC_BATCH, PAST, n_kv),
             cache_gqa_v[:, 0].reshape(DEC_BATCH, PAST, n_kv),
             (None, PAST, GQA_HD), lambda b, hb, qb: (b, 0, hb // group),
             (None, PAST, GQA_HD), lambda b, hb, qb: (b, 0, hb // group))
    o = attend(q, k, v, hq=GQA_Q_HEADS, group=group, cache=cache, diff=None, lambda_init=0.0,
               tq_lat=512, lat_heads=1)
    x = post(l, x, o, gqa_w_o[0])
    new_gqa_k = k_new[:N_CTX].reshape(BATCH, 1, SEQ, GQA_KV_HEADS, GQA_HD)
    new_gqa_v = v_new[:N_CTX].reshape(BATCH, 1, SEQ, GQA_KV_HEADS, GQA_HD)

    l = 3
    gate, xr = _lru_in(x, layer_mod(l), g_norm1[l][None, :], lru_w_in[0].astype(BF16))
    wg = lru_w_gate[0]
    wg = jnp.concatenate([wg[:, 0], wg[:, 1]], axis=-1).astype(BF16)
    h0 = jnp.concatenate([jnp.zeros((BATCH, 2, D), F32), state_lru_h[:, 0]], axis=0)
    conv_b = lru_conv_b[0][None, :]
    hf, st_f = _lru_scan(xr, lru_conv_w[0], conv_b, wg[0], lru_b_gate[0, 0], lru_lambda[0, 0][None, :],
                         h0[:, 0:1], reverse=False)
    y, st_b = _lru_scan(xr, lru_conv_w[0], conv_b, wg[1], lru_b_gate[0, 1], lru_lambda[0, 1][None, :],
                        h0[:, 1:2], reverse=True, hf=hf, gate=gate)
    x = post(l, x, y, lru_w_out[0])
    new_lru_h = jnp.concatenate([st_f[:BATCH], st_b[:BATCH]], axis=1)[:, None]

    y_prompt = x[:N_CTX].reshape(BATCH, SEQ, D)
    y_sample = x[N_CTX:].reshape(DEC_BATCH, DEC_SEQ, D)
    return (y_prompt, y_sample, new_mla_ckv, new_mla_krope, new_diff_k, new_diff_v,
            new_gqa_k, new_gqa_v, new_lru_h)
```

```python
import functools
import math

import jax
import jax.numpy as jnp
from jax import lax
from jax.experimental import pallas as pl
from jax.experimental.pallas import tpu as pltpu

F32 = jnp.float32
BF16 = jnp.bfloat16

D = 1024
BATCH = 32
SEQ = 256
DEPTH = 4
DEC_BATCH = 2
DEC_SEQ = 4096
PAST = 256
GRID_W = 64
EPS = 1e-6
ROPE_THETA = 10000.0
FFN_H = 2816
N_CTX = BATCH * SEQ
N_LAT = DEC_BATCH * DEC_SEQ
T = N_CTX + N_LAT
N_GROUPS = 1 + DEC_BATCH

MLA_HEADS = 8
MLA_NOPE = 128
MLA_ROPE = 64
MLA_V = 128
MLA_Q_RANK = 384
MLA_KV_RANK = 256
MLA_DQ = 256
DIFF_HD = 64
DIFF_HEADS = 8
GQA_HD = 128
GQA_Q_HEADS = 8
GQA_KV_HEADS = 2
LRU_BLOCKS = 8
LRU_BLK = 128
LRU_C = 8.0

LANES = 128
SUBLANES = 8
VMEM_LIMIT = 56 * 1024 * 1024

TM = 512
FFN_CHUNK = 256
LRU_TB = 256
ATTN_TQ = 1024
ATTN_BK = 1024
ATTN_ROWS = 1024
LOG2E = 1.4426950408889634


def _cparams(sem):
    return pltpu.CompilerParams(dimension_semantics=sem, vmem_limit_bytes=VMEM_LIMIT)


def _group_of_block(i, rows_per_block):
    n_ctx_blocks = N_CTX // rows_per_block
    per = DEC_SEQ // rows_per_block
    return jnp.maximum(i - (n_ctx_blocks - per), 0) // per


def _rms(x, gain):
    y = x * lax.rsqrt(jnp.mean(x * x, axis=-1, keepdims=True) + EPS)
    return y * gain


def _rms_halves(x, gain):
    lane = lax.broadcasted_iota(jnp.int32, x.shape, 1)
    low = lane < 64
    sq = x * x
    s_low = jnp.sum(jnp.where(low, sq, 0.0), axis=-1, keepdims=True)
    s_all = jnp.sum(sq, axis=-1, keepdims=True)
    ms = jnp.where(low, s_low, s_all - s_low) * (1.0 / 64.0)
    return x * lax.rsqrt(ms + EPS) * gain


def _rope128(x, c, s):
    return x * c + pltpu.roll(x, 64, 1) * s


def _rope64x2(x, c, s):
    lane = lax.broadcasted_iota(jnp.int32, x.shape, 1)
    low = (lane % 64) < 32
    partner = jnp.where(low, pltpu.roll(x, 96, 1), pltpu.roll(x, 32, 1))
    return x * c + partner * s


def _norm_mod(x_ref, g_ref, mod_ref, shift_row):
    x = x_ref[...]
    y = _rms(x, g_ref[...])
    shift = mod_ref[shift_row:shift_row + 1, :]
    scale = mod_ref[shift_row + 1:shift_row + 2, :]
    return (y * (1.0 + scale) + shift).astype(BF16)


def _mod_kernel(cond_ref, w_ref, b_ref, o_ref):
    s = jax.nn.silu(cond_ref[...]).astype(BF16)
    w = w_ref[...].astype(BF16)
    o_ref[...] = jnp.dot(s, w, preferred_element_type=F32) + b_ref[...]


def _modulation(cond, w_mod, b_mod):
    tn = 1536
    return pl.pallas_call(
        _mod_kernel,
        grid=(DEPTH, 6 * D // tn),
        in_specs=[
            pl.BlockSpec((SUBLANES, D), lambda l, j: (0, 0)),
            pl.BlockSpec((None, D, tn), lambda l, j: (l, 0, j)),
            pl.BlockSpec((None, 1, tn), lambda l, j: (l, 0, j)),
        ],
        out_specs=pl.BlockSpec((None, SUBLANES, tn), lambda l, j: (l, 0, j)),
        out_shape=jax.ShapeDtypeStruct((DEPTH, SUBLANES, 6 * D), F32),
        compiler_params=_cparams(("arbitrary", "arbitrary")),
        name="adaln_mod",
    )(cond, w_mod, b_mod.reshape(DEPTH, 1, 6 * D))


def _row_spec(width):
    return pl.BlockSpec((TM, width), lambda i: (i, 0))


def _full_spec(shape):
    return pl.BlockSpec(shape, lambda i: (0,) * len(shape))


def _mod_spec():
    return pl.BlockSpec((None, 6, D), lambda i: (_group_of_block(i, TM), 0, 0))


def _heads_spec(heads, width):
    return pl.BlockSpec((heads, TM, width), lambda i: (0, i, 0))


def _gqa_in_kernel(x_ref, mod_ref, g_ref, w_ref, gqk_ref, cos_ref, sin_ref,
                   q_ref, k_ref, v_ref, ks_ref, vs_ref):
    h = _norm_mod(x_ref, g_ref, mod_ref, 0)
    proj = jnp.dot(h, w_ref[...], preferred_element_type=F32)
    c = cos_ref[...]
    s = sin_ref[...]
    scale = LOG2E * GQA_HD ** -0.5
    for hd in range(GQA_Q_HEADS):
        q = _rms(proj[:, hd * GQA_HD:(hd + 1) * GQA_HD], gqk_ref[0:1, :])
        q_ref[hd] = (_rope128(q, c, s) * scale).astype(BF16)
    for hd in range(GQA_KV_HEADS):
        lo = (GQA_Q_HEADS + hd) * GQA_HD
        k = _rms(proj[:, lo:lo + GQA_HD], gqk_ref[1:2, :])
        ks_ref[:, hd * GQA_HD:(hd + 1) * GQA_HD] = k
        k_ref[hd] = _rope128(k, c, s).astype(BF16)
        lo = (GQA_Q_HEADS + GQA_KV_HEADS + hd) * GQA_HD
        v = proj[:, lo:lo + GQA_HD]
        vs_ref[:, hd * GQA_HD:(hd + 1) * GQA_HD] = v
        v_ref[hd] = v.astype(BF16)


def _gqa_in(x, mod, g1, w_in, g_qk, cos, sin):
    n_kv = GQA_KV_HEADS * GQA_HD
    return pl.pallas_call(
        _gqa_in_kernel,
        grid=(T // TM,),
        in_specs=[_row_spec(D), _mod_spec(), _full_spec((1, D)), _full_spec(w_in.shape),
                  _full_spec((2, GQA_HD)), _row_spec(LANES), _row_spec(LANES)],
        out_specs=[_heads_spec(GQA_Q_HEADS, GQA_HD), _heads_spec(GQA_KV_HEADS, GQA_HD),
                   _heads_spec(GQA_KV_HEADS, GQA_HD), _row_spec(n_kv), _row_spec(n_kv)],
        out_shape=[jax.ShapeDtypeStruct((GQA_Q_HEADS, T, GQA_HD), BF16),
                   jax.ShapeDtypeStruct((GQA_KV_HEADS, T, GQA_HD), BF16),
                   jax.ShapeDtypeStruct((GQA_KV_HEADS, T, GQA_HD), BF16),
                   jax.ShapeDtypeStruct((T, n_kv), F32),
                   jax.ShapeDtypeStruct((T, n_kv), F32)],
        compiler_params=_cparams(("arbitrary",)),
        name="gqa_in",
    )(x, mod, g1, w_in, g_qk, cos, sin)


def _diff_in_kernel(x_ref, mod_ref, g_ref, w_ref, gqk_ref, cos_ref, sin_ref,
                    q_ref, k_ref, v_ref, ks_ref, vs_ref):
    h = _norm_mod(x_ref, g_ref, mod_ref, 0)
    proj = jnp.dot(h, w_ref[...], preferred_element_type=F32)
    c = cos_ref[...]
    s = sin_ref[...]
    scale = LOG2E * DIFF_HD ** -0.5
    lane = lax.broadcasted_iota(jnp.int32, (TM, LANES), 1)
    low = lane < DIFF_HD
    n = DIFF_HEADS * 2 * DIFF_HD
    for hd in range(DIFF_HEADS):
        sl = slice(hd * LANES, (hd + 1) * LANES)
        q = _rope64x2(_rms_halves(proj[:, sl], gqk_ref[0:1, :]), c, s) * scale
        q_ref[2 * hd] = jnp.where(low, q, 0.0).astype(BF16)
        q_ref[2 * hd + 1] = jnp.where(low, 0.0, q).astype(BF16)
        k = _rms_halves(proj[:, n + hd * LANES:n + (hd + 1) * LANES], gqk_ref[1:2, :])
        ks_ref[:, sl] = k
        k_ref[hd] = _rope64x2(k, c, s).astype(BF16)
        v = proj[:, 2 * n + hd * LANES:2 * n + (hd + 1) * LANES]
        vs_ref[:, sl] = v
        v_ref[hd] = v.astype(BF16)


def _diff_in(x, mod, g1, w_in, g_qk2, cos, sin):
    n = DIFF_HEADS * 2 * DIFF_HD
    return pl.pallas_call(
        _diff_in_kernel,
        grid=(T // TM,),
        in_specs=[_row_spec(D), _mod_spec(), _full_spec((1, D)), _full_spec(w_in.shape),
                  _full_spec((2, LANES)), _row_spec(LANES), _row_spec(LANES)],
        out_specs=[_heads_spec(2 * DIFF_HEADS, LANES), _heads_spec(DIFF_HEADS, LANES),
                   _heads_spec(DIFF_HEADS, LANES), _row_spec(n), _row_spec(n)],
        out_shape=[jax.ShapeDtypeStruct((2 * DIFF_HEADS, T, LANES), BF16),
                   jax.ShapeDtypeStruct((DIFF_HEADS, T, LANES), BF16),
                   jax.ShapeDtypeStruct((DIFF_HEADS, T, LANES), BF16),
                   jax.ShapeDtypeStruct((T, n), F32),
                   jax.ShapeDtypeStruct((T, n), F32)],
        compiler_params=_cparams(("arbitrary",)),
        name="diff_in",
    )(x, mod, g1, w_in, g_qk2, cos, sin)


def _mla_keys_values(ckv_bf16, kr2, w_ukv_ref, gk_ref, k_ref, v_ref):
    kv = jnp.dot(ckv_bf16, w_ukv_ref[...], preferred_element_type=F32)
    lane = lax.broadcasted_iota(jnp.int32, kr2.shape, 1)
    low = lane < MLA_ROPE
    kr_low = jnp.where(low, kr2, 0.0).astype(BF16)
    kr_high = jnp.where(low, 0.0, kr2).astype(BF16)
    width = MLA_NOPE + MLA_V
    for hd in range(MLA_HEADS):
        k_nope = _rms(kv[:, hd * width:hd * width + MLA_NOPE], gk_ref[...])
        k_ref[hd, :, 0:MLA_NOPE] = k_nope.astype(BF16)
        k_ref[hd, :, MLA_NOPE:MLA_DQ] = kr_low if hd % 2 == 0 else kr_high
        v_ref[hd] = kv[:, hd * width + MLA_NOPE:(hd + 1) * width].astype(BF16)


def _mla_in_kernel(x_ref, mod_ref, g_ref, w_ref, gcq_ref, gckv_ref, gqn_ref, gqr_ref, gkn_ref,
                   gkr_ref, wqn_ref, wqr_ref, wukv_ref, cos_ref, sin_ref,
                   q_ref, k_ref, v_ref, ckv_ref, kr_ref):
    h = _norm_mod(x_ref, g_ref, mod_ref, 0)
    proj = jnp.dot(h, w_ref[...], preferred_element_type=F32)
    c = cos_ref[...]
    s = sin_ref[...]
    cq = _rms(proj[:, 0:MLA_Q_RANK], gcq_ref[...]).astype(BF16)
    ckv = _rms(proj[:, MLA_Q_RANK:MLA_Q_RANK + MLA_KV_RANK], gckv_ref[...])
    ckv_ref[...] = ckv
    kr2 = _rms_halves(proj[:, MLA_Q_RANK + MLA_KV_RANK:], gkr_ref[...])
    kr_ref[...] = kr2[:, 0:MLA_ROPE]
    _mla_keys_values(ckv.astype(BF16), _rope64x2(kr2, c, s), wukv_ref, gkn_ref, k_ref, v_ref)

    scale = LOG2E * (MLA_NOPE + MLA_ROPE) ** -0.5
    qn = jnp.dot(cq, wqn_ref[...], preferred_element_type=F32)
    qr = jnp.dot(cq, wqr_ref[...], preferred_element_type=F32)
    lane = lax.broadcasted_iota(jnp.int32, (TM, LANES), 1)
    low = lane < MLA_ROPE
    for pair in range(MLA_HEADS // 2):
        r = _rope64x2(_rms_halves(qr[:, pair * LANES:(pair + 1) * LANES], gqr_ref[...]), c, s) * scale
        q_ref[2 * pair, :, MLA_NOPE:MLA_DQ] = jnp.where(low, r, 0.0).astype(BF16)
        q_ref[2 * pair + 1, :, MLA_NOPE:MLA_DQ] = jnp.where(low, 0.0, r).astype(BF16)
    for hd in range(MLA_HEADS):
        q = _rms(qn[:, hd * MLA_NOPE:(hd + 1) * MLA_NOPE], gqn_ref[...]) * scale
        q_ref[hd, :, 0:MLA_NOPE] = q.astype(BF16)


def _mla_in(x, mod, g1, w_in, g_cq, g_ckv, gqn, gqr, gkn, gkr, wqn, wqr, wukv, cos, sin):
    return pl.pallas_call(
        _mla_in_kernel,
        grid=(T // TM,),
        in_specs=[_row_spec(D), _mod_spec(), _full_spec((1, D)), _full_spec(w_in.shape),
                  _full_spec((1, MLA_Q_RANK)), _full_spec((1, MLA_KV_RANK)),
                  _full_spec((1, LANES)), _full_spec((1, LANES)), _full_spec((1, LANES)),
                  _full_spec((1, LANES)), _full_spec(wqn.shape), _full_spec(wqr.shape),
                  _full_spec(wukv.shape), _row_spec(LANES), _row_spec(LANES)],
        out_specs=[_heads_spec(MLA_HEADS, MLA_DQ), _heads_spec(MLA_HEADS, MLA_DQ),
                   _heads_spec(MLA_HEADS, MLA_V), _row_spec(MLA_KV_RANK), _row_spec(MLA_ROPE)],
        out_shape=[jax.ShapeDtypeStruct((MLA_HEADS, T, MLA_DQ), BF16),
                   jax.ShapeDtypeStruct((MLA_HEADS, T, MLA_DQ), BF16),
                   jax.ShapeDtypeStruct((MLA_HEADS, T, MLA_V), BF16),
                   jax.ShapeDtypeStruct((T, MLA_KV_RANK), F32),
                   jax.ShapeDtypeStruct((T, MLA_ROPE), F32)],
        compiler_params=_cparams(("arbitrary",)),
        name="mla_in",
    )(x, mod, g1, w_in, g_cq, g_ckv, gqn, gqr, gkn, gkr, wqn, wqr, wukv, cos, sin)


def _mla_cache_kernel(ckv_ref, kr_ref, wukv_ref, gkn_ref, k_ref, v_ref):
    kr = kr_ref[...]
    kr2 = jnp.concatenate([kr, kr], axis=-1)
    _mla_keys_values(ckv_ref[...].astype(BF16), kr2, wukv_ref, gkn_ref, k_ref, v_ref)


def _mla_cache(ckv, kr, wukv, gkn):
    rows = ckv.shape[0]
    return pl.pallas_call(
        _mla_cache_kernel,
        grid=(1,),
        in_specs=[_full_spec(ckv.shape), _full_spec(kr.shape), _full_spec(wukv.shape),
                  _full_spec((1, LANES))],
        out_specs=[_full_spec((MLA_HEADS, rows, MLA_DQ)), _full_spec((MLA_HEADS, rows, MLA_V))],
        out_shape=[jax.ShapeDtypeStruct((MLA_HEADS, rows, MLA_DQ), BF16),
                   jax.ShapeDtypeStruct((MLA_HEADS, rows, MLA_V), BF16)],
        compiler_params=_cparams(("arbitrary",)),
        name="mla_cache_kv",
    )(ckv, kr, wukv, gkn)


def _lru_in_kernel(x_ref, mod_ref, g_ref, w_ref, gate_ref, xr_ref):
    h = _norm_mod(x_ref, g_ref, mod_ref, 0)
    proj = jnp.dot(h, w_ref[...], preferred_element_type=F32)
    gate_ref[...] = proj[:, 0:D]
    xr_ref[...] = proj[:, D:2 * D]


def _lru_in(x, mod, g1, w_in):
    return pl.pallas_call(
        _lru_in_kernel,
        grid=(T // TM,),
        in_specs=[_row_spec(D), _mod_spec(), _full_spec((1, D)), _full_spec(w_in.shape)],
        out_specs=[_row_spec(D), _row_spec(D)],
        out_shape=[jax.ShapeDtypeStruct((T, D), F32), jax.ShapeDtypeStruct((T, D), F32)],
        compiler_params=_cparams(("arbitrary",)),
        name="lru_in",
    )(x, mod, g1, w_in)


def _softmax_pv(q, key_blocks):
    nt = (((1,), (1,)), ((), ()))
    m = l = acc = None
    for k, v in key_blocks:
        s = lax.dot_general(q, k, nt, preferred_element_type=F32)
        mb = jnp.max(s, axis=-1, keepdims=True)
        if m is None:
            m = mb
            p = jnp.exp2(s - m)
            l = jnp.sum(p, axis=-1, keepdims=True)
            acc = jnp.dot(p.astype(BF16), v, preferred_element_type=F32)
        else:
            m_new = jnp.maximum(m, mb)
            alpha = jnp.exp2(m - m_new)
            p = jnp.exp2(s - m_new)
            l = alpha * l + jnp.sum(p, axis=-1, keepdims=True)
            acc = alpha * acc + jnp.dot(p.astype(BF16), v, preferred_element_type=F32)
            m = m_new
    return acc / l


def _attn_kernel(*refs, heads, group, sub, bk, has_cache, diff, dv, lambda_init):
    it = iter(refs)
    q_ref, kn_ref, vn_ref = next(it), next(it), next(it)
    kc_ref = vc_ref = lam_ref = gsub_ref = None
    if has_cache:
        kc_ref, vc_ref = next(it), next(it)
    if diff:
        lam_ref, gsub_ref = next(it), next(it)
    o_ref = next(it)

    tq = q_ref.shape[1]
    seq_len = kn_ref.shape[1]
    stacked = min(group, heads)
    cached = []
    if has_cache:
        cached = [(kc_ref[...].astype(BF16), vc_ref[...].astype(BF16))]
    if diff:
        lam = lam_ref[...]
        lam_full = (jnp.exp(jnp.sum(lam[0:1] * lam[1:2], axis=-1, keepdims=True))
                    - jnp.exp(jnp.sum(lam[2:3] * lam[3:4], axis=-1, keepdims=True)) + lambda_init)

    for kv in range(max(heads // group, 1)):
        def body(i, carry, kv=kv):
            rows = pl.ds(pl.multiple_of(i * sub, sub), sub)
            q = jnp.concatenate([q_ref[kv * stacked + g, rows, :] for g in range(stacked)], axis=0)
            blocks = cached + [(kn_ref[kv, j * bk:(j + 1) * bk, :], vn_ref[kv, j * bk:(j + 1) * bk, :])
                               for j in range(seq_len // bk)]
            o = _softmax_pv(q, blocks)
            if diff:
                od = o[0:sub] - lam_full * o[sub:2 * sub]
                od = _rms(od, gsub_ref[...]) * (1.0 - lambda_init)
                o_ref[rows, kv * dv:(kv + 1) * dv] = od.astype(BF16)
            else:
                for g in range(stacked):
                    hd = kv * stacked + g
                    o_ref[rows, hd * dv:(hd + 1) * dv] = o[g * sub:(g + 1) * sub].astype(BF16)
            return carry
        lax.fori_loop(0, tq // sub, body, 0)


def _attention(q, kn, vn, *, seq_len, n_seq, row0, tq, sub, bk, heads_per_step, group, cache=None,
               diff=None, lambda_init=0.0):
    hq, _, dq = q.shape
    hkv, _, dv = vn.shape
    kv_per_step = max(heads_per_step // group, 1)
    n_hblk = hq // heads_per_step
    n_qblk = seq_len // tq
    seq0 = row0 // seq_len
    qblk0 = row0 // tq

    def q_map(b, hb, qb):
        return (hb, qblk0 + b * n_qblk + qb, 0)

    def kv_map(b, hb, qb):
        return ((hb * heads_per_step) // (group * kv_per_step), seq0 + b, 0)

    in_specs = [pl.BlockSpec((heads_per_step, tq, dq), q_map),
                pl.BlockSpec((kv_per_step, seq_len, dq), kv_map),
                pl.BlockSpec((kv_per_step, seq_len, dv), kv_map)]
    args = [q, kn, vn]
    if cache is not None:
        kc, vc, kc_block, kc_map, vc_block, vc_map = cache
        in_specs += [pl.BlockSpec(kc_block, kc_map), pl.BlockSpec(vc_block, vc_map)]
        args += [kc, vc]
    out_heads = heads_per_step
    if diff is not None:
        lam, gsub = diff
        in_specs += [pl.BlockSpec(lam.shape, lambda b, hb, qb: (0, 0)),
                     pl.BlockSpec(gsub.shape, lambda b, hb, qb: (0, 0))]
        args += [lam, gsub]
        out_heads = heads_per_step // 2
    n_out = (hq // 2 if diff is not None else hq) * dv
    kernel = functools.partial(_attn_kernel, heads=heads_per_step, group=group, sub=sub, bk=bk,
                               has_cache=cache is not None, diff=diff is not None, dv=dv,
                               lambda_init=lambda_init)
    return pl.pallas_call(
        kernel,
        grid=(n_seq, n_hblk, n_qblk),
        in_specs=in_specs,
        out_specs=pl.BlockSpec((tq, out_heads * dv), lambda b, hb, qb: (b * n_qblk + qb, hb)),
        out_shape=jax.ShapeDtypeStruct((n_seq * seq_len, n_out), BF16),
        compiler_params=_cparams(("arbitrary", "arbitrary", "arbitrary")),
        name="attention",
    )(*args)


def _lru_seq_pos(i):
    n_ctx_blocks = N_CTX // LRU_TB
    per_ctx = SEQ // LRU_TB
    per_lat = DEC_SEQ // LRU_TB
    pos = jnp.where(i < n_ctx_blocks, i % per_ctx, (i - n_ctx_blocks) % per_lat)
    length = jnp.where(i < n_ctx_blocks, per_ctx, per_lat)
    return pos == 0, pos == length - 1


def _lru_seq_of_block(i):
    n_ctx_blocks = N_CTX // LRU_TB
    return jnp.where(i < n_ctx_blocks, i // (SEQ // LRU_TB),
                     BATCH + (i - n_ctx_blocks) // (DEC_SEQ // LRU_TB))


def _tile_scan(a, u, reverse):
    row = lax.broadcasted_iota(jnp.int32, a.shape, 0)
    for k in (1, 2, 4):
        if reverse:
            valid = row < SUBLANES - k
            shift = SUBLANES - k
        else:
            valid = row >= k
            shift = k
        a_s = jnp.where(valid, pltpu.roll(a, shift, 0), 1.0)
        u_s = jnp.where(valid, pltpu.roll(u, shift, 0), 0.0)
        u = a * u_s + u
        a = a * a_s
    return a, u


def _lru_scan_kernel(x_ref, prev_ref, next_ref, cw_ref, cb_ref, wg_ref, bg_ref, lam_ref, h0_ref,
                     *rest, reverse, combine):
    if combine:
        hf_ref, gate_ref, y_ref, st_ref, xpad, a_s, u_s, carry = rest
    else:
        y_ref, st_ref, xpad, a_s, u_s, carry = rest
    j = pl.program_id(0)
    i = (T // LRU_TB - 1 - j) if reverse else j
    first, last = _lru_seq_pos(i)
    starts = last if reverse else first
    ends = first if reverse else last

    xpad[0:SUBLANES, :] = jnp.where(first, 0.0, prev_ref[...])
    xpad[SUBLANES:SUBLANES + LRU_TB, :] = x_ref[...]
    xpad[SUBLANES + LRU_TB:, :] = jnp.where(last, 0.0, next_ref[...])
    xr = xpad[SUBLANES - 1:SUBLANES - 1 + LRU_TB, :] * cw_ref[0:1, :]
    for t in range(1, 4):
        xr = xr + xpad[SUBLANES - 1 + t:SUBLANES - 1 + t + LRU_TB, :] * cw_ref[t:t + 1, :]
    xr = xr + cb_ref[...]

    xr_b = xr.astype(BF16)
    lam = lam_ref[...]
    neg = -lam
    softplus = jnp.maximum(neg, 0.0) + jnp.log1p(jnp.exp(-jnp.abs(neg)))
    for n in range(LRU_BLOCKS):
        sl = slice(n * LRU_BLK, (n + 1) * LRU_BLK)
        g = jnp.dot(xr_b[:, sl], wg_ref[n], preferred_element_type=F32)
        r = jax.nn.sigmoid(g[:, 0:LRU_BLK] + bg_ref[0:1, sl])
        gi = jax.nn.sigmoid(g[:, LRU_BLK:] + bg_ref[1:2, sl])
        log_a = -LRU_C * r * softplus[:, sl]
        th = jnp.tanh(log_a)
        one_minus_a2 = -2.0 * th / (1.0 - th)
        a_s[:, sl] = jnp.exp(log_a)
        u_s[:, sl] = jnp.sqrt(one_minus_a2) * (gi * xr[:, sl])

    @pl.when(starts)
    def _():
        carry[...] = h0_ref[...]

    n_tiles = LRU_TB // SUBLANES

    def body(t, h):
        tt = (n_tiles - 1 - t) if reverse else t
        rows = pl.ds(pl.multiple_of(tt * SUBLANES, SUBLANES), SUBLANES)
        a_c, u_c = _tile_scan(a_s[rows, :], u_s[rows, :], reverse)
        hs = a_c * h + u_c
        if combine:
            gate = gate_ref[rows, :]
            y_ref[rows, :] = ((hf_ref[rows, :] + hs) * jax.nn.gelu(gate)).astype(BF16)
        else:
            y_ref[rows, :] = hs
        return hs[0:1, :] if reverse else hs[SUBLANES - 1:SUBLANES, :]

    h_end = lax.fori_loop(0, n_tiles, body, carry[...])
    carry[...] = h_end

    @pl.when(ends)
    def _():
        st_ref[...] = h_end


def _lru_scan(xr, conv_w, conv_b, w_gate, b_gate, lam, h0, *, reverse, hf=None, gate=None):
    nb = T // LRU_TB
    hb = LRU_TB // SUBLANES
    n_halo = T // SUBLANES

    def blk(j):
        return (nb - 1 - j) if reverse else j

    in_specs = [
        pl.BlockSpec((LRU_TB, D), lambda j: (blk(j), 0)),
        pl.BlockSpec((SUBLANES, D), lambda j: (jnp.maximum(blk(j) * hb - 1, 0), 0)),
        pl.BlockSpec((SUBLANES, D), lambda j: (jnp.minimum((blk(j) + 1) * hb, n_halo - 1), 0)),
        pl.BlockSpec((4, D), lambda j: (0, 0)),
        pl.BlockSpec((1, D), lambda j: (0, 0)),
        pl.BlockSpec((LRU_BLOCKS, LRU_BLK, 2 * LRU_BLK), lambda j: (0, 0, 0)),
        pl.BlockSpec((2, D), lambda j: (0, 0)),
        pl.BlockSpec((1, D), lambda j: (0, 0)),
        pl.BlockSpec((None, 1, D), lambda j: (_lru_seq_of_block(blk(j)), 0, 0)),
    ]
    args = [xr, xr, xr, conv_w, conv_b, w_gate, b_gate, lam, h0]
    combine = hf is not None
    if combine:
        in_specs += [pl.BlockSpec((LRU_TB, D), lambda j: (blk(j), 0)),
                     pl.BlockSpec((LRU_TB, D), lambda j: (blk(j), 0))]
        args += [hf, gate]
    n_seq = BATCH + DEC_BATCH
    return pl.pallas_call(
        functools.partial(_lru_scan_kernel, reverse=reverse, combine=combine),
        grid=(nb,),
        in_specs=in_specs,
        out_specs=[pl.BlockSpec((LRU_TB, D), lambda j: (blk(j), 0)),
                   pl.BlockSpec((None, 1, D), lambda j: (_lru_seq_of_block(blk(j)), 0, 0))],
        out_shape=[jax.ShapeDtypeStruct((T, D), BF16 if combine else F32),
                   jax.ShapeDtypeStruct((n_seq, 1, D), F32)],
        scratch_shapes=[pltpu.VMEM((LRU_TB + 2 * SUBLANES, D), F32),
                        pltpu.VMEM((LRU_TB, D), F32),
                        pltpu.VMEM((LRU_TB, D), F32),
                        pltpu.VMEM((1, D), F32)],
        compiler_params=_cparams(("arbitrary",)),
        name="lru_scan_bwd" if reverse else "lru_scan_fwd",
    )(*args)


def _post_kernel(x_ref, o_ref, mod_ref, g2_ref, wo_ref, win_ref, wout_ref, y_ref):
    x1 = x_ref[...] + mod_ref[2:3, :] * jnp.dot(o_ref[...], wo_ref[...], preferred_element_type=F32)
    h = _rms(x1, g2_ref[...])
    h = (h * (1.0 + mod_ref[4:5, :]) + mod_ref[3:4, :]).astype(BF16)
    acc = jnp.zeros((TM, D), F32)
    for c in range(FFN_H // FFN_CHUNK):
        lo = c * FFN_CHUNK
        g = jnp.dot(h, win_ref[:, lo:lo + FFN_CHUNK], preferred_element_type=F32)
        u = jnp.dot(h, win_ref[:, FFN_H + lo:FFN_H + lo + FFN_CHUNK], preferred_element_type=F32)
        a = (jax.nn.silu(g) * u).astype(BF16)
        acc = acc + jnp.dot(a, wout_ref[lo:lo + FFN_CHUNK, :], preferred_element_type=F32)
    y_ref[...] = x1 + mod_ref[5:6, :] * acc


def _resident_spec(shape):
    return pl.BlockSpec(shape, lambda i: (0,) * len(shape), pipeline_mode=pl.Buffered(1))


def _post(x, o, mod, g2, w_o, w_in, w_out):
    return pl.pallas_call(
        _post_kernel,
        grid=(T // TM,),
        in_specs=[_row_spec(D), _row_spec(D), _mod_spec(), _full_spec((1, D)),
                  _resident_spec(w_o.shape), _resident_spec(w_in.shape), _resident_spec(w_out.shape)],
        out_specs=_row_spec(D),
        out_shape=jax.ShapeDtypeStruct((T, D), F32),
        compiler_params=_cparams(("arbitrary",)),
        name="post_mixer_ffn",
    )(x, o, mod, g2, w_o, w_in, w_out)


def _axial_tables(rot_dim):
    row = jnp.repeat(jnp.arange(DEC_SEQ // GRID_W), GRID_W).astype(F32)
    col = jnp.tile(jnp.arange(GRID_W), DEC_SEQ // GRID_W).astype(F32)
    n_freq = rot_dim // 4
    inv = ROPE_THETA ** (-jnp.arange(n_freq, dtype=F32) / n_freq)
    ang = jnp.concatenate([row[:, None] * inv, col[:, None] * inv], axis=-1)
    cos, sin = jnp.cos(ang), jnp.sin(ang)
    reps = LANES // rot_dim
    cos_t = jnp.tile(jnp.concatenate([cos, cos], axis=-1), (DEC_BATCH, reps))
    sin_t = jnp.tile(jnp.concatenate([-sin, sin], axis=-1), (DEC_BATCH, reps))
    cos_t = jnp.concatenate([jnp.ones((N_CTX, LANES), F32), cos_t], axis=0)
    sin_t = jnp.concatenate([jnp.zeros((N_CTX, LANES), F32), sin_t], axis=0)
    return cos_t, sin_t


def kernel(x_prompt, x_sample, cache_mla_ckv, cache_mla_krope, cache_diff_k, cache_diff_v, cache_gqa_k, cache_gqa_v, state_lru_h, c, c_ctx, w_mod, b_mod, g_norm1, g_norm2, w_ffn_in, w_ffn_out, mla_w_in, mla_g_cq, mla_g_ckv, mla_w_uq, mla_w_ukv, mla_g_qk, mla_w_o, diff_w_in, diff_g_qk, diff_lambda, diff_g_sub, diff_w_o, gqa_w_in, gqa_g_qk, gqa_w_o, lru_w_in, lru_conv_w, lru_conv_b, lru_w_gate, lru_b_gate, lru_lambda, lru_w_out):
    x = jnp.concatenate([x_prompt.reshape(N_CTX, D), x_sample.reshape(N_LAT, D)], axis=0)
    cond = jnp.concatenate([c_ctx[None, :], c, jnp.zeros((SUBLANES - N_GROUPS, D), F32)], axis=0)
    mod_all = _modulation(cond, w_mod, b_mod)
    cos128, sin128 = _axial_tables(GQA_HD)
    cos64, sin64 = _axial_tables(DIFF_HD)

    def layer_mod(l):
        return mod_all[l, :N_GROUPS].reshape(N_GROUPS, 6, D)

    def post(l, x, o, w_o):
        return _post(x, o, layer_mod(l), g_norm2[l][None, :], w_o.astype(BF16),
                     w_ffn_in[l].astype(BF16), w_ffn_out[l].astype(BF16))

    def attend(q, k, v, *, hq, group, cache, diff, lambda_init):
        common = dict(group=group, diff=diff, lambda_init=lambda_init)
        sub = ATTN_ROWS // group
        o_ctx = _attention(q, k, v, seq_len=SEQ, n_seq=BATCH, row0=0, tq=SEQ, bk=SEQ,
                           sub=min(sub, SEQ), heads_per_step=hq, cache=None, **common)
        o_lat = _attention(q, k, v, seq_len=DEC_SEQ, n_seq=DEC_BATCH, row0=N_CTX, tq=ATTN_TQ,
                           bk=ATTN_BK, sub=sub, heads_per_step=group, cache=cache, **common)
        return jnp.concatenate([o_ctx, o_lat], axis=0)

    l = 0
    w_in = mla_w_in[0]
    kr_cols = w_in[:, MLA_Q_RANK + MLA_KV_RANK:]
    w_in2 = jnp.concatenate([w_in, kr_cols], axis=1).astype(BF16)
    w_uq = mla_w_uq[0].reshape(MLA_Q_RANK, MLA_HEADS, MLA_NOPE + MLA_ROPE)
    wqn = w_uq[:, :, :MLA_NOPE].reshape(MLA_Q_RANK, MLA_HEADS * MLA_NOPE).astype(BF16)
    wqr = w_uq[:, :, MLA_NOPE:].reshape(MLA_Q_RANK, MLA_HEADS * MLA_ROPE).astype(BF16)
    wukv = mla_w_ukv[0].astype(BF16)
    gqk = mla_g_qk[0]
    gqn = gqk[0:1, :MLA_NOPE]
    gqr = jnp.tile(gqk[0:1, MLA_NOPE:], (1, 2))
    gkn = gqk[1:2, :MLA_NOPE]
    gkr = jnp.tile(gqk[1:2, MLA_NOPE:], (1, 2))
    q, k, v, ckv_new, kr_new = _mla_in(x, layer_mod(l), g_norm1[l][None, :], w_in2,
                                       mla_g_cq[0][None, :], mla_g_ckv[0][None, :],
                                       gqn, gqr, gkn, gkr, wqn, wqr, wukv, cos64, sin64)
    kc, vc = _mla_cache(cache_mla_ckv[:, 0].reshape(DEC_BATCH * PAST, MLA_KV_RANK),
                        cache_mla_krope[:, 0].reshape(DEC_BATCH * PAST, MLA_ROPE), wukv, gkn)
    cache = (kc, vc,
             (None, PAST, MLA_DQ), lambda b, hb, qb: (hb, b, 0),
             (None, PAST, MLA_V), lambda b, hb, qb: (hb, b, 0))
    o = attend(q, k, v, hq=MLA_HEADS, group=1, cache=cache, diff=None, lambda_init=0.0)
    x = post(l, x, o, mla_w_o[0])
    new_mla_ckv = ckv_new[:N_CTX].reshape(BATCH, 1, SEQ, MLA_KV_RANK)
    new_mla_krope = kr_new[:N_CTX].reshape(BATCH, 1, SEQ, MLA_ROPE)

    l = 1
    lambda_init = 0.8 - 0.6 * math.exp(-0.3 * l)
    q, k, v, k_new, v_new = _diff_in(x, layer_mod(l), g_norm1[l][None, :], diff_w_in[0].astype(BF16),
                                     jnp.tile(diff_g_qk[0], (1, 2)), cos64, sin64)
    n_diff = DIFF_HEADS * 2 * DIFF_HD
    cache = (cache_diff_k[:, 0].reshape(DEC_BATCH, PAST, n_diff),
             cache_diff_v[:, 0].reshape(DEC_BATCH, PAST, n_diff),
             (None, PAST, LANES), lambda b, hb, qb: (b, 0, hb),
             (None, PAST, LANES), lambda b, hb, qb: (b, 0, hb))
    o = attend(q, k, v, hq=2 * DIFF_HEADS, group=2, cache=cache,
               diff=(diff_lambda[0], diff_g_sub[0][None, :]), lambda_init=lambda_init)
    x = post(l, x, o, diff_w_o[0])
    new_diff_k = k_new[:N_CTX].reshape(BATCH, 1, SEQ, DIFF_HEADS, 2, DIFF_HD)
    new_diff_v = v_new[:N_CTX].reshape(BATCH, 1, SEQ, DIFF_HEADS, 2 * DIFF_HD)

    l = 2
    q, k, v, k_new, v_new = _gqa_in(x, layer_mod(l), g_norm1[l][None, :], gqa_w_in[0].astype(BF16),
                                    gqa_g_qk[0], cos128, sin128)
    n_kv = GQA_KV_HEADS * GQA_HD
    group = GQA_Q_HEADS // GQA_KV_HEADS
    cache = (cache_gqa_k[:, 0].reshape(DEC_BATCH, PAST, n_kv),
             cache_gqa_v[:, 0].reshape(DEC_BATCH, PAST, n_kv),
             (None, PAST, GQA_HD), lambda b, hb, qb: (b, 0, hb),
             (None, PAST, GQA_HD), lambda b, hb, qb: (b, 0, hb))
    o = attend(q, k, v, hq=GQA_Q_HEADS, group=group, cache=cache, diff=None, lambda_init=0.0)
    x = post(l, x, o, gqa_w_o[0])
    new_gqa_k = k_new[:N_CTX].reshape(BATCH, 1, SEQ, GQA_KV_HEADS, GQA_HD)
    new_gqa_v = v_new[:N_CTX].reshape(BATCH, 1, SEQ, GQA_KV_HEADS, GQA_HD)

    l = 3
    gate, xr = _lru_in(x, layer_mod(l), g_norm1[l][None, :], lru_w_in[0].astype(BF16))
    wg = lru_w_gate[0]
    wg = jnp.concatenate([wg[:, 0], wg[:, 1]], axis=-1).astype(BF16)
    h0 = jnp.concatenate([jnp.zeros((BATCH, 2, D), F32), state_lru_h[:, 0]], axis=0)
    conv_b = lru_conv_b[0][None, :]
    hf, st_f = _lru_scan(xr, lru_conv_w[0], conv_b, wg[0], lru_b_gate[0, 0], lru_lambda[0, 0][None, :],
                         h0[:, 0:1], reverse=False)
    y, st_b = _lru_scan(xr, lru_conv_w[0], conv_b, wg[1], lru_b_gate[0, 1], lru_lambda[0, 1][None, :],
                        h0[:, 1:2], reverse=True, hf=hf, gate=gate)
    x = post(l, x, y, lru_w_out[0])
    new_lru_h = jnp.concatenate([st_f[:BATCH], st_b[:BATCH]], axis=1)[:, None]

    y_prompt = x[:N_CTX].reshape(BATCH, SEQ, D)
    y_sample = x[N_CTX:].reshape(DEC_BATCH, DEC_SEQ, D)
    return (y_prompt, y_sample, new_mla_ckv, new_mla_krope, new_diff_k, new_diff_v,
            new_gqa_k, new_gqa_v, new_lru_h)
```

```python
import functools
import math

import jax
import jax.numpy as jnp
from jax import lax
from jax.experimental import pallas as pl
from jax.experimental.pallas import tpu as pltpu

F32 = jnp.float32
BF16 = jnp.bfloat16

D = 1024
BATCH = 32
SEQ = 256
DEPTH = 4
DEC_BATCH = 2
DEC_SEQ = 4096
PAST = 256
GRID_W = 64
EPS = 1e-6
ROPE_THETA = 10000.0
FFN_H = 2816
N_CTX = BATCH * SEQ
N_LAT = DEC_BATCH * DEC_SEQ
T = N_CTX + N_LAT
N_GROUPS = 1 + DEC_BATCH

MLA_HEADS = 8
MLA_NOPE = 128
MLA_ROPE = 64
MLA_V = 128
MLA_Q_RANK = 384
MLA_KV_RANK = 256
MLA_DQ = 256
DIFF_HD = 64
DIFF_HEADS = 8
GQA_HD = 128
GQA_Q_HEADS = 8
GQA_KV_HEADS = 2
LRU_BLOCKS = 8
LRU_BLK = 128
LRU_C = 8.0

LANES = 128
SUBLANES = 8
VMEM_LIMIT = 56 * 1024 * 1024

TM = 512
FFN_CHUNK = 256
LRU_TB = 256
ATTN_TQ = 1024
ATTN_BK = 1024
ATTN_ROWS = 1024
LOG2E = 1.4426950408889634


def _cparams(sem):
    return pltpu.CompilerParams(dimension_semantics=sem, vmem_limit_bytes=VMEM_LIMIT)


def _group_of_block(i, rows_per_block):
    n_ctx_blocks = N_CTX // rows_per_block
    per = DEC_SEQ // rows_per_block
    return jnp.maximum(i - (n_ctx_blocks - per), 0) // per


def _rms(x, gain):
    y = x * lax.rsqrt(jnp.mean(x * x, axis=-1, keepdims=True) + EPS)
    return y * gain


def _rms_halves(x, gain):
    lane = lax.broadcasted_iota(jnp.int32, x.shape, 1)
    low = lane < 64
    sq = x * x
    s_low = jnp.sum(jnp.where(low, sq, 0.0), axis=-1, keepdims=True)
    s_all = jnp.sum(sq, axis=-1, keepdims=True)
    ms = jnp.where(low, s_low, s_all - s_low) * (1.0 / 64.0)
    return x * lax.rsqrt(ms + EPS) * gain


def _rope128(x, c, s):
    return x * c + pltpu.roll(x, 64, 1) * s


def _rope64x2(x, c, s):
    lane = lax.broadcasted_iota(jnp.int32, x.shape, 1)
    low = (lane % 64) < 32
    partner = jnp.where(low, pltpu.roll(x, 96, 1), pltpu.roll(x, 32, 1))
    return x * c + partner * s


def _is_ctx_block():
    return pl.program_id(0) < N_CTX // TM


def _norm_mod(x, g_ref, mod_ref, shift_row):
    y = _rms(x, g_ref[...])
    shift = mod_ref[shift_row:shift_row + 1, :]
    scale = mod_ref[shift_row + 1:shift_row + 2, :]
    return (y * (1.0 + scale) + shift).astype(BF16)


def _mod_kernel(cond_ref, w_ref, b_ref, o_ref):
    s = jax.nn.silu(cond_ref[...]).astype(BF16)
    w = w_ref[...].astype(BF16)
    o_ref[...] = jnp.dot(s, w, preferred_element_type=F32) + b_ref[...]


def _modulation(cond, w_mod, b_mod):
    tn = 1536
    return pl.pallas_call(
        _mod_kernel,
        grid=(DEPTH, 6 * D // tn),
        in_specs=[
            pl.BlockSpec((SUBLANES, D), lambda l, j: (0, 0)),
            pl.BlockSpec((None, D, tn), lambda l, j: (l, 0, j)),
            pl.BlockSpec((None, 1, tn), lambda l, j: (l, 0, j)),
        ],
        out_specs=pl.BlockSpec((None, SUBLANES, tn), lambda l, j: (l, 0, j)),
        out_shape=jax.ShapeDtypeStruct((DEPTH, SUBLANES, 6 * D), F32),
        compiler_params=_cparams(("arbitrary", "arbitrary")),
        name="adaln_mod",
    )(cond, w_mod, b_mod.reshape(DEPTH, 1, 6 * D))


def _row_spec(width):
    return pl.BlockSpec((TM, width), lambda i: (i, 0))


def _ctx_row_spec(width):
    return pl.BlockSpec((TM, width), lambda i: (jnp.minimum(i, N_CTX // TM - 1), 0))


def _lat_row_spec(width):
    return pl.BlockSpec((TM, width), lambda i: (jnp.maximum(i - N_CTX // TM, 0), 0))


def _full_spec(shape):
    return pl.BlockSpec(shape, lambda i: (0,) * len(shape))


def _mod_spec():
    return pl.BlockSpec((None, 6, D), lambda i: (_group_of_block(i, TM), 0, 0))


def _heads_spec(heads, width):
    return pl.BlockSpec((heads, TM, width), lambda i: (0, i, 0))


def _gqa_in_kernel(x_ref, mod_ref, g_ref, w_ref, gqk_ref, cos_ref, sin_ref,
                   q_ref, k_ref, v_ref, ks_ref, vs_ref):
    h = _norm_mod(x_ref[...], g_ref, mod_ref, 0)
    proj = jnp.dot(h, w_ref[...], preferred_element_type=F32)
    c = cos_ref[...]
    s = sin_ref[...]
    scale = LOG2E * GQA_HD ** -0.5
    n_q = GQA_Q_HEADS * GQA_HD
    n_kv = GQA_KV_HEADS * GQA_HD
    for hd in range(GQA_Q_HEADS):
        q = _rms(proj[:, hd * GQA_HD:(hd + 1) * GQA_HD], gqk_ref[0:1, :])
        q_ref[hd] = (_rope128(q, c, s) * scale).astype(BF16)
    ks = []
    for hd in range(GQA_KV_HEADS):
        lo = n_q + hd * GQA_HD
        ks.append(_rms(proj[:, lo:lo + GQA_HD], gqk_ref[1:2, :]))
        k_ref[hd] = _rope128(ks[-1], c, s).astype(BF16)
        lo = n_q + n_kv + hd * GQA_HD
        v_ref[hd] = proj[:, lo:lo + GQA_HD].astype(BF16)

    @pl.when(_is_ctx_block())
    def _():
        for hd in range(GQA_KV_HEADS):
            ks_ref[:, hd * GQA_HD:(hd + 1) * GQA_HD] = ks[hd]
        vs_ref[...] = proj[:, n_q + n_kv:]


def _gqa_in(x, mod, g1, w_in, g_qk, cos, sin):
    n_kv = GQA_KV_HEADS * GQA_HD
    return pl.pallas_call(
        _gqa_in_kernel,
        grid=(T // TM,),
        in_specs=[_row_spec(D), _mod_spec(), _full_spec((1, D)), _full_spec(w_in.shape),
                  _full_spec((2, GQA_HD)), _row_spec(LANES), _row_spec(LANES)],
        out_specs=[_heads_spec(GQA_Q_HEADS, GQA_HD), _heads_spec(GQA_KV_HEADS, GQA_HD),
                   _heads_spec(GQA_KV_HEADS, GQA_HD), _ctx_row_spec(n_kv), _ctx_row_spec(n_kv)],
        out_shape=[jax.ShapeDtypeStruct((GQA_Q_HEADS, T, GQA_HD), BF16),
                   jax.ShapeDtypeStruct((GQA_KV_HEADS, T, GQA_HD), BF16),
                   jax.ShapeDtypeStruct((GQA_KV_HEADS, T, GQA_HD), BF16),
                   jax.ShapeDtypeStruct((N_CTX, n_kv), F32),
                   jax.ShapeDtypeStruct((N_CTX, n_kv), F32)],
        compiler_params=_cparams(("arbitrary",)),
        name="gqa_in",
    )(x, mod, g1, w_in, g_qk, cos, sin)


def _diff_in_kernel(x_ref, mod_ref, g_ref, w_ref, gqk_ref, cos_ref, sin_ref,
                    q_ref, k_ref, v_ref, ks_ref, vs_ref):
    h = _norm_mod(x_ref[...], g_ref, mod_ref, 0)
    proj = jnp.dot(h, w_ref[...], preferred_element_type=F32)
    c = cos_ref[...]
    s = sin_ref[...]
    scale = LOG2E * DIFF_HD ** -0.5
    lane = lax.broadcasted_iota(jnp.int32, (TM, LANES), 1)
    low = lane < DIFF_HD
    n = DIFF_HEADS * 2 * DIFF_HD
    ks = []
    for hd in range(DIFF_HEADS):
        sl = slice(hd * LANES, (hd + 1) * LANES)
        q = _rope64x2(_rms_halves(proj[:, sl], gqk_ref[0:1, :]), c, s) * scale
        q_ref[2 * hd] = jnp.where(low, q, 0.0).astype(BF16)
        q_ref[2 * hd + 1] = jnp.where(low, 0.0, q).astype(BF16)
        ks.append(_rms_halves(proj[:, n + hd * LANES:n + (hd + 1) * LANES], gqk_ref[1:2, :]))
        k_ref[hd] = _rope64x2(ks[-1], c, s).astype(BF16)
        v_ref[hd] = proj[:, 2 * n + hd * LANES:2 * n + (hd + 1) * LANES].astype(BF16)

    @pl.when(_is_ctx_block())
    def _():
        for hd in range(DIFF_HEADS):
            ks_ref[:, hd * LANES:(hd + 1) * LANES] = ks[hd]
        vs_ref[...] = proj[:, 2 * n:]


def _diff_in(x, mod, g1, w_in, g_qk2, cos, sin):
    n = DIFF_HEADS * 2 * DIFF_HD
    return pl.pallas_call(
        _diff_in_kernel,
        grid=(T // TM,),
        in_specs=[_row_spec(D), _mod_spec(), _full_spec((1, D)), _full_spec(w_in.shape),
                  _full_spec((2, LANES)), _row_spec(LANES), _row_spec(LANES)],
        out_specs=[_heads_spec(2 * DIFF_HEADS, LANES), _heads_spec(DIFF_HEADS, LANES),
                   _heads_spec(DIFF_HEADS, LANES), _ctx_row_spec(n), _ctx_row_spec(n)],
        out_shape=[jax.ShapeDtypeStruct((2 * DIFF_HEADS, T, LANES), BF16),
                   jax.ShapeDtypeStruct((DIFF_HEADS, T, LANES), BF16),
                   jax.ShapeDtypeStruct((DIFF_HEADS, T, LANES), BF16),
                   jax.ShapeDtypeStruct((N_CTX, n), F32),
                   jax.ShapeDtypeStruct((N_CTX, n), F32)],
        compiler_params=_cparams(("arbitrary",)),
        name="diff_in",
    )(x, mod, g1, w_in, g_qk2, cos, sin)


def _mla_keys_values(ckv_bf16, kr2, w_ukv_ref, gk_ref, k_ref, v_ref):
    kv = jnp.dot(ckv_bf16, w_ukv_ref[...], preferred_element_type=F32)
    lane = lax.broadcasted_iota(jnp.int32, kr2.shape, 1)
    low = lane < MLA_ROPE
    kr_low = jnp.where(low, kr2, 0.0).astype(BF16)
    kr_high = jnp.where(low, 0.0, kr2).astype(BF16)
    width = MLA_NOPE + MLA_V
    for hd in range(MLA_HEADS):
        k_nope = _rms(kv[:, hd * width:hd * width + MLA_NOPE], gk_ref[...])
        k_ref[hd, :, 0:MLA_NOPE] = k_nope.astype(BF16)
        k_ref[hd, :, MLA_NOPE:MLA_DQ] = kr_low if hd % 2 == 0 else kr_high
        v_ref[hd] = kv[:, hd * width + MLA_NOPE:(hd + 1) * width].astype(BF16)


def _mla_in_kernel(xc_ref, xl_ref, mod_ref, g_ref, w_ref, gcq_ref, gckv_ref, gqn_ref, gqr_ref,
                   gkn_ref, gkr_ref, wqn_ref, wqr_ref, wukv_ref, cos_ref, sin_ref,
                   q_ref, k_ref, v_ref, ckv_ref, kr_ref):
    x = jnp.where(_is_ctx_block(), xc_ref[...], xl_ref[...])
    h = _norm_mod(x, g_ref, mod_ref, 0)
    proj = jnp.dot(h, w_ref[...], preferred_element_type=F32)
    c = cos_ref[...]
    s = sin_ref[...]
    cq = _rms(proj[:, 0:MLA_Q_RANK], gcq_ref[...]).astype(BF16)
    ckv = _rms(proj[:, MLA_Q_RANK:MLA_Q_RANK + MLA_KV_RANK], gckv_ref[...])
    kr2 = _rms_halves(proj[:, MLA_Q_RANK + MLA_KV_RANK:], gkr_ref[...])

    @pl.when(_is_ctx_block())
    def _():
        ckv_ref[...] = ckv
        kr_ref[...] = kr2[:, 0:MLA_ROPE]

    _mla_keys_values(ckv.astype(BF16), _rope64x2(kr2, c, s), wukv_ref, gkn_ref, k_ref, v_ref)

    scale = LOG2E * (MLA_NOPE + MLA_ROPE) ** -0.5
    qn = jnp.dot(cq, wqn_ref[...], preferred_element_type=F32)
    qr = jnp.dot(cq, wqr_ref[...], preferred_element_type=F32)
    lane = lax.broadcasted_iota(jnp.int32, (TM, LANES), 1)
    low = lane < MLA_ROPE
    for pair in range(MLA_HEADS // 2):
        r = _rope64x2(_rms_halves(qr[:, pair * LANES:(pair + 1) * LANES], gqr_ref[...]), c, s) * scale
        q_ref[2 * pair, :, MLA_NOPE:MLA_DQ] = jnp.where(low, r, 0.0).astype(BF16)
        q_ref[2 * pair + 1, :, MLA_NOPE:MLA_DQ] = jnp.where(low, 0.0, r).astype(BF16)
    for hd in range(MLA_HEADS):
        q = _rms(qn[:, hd * MLA_NOPE:(hd + 1) * MLA_NOPE], gqn_ref[...]) * scale
        q_ref[hd, :, 0:MLA_NOPE] = q.astype(BF16)


def _mla_in(x_ctx, x_lat, mod, g1, w_in, g_cq, g_ckv, gqn, gqr, gkn, gkr, wqn, wqr, wukv, cos, sin):
    return pl.pallas_call(
        _mla_in_kernel,
        grid=(T // TM,),
        in_specs=[_ctx_row_spec(D), _lat_row_spec(D), _mod_spec(), _full_spec((1, D)),
                  _full_spec(w_in.shape),
                  _full_spec((1, MLA_Q_RANK)), _full_spec((1, MLA_KV_RANK)),
                  _full_spec((1, LANES)), _full_spec((1, LANES)), _full_spec((1, LANES)),
                  _full_spec((1, LANES)), _full_spec(wqn.shape), _full_spec(wqr.shape),
                  _full_spec(wukv.shape), _row_spec(LANES), _row_spec(LANES)],
        out_specs=[_heads_spec(MLA_HEADS, MLA_DQ), _heads_spec(MLA_HEADS, MLA_DQ),
                   _heads_spec(MLA_HEADS, MLA_V), _ctx_row_spec(MLA_KV_RANK),
                   _ctx_row_spec(MLA_ROPE)],
        out_shape=[jax.ShapeDtypeStruct((MLA_HEADS, T, MLA_DQ), BF16),
                   jax.ShapeDtypeStruct((MLA_HEADS, T, MLA_DQ), BF16),
                   jax.ShapeDtypeStruct((MLA_HEADS, T, MLA_V), BF16),
                   jax.ShapeDtypeStruct((N_CTX, MLA_KV_RANK), F32),
                   jax.ShapeDtypeStruct((N_CTX, MLA_ROPE), F32)],
        compiler_params=_cparams(("arbitrary",)),
        name="mla_in",
    )(x_ctx, x_lat, mod, g1, w_in, g_cq, g_ckv, gqn, gqr, gkn, gkr, wqn, wqr, wukv, cos, sin)


def _mla_cache_kernel(ckv_ref, kr_ref, wukv_ref, gkn_ref, k_ref, v_ref):
    kr = kr_ref[...]
    kr2 = jnp.concatenate([kr, kr], axis=-1)
    _mla_keys_values(ckv_ref[...].astype(BF16), kr2, wukv_ref, gkn_ref, k_ref, v_ref)


def _mla_cache(ckv, kr, wukv, gkn):
    rows = ckv.shape[0]
    return pl.pallas_call(
        _mla_cache_kernel,
        grid=(1,),
        in_specs=[_full_spec(ckv.shape), _full_spec(kr.shape), _full_spec(wukv.shape),
                  _full_spec((1, LANES))],
        out_specs=[_full_spec((MLA_HEADS, rows, MLA_DQ)), _full_spec((MLA_HEADS, rows, MLA_V))],
        out_shape=[jax.ShapeDtypeStruct((MLA_HEADS, rows, MLA_DQ), BF16),
                   jax.ShapeDtypeStruct((MLA_HEADS, rows, MLA_V), BF16)],
        compiler_params=_cparams(("arbitrary",)),
        name="mla_cache_kv",
    )(ckv, kr, wukv, gkn)


def _lru_in_kernel(x_ref, mod_ref, g_ref, w_ref, gate_ref, xr_ref):
    h = _norm_mod(x_ref[...], g_ref, mod_ref, 0)
    proj = jnp.dot(h, w_ref[...], preferred_element_type=F32)
    gate_ref[...] = proj[:, 0:D]
    xr_ref[...] = proj[:, D:2 * D]


def _lru_in(x, mod, g1, w_in):
    return pl.pallas_call(
        _lru_in_kernel,
        grid=(T // TM,),
        in_specs=[_row_spec(D), _mod_spec(), _full_spec((1, D)), _full_spec(w_in.shape)],
        out_specs=[_row_spec(D), _row_spec(D)],
        out_shape=[jax.ShapeDtypeStruct((T, D), F32), jax.ShapeDtypeStruct((T, D), F32)],
        compiler_params=_cparams(("arbitrary",)),
        name="lru_in",
    )(x, mod, g1, w_in)


def _softmax_pv(q, key_blocks):
    nt = (((1,), (1,)), ((), ()))
    m = l = acc = None
    for k, v in key_blocks:
        s = lax.dot_general(q, k, nt, preferred_element_type=F32)
        mb = jnp.max(s, axis=-1, keepdims=True)
        if m is None:
            m = mb
            p = jnp.exp2(s - m)
            l = jnp.sum(p, axis=-1, keepdims=True)
            acc = jnp.dot(p.astype(BF16), v, preferred_element_type=F32)
        else:
            m_new = jnp.maximum(m, mb)
            alpha = jnp.exp2(m - m_new)
            p = jnp.exp2(s - m_new)
            l = alpha * l + jnp.sum(p, axis=-1, keepdims=True)
            acc = alpha * acc + jnp.dot(p.astype(BF16), v, preferred_element_type=F32)
            m = m_new
    return acc / l


def _attn_kernel(*refs, heads, group, sub, bk, has_cache, diff, aliased, dv, lambda_init):
    it = iter(refs)
    q_ref, kn_ref, vn_ref = next(it), next(it), next(it)
    kc_ref = vc_ref = lam_ref = gsub_ref = None
    if has_cache:
        kc_ref, vc_ref = next(it), next(it)
    if diff:
        lam_ref, gsub_ref = next(it), next(it)
    if aliased:
        next(it)
    o_ref = next(it)

    tq = q_ref.shape[1]
    seq_len = kn_ref.shape[1]
    stacked = min(group, heads)
    cached = []
    if has_cache:
        cached = [(kc_ref[...].astype(BF16), vc_ref[...].astype(BF16))]
    if diff:
        lam = lam_ref[...]
        lam_full = (jnp.exp(jnp.sum(lam[0:1] * lam[1:2], axis=-1, keepdims=True))
                    - jnp.exp(jnp.sum(lam[2:3] * lam[3:4], axis=-1, keepdims=True)) + lambda_init)

    for kv in range(max(heads // group, 1)):
        def body(i, carry, kv=kv):
            rows = pl.ds(pl.multiple_of(i * sub, sub), sub)
            q = jnp.concatenate([q_ref[kv * stacked + g, rows, :] for g in range(stacked)], axis=0)
            blocks = cached + [(kn_ref[kv, j * bk:(j + 1) * bk, :], vn_ref[kv, j * bk:(j + 1) * bk, :])
                               for j in range(seq_len // bk)]
            o = _softmax_pv(q, blocks)
            if diff:
                od = o[0:sub] - lam_full * o[sub:2 * sub]
                od = _rms(od, gsub_ref[...]) * (1.0 - lambda_init)
                o_ref[rows, kv * dv:(kv + 1) * dv] = od.astype(BF16)
            else:
                for g in range(stacked):
                    hd = kv * stacked + g
                    o_ref[rows, hd * dv:(hd + 1) * dv] = o[g * sub:(g + 1) * sub].astype(BF16)
            return carry
        lax.fori_loop(0, tq // sub, body, 0)


def _attention(q, kn, vn, *, seq_len, n_seq, row0, tq, sub, bk, heads_per_step, group, cache=None,
               diff=None, lambda_init=0.0, out=None):
    hq, _, dq = q.shape
    hkv, _, dv = vn.shape
    kv_per_step = max(heads_per_step // group, 1)
    n_hblk = hq // heads_per_step
    n_qblk = seq_len // tq
    seq0 = row0 // seq_len
    qblk0 = row0 // tq

    def q_map(b, hb, qb):
        return (hb, qblk0 + b * n_qblk + qb, 0)

    def kv_map(b, hb, qb):
        return ((hb * heads_per_step) // (group * kv_per_step), seq0 + b, 0)

    in_specs = [pl.BlockSpec((heads_per_step, tq, dq), q_map),
                pl.BlockSpec((kv_per_step, seq_len, dq), kv_map),
                pl.BlockSpec((kv_per_step, seq_len, dv), kv_map)]
    args = [q, kn, vn]
    if cache is not None:
        kc, vc, kc_block, kc_map, vc_block, vc_map = cache
        in_specs += [pl.BlockSpec(kc_block, kc_map), pl.BlockSpec(vc_block, vc_map)]
        args += [kc, vc]
    out_heads = heads_per_step
    if diff is not None:
        lam, gsub = diff
        in_specs += [pl.BlockSpec(lam.shape, lambda b, hb, qb: (0, 0)),
                     pl.BlockSpec(gsub.shape, lambda b, hb, qb: (0, 0))]
        args += [lam, gsub]
        out_heads = heads_per_step // 2
    n_out = (hq // 2 if diff is not None else hq) * dv
    aliases = {}
    if out is not None:
        in_specs += [pl.BlockSpec(memory_space=pl.ANY)]
        args += [out]
        aliases = {len(args) - 1: 0}
    kernel = functools.partial(_attn_kernel, heads=heads_per_step, group=group, sub=sub, bk=bk,
                               has_cache=cache is not None, diff=diff is not None,
                               aliased=out is not None, dv=dv, lambda_init=lambda_init)
    return pl.pallas_call(
        kernel,
        grid=(n_seq, n_hblk, n_qblk),
        in_specs=in_specs,
        out_specs=pl.BlockSpec((tq, out_heads * dv),
                               lambda b, hb, qb: (qblk0 + b * n_qblk + qb, hb)),
        out_shape=jax.ShapeDtypeStruct((T, n_out), BF16),
        input_output_aliases=aliases,
        compiler_params=_cparams(("arbitrary", "arbitrary", "arbitrary")),
        name="attention",
    )(*args)


def _lru_seq_pos(i):
    n_ctx_blocks = N_CTX // LRU_TB
    per_ctx = SEQ // LRU_TB
    per_lat = DEC_SEQ // LRU_TB
    pos = jnp.where(i < n_ctx_blocks, i % per_ctx, (i - n_ctx_blocks) % per_lat)
    length = jnp.where(i < n_ctx_blocks, per_ctx, per_lat)
    return pos == 0, pos == length - 1


def _lru_seq_of_block(i):
    n_ctx_blocks = N_CTX // LRU_TB
    return jnp.where(i < n_ctx_blocks, i // (SEQ // LRU_TB),
                     BATCH + (i - n_ctx_blocks) // (DEC_SEQ // LRU_TB))


def _tile_scan(a, u, reverse):
    row = lax.broadcasted_iota(jnp.int32, a.shape, 0)
    for k in (1, 2, 4):
        if reverse:
            valid = row < SUBLANES - k
            shift = SUBLANES - k
        else:
            valid = row >= k
            shift = k
        a_s = jnp.where(valid, pltpu.roll(a, shift, 0), 1.0)
        u_s = jnp.where(valid, pltpu.roll(u, shift, 0), 0.0)
        u = a * u_s + u
        a = a * a_s
    return a, u


def _lru_scan_kernel(x_ref, prev_ref, next_ref, cw_ref, cb_ref, wg_ref, bg_ref, lam_ref, h0_ref,
                     *rest, reverse, combine):
    if combine:
        hf_ref, gate_ref, y_ref, st_ref, xpad, a_s, u_s, carry = rest
    else:
        y_ref, st_ref, xpad, a_s, u_s, carry = rest
    j = pl.program_id(0)
    i = (T // LRU_TB - 1 - j) if reverse else j
    first, last = _lru_seq_pos(i)
    starts = last if reverse else first
    ends = first if reverse else last

    xpad[0:SUBLANES, :] = jnp.where(first, 0.0, prev_ref[...])
    xpad[SUBLANES:SUBLANES + LRU_TB, :] = x_ref[...]
    xpad[SUBLANES + LRU_TB:, :] = jnp.where(last, 0.0, next_ref[...])
    xr = xpad[SUBLANES - 1:SUBLANES - 1 + LRU_TB, :] * cw_ref[0:1, :]
    for t in range(1, 4):
        xr = xr + xpad[SUBLANES - 1 + t:SUBLANES - 1 + t + LRU_TB, :] * cw_ref[t:t + 1, :]
    xr = xr + cb_ref[...]

    xr_b = xr.astype(BF16)
    lam = lam_ref[...]
    neg = -lam
    softplus = jnp.maximum(neg, 0.0) + jnp.log1p(jnp.exp(-jnp.abs(neg)))
    for n in range(LRU_BLOCKS):
        sl = slice(n * LRU_BLK, (n + 1) * LRU_BLK)
        g = jnp.dot(xr_b[:, sl], wg_ref[n], preferred_element_type=F32)
        r = jax.nn.sigmoid(g[:, 0:LRU_BLK] + bg_ref[0:1, sl])
        gi = jax.nn.sigmoid(g[:, LRU_BLK:] + bg_ref[1:2, sl])
        log_a = -LRU_C * r * softplus[:, sl]
        th = jnp.tanh(log_a)
        one_minus_a2 = -2.0 * th / (1.0 - th)
        a_s[:, sl] = jnp.exp(log_a)
        u_s[:, sl] = jnp.sqrt(one_minus_a2) * (gi * xr[:, sl])

    @pl.when(starts)
    def _():
        carry[...] = h0_ref[...]

    n_tiles = LRU_TB // SUBLANES

    def body(t, h):
        tt = (n_tiles - 1 - t) if reverse else t
        rows = pl.ds(pl.multiple_of(tt * SUBLANES, SUBLANES), SUBLANES)
        a_c, u_c = _tile_scan(a_s[rows, :], u_s[rows, :], reverse)
        hs = a_c * h + u_c
        if combine:
            gate = gate_ref[rows, :]
            y_ref[rows, :] = ((hf_ref[rows, :] + hs) * jax.nn.gelu(gate)).astype(BF16)
        else:
            y_ref[rows, :] = hs
        return hs[0:1, :] if reverse else hs[SUBLANES - 1:SUBLANES, :]

    h_end = lax.fori_loop(0, n_tiles, body, carry[...])
    carry[...] = h_end

    @pl.when(ends)
    def _():
        st_ref[...] = h_end


def _lru_scan(xr, conv_w, conv_b, w_gate, b_gate, lam, h0, *, reverse, hf=None, gate=None):
    nb = T // LRU_TB
    hb = LRU_TB // SUBLANES
    n_halo = T // SUBLANES

    def blk(j):
        return (nb - 1 - j) if reverse else j

    in_specs = [
        pl.BlockSpec((LRU_TB, D), lambda j: (blk(j), 0)),
        pl.BlockSpec((SUBLANES, D), lambda j: (jnp.maximum(blk(j) * hb - 1, 0), 0)),
        pl.BlockSpec((SUBLANES, D), lambda j: (jnp.minimum((blk(j) + 1) * hb, n_halo - 1), 0)),
        pl.BlockSpec((4, D), lambda j: (0, 0)),
        pl.BlockSpec((1, D), lambda j: (0, 0)),
        pl.BlockSpec((LRU_BLOCKS, LRU_BLK, 2 * LRU_BLK), lambda j: (0, 0, 0)),
        pl.BlockSpec((2, D), lambda j: (0, 0)),
        pl.BlockSpec((1, D), lambda j: (0, 0)),
        pl.BlockSpec((None, 1, D), lambda j: (_lru_seq_of_block(blk(j)), 0, 0)),
    ]
    args = [xr, xr, xr, conv_w, conv_b, w_gate, b_gate, lam, h0]
    combine = hf is not None
    if combine:
        in_specs += [pl.BlockSpec((LRU_TB, D), lambda j: (blk(j), 0)),
                     pl.BlockSpec((LRU_TB, D), lambda j: (blk(j), 0))]
        args += [hf, gate]
    n_seq = BATCH + DEC_BATCH
    return pl.pallas_call(
        functools.partial(_lru_scan_kernel, reverse=reverse, combine=combine),
        grid=(nb,),
        in_specs=in_specs,
        out_specs=[pl.BlockSpec((LRU_TB, D), lambda j: (blk(j), 0)),
                   pl.BlockSpec((None, 1, D), lambda j: (_lru_seq_of_block(blk(j)), 0, 0))],
        out_shape=[jax.ShapeDtypeStruct((T, D), BF16 if combine else F32),
                   jax.ShapeDtypeStruct((n_seq, 1, D), F32)],
        scratch_shapes=[pltpu.VMEM((LRU_TB + 2 * SUBLANES, D), F32),
                        pltpu.VMEM((LRU_TB, D), F32),
                        pltpu.VMEM((LRU_TB, D), F32),
                        pltpu.VMEM((1, D), F32)],
        compiler_params=_cparams(("arbitrary",)),
        name="lru_scan_bwd" if reverse else "lru_scan_fwd",
    )(*args)


def _post_kernel(*refs, split_in, split_out):
    it = iter(refs)
    if split_in:
        x = jnp.where(_is_ctx_block(), next(it)[...], next(it)[...])
    else:
        x = next(it)[...]
    o_ref, mod_ref, g2_ref, wo_ref, win_ref, wout_ref = (next(it) for _ in range(6))
    y_refs = list(it)
    x1 = x + mod_ref[2:3, :] * jnp.dot(o_ref[...], wo_ref[...], preferred_element_type=F32)
    h = _rms(x1, g2_ref[...])
    h = (h * (1.0 + mod_ref[4:5, :]) + mod_ref[3:4, :]).astype(BF16)
    acc = jnp.zeros((TM, D), F32)
    for c in range(FFN_H // FFN_CHUNK):
        lo = c * FFN_CHUNK
        g = jnp.dot(h, win_ref[:, lo:lo + FFN_CHUNK], preferred_element_type=F32)
        u = jnp.dot(h, win_ref[:, FFN_H + lo:FFN_H + lo + FFN_CHUNK], preferred_element_type=F32)
        a = (jax.nn.silu(g) * u).astype(BF16)
        acc = acc + jnp.dot(a, wout_ref[lo:lo + FFN_CHUNK, :], preferred_element_type=F32)
    y = x1 + mod_ref[5:6, :] * acc
    if split_out:
        @pl.when(_is_ctx_block())
        def _():
            y_refs[0][...] = y

        @pl.when(jnp.logical_not(_is_ctx_block()))
        def _():
            y_refs[1][...] = y
    else:
        y_refs[0][...] = y


def _layer_weight_spec(shape, layer):
    return pl.BlockSpec((None,) + shape[1:], lambda i: (layer,) + (0,) * (len(shape) - 1),
                        pipeline_mode=pl.Buffered(1))


def _post(xs, o, mod, g2, w_o, w_in, w_out, layer, *, split_out):
    split_in = len(xs) == 2
    x_specs = [_ctx_row_spec(D), _lat_row_spec(D)] if split_in else [_row_spec(D)]
    if split_out:
        out_specs = [_ctx_row_spec(D), _lat_row_spec(D)]
        out_shape = [jax.ShapeDtypeStruct((N_CTX, D), F32), jax.ShapeDtypeStruct((N_LAT, D), F32)]
    else:
        out_specs = [_row_spec(D)]
        out_shape = [jax.ShapeDtypeStruct((T, D), F32)]
    return pl.pallas_call(
        functools.partial(_post_kernel, split_in=split_in, split_out=split_out),
        grid=(T // TM,),
        in_specs=x_specs + [_row_spec(D), _mod_spec(), _full_spec((1, D)),
                            _layer_weight_spec((1,) + w_o.shape, 0),
                            _layer_weight_spec(w_in.shape, layer),
                            _layer_weight_spec(w_out.shape, layer)],
        out_specs=out_specs,
        out_shape=out_shape,
        compiler_params=_cparams(("arbitrary",)),
        name="post_mixer_ffn",
    )(*xs, o, mod, g2, w_o[None], w_in, w_out)


def _axial_tables(rot_dim):
    row = jnp.repeat(jnp.arange(DEC_SEQ // GRID_W), GRID_W).astype(F32)
    col = jnp.tile(jnp.arange(GRID_W), DEC_SEQ // GRID_W).astype(F32)
    n_freq = rot_dim // 4
    inv = ROPE_THETA ** (-jnp.arange(n_freq, dtype=F32) / n_freq)
    ang = jnp.concatenate([row[:, None] * inv, col[:, None] * inv], axis=-1)
    cos, sin = jnp.cos(ang), jnp.sin(ang)
    reps = LANES // rot_dim
    cos_t = jnp.tile(jnp.concatenate([cos, cos], axis=-1), (DEC_BATCH, reps))
    sin_t = jnp.tile(jnp.concatenate([-sin, sin], axis=-1), (DEC_BATCH, reps))
    cos_t = jnp.concatenate([jnp.ones((N_CTX, LANES), F32), cos_t], axis=0)
    sin_t = jnp.concatenate([jnp.zeros((N_CTX, LANES), F32), sin_t], axis=0)
    return cos_t, sin_t


def kernel(x_prompt, x_sample, cache_mla_ckv, cache_mla_krope, cache_diff_k, cache_diff_v, cache_gqa_k, cache_gqa_v, state_lru_h, c, c_ctx, w_mod, b_mod, g_norm1, g_norm2, w_ffn_in, w_ffn_out, mla_w_in, mla_g_cq, mla_g_ckv, mla_w_uq, mla_w_ukv, mla_g_qk, mla_w_o, diff_w_in, diff_g_qk, diff_lambda, diff_g_sub, diff_w_o, gqa_w_in, gqa_g_qk, gqa_w_o, lru_w_in, lru_conv_w, lru_conv_b, lru_w_gate, lru_b_gate, lru_lambda, lru_w_out):
    x_in = (x_prompt.reshape(N_CTX, D), x_sample.reshape(N_LAT, D))
    cond = jnp.concatenate([c_ctx[None, :], c, jnp.zeros((SUBLANES - N_GROUPS, D), F32)], axis=0)
    mod_all = _modulation(cond, w_mod, b_mod)
    cos128, sin128 = _axial_tables(GQA_HD)
    cos64, sin64 = _axial_tables(DIFF_HD)
    w_ffn_in_b = w_ffn_in.astype(BF16)
    w_ffn_out_b = w_ffn_out.astype(BF16)

    def layer_mod(l):
        return mod_all[l, :N_GROUPS].reshape(N_GROUPS, 6, D)

    def post(l, xs, o, w_o):
        out = _post(xs, o, layer_mod(l), g_norm2[l][None, :], w_o.astype(BF16),
                    w_ffn_in_b, w_ffn_out_b, l, split_out=l == DEPTH - 1)
        return out if l == DEPTH - 1 else out[0]

    def attend(q, k, v, *, hq, group, cache, diff, lambda_init):
        common = dict(group=group, diff=diff, lambda_init=lambda_init)
        sub = ATTN_ROWS // group
        o_ctx = _attention(q, k, v, seq_len=SEQ, n_seq=BATCH, row0=0, tq=SEQ, bk=SEQ,
                           sub=min(sub, SEQ), heads_per_step=hq, cache=None, **common)
        return _attention(q, k, v, seq_len=DEC_SEQ, n_seq=DEC_BATCH, row0=N_CTX, tq=ATTN_TQ,
                          bk=ATTN_BK, sub=sub, heads_per_step=group, cache=cache, out=o_ctx,
                          **common)

    l = 0
    w_in = mla_w_in[0]
    kr_cols = w_in[:, MLA_Q_RANK + MLA_KV_RANK:]
    w_in2 = jnp.concatenate([w_in, kr_cols], axis=1).astype(BF16)
    w_uq = mla_w_uq[0].reshape(MLA_Q_RANK, MLA_HEADS, MLA_NOPE + MLA_ROPE)
    wqn = w_uq[:, :, :MLA_NOPE].reshape(MLA_Q_RANK, MLA_HEADS * MLA_NOPE).astype(BF16)
    wqr = w_uq[:, :, MLA_NOPE:].reshape(MLA_Q_RANK, MLA_HEADS * MLA_ROPE).astype(BF16)
    wukv = mla_w_ukv[0].astype(BF16)
    gqk = mla_g_qk[0]
    gqn = gqk[0:1, :MLA_NOPE]
    gqr = jnp.tile(gqk[0:1, MLA_NOPE:], (1, 2))
    gkn = gqk[1:2, :MLA_NOPE]
    gkr = jnp.tile(gqk[1:2, MLA_NOPE:], (1, 2))
    q, k, v, ckv_new, kr_new = _mla_in(*x_in, layer_mod(l), g_norm1[l][None, :], w_in2,
                                       mla_g_cq[0][None, :], mla_g_ckv[0][None, :],
                                       gqn, gqr, gkn, gkr, wqn, wqr, wukv, cos64, sin64)
    kc, vc = _mla_cache(cache_mla_ckv[:, 0].reshape(DEC_BATCH * PAST, MLA_KV_RANK),
                        cache_mla_krope[:, 0].reshape(DEC_BATCH * PAST, MLA_ROPE), wukv, gkn)
    cache = (kc, vc,
             (None, PAST, MLA_DQ), lambda b, hb, qb: (hb, b, 0),
             (None, PAST, MLA_V), lambda b, hb, qb: (hb, b, 0))
    o = attend(q, k, v, hq=MLA_HEADS, group=1, cache=cache, diff=None, lambda_init=0.0)
    x = post(l, x_in, o, mla_w_o[0])
    new_mla_ckv = ckv_new.reshape(BATCH, 1, SEQ, MLA_KV_RANK)
    new_mla_krope = kr_new.reshape(BATCH, 1, SEQ, MLA_ROPE)

    l = 1
    lambda_init = 0.8 - 0.6 * math.exp(-0.3 * l)
    q, k, v, k_new, v_new = _diff_in(x, layer_mod(l), g_norm1[l][None, :], diff_w_in[0].astype(BF16),
                                     jnp.tile(diff_g_qk[0], (1, 2)), cos64, sin64)
    n_diff = DIFF_HEADS * 2 * DIFF_HD
    cache = (cache_diff_k[:, 0].reshape(DEC_BATCH, PAST, n_diff),
             cache_diff_v[:, 0].reshape(DEC_BATCH, PAST, n_diff),
             (None, PAST, LANES), lambda b, hb, qb: (b, 0, hb),
             (None, PAST, LANES), lambda b, hb, qb: (b, 0, hb))
    o = attend(q, k, v, hq=2 * DIFF_HEADS, group=2, cache=cache,
               diff=(diff_lambda[0], diff_g_sub[0][None, :]), lambda_init=lambda_init)
    x = post(l, (x,), o, diff_w_o[0])
    new_diff_k = k_new.reshape(BATCH, 1, SEQ, DIFF_HEADS, 2, DIFF_HD)
    new_diff_v = v_new.reshape(BATCH, 1, SEQ, DIFF_HEADS, 2 * DIFF_HD)

    l = 2
    q, k, v, k_new, v_new = _gqa_in(x, layer_mod(l), g_norm1[l][None, :], gqa_w_in[0].astype(BF16),
                                    gqa_g_qk[0], cos128, sin128)
    n_kv = GQA_KV_HEADS * GQA_HD
    group = GQA_Q_HEADS // GQA_KV_HEADS
    cache = (cache_gqa_k[:, 0].reshape(DEC_BATCH, PAST, n_kv),
             cache_gqa_v[:, 0].reshape(DEC_BATCH, PAST, n_kv),
             (None, PAST, GQA_HD), lambda b, hb, qb: (b, 0, hb),
             (None, PAST, GQA_HD), lambda b, hb, qb: (b, 0, hb))
    o = attend(q, k, v, hq=GQA_Q_HEADS, group=group, cache=cache, diff=None, lambda_init=0.0)
    x = post(l, (x,), o, gqa_w_o[0])
    new_gqa_k = k_new.reshape(BATCH, 1, SEQ, GQA_KV_HEADS, GQA_HD)
    new_gqa_v = v_new.reshape(BATCH, 1, SEQ, GQA_KV_HEADS, GQA_HD)

    l = 3
    gate, xr = _lru_in(x, layer_mod(l), g_norm1[l][None, :], lru_w_in[0].astype(BF16))
    wg = lru_w_gate[0]
    wg = jnp.concatenate([wg[:, 0], wg[:, 1]], axis=-1).astype(BF16)
    h0 = jnp.concatenate([jnp.zeros((BATCH, 2, D), F32), state_lru_h[:, 0]], axis=0)
    conv_b = lru_conv_b[0][None, :]
    hf, st_f = _lru_scan(xr, lru_conv_w[0], conv_b, wg[0], lru_b_gate[0, 0], lru_lambda[0, 0][None, :],
                         h0[:, 0:1], reverse=False)
    y, st_b = _lru_scan(xr, lru_conv_w[0], conv_b, wg[1], lru_b_gate[0, 1], lru_lambda[0, 1][None, :],
                        h0[:, 1:2], reverse=True, hf=hf, gate=gate)
    y_ctx, y_lat = post(l, (x,), y, lru_w_out[0])
    new_lru_h = jnp.concatenate([st_f[:BATCH], st_b[:BATCH]], axis=1)[:, None]

    y_prompt = y_ctx.reshape(BATCH, SEQ, D)
    y_sample = y_lat.reshape(DEC_BATCH, DEC_SEQ, D)
    return (y_prompt, y_sample, new_mla_ckv, new_mla_krope, new_diff_k, new_diff_v,
            new_gqa_k, new_gqa_v, new_lru_h)
```

```python
import functools
import math

import jax
import jax.numpy as jnp
from jax import lax
from jax.experimental import pallas as pl
from jax.experimental.pallas import tpu as pltpu

F32 = jnp.float32
BF16 = jnp.bfloat16

D = 1024
BATCH = 32
SEQ = 256
DEPTH = 4
DEC_BATCH = 2
DEC_SEQ = 4096
PAST = 256
GRID_W = 64
EPS = 1e-6
ROPE_THETA = 10000.0
FFN_H = 2816
N_CTX = BATCH * SEQ
N_LAT = DEC_BATCH * DEC_SEQ
T = N_CTX + N_LAT
N_GROUPS = 1 + DEC_BATCH

MLA_HEADS = 8
MLA_NOPE = 128
MLA_ROPE = 64
MLA_V = 128
MLA_Q_RANK = 384
MLA_KV_RANK = 256
MLA_DQ = 256
DIFF_HD = 64
DIFF_HEADS = 8
GQA_HD = 128
GQA_Q_HEADS = 8
GQA_KV_HEADS = 2
LRU_BLOCKS = 8
LRU_BLK = 128
LRU_C = 8.0

LANES = 128
SUBLANES = 8
VMEM_LIMIT = 56 * 1024 * 1024

TM = 512
FFN_CHUNK = 256
LRU_TB = 256
ATTN_TQ = 1024
ATTN_BK = 1024
ATTN_ROWS = 1024
LOG2E = 1.4426950408889634


def _cparams(sem):
    return pltpu.CompilerParams(dimension_semantics=sem, vmem_limit_bytes=VMEM_LIMIT)


def _group_of_block(i, rows_per_block):
    n_ctx_blocks = N_CTX // rows_per_block
    per = DEC_SEQ // rows_per_block
    return jnp.maximum(i - (n_ctx_blocks - per), 0) // per


def _rms(x, gain):
    y = x * lax.rsqrt(jnp.mean(x * x, axis=-1, keepdims=True) + EPS)
    return y * gain


def _rms_halves(x, gain):
    lane = lax.broadcasted_iota(jnp.int32, x.shape, 1)
    low = lane < 64
    sq = x * x
    s_low = jnp.sum(jnp.where(low, sq, 0.0), axis=-1, keepdims=True)
    s_all = jnp.sum(sq, axis=-1, keepdims=True)
    ms = jnp.where(low, s_low, s_all - s_low) * (1.0 / 64.0)
    return x * lax.rsqrt(ms + EPS) * gain


def _rope128(x, c, s):
    return x * c + pltpu.roll(x, 64, 1) * s


def _rope64x2(x, c, s):
    lane = lax.broadcasted_iota(jnp.int32, x.shape, 1)
    low = (lane % 64) < 32
    partner = jnp.where(low, pltpu.roll(x, 96, 1), pltpu.roll(x, 32, 1))
    return x * c + partner * s


def _is_ctx_block():
    return pl.program_id(0) < N_CTX // TM


def _norm_mod(x, g_ref, mod_ref, shift_row):
    y = _rms(x, g_ref[...])
    shift = mod_ref[shift_row:shift_row + 1, :]
    scale = mod_ref[shift_row + 1:shift_row + 2, :]
    return (y * (1.0 + scale) + shift).astype(BF16)


def _mod_kernel(cond_ref, w_ref, b_ref, o_ref):
    s = jax.nn.silu(cond_ref[...]).astype(BF16)
    w = w_ref[...].astype(BF16)
    o_ref[...] = jnp.dot(s, w, preferred_element_type=F32) + b_ref[...]


def _modulation(cond, w_mod, b_mod):
    tn = 1536
    return pl.pallas_call(
        _mod_kernel,
        grid=(DEPTH, 6 * D // tn),
        in_specs=[
            pl.BlockSpec((SUBLANES, D), lambda l, j: (0, 0)),
            pl.BlockSpec((None, D, tn), lambda l, j: (l, 0, j)),
            pl.BlockSpec((None, 1, tn), lambda l, j: (l, 0, j)),
        ],
        out_specs=pl.BlockSpec((None, SUBLANES, tn), lambda l, j: (l, 0, j)),
        out_shape=jax.ShapeDtypeStruct((DEPTH, SUBLANES, 6 * D), F32),
        compiler_params=_cparams(("arbitrary", "arbitrary")),
        name="adaln_mod",
    )(cond, w_mod, b_mod.reshape(DEPTH, 1, 6 * D))


def _row_spec(width):
    return pl.BlockSpec((TM, width), lambda i: (i, 0))


def _ctx_row_spec(width):
    return pl.BlockSpec((TM, width), lambda i: (jnp.minimum(i, N_CTX // TM - 1), 0))


def _lat_row_spec(width):
    return pl.BlockSpec((TM, width), lambda i: (jnp.maximum(i - N_CTX // TM, 0), 0))


def _full_spec(shape):
    return pl.BlockSpec(shape, lambda i: (0,) * len(shape))


def _mod_spec():
    return pl.BlockSpec((None, 6, D), lambda i: (_group_of_block(i, TM), 0, 0))


def _heads_spec(heads, width):
    return pl.BlockSpec((heads, TM, width), lambda i: (0, i, 0))


N_BLOCKS = T // TM
N_CTX_BLOCKS = N_CTX // TM


def _proj_block(i):
    return jnp.minimum(i, N_BLOCKS - 1)


def _epi_block(i):
    return jnp.maximum(i - 1, 0)


def _proj_row_spec(width):
    return pl.BlockSpec((TM, width), lambda i: (_proj_block(i), 0))


def _proj_mod_spec():
    return pl.BlockSpec((None, 6, D), lambda i: (_group_of_block(_proj_block(i), TM), 0, 0))


def _epi_row_spec(width):
    return pl.BlockSpec((TM, width), lambda i: (_epi_block(i), 0))


def _epi_heads_spec(heads, width):
    return pl.BlockSpec((heads, TM, width), lambda i: (0, _epi_block(i), 0))


def _epi_ctx_row_spec(width):
    return pl.BlockSpec((TM, width), lambda i: (jnp.minimum(_epi_block(i), N_CTX_BLOCKS - 1), 0))


def _epi_is_ctx():
    i = pl.program_id(0)
    return jnp.logical_and(i >= 1, i <= N_CTX_BLOCKS)


def _two_stage(step, proj_a, proj_b):
    i = pl.program_id(0)

    @pl.when(i == 0)
    def _():
        proj_b[...] = jnp.zeros(proj_b.shape, F32)

    @pl.when(i % 2 == 0)
    def _():
        step(proj_b, proj_a)

    @pl.when(i % 2 == 1)
    def _():
        step(proj_a, proj_b)


def _gqa_in_kernel(x_ref, mod_ref, g_ref, w_ref, gqk_ref, cos_ref, sin_ref,
                   q_ref, k_ref, v_ref, ks_ref, vs_ref, proj_a, proj_b):
    n_q = GQA_Q_HEADS * GQA_HD
    n_kv = GQA_KV_HEADS * GQA_HD

    def step(prev, cur):
        h = _norm_mod(x_ref[...], g_ref, mod_ref, 0)
        cur[...] = jnp.dot(h, w_ref[...], preferred_element_type=F32)
        c = cos_ref[...]
        s = sin_ref[...]
        scale = LOG2E * GQA_HD ** -0.5
        for hd in range(GQA_Q_HEADS):
            q = _rms(prev[:, hd * GQA_HD:(hd + 1) * GQA_HD], gqk_ref[0:1, :])
            q_ref[hd] = (_rope128(q, c, s) * scale).astype(BF16)
        ks = []
        for hd in range(GQA_KV_HEADS):
            lo = n_q + hd * GQA_HD
            ks.append(_rms(prev[:, lo:lo + GQA_HD], gqk_ref[1:2, :]))
            k_ref[hd] = _rope128(ks[-1], c, s).astype(BF16)
            lo = n_q + n_kv + hd * GQA_HD
            v_ref[hd] = prev[:, lo:lo + GQA_HD].astype(BF16)

        @pl.when(_epi_is_ctx())
        def _():
            for hd in range(GQA_KV_HEADS):
                ks_ref[:, hd * GQA_HD:(hd + 1) * GQA_HD] = ks[hd]
            vs_ref[...] = prev[:, n_q + n_kv:]

    _two_stage(step, proj_a, proj_b)


def _proj_scratch(width):
    return [pltpu.VMEM((TM, width), F32), pltpu.VMEM((TM, width), F32)]


def _gqa_in(x, mod, g1, w_in, g_qk, cos, sin):
    n_kv = GQA_KV_HEADS * GQA_HD
    return pl.pallas_call(
        _gqa_in_kernel,
        grid=(N_BLOCKS + 1,),
        in_specs=[_proj_row_spec(D), _proj_mod_spec(), _full_spec((1, D)), _full_spec(w_in.shape),
                  _full_spec((2, GQA_HD)), _epi_row_spec(LANES), _epi_row_spec(LANES)],
        out_specs=[_epi_heads_spec(GQA_Q_HEADS, GQA_HD), _epi_heads_spec(GQA_KV_HEADS, GQA_HD),
                   _epi_heads_spec(GQA_KV_HEADS, GQA_HD), _epi_ctx_row_spec(n_kv),
                   _epi_ctx_row_spec(n_kv)],
        scratch_shapes=_proj_scratch(w_in.shape[1]),
        out_shape=[jax.ShapeDtypeStruct((GQA_Q_HEADS, T, GQA_HD), BF16),
                   jax.ShapeDtypeStruct((GQA_KV_HEADS, T, GQA_HD), BF16),
                   jax.ShapeDtypeStruct((GQA_KV_HEADS, T, GQA_HD), BF16),
                   jax.ShapeDtypeStruct((N_CTX, n_kv), F32),
                   jax.ShapeDtypeStruct((N_CTX, n_kv), F32)],
        compiler_params=_cparams(("arbitrary",)),
        name="gqa_in",
    )(x, mod, g1, w_in, g_qk, cos, sin)


def _diff_in_kernel(x_ref, mod_ref, g_ref, w_ref, gqk_ref, cos_ref, sin_ref,
                    q_ref, k_ref, v_ref, ks_ref, vs_ref, proj_a, proj_b):
    n = DIFF_HEADS * 2 * DIFF_HD

    def step(prev, cur):
        h = _norm_mod(x_ref[...], g_ref, mod_ref, 0)
        cur[...] = jnp.dot(h, w_ref[...], preferred_element_type=F32)
        c = cos_ref[...]
        s = sin_ref[...]
        scale = LOG2E * DIFF_HD ** -0.5
        lane = lax.broadcasted_iota(jnp.int32, (TM, LANES), 1)
        low = lane < DIFF_HD
        ks = []
        for hd in range(DIFF_HEADS):
            sl = slice(hd * LANES, (hd + 1) * LANES)
            q = _rope64x2(_rms_halves(prev[:, sl], gqk_ref[0:1, :]), c, s) * scale
            q_ref[2 * hd] = jnp.where(low, q, 0.0).astype(BF16)
            q_ref[2 * hd + 1] = jnp.where(low, 0.0, q).astype(BF16)
            ks.append(_rms_halves(prev[:, n + hd * LANES:n + (hd + 1) * LANES], gqk_ref[1:2, :]))
            k_ref[hd] = _rope64x2(ks[-1], c, s).astype(BF16)
            v_ref[hd] = prev[:, 2 * n + hd * LANES:2 * n + (hd + 1) * LANES].astype(BF16)

        @pl.when(_epi_is_ctx())
        def _():
            for hd in range(DIFF_HEADS):
                ks_ref[:, hd * LANES:(hd + 1) * LANES] = ks[hd]
            vs_ref[...] = prev[:, 2 * n:]

    _two_stage(step, proj_a, proj_b)


def _diff_in(x, mod, g1, w_in, g_qk2, cos, sin):
    n = DIFF_HEADS * 2 * DIFF_HD
    return pl.pallas_call(
        _diff_in_kernel,
        grid=(N_BLOCKS + 1,),
        in_specs=[_proj_row_spec(D), _proj_mod_spec(), _full_spec((1, D)), _full_spec(w_in.shape),
                  _full_spec((2, LANES)), _epi_row_spec(LANES), _epi_row_spec(LANES)],
        out_specs=[_epi_heads_spec(2 * DIFF_HEADS, LANES), _epi_heads_spec(DIFF_HEADS, LANES),
                   _epi_heads_spec(DIFF_HEADS, LANES), _epi_ctx_row_spec(n), _epi_ctx_row_spec(n)],
        scratch_shapes=_proj_scratch(w_in.shape[1]),
        out_shape=[jax.ShapeDtypeStruct((2 * DIFF_HEADS, T, LANES), BF16),
                   jax.ShapeDtypeStruct((DIFF_HEADS, T, LANES), BF16),
                   jax.ShapeDtypeStruct((DIFF_HEADS, T, LANES), BF16),
                   jax.ShapeDtypeStruct((N_CTX, n), F32),
                   jax.ShapeDtypeStruct((N_CTX, n), F32)],
        compiler_params=_cparams(("arbitrary",)),
        name="diff_in",
    )(x, mod, g1, w_in, g_qk2, cos, sin)


def _mla_keys_values(ckv_bf16, kr2, w_ukv_ref, gk_ref, k_ref, v_ref):
    kv = jnp.dot(ckv_bf16, w_ukv_ref[...], preferred_element_type=F32)
    lane = lax.broadcasted_iota(jnp.int32, kr2.shape, 1)
    low = lane < MLA_ROPE
    kr_low = jnp.where(low, kr2, 0.0).astype(BF16)
    kr_high = jnp.where(low, 0.0, kr2).astype(BF16)
    width = MLA_NOPE + MLA_V
    for hd in range(MLA_HEADS):
        k_nope = _rms(kv[:, hd * width:hd * width + MLA_NOPE], gk_ref[...])
        k_ref[hd, :, 0:MLA_NOPE] = k_nope.astype(BF16)
        k_ref[hd, :, MLA_NOPE:MLA_DQ] = kr_low if hd % 2 == 0 else kr_high
        v_ref[hd] = kv[:, hd * width + MLA_NOPE:(hd + 1) * width].astype(BF16)


def _mla_in_kernel(xc_ref, xl_ref, mod_ref, g_ref, w_ref, gcq_ref, gckv_ref, gqn_ref, gqr_ref,
                   gkn_ref, gkr_ref, wqn_ref, wqr_ref, wukv_ref, cos_ref, sin_ref,
                   q_ref, k_ref, v_ref, ckv_ref, kr_ref, proj_a, proj_b):
    def step(prev, cur):
        x = jnp.where(_is_ctx_block(), xc_ref[...], xl_ref[...])
        h = _norm_mod(x, g_ref, mod_ref, 0)
        cur[...] = jnp.dot(h, w_ref[...], preferred_element_type=F32)
        c = cos_ref[...]
        s = sin_ref[...]
        cq = _rms(prev[:, 0:MLA_Q_RANK], gcq_ref[...]).astype(BF16)
        ckv = _rms(prev[:, MLA_Q_RANK:MLA_Q_RANK + MLA_KV_RANK], gckv_ref[...])
        kr2 = _rms_halves(prev[:, MLA_Q_RANK + MLA_KV_RANK:], gkr_ref[...])
        _mla_keys_values(ckv.astype(BF16), _rope64x2(kr2, c, s), wukv_ref, gkn_ref, k_ref, v_ref)

        scale = LOG2E * (MLA_NOPE + MLA_ROPE) ** -0.5
        qn = jnp.dot(cq, wqn_ref[...], preferred_element_type=F32)
        qr = jnp.dot(cq, wqr_ref[...], preferred_element_type=F32)
        lane = lax.broadcasted_iota(jnp.int32, (TM, LANES), 1)
        low = lane < MLA_ROPE
        for pair in range(MLA_HEADS // 2):
            r = _rms_halves(qr[:, pair * LANES:(pair + 1) * LANES], gqr_ref[...])
            r = _rope64x2(r, c, s) * scale
            q_ref[2 * pair, :, MLA_NOPE:MLA_DQ] = jnp.where(low, r, 0.0).astype(BF16)
            q_ref[2 * pair + 1, :, MLA_NOPE:MLA_DQ] = jnp.where(low, 0.0, r).astype(BF16)
        for hd in range(MLA_HEADS):
            q = _rms(qn[:, hd * MLA_NOPE:(hd + 1) * MLA_NOPE], gqn_ref[...]) * scale
            q_ref[hd, :, 0:MLA_NOPE] = q.astype(BF16)

        @pl.when(_epi_is_ctx())
        def _():
            ckv_ref[...] = ckv
            kr_ref[...] = kr2[:, 0:MLA_ROPE]

    _two_stage(step, proj_a, proj_b)


def _mla_in(x_ctx, x_lat, mod, g1, w_in, g_cq, g_ckv, gqn, gqr, gkn, gkr, wqn, wqr, wukv, cos, sin):
    n_lat_blocks = N_BLOCKS - N_CTX_BLOCKS
    return pl.pallas_call(
        _mla_in_kernel,
        grid=(N_BLOCKS + 1,),
        in_specs=[_ctx_row_spec(D),
                  pl.BlockSpec((TM, D), lambda i: (jnp.clip(i - N_CTX_BLOCKS, 0, n_lat_blocks - 1), 0)),
                  _proj_mod_spec(), _full_spec((1, D)), _full_spec(w_in.shape),
                  _full_spec((1, MLA_Q_RANK)), _full_spec((1, MLA_KV_RANK)),
                  _full_spec((1, LANES)), _full_spec((1, LANES)), _full_spec((1, LANES)),
                  _full_spec((1, LANES)), _full_spec(wqn.shape), _full_spec(wqr.shape),
                  _full_spec(wukv.shape), _epi_row_spec(LANES), _epi_row_spec(LANES)],
        out_specs=[_epi_heads_spec(MLA_HEADS, MLA_DQ), _epi_heads_spec(MLA_HEADS, MLA_DQ),
                   _epi_heads_spec(MLA_HEADS, MLA_V), _epi_ctx_row_spec(MLA_KV_RANK),
                   _epi_ctx_row_spec(MLA_ROPE)],
        scratch_shapes=_proj_scratch(w_in.shape[1]),
        out_shape=[jax.ShapeDtypeStruct((MLA_HEADS, T, MLA_DQ), BF16),
                   jax.ShapeDtypeStruct((MLA_HEADS, T, MLA_DQ), BF16),
                   jax.ShapeDtypeStruct((MLA_HEADS, T, MLA_V), BF16),
                   jax.ShapeDtypeStruct((N_CTX, MLA_KV_RANK), F32),
                   jax.ShapeDtypeStruct((N_CTX, MLA_ROPE), F32)],
        compiler_params=_cparams(("arbitrary",)),
        name="mla_in",
    )(x_ctx, x_lat, mod, g1, w_in, g_cq, g_ckv, gqn, gqr, gkn, gkr, wqn, wqr, wukv, cos, sin)


def _mla_cache_kernel(ckv_ref, kr_ref, wukv_ref, gkn_ref, k_ref, v_ref):
    kr = kr_ref[...]
    kr2 = jnp.concatenate([kr, kr], axis=-1)
    _mla_keys_values(ckv_ref[...].astype(BF16), kr2, wukv_ref, gkn_ref, k_ref, v_ref)


def _mla_cache(ckv, kr, wukv, gkn):
    rows = ckv.shape[0]
    return pl.pallas_call(
        _mla_cache_kernel,
        grid=(1,),
        in_specs=[_full_spec(ckv.shape), _full_spec(kr.shape), _full_spec(wukv.shape),
                  _full_spec((1, LANES))],
        out_specs=[_full_spec((MLA_HEADS, rows, MLA_DQ)), _full_spec((MLA_HEADS, rows, MLA_V))],
        out_shape=[jax.ShapeDtypeStruct((MLA_HEADS, rows, MLA_DQ), BF16),
                   jax.ShapeDtypeStruct((MLA_HEADS, rows, MLA_V), BF16)],
        compiler_params=_cparams(("arbitrary",)),
        name="mla_cache_kv",
    )(ckv, kr, wukv, gkn)


def _lru_in_kernel(x_ref, mod_ref, g_ref, w_ref, gate_ref, xr_ref):
    h = _norm_mod(x_ref[...], g_ref, mod_ref, 0)
    proj = jnp.dot(h, w_ref[...], preferred_element_type=F32)
    gate_ref[...] = proj[:, 0:D]
    xr_ref[...] = proj[:, D:2 * D]


def _lru_in(x, mod, g1, w_in):
    return pl.pallas_call(
        _lru_in_kernel,
        grid=(T // TM,),
        in_specs=[_row_spec(D), _mod_spec(), _full_spec((1, D)), _full_spec(w_in.shape)],
        out_specs=[_row_spec(D), _row_spec(D)],
        out_shape=[jax.ShapeDtypeStruct((T, D), F32), jax.ShapeDtypeStruct((T, D), F32)],
        compiler_params=_cparams(("arbitrary",)),
        name="lru_in",
    )(x, mod, g1, w_in)


def _softmax_pv(q, key_blocks):
    nt = (((1,), (1,)), ((), ()))
    m = l = acc = None
    for k, v in key_blocks:
        s = lax.dot_general(q, k, nt, preferred_element_type=F32)
        mb = jnp.max(s, axis=-1, keepdims=True)
        if m is None:
            m = mb
            p = jnp.exp2(s - m)
            l = jnp.sum(p, axis=-1, keepdims=True)
            acc = jnp.dot(p.astype(BF16), v, preferred_element_type=F32)
        else:
            m_new = jnp.maximum(m, mb)
            alpha = jnp.exp2(m - m_new)
            p = jnp.exp2(s - m_new)
            l = alpha * l + jnp.sum(p, axis=-1, keepdims=True)
            acc = alpha * acc + jnp.dot(p.astype(BF16), v, preferred_element_type=F32)
            m = m_new
    return acc / l


def _attn_kernel(*refs, heads, group, sub, bk, has_cache, diff, aliased, dv, lambda_init):
    it = iter(refs)
    q_ref, kn_ref, vn_ref = next(it), next(it), next(it)
    kc_ref = vc_ref = lam_ref = gsub_ref = None
    if has_cache:
        kc_ref, vc_ref = next(it), next(it)
    if diff:
        lam_ref, gsub_ref = next(it), next(it)
    if aliased:
        next(it)
    o_ref = next(it)

    tq = q_ref.shape[1]
    seq_len = kn_ref.shape[1]
    stacked = min(group, heads)
    cached = []
    if has_cache:
        cached = [(kc_ref[...].astype(BF16), vc_ref[...].astype(BF16))]
    if diff:
        lam = lam_ref[...]
        lam_full = (jnp.exp(jnp.sum(lam[0:1] * lam[1:2], axis=-1, keepdims=True))
                    - jnp.exp(jnp.sum(lam[2:3] * lam[3:4], axis=-1, keepdims=True)) + lambda_init)

    for kv in range(max(heads // group, 1)):
        def body(i, carry, kv=kv):
            rows = pl.ds(pl.multiple_of(i * sub, sub), sub)
            q = jnp.concatenate([q_ref[kv * stacked + g, rows, :] for g in range(stacked)], axis=0)
            blocks = cached + [(kn_ref[kv, j * bk:(j + 1) * bk, :], vn_ref[kv, j * bk:(j + 1) * bk, :])
                               for j in range(seq_len // bk)]
            o = _softmax_pv(q, blocks)
            if diff:
                od = o[0:sub] - lam_full * o[sub:2 * sub]
                od = _rms(od, gsub_ref[...]) * (1.0 - lambda_init)
                o_ref[rows, kv * dv:(kv + 1) * dv] = od.astype(BF16)
            else:
                for g in range(stacked):
                    hd = kv * stacked + g
                    o_ref[rows, hd * dv:(hd + 1) * dv] = o[g * sub:(g + 1) * sub].astype(BF16)
            return carry
        lax.fori_loop(0, tq // sub, body, 0)


def _attention(q, kn, vn, *, seq_len, n_seq, row0, tq, sub, bk, heads_per_step, group, cache=None,
               diff=None, lambda_init=0.0, out=None):
    hq, _, dq = q.shape
    hkv, _, dv = vn.shape
    kv_per_step = max(heads_per_step // group, 1)
    n_hblk = hq // heads_per_step
    n_qblk = seq_len // tq
    seq0 = row0 // seq_len
    qblk0 = row0 // tq

    def q_map(b, hb, qb):
        return (hb, qblk0 + b * n_qblk + qb, 0)

    def kv_map(b, hb, qb):
        return ((hb * heads_per_step) // (group * kv_per_step), seq0 + b, 0)

    in_specs = [pl.BlockSpec((heads_per_step, tq, dq), q_map),
                pl.BlockSpec((kv_per_step, seq_len, dq), kv_map),
                pl.BlockSpec((kv_per_step, seq_len, dv), kv_map)]
    args = [q, kn, vn]
    if cache is not None:
        kc, vc, kc_block, kc_map, vc_block, vc_map = cache
        in_specs += [pl.BlockSpec(kc_block, kc_map), pl.BlockSpec(vc_block, vc_map)]
        args += [kc, vc]
    out_heads = heads_per_step
    if diff is not None:
        lam, gsub = diff
        in_specs += [pl.BlockSpec(lam.shape, lambda b, hb, qb: (0, 0)),
                     pl.BlockSpec(gsub.shape, lambda b, hb, qb: (0, 0))]
        args += [lam, gsub]
        out_heads = heads_per_step // 2
    n_out = (hq // 2 if diff is not None else hq) * dv
    aliases = {}
    if out is not None:
        in_specs += [pl.BlockSpec(memory_space=pl.ANY)]
        args += [out]
        aliases = {len(args) - 1: 0}
    kernel = functools.partial(_attn_kernel, heads=heads_per_step, group=group, sub=sub, bk=bk,
                               has_cache=cache is not None, diff=diff is not None,
                               aliased=out is not None, dv=dv, lambda_init=lambda_init)
    return pl.pallas_call(
        kernel,
        grid=(n_seq, n_hblk, n_qblk),
        in_specs=in_specs,
        out_specs=pl.BlockSpec((tq, out_heads * dv),
                               lambda b, hb, qb: (qblk0 + b * n_qblk + qb, hb)),
        out_shape=jax.ShapeDtypeStruct((T, n_out), BF16),
        input_output_aliases=aliases,
        compiler_params=_cparams(("arbitrary", "arbitrary", "arbitrary")),
        name="attention",
    )(*args)


def _lru_seq_pos(i):
    n_ctx_blocks = N_CTX // LRU_TB
    per_ctx = SEQ // LRU_TB
    per_lat = DEC_SEQ // LRU_TB
    pos = jnp.where(i < n_ctx_blocks, i % per_ctx, (i - n_ctx_blocks) % per_lat)
    length = jnp.where(i < n_ctx_blocks, per_ctx, per_lat)
    return pos == 0, pos == length - 1


def _lru_seq_of_block(i):
    n_ctx_blocks = N_CTX // LRU_TB
    return jnp.where(i < n_ctx_blocks, i // (SEQ // LRU_TB),
                     BATCH + (i - n_ctx_blocks) // (DEC_SEQ // LRU_TB))


def _tile_scan(a, u, reverse):
    row = lax.broadcasted_iota(jnp.int32, a.shape, 0)
    for k in (1, 2, 4):
        if reverse:
            valid = row < SUBLANES - k
            shift = SUBLANES - k
        else:
            valid = row >= k
            shift = k
        a_s = jnp.where(valid, pltpu.roll(a, shift, 0), 1.0)
        u_s = jnp.where(valid, pltpu.roll(u, shift, 0), 0.0)
        u = a * u_s + u
        a = a * a_s
    return a, u


def _lru_scan_kernel(x_ref, prev_ref, next_ref, cw_ref, cb_ref, wg_ref, bg_ref, lam_ref, h0_ref,
                     *rest, reverse, combine):
    if combine:
        hf_ref, gate_ref, y_ref, st_ref, xpad, a_s, u_s, carry = rest
    else:
        y_ref, st_ref, xpad, a_s, u_s, carry = rest
    j = pl.program_id(0)
    i = (T // LRU_TB - 1 - j) if reverse else j
    first, last = _lru_seq_pos(i)
    starts = last if reverse else first
    ends = first if reverse else last

    xpad[0:SUBLANES, :] = jnp.where(first, 0.0, prev_ref[...])
    xpad[SUBLANES:SUBLANES + LRU_TB, :] = x_ref[...]
    xpad[SUBLANES + LRU_TB:, :] = jnp.where(last, 0.0, next_ref[...])
    xr = xpad[SUBLANES - 1:SUBLANES - 1 + LRU_TB, :] * cw_ref[0:1, :]
    for t in range(1, 4):
        xr = xr + xpad[SUBLANES - 1 + t:SUBLANES - 1 + t + LRU_TB, :] * cw_ref[t:t + 1, :]
    xr = xr + cb_ref[...]

    xr_b = xr.astype(BF16)
    lam = lam_ref[...]
    neg = -lam
    softplus = jnp.maximum(neg, 0.0) + jnp.log1p(jnp.exp(-jnp.abs(neg)))
    for n in range(LRU_BLOCKS):
        sl = slice(n * LRU_BLK, (n + 1) * LRU_BLK)
        g = jnp.dot(xr_b[:, sl], wg_ref[n], preferred_element_type=F32)
        r = jax.nn.sigmoid(g[:, 0:LRU_BLK] + bg_ref[0:1, sl])
        gi = jax.nn.sigmoid(g[:, LRU_BLK:] + bg_ref[1:2, sl])
        log_a = -LRU_C * r * softplus[:, sl]
        th = jnp.tanh(log_a)
        one_minus_a2 = -2.0 * th / (1.0 - th)
        a_s[:, sl] = jnp.exp(log_a)
        u_s[:, sl] = jnp.sqrt(one_minus_a2) * (gi * xr[:, sl])

    @pl.when(starts)
    def _():
        carry[...] = h0_ref[...]

    n_tiles = LRU_TB // SUBLANES

    def body(t, h):
        tt = (n_tiles - 1 - t) if reverse else t
        rows = pl.ds(pl.multiple_of(tt * SUBLANES, SUBLANES), SUBLANES)
        a_c, u_c = _tile_scan(a_s[rows, :], u_s[rows, :], reverse)
        hs = a_c * h + u_c
        if combine:
            gate = gate_ref[rows, :]
            y_ref[rows, :] = ((hf_ref[rows, :] + hs) * jax.nn.gelu(gate)).astype(BF16)
        else:
            y_ref[rows, :] = hs
        return hs[0:1, :] if reverse else hs[SUBLANES - 1:SUBLANES, :]

    h_end = lax.fori_loop(0, n_tiles, body, carry[...])
    carry[...] = h_end

    @pl.when(ends)
    def _():
        st_ref[...] = h_end


def _lru_scan(xr, conv_w, conv_b, w_gate, b_gate, lam, h0, *, reverse, hf=None, gate=None):
    nb = T // LRU_TB
    hb = LRU_TB // SUBLANES
    n_halo = T // SUBLANES

    def blk(j):
        return (nb - 1 - j) if reverse else j

    in_specs = [
        pl.BlockSpec((LRU_TB, D), lambda j: (blk(j), 0)),
        pl.BlockSpec((SUBLANES, D), lambda j: (jnp.maximum(blk(j) * hb - 1, 0), 0)),
        pl.BlockSpec((SUBLANES, D), lambda j: (jnp.minimum((blk(j) + 1) * hb, n_halo - 1), 0)),
        pl.BlockSpec((4, D), lambda j: (0, 0)),
        pl.BlockSpec((1, D), lambda j: (0, 0)),
        pl.BlockSpec((LRU_BLOCKS, LRU_BLK, 2 * LRU_BLK), lambda j: (0, 0, 0)),
        pl.BlockSpec((2, D), lambda j: (0, 0)),
        pl.BlockSpec((1, D), lambda j: (0, 0)),
        pl.BlockSpec((None, 1, D), lambda j: (_lru_seq_of_block(blk(j)), 0, 0)),
    ]
    args = [xr, xr, xr, conv_w, conv_b, w_gate, b_gate, lam, h0]
    combine = hf is not None
    if combine:
        in_specs += [pl.BlockSpec((LRU_TB, D), lambda j: (blk(j), 0)),
                     pl.BlockSpec((LRU_TB, D), lambda j: (blk(j), 0))]
        args += [hf, gate]
    n_seq = BATCH + DEC_BATCH
    return pl.pallas_call(
        functools.partial(_lru_scan_kernel, reverse=reverse, combine=combine),
        grid=(nb,),
        in_specs=in_specs,
        out_specs=[pl.BlockSpec((LRU_TB, D), lambda j: (blk(j), 0)),
                   pl.BlockSpec((None, 1, D), lambda j: (_lru_seq_of_block(blk(j)), 0, 0))],
        out_shape=[jax.ShapeDtypeStruct((T, D), BF16 if combine else F32),
                   jax.ShapeDtypeStruct((n_seq, 1, D), F32)],
        scratch_shapes=[pltpu.VMEM((LRU_TB + 2 * SUBLANES, D), F32),
                        pltpu.VMEM((LRU_TB, D), F32),
                        pltpu.VMEM((LRU_TB, D), F32),
                        pltpu.VMEM((1, D), F32)],
        compiler_params=_cparams(("arbitrary",)),
        name="lru_scan_bwd" if reverse else "lru_scan_fwd",
    )(*args)


def _post_kernel(*refs, split_in, split_out):
    it = iter(refs)
    if split_in:
        x = jnp.where(_is_ctx_block(), next(it)[...], next(it)[...])
    else:
        x = next(it)[...]
    o_ref, mod_ref, g2_ref, wo_ref, win_ref, wout_ref = (next(it) for _ in range(6))
    y_refs = list(it)
    x1 = x + mod_ref[2:3, :] * jnp.dot(o_ref[...], wo_ref[...], preferred_element_type=F32)
    h = _rms(x1, g2_ref[...])
    h = (h * (1.0 + mod_ref[4:5, :]) + mod_ref[3:4, :]).astype(BF16)
    acc = jnp.zeros((TM, D), F32)
    for c in range(FFN_H // FFN_CHUNK):
        lo = c * FFN_CHUNK
        g = jnp.dot(h, win_ref[:, lo:lo + FFN_CHUNK], preferred_element_type=F32)
        u = jnp.dot(h, win_ref[:, FFN_H + lo:FFN_H + lo + FFN_CHUNK], preferred_element_type=F32)
        a = (jax.nn.silu(g) * u).astype(BF16)
        acc = acc + jnp.dot(a, wout_ref[lo:lo + FFN_CHUNK, :], preferred_element_type=F32)
    y = x1 + mod_ref[5:6, :] * acc
    if split_out:
        @pl.when(_is_ctx_block())
        def _():
            y_refs[0][...] = y

        @pl.when(jnp.logical_not(_is_ctx_block()))
        def _():
            y_refs[1][...] = y
    else:
        y_refs[0][...] = y


def _layer_weight_spec(shape, layer):
    return pl.BlockSpec((None,) + shape[1:], lambda i: (layer,) + (0,) * (len(shape) - 1),
                        pipeline_mode=pl.Buffered(1))


def _post(xs, o, mod, g2, w_o, w_in, w_out, layer, *, split_out):
    split_in = len(xs) == 2
    x_specs = [_ctx_row_spec(D), _lat_row_spec(D)] if split_in else [_row_spec(D)]
    if split_out:
        out_specs = [_ctx_row_spec(D), _lat_row_spec(D)]
        out_shape = [jax.ShapeDtypeStruct((N_CTX, D), F32), jax.ShapeDtypeStruct((N_LAT, D), F32)]
    else:
        out_specs = [_row_spec(D)]
        out_shape = [jax.ShapeDtypeStruct((T, D), F32)]
    return pl.pallas_call(
        functools.partial(_post_kernel, split_in=split_in, split_out=split_out),
        grid=(T // TM,),
        in_specs=x_specs + [_row_spec(D), _mod_spec(), _full_spec((1, D)),
                            _layer_weight_spec((1,) + w_o.shape, 0),
                            _layer_weight_spec(w_in.shape, layer),
                            _layer_weight_spec(w_out.shape, layer)],
        out_specs=out_specs,
        out_shape=out_shape,
        compiler_params=_cparams(("arbitrary",)),
        name="post_mixer_ffn",
    )(*xs, o, mod, g2, w_o[None], w_in, w_out)


def _axial_tables(rot_dim):
    row = jnp.repeat(jnp.arange(DEC_SEQ // GRID_W), GRID_W).astype(F32)
    col = jnp.tile(jnp.arange(GRID_W), DEC_SEQ // GRID_W).astype(F32)
    n_freq = rot_dim // 4
    inv = ROPE_THETA ** (-jnp.arange(n_freq, dtype=F32) / n_freq)
    ang = jnp.concatenate([row[:, None] * inv, col[:, None] * inv], axis=-1)
    cos, sin = jnp.cos(ang), jnp.sin(ang)
    reps = LANES // rot_dim
    cos_t = jnp.tile(jnp.concatenate([cos, cos], axis=-1), (DEC_BATCH, reps))
    sin_t = jnp.tile(jnp.concatenate([-sin, sin], axis=-1), (DEC_BATCH, reps))
    cos_t = jnp.concatenate([jnp.ones((N_CTX, LANES), F32), cos_t], axis=0)
    sin_t = jnp.concatenate([jnp.zeros((N_CTX, LANES), F32), sin_t], axis=0)
    return cos_t, sin_t


def kernel(x_prompt, x_sample, cache_mla_ckv, cache_mla_krope, cache_diff_k, cache_diff_v, cache_gqa_k, cache_gqa_v, state_lru_h, c, c_ctx, w_mod, b_mod, g_norm1, g_norm2, w_ffn_in, w_ffn_out, mla_w_in, mla_g_cq, mla_g_ckv, mla_w_uq, mla_w_ukv, mla_g_qk, mla_w_o, diff_w_in, diff_g_qk, diff_lambda, diff_g_sub, diff_w_o, gqa_w_in, gqa_g_qk, gqa_w_o, lru_w_in, lru_conv_w, lru_conv_b, lru_w_gate, lru_b_gate, lru_lambda, lru_w_out):
    x_in = (x_prompt.reshape(N_CTX, D), x_sample.reshape(N_LAT, D))
    cond = jnp.concatenate([c_ctx[None, :], c, jnp.zeros((SUBLANES - N_GROUPS, D), F32)], axis=0)
    mod_all = _modulation(cond, w_mod, b_mod)
    cos128, sin128 = _axial_tables(GQA_HD)
    cos64, sin64 = _axial_tables(DIFF_HD)
    w_ffn_in_b = w_ffn_in.astype(BF16)
    w_ffn_out_b = w_ffn_out.astype(BF16)

    def layer_mod(l):
        return mod_all[l, :N_GROUPS].reshape(N_GROUPS, 6, D)

    def post(l, xs, o, w_o):
        out = _post(xs, o, layer_mod(l), g_norm2[l][None, :], w_o.astype(BF16),
                    w_ffn_in_b, w_ffn_out_b, l, split_out=l == DEPTH - 1)
        return out if l == DEPTH - 1 else out[0]

    def attend(q, k, v, *, hq, group, cache, diff, lambda_init):
        common = dict(group=group, diff=diff, lambda_init=lambda_init)
        sub = ATTN_ROWS // group
        o_ctx = _attention(q, k, v, seq_len=SEQ, n_seq=BATCH, row0=0, tq=SEQ, bk=SEQ,
                           sub=min(sub, SEQ), heads_per_step=hq, cache=None, **common)
        return _attention(q, k, v, seq_len=DEC_SEQ, n_seq=DEC_BATCH, row0=N_CTX, tq=ATTN_TQ,
                          bk=ATTN_BK, sub=sub, heads_per_step=group, cache=cache, out=o_ctx,
                          **common)

    l = 0
    w_in = mla_w_in[0]
    kr_cols = w_in[:, MLA_Q_RANK + MLA_KV_RANK:]
    w_in2 = jnp.concatenate([w_in, kr_cols], axis=1).astype(BF16)
    w_uq = mla_w_uq[0].reshape(MLA_Q_RANK, MLA_HEADS, MLA_NOPE + MLA_ROPE)
    wqn = w_uq[:, :, :MLA_NOPE].reshape(MLA_Q_RANK, MLA_HEADS * MLA_NOPE).astype(BF16)
    wqr = w_uq[:, :, MLA_NOPE:].reshape(MLA_Q_RANK, MLA_HEADS * MLA_ROPE).astype(BF16)
    wukv = mla_w_ukv[0].astype(BF16)
    gqk = mla_g_qk[0]
    gqn = gqk[0:1, :MLA_NOPE]
    gqr = jnp.tile(gqk[0:1, MLA_NOPE:], (1, 2))
    gkn = gqk[1:2, :MLA_NOPE]
    gkr = jnp.tile(gqk[1:2, MLA_NOPE:], (1, 2))
    q, k, v, ckv_new, kr_new = _mla_in(*x_in, layer_mod(l), g_norm1[l][None, :], w_in2,
                                       mla_g_cq[0][None, :], mla_g_ckv[0][None, :],
                                       gqn, gqr, gkn, gkr, wqn, wqr, wukv, cos64, sin64)
    kc, vc = _mla_cache(cache_mla_ckv[:, 0].reshape(DEC_BATCH * PAST, MLA_KV_RANK),
                        cache_mla_krope[:, 0].reshape(DEC_BATCH * PAST, MLA_ROPE), wukv, gkn)
    cache = (kc, vc,
             (None, PAST, MLA_DQ), lambda b, hb, qb: (hb, b, 0),
             (None, PAST, MLA_V), lambda b, hb, qb: (hb, b, 0))
    o = attend(q, k, v, hq=MLA_HEADS, group=1, cache=cache, diff=None, lambda_init=0.0)
    x = post(l, x_in, o, mla_w_o[0])
    new_mla_ckv = ckv_new.reshape(BATCH, 1, SEQ, MLA_KV_RANK)
    new_mla_krope = kr_new.reshape(BATCH, 1, SEQ, MLA_ROPE)

    l = 1
    lambda_init = 0.8 - 0.6 * math.exp(-0.3 * l)
    q, k, v, k_new, v_new = _diff_in(x, layer_mod(l), g_norm1[l][None, :], diff_w_in[0].astype(BF16),
                                     jnp.tile(diff_g_qk[0], (1, 2)), cos64, sin64)
    n_diff = DIFF_HEADS * 2 * DIFF_HD
    cache = (cache_diff_k[:, 0].reshape(DEC_BATCH, PAST, n_diff),
             cache_diff_v[:, 0].reshape(DEC_BATCH, PAST, n_diff),
             (None, PAST, LANES), lambda b, hb, qb: (b, 0, hb),
             (None, PAST, LANES), lambda b, hb, qb: (b, 0, hb))
    o = attend(q, k, v, hq=2 * DIFF_HEADS, group=2, cache=cache,
               diff=(diff_lambda[0], diff_g_sub[0][None, :]), lambda_init=lambda_init)
    x = post(l, (x,), o, diff_w_o[0])
    new_diff_k = k_new.reshape(BATCH, 1, SEQ, DIFF_HEADS, 2, DIFF_HD)
    new_diff_v = v_new.reshape(BATCH, 1, SEQ, DIFF_HEADS, 2 * DIFF_HD)

    l = 2
    q, k, v, k_new, v_new = _gqa_in(x, layer_mod(l), g_norm1[l][None, :], gqa_w_in[0].astype(BF16),
                                    gqa_g_qk[0], cos128, sin128)
    n_kv = GQA_KV_HEADS * GQA_HD
    group = GQA_Q_HEADS // GQA_KV_HEADS
    cache = (cache_gqa_k[:, 0].reshape(DEC_BATCH, PAST, n_kv),
             cache_gqa_v[:, 0].reshape(DEC_BATCH, PAST, n_kv),
             (None, PAST, GQA_HD), lambda b, hb, qb: (b, 0, hb),
             (None, PAST, GQA_HD), lambda b, hb, qb: (b, 0, hb))
    o = attend(q, k, v, hq=GQA_Q_HEADS, group=group, cache=cache, diff=None, lambda_init=0.0)
    x = post(l, (x,), o, gqa_w_o[0])
    new_gqa_k = k_new.reshape(BATCH, 1, SEQ, GQA_KV_HEADS, GQA_HD)
    new_gqa_v = v_new.reshape(BATCH, 1, SEQ, GQA_KV_HEADS, GQA_HD)

    l = 3
    gate, xr = _lru_in(x, layer_mod(l), g_norm1[l][None, :], lru_w_in[0].astype(BF16))
    wg = lru_w_gate[0]
    wg = jnp.concatenate([wg[:, 0], wg[:, 1]], axis=-1).astype(BF16)
    h0 = jnp.concatenate([jnp.zeros((BATCH, 2, D), F32), state_lru_h[:, 0]], axis=0)
    conv_b = lru_conv_b[0][None, :]
    hf, st_f = _lru_scan(xr, lru_conv_w[0], conv_b, wg[0], lru_b_gate[0, 0], lru_lambda[0, 0][None, :],
                         h0[:, 0:1], reverse=False)
    y, st_b = _lru_scan(xr, lru_conv_w[0], conv_b, wg[1], lru_b_gate[0, 1], lru_lambda[0, 1][None, :],
                        h0[:, 1:2], reverse=True, hf=hf, gate=gate)
    y_ctx, y_lat = post(l, (x,), y, lru_w_out[0])
    new_lru_h = jnp.concatenate([st_f[:BATCH], st_b[:BATCH]], axis=1)[:, None]

    y_prompt = y_ctx.reshape(BATCH, SEQ, D)
    y_sample = y_lat.reshape(DEC_BATCH, DEC_SEQ, D)
    return (y_prompt, y_sample, new_mla_ckv, new_mla_krope, new_diff_k, new_diff_v,
            new_gqa_k, new_gqa_v, new_lru_h)
```

```python
import functools
import math

import jax
import jax.numpy as jnp
from jax import lax
from jax.experimental import pallas as pl
from jax.experimental.pallas import tpu as pltpu

F32 = jnp.float32
BF16 = jnp.bfloat16

D = 1024
BATCH = 32
SEQ = 256
DEPTH = 4
DEC_BATCH = 2
DEC_SEQ = 4096
PAST = 256
GRID_W = 64
EPS = 1e-6
ROPE_THETA = 10000.0
FFN_H = 2816
N_CTX = BATCH * SEQ
N_LAT = DEC_BATCH * DEC_SEQ
T = N_CTX + N_LAT
N_GROUPS = 1 + DEC_BATCH

MLA_HEADS = 8
MLA_NOPE = 128
MLA_ROPE = 64
MLA_V = 128
MLA_Q_RANK = 384
MLA_KV_RANK = 256
MLA_DQ = 256
DIFF_HD = 64
DIFF_HEADS = 8
GQA_HD = 128
GQA_Q_HEADS = 8
GQA_KV_HEADS = 2
LRU_BLOCKS = 8
LRU_BLK = 128
LRU_C = 8.0

LANES = 128
SUBLANES = 8
VMEM_LIMIT = 56 * 1024 * 1024

TM = 512
FFN_CHUNK = 256
LRU_TB = 256
ATTN_TQ = 1024
ATTN_BK = 1024
ATTN_ROWS = 1024
LOG2E = 1.4426950408889634


def _cparams(sem):
    return pltpu.CompilerParams(dimension_semantics=sem, vmem_limit_bytes=VMEM_LIMIT)


def _group_of_block(i, rows_per_block):
    n_ctx_blocks = N_CTX // rows_per_block
    per = DEC_SEQ // rows_per_block
    return jnp.maximum(i - (n_ctx_blocks - per), 0) // per


def _rms(x, gain):
    y = x * lax.rsqrt(jnp.mean(x * x, axis=-1, keepdims=True) + EPS)
    return y * gain


def _rms_halves(x, gain):
    lane = lax.broadcasted_iota(jnp.int32, x.shape, 1)
    low = lane < 64
    sq = x * x
    s_low = jnp.sum(jnp.where(low, sq, 0.0), axis=-1, keepdims=True)
    s_all = jnp.sum(sq, axis=-1, keepdims=True)
    ms = jnp.where(low, s_low, s_all - s_low) * (1.0 / 64.0)
    return x * lax.rsqrt(ms + EPS) * gain


def _rope128(x, c, s):
    return x * c + pltpu.roll(x, 64, 1) * s


def _rope64x2(x, c, s):
    lane = lax.broadcasted_iota(jnp.int32, x.shape, 1)
    low = (lane % 64) < 32
    partner = jnp.where(low, pltpu.roll(x, 96, 1), pltpu.roll(x, 32, 1))
    return x * c + partner * s


def _is_ctx_block():
    return pl.program_id(0) < N_CTX // TM


def _norm_mod(x, g_ref, mod_ref, shift_row):
    y = _rms(x, g_ref[...])
    shift = mod_ref[shift_row:shift_row + 1, :]
    scale = mod_ref[shift_row + 1:shift_row + 2, :]
    return (y * (1.0 + scale) + shift).astype(BF16)


def _mod_kernel(cond_ref, w_ref, b_ref, o_ref):
    s = jax.nn.silu(cond_ref[...]).astype(BF16)
    w = w_ref[...].astype(BF16)
    o_ref[...] = jnp.dot(s, w, preferred_element_type=F32) + b_ref[...]


def _modulation(cond, w_mod, b_mod):
    tn = 1536
    return pl.pallas_call(
        _mod_kernel,
        grid=(DEPTH, 6 * D // tn),
        in_specs=[
            pl.BlockSpec((SUBLANES, D), lambda l, j: (0, 0)),
            pl.BlockSpec((None, D, tn), lambda l, j: (l, 0, j)),
            pl.BlockSpec((None, 1, tn), lambda l, j: (l, 0, j)),
        ],
        out_specs=pl.BlockSpec((None, SUBLANES, tn), lambda l, j: (l, 0, j)),
        out_shape=jax.ShapeDtypeStruct((DEPTH, SUBLANES, 6 * D), F32),
        compiler_params=_cparams(("arbitrary", "arbitrary")),
        name="adaln_mod",
    )(cond, w_mod, b_mod.reshape(DEPTH, 1, 6 * D))


def _row_spec(width):
    return pl.BlockSpec((TM, width), lambda i: (i, 0))


def _ctx_row_spec(width):
    return pl.BlockSpec((TM, width), lambda i: (jnp.minimum(i, N_CTX // TM - 1), 0))


def _lat_row_spec(width):
    return pl.BlockSpec((TM, width), lambda i: (jnp.maximum(i - N_CTX // TM, 0), 0))


def _full_spec(shape):
    return pl.BlockSpec(shape, lambda i: (0,) * len(shape))


def _mod_spec():
    return pl.BlockSpec((None, 6, D), lambda i: (_group_of_block(i, TM), 0, 0))


def _heads_spec(heads, width):
    return pl.BlockSpec((heads, TM, width), lambda i: (0, i, 0))


N_BLOCKS = T // TM
N_CTX_BLOCKS = N_CTX // TM


def _proj_block(i):
    return jnp.minimum(i, N_BLOCKS - 1)


def _epi_block(i):
    return jnp.maximum(i - 1, 0)


def _proj_row_spec(width):
    return pl.BlockSpec((TM, width), lambda i: (_proj_block(i), 0))


def _proj_mod_spec():
    return pl.BlockSpec((None, 6, D), lambda i: (_group_of_block(_proj_block(i), TM), 0, 0))


def _epi_row_spec(width):
    return pl.BlockSpec((TM, width), lambda i: (_epi_block(i), 0))


def _epi_heads_spec(heads, width):
    return pl.BlockSpec((heads, TM, width), lambda i: (0, _epi_block(i), 0))


def _epi_ctx_row_spec(width):
    return pl.BlockSpec((TM, width), lambda i: (jnp.minimum(_epi_block(i), N_CTX_BLOCKS - 1), 0))


def _epi_is_ctx():
    i = pl.program_id(0)
    return jnp.logical_and(i >= 1, i <= N_CTX_BLOCKS)


def _two_stage(step, proj_a, proj_b):
    i = pl.program_id(0)

    @pl.when(i == 0)
    def _():
        proj_b[...] = jnp.zeros(proj_b.shape, F32)

    @pl.when(i % 2 == 0)
    def _():
        step(proj_b, proj_a)

    @pl.when(i % 2 == 1)
    def _():
        step(proj_a, proj_b)


def _gqa_in_kernel(x_ref, mod_ref, g_ref, w_ref, gqk_ref, cos_ref, sin_ref,
                   q_ref, k_ref, v_ref, ks_ref, vs_ref, proj_a, proj_b):
    n_q = GQA_Q_HEADS * GQA_HD
    n_kv = GQA_KV_HEADS * GQA_HD

    def step(prev, cur):
        h = _norm_mod(x_ref[...], g_ref, mod_ref, 0)
        cur[...] = jnp.dot(h, w_ref[...], preferred_element_type=F32)
        c = cos_ref[...]
        s = sin_ref[...]
        scale = LOG2E * GQA_HD ** -0.5
        for hd in range(GQA_Q_HEADS):
            q = _rms(prev[:, hd * GQA_HD:(hd + 1) * GQA_HD], gqk_ref[0:1, :])
            q_ref[hd] = (_rope128(q, c, s) * scale).astype(BF16)
        ks = []
        for hd in range(GQA_KV_HEADS):
            lo = n_q + hd * GQA_HD
            ks.append(_rms(prev[:, lo:lo + GQA_HD], gqk_ref[1:2, :]))
            k_ref[hd] = _rope128(ks[-1], c, s).astype(BF16)
            lo = n_q + n_kv + hd * GQA_HD
            v_ref[hd] = prev[:, lo:lo + GQA_HD].astype(BF16)

        @pl.when(_epi_is_ctx())
        def _():
            for hd in range(GQA_KV_HEADS):
                ks_ref[:, hd * GQA_HD:(hd + 1) * GQA_HD] = ks[hd]
            vs_ref[...] = prev[:, n_q + n_kv:]

    _two_stage(step, proj_a, proj_b)


def _proj_scratch(width):
    return [pltpu.VMEM((TM, width), F32), pltpu.VMEM((TM, width), F32)]


def _gqa_in(x, mod, g1, w_in, g_qk, cos, sin):
    n_kv = GQA_KV_HEADS * GQA_HD
    return pl.pallas_call(
        _gqa_in_kernel,
        grid=(N_BLOCKS + 1,),
        in_specs=[_proj_row_spec(D), _proj_mod_spec(), _full_spec((1, D)), _full_spec(w_in.shape),
                  _full_spec((2, GQA_HD)), _epi_row_spec(LANES), _epi_row_spec(LANES)],
        out_specs=[_epi_heads_spec(GQA_Q_HEADS, GQA_HD), _epi_heads_spec(GQA_KV_HEADS, GQA_HD),
                   _epi_heads_spec(GQA_KV_HEADS, GQA_HD), _epi_ctx_row_spec(n_kv),
                   _epi_ctx_row_spec(n_kv)],
        scratch_shapes=_proj_scratch(w_in.shape[1]),
        out_shape=[jax.ShapeDtypeStruct((GQA_Q_HEADS, T, GQA_HD), BF16),
                   jax.ShapeDtypeStruct((GQA_KV_HEADS, T, GQA_HD), BF16),
                   jax.ShapeDtypeStruct((GQA_KV_HEADS, T, GQA_HD), BF16),
                   jax.ShapeDtypeStruct((N_CTX, n_kv), F32),
                   jax.ShapeDtypeStruct((N_CTX, n_kv), F32)],
        compiler_params=_cparams(("arbitrary",)),
        name="gqa_in",
    )(x, mod, g1, w_in, g_qk, cos, sin)


def _diff_in_kernel(x_ref, mod_ref, g_ref, w_ref, gqk_ref, cos_ref, sin_ref,
                    q_ref, k_ref, v_ref, ks_ref, vs_ref, proj_a, proj_b):
    n = DIFF_HEADS * 2 * DIFF_HD

    def step(prev, cur):
        h = _norm_mod(x_ref[...], g_ref, mod_ref, 0)
        cur[...] = jnp.dot(h, w_ref[...], preferred_element_type=F32)
        c = cos_ref[...]
        s = sin_ref[...]
        scale = LOG2E * DIFF_HD ** -0.5
        lane = lax.broadcasted_iota(jnp.int32, (TM, LANES), 1)
        low = lane < DIFF_HD
        ks = []
        for hd in range(DIFF_HEADS):
            sl = slice(hd * LANES, (hd + 1) * LANES)
            q = _rope64x2(_rms_halves(prev[:, sl], gqk_ref[0:1, :]), c, s) * scale
            q_ref[2 * hd] = jnp.where(low, q, 0.0).astype(BF16)
            q_ref[2 * hd + 1] = jnp.where(low, 0.0, q).astype(BF16)
            ks.append(_rms_halves(prev[:, n + hd * LANES:n + (hd + 1) * LANES], gqk_ref[1:2, :]))
            k_ref[hd] = _rope64x2(ks[-1], c, s).astype(BF16)
            v_ref[hd] = prev[:, 2 * n + hd * LANES:2 * n + (hd + 1) * LANES].astype(BF16)

        @pl.when(_epi_is_ctx())
        def _():
            for hd in range(DIFF_HEADS):
                ks_ref[:, hd * LANES:(hd + 1) * LANES] = ks[hd]
            vs_ref[...] = prev[:, 2 * n:]

    _two_stage(step, proj_a, proj_b)


def _diff_in(x, mod, g1, w_in, g_qk2, cos, sin):
    n = DIFF_HEADS * 2 * DIFF_HD
    return pl.pallas_call(
        _diff_in_kernel,
        grid=(N_BLOCKS + 1,),
        in_specs=[_proj_row_spec(D), _proj_mod_spec(), _full_spec((1, D)), _full_spec(w_in.shape),
                  _full_spec((2, LANES)), _epi_row_spec(LANES), _epi_row_spec(LANES)],
        out_specs=[_epi_heads_spec(2 * DIFF_HEADS, LANES), _epi_heads_spec(DIFF_HEADS, LANES),
                   _epi_heads_spec(DIFF_HEADS, LANES), _epi_ctx_row_spec(n), _epi_ctx_row_spec(n)],
        scratch_shapes=_proj_scratch(w_in.shape[1]),
        out_shape=[jax.ShapeDtypeStruct((2 * DIFF_HEADS, T, LANES), BF16),
                   jax.ShapeDtypeStruct((DIFF_HEADS, T, LANES), BF16),
                   jax.ShapeDtypeStruct((DIFF_HEADS, T, LANES), BF16),
                   jax.ShapeDtypeStruct((N_CTX, n), F32),
                   jax.ShapeDtypeStruct((N_CTX, n), F32)],
        compiler_params=_cparams(("arbitrary",)),
        name="diff_in",
    )(x, mod, g1, w_in, g_qk2, cos, sin)


def _mla_keys_values(ckv_bf16, kr2, w_ukv_ref, gk_ref, k_ref, v_ref):
    kv = jnp.dot(ckv_bf16, w_ukv_ref[...], preferred_element_type=F32)
    lane = lax.broadcasted_iota(jnp.int32, kr2.shape, 1)
    low = lane < MLA_ROPE
    kr_low = jnp.where(low, kr2, 0.0).astype(BF16)
    kr_high = jnp.where(low, 0.0, kr2).astype(BF16)
    width = MLA_NOPE + MLA_V
    for hd in range(MLA_HEADS):
        k_nope = _rms(kv[:, hd * width:hd * width + MLA_NOPE], gk_ref[...])
        k_ref[hd, :, 0:MLA_NOPE] = k_nope.astype(BF16)
        k_ref[hd, :, MLA_NOPE:MLA_DQ] = kr_low if hd % 2 == 0 else kr_high
        v_ref[hd] = kv[:, hd * width + MLA_NOPE:(hd + 1) * width].astype(BF16)


def _mla_in_kernel(xc_ref, xl_ref, mod_ref, g_ref, w_ref, gcq_ref, gckv_ref, gqn_ref, gqr_ref,
                   gkn_ref, gkr_ref, wqn_ref, wqr_ref, wukv_ref, cos_ref, sin_ref,
                   q_ref, k_ref, v_ref, ckv_ref, kr_ref, proj_a, proj_b):
    def step(prev, cur):
        x = jnp.where(_is_ctx_block(), xc_ref[...], xl_ref[...])
        h = _norm_mod(x, g_ref, mod_ref, 0)
        cur[...] = jnp.dot(h, w_ref[...], preferred_element_type=F32)
        c = cos_ref[...]
        s = sin_ref[...]
        cq = _rms(prev[:, 0:MLA_Q_RANK], gcq_ref[...]).astype(BF16)
        ckv = _rms(prev[:, MLA_Q_RANK:MLA_Q_RANK + MLA_KV_RANK], gckv_ref[...])
        kr2 = _rms_halves(prev[:, MLA_Q_RANK + MLA_KV_RANK:], gkr_ref[...])
        _mla_keys_values(ckv.astype(BF16), _rope64x2(kr2, c, s), wukv_ref, gkn_ref, k_ref, v_ref)

        scale = LOG2E * (MLA_NOPE + MLA_ROPE) ** -0.5
        qn = jnp.dot(cq, wqn_ref[...], preferred_element_type=F32)
        qr = jnp.dot(cq, wqr_ref[...], preferred_element_type=F32)
        lane = lax.broadcasted_iota(jnp.int32, (TM, LANES), 1)
        low = lane < MLA_ROPE
        for pair in range(MLA_HEADS // 2):
            r = _rms_halves(qr[:, pair * LANES:(pair + 1) * LANES], gqr_ref[...])
            r = _rope64x2(r, c, s) * scale
            q_ref[2 * pair, :, MLA_NOPE:MLA_DQ] = jnp.where(low, r, 0.0).astype(BF16)
            q_ref[2 * pair + 1, :, MLA_NOPE:MLA_DQ] = jnp.where(low, 0.0, r).astype(BF16)
        for hd in range(MLA_HEADS):
            q = _rms(qn[:, hd * MLA_NOPE:(hd + 1) * MLA_NOPE], gqn_ref[...]) * scale
            q_ref[hd, :, 0:MLA_NOPE] = q.astype(BF16)

        @pl.when(_epi_is_ctx())
        def _():
            ckv_ref[...] = ckv
            kr_ref[...] = kr2[:, 0:MLA_ROPE]

    _two_stage(step, proj_a, proj_b)


def _mla_in(x_ctx, x_lat, mod, g1, w_in, g_cq, g_ckv, gqn, gqr, gkn, gkr, wqn, wqr, wukv, cos, sin):
    n_lat_blocks = N_BLOCKS - N_CTX_BLOCKS
    return pl.pallas_call(
        _mla_in_kernel,
        grid=(N_BLOCKS + 1,),
        in_specs=[_ctx_row_spec(D),
                  pl.BlockSpec((TM, D), lambda i: (jnp.clip(i - N_CTX_BLOCKS, 0, n_lat_blocks - 1), 0)),
                  _proj_mod_spec(), _full_spec((1, D)), _full_spec(w_in.shape),
                  _full_spec((1, MLA_Q_RANK)), _full_spec((1, MLA_KV_RANK)),
                  _full_spec((1, LANES)), _full_spec((1, LANES)), _full_spec((1, LANES)),
                  _full_spec((1, LANES)), _full_spec(wqn.shape), _full_spec(wqr.shape),
                  _full_spec(wukv.shape), _epi_row_spec(LANES), _epi_row_spec(LANES)],
        out_specs=[_epi_heads_spec(MLA_HEADS, MLA_DQ), _epi_heads_spec(MLA_HEADS, MLA_DQ),
                   _epi_heads_spec(MLA_HEADS, MLA_V), _epi_ctx_row_spec(MLA_KV_RANK),
                   _epi_ctx_row_spec(MLA_ROPE)],
        scratch_shapes=_proj_scratch(w_in.shape[1]),
        out_shape=[jax.ShapeDtypeStruct((MLA_HEADS, T, MLA_DQ), BF16),
                   jax.ShapeDtypeStruct((MLA_HEADS, T, MLA_DQ), BF16),
                   jax.ShapeDtypeStruct((MLA_HEADS, T, MLA_V), BF16),
                   jax.ShapeDtypeStruct((N_CTX, MLA_KV_RANK), F32),
                   jax.ShapeDtypeStruct((N_CTX, MLA_ROPE), F32)],
        compiler_params=_cparams(("arbitrary",)),
        name="mla_in",
    )(x_ctx, x_lat, mod, g1, w_in, g_cq, g_ckv, gqn, gqr, gkn, gkr, wqn, wqr, wukv, cos, sin)


def _mla_cache_kernel(ckv_ref, kr_ref, wukv_ref, gkn_ref, k_ref, v_ref):
    kr = kr_ref[...]
    kr2 = jnp.concatenate([kr, kr], axis=-1)
    _mla_keys_values(ckv_ref[...].astype(BF16), kr2, wukv_ref, gkn_ref, k_ref, v_ref)


def _mla_cache(ckv, kr, wukv, gkn):
    rows = ckv.shape[0]
    return pl.pallas_call(
        _mla_cache_kernel,
        grid=(1,),
        in_specs=[_full_spec(ckv.shape), _full_spec(kr.shape), _full_spec(wukv.shape),
                  _full_spec((1, LANES))],
        out_specs=[_full_spec((MLA_HEADS, rows, MLA_DQ)), _full_spec((MLA_HEADS, rows, MLA_V))],
        out_shape=[jax.ShapeDtypeStruct((MLA_HEADS, rows, MLA_DQ), BF16),
                   jax.ShapeDtypeStruct((MLA_HEADS, rows, MLA_V), BF16)],
        compiler_params=_cparams(("arbitrary",)),
        name="mla_cache_kv",
    )(ckv, kr, wukv, gkn)


def _lru_in_kernel(x_ref, mod_ref, g_ref, w_ref, gate_ref, xr_ref):
    h = _norm_mod(x_ref[...], g_ref, mod_ref, 0)
    proj = jnp.dot(h, w_ref[...], preferred_element_type=F32)
    gate_ref[...] = proj[:, 0:D]
    xr_ref[...] = proj[:, D:2 * D]


def _lru_in(x, mod, g1, w_in):
    return pl.pallas_call(
        _lru_in_kernel,
        grid=(T // TM,),
        in_specs=[_row_spec(D), _mod_spec(), _full_spec((1, D)), _full_spec(w_in.shape)],
        out_specs=[_row_spec(D), _row_spec(D)],
        out_shape=[jax.ShapeDtypeStruct((T, D), F32), jax.ShapeDtypeStruct((T, D), F32)],
        compiler_params=_cparams(("arbitrary",)),
        name="lru_in",
    )(x, mod, g1, w_in)


def _softmax_pv(q, key_blocks):
    nt = (((1,), (1,)), ((), ()))
    m = l = acc = None
    for k, v in key_blocks:
        s = lax.dot_general(q, k, nt, preferred_element_type=F32)
        mb = jnp.max(s, axis=-1, keepdims=True)
        if m is None:
            m = mb
            p = jnp.exp2(s - m)
            l = jnp.sum(p, axis=-1, keepdims=True)
            acc = jnp.dot(p.astype(BF16), v, preferred_element_type=F32)
        else:
            m_new = jnp.maximum(m, mb)
            alpha = jnp.exp2(m - m_new)
            p = jnp.exp2(s - m_new)
            l = alpha * l + jnp.sum(p, axis=-1, keepdims=True)
            acc = alpha * acc + jnp.dot(p.astype(BF16), v, preferred_element_type=F32)
            m = m_new
    return acc / l


def _attn_kernel(*refs, heads, group, sub, bk, has_cache, diff, dv, lambda_init):
    it = iter(refs)
    q_ref, kn_ref, vn_ref = next(it), next(it), next(it)
    kc_ref = vc_ref = lam_ref = gsub_ref = None
    if has_cache:
        kc_ref, vc_ref = next(it), next(it)
    if diff:
        lam_ref, gsub_ref = next(it), next(it)
    o_ref = next(it)

    tq = q_ref.shape[1]
    seq_len = kn_ref.shape[1]
    stacked = min(group, heads)
    cached = []
    if has_cache:
        cached = [(kc_ref[...].astype(BF16), vc_ref[...].astype(BF16))]
    if diff:
        lam = lam_ref[...]
        lam_full = (jnp.exp(jnp.sum(lam[0:1] * lam[1:2], axis=-1, keepdims=True))
                    - jnp.exp(jnp.sum(lam[2:3] * lam[3:4], axis=-1, keepdims=True)) + lambda_init)

    for kv in range(max(heads // group, 1)):
        def body(i, carry, kv=kv):
            rows = pl.ds(pl.multiple_of(i * sub, sub), sub)
            q = jnp.concatenate([q_ref[kv * stacked + g, rows, :] for g in range(stacked)], axis=0)
            blocks = cached + [(kn_ref[kv, j * bk:(j + 1) * bk, :], vn_ref[kv, j * bk:(j + 1) * bk, :])
                               for j in range(seq_len // bk)]
            o = _softmax_pv(q, blocks)
            if diff:
                od = o[0:sub] - lam_full * o[sub:2 * sub]
                od = _rms(od, gsub_ref[...]) * (1.0 - lambda_init)
                o_ref[rows, kv * dv:(kv + 1) * dv] = od.astype(BF16)
            else:
                for g in range(stacked):
                    hd = kv * stacked + g
                    o_ref[rows, hd * dv:(hd + 1) * dv] = o[g * sub:(g + 1) * sub].astype(BF16)
            return carry
        lax.fori_loop(0, tq // sub, body, 0)


def _attention(q, kn, vn, *, seq_len, n_seq, row0, tq, sub, bk, heads_per_step, group, cache=None,
               diff=None, lambda_init=0.0):
    hq, _, dq = q.shape
    hkv, _, dv = vn.shape
    kv_per_step = max(heads_per_step // group, 1)
    n_hblk = hq // heads_per_step
    n_qblk = seq_len // tq
    seq0 = row0 // seq_len
    qblk0 = row0 // tq

    def q_map(b, hb, qb):
        return (hb, qblk0 + b * n_qblk + qb, 0)

    def kv_map(b, hb, qb):
        return ((hb * heads_per_step) // (group * kv_per_step), seq0 + b, 0)

    in_specs = [pl.BlockSpec((heads_per_step, tq, dq), q_map),
                pl.BlockSpec((kv_per_step, seq_len, dq), kv_map),
                pl.BlockSpec((kv_per_step, seq_len, dv), kv_map)]
    args = [q, kn, vn]
    if cache is not None:
        kc, vc, kc_block, kc_map, vc_block, vc_map = cache
        in_specs += [pl.BlockSpec(kc_block, kc_map), pl.BlockSpec(vc_block, vc_map)]
        args += [kc, vc]
    out_heads = heads_per_step
    if diff is not None:
        lam, gsub = diff
        in_specs += [pl.BlockSpec(lam.shape, lambda b, hb, qb: (0, 0)),
                     pl.BlockSpec(gsub.shape, lambda b, hb, qb: (0, 0))]
        args += [lam, gsub]
        out_heads = heads_per_step // 2
    n_out = (hq // 2 if diff is not None else hq) * dv
    kernel = functools.partial(_attn_kernel, heads=heads_per_step, group=group, sub=sub, bk=bk,
                               has_cache=cache is not None, diff=diff is not None, dv=dv,
                               lambda_init=lambda_init)
    return pl.pallas_call(
        kernel,
        grid=(n_seq, n_hblk, n_qblk),
        in_specs=in_specs,
        out_specs=pl.BlockSpec((tq, out_heads * dv), lambda b, hb, qb: (b * n_qblk + qb, hb)),
        out_shape=jax.ShapeDtypeStruct((n_seq * seq_len, n_out), BF16),
        compiler_params=_cparams(("arbitrary", "arbitrary", "arbitrary")),
        name="attention",
    )(*args)


def _lru_seq_pos(i):
    n_ctx_blocks = N_CTX // LRU_TB
    per_ctx = SEQ // LRU_TB
    per_lat = DEC_SEQ // LRU_TB
    pos = jnp.where(i < n_ctx_blocks, i % per_ctx, (i - n_ctx_blocks) % per_lat)
    length = jnp.where(i < n_ctx_blocks, per_ctx, per_lat)
    return pos == 0, pos == length - 1


def _lru_seq_of_block(i):
    n_ctx_blocks = N_CTX // LRU_TB
    return jnp.where(i < n_ctx_blocks, i // (SEQ // LRU_TB),
                     BATCH + (i - n_ctx_blocks) // (DEC_SEQ // LRU_TB))


LRU_PITCH = LRU_TB + SUBLANES


def _lru_scan_kernel(x_ref, prev_ref, next_ref, cw_ref, cb_ref, wg_ref, bg_ref, lam_ref, h0_ref,
                     *rest, reverse, combine):
    if combine:
        hf_ref, gate_ref, y_ref, st_ref, xpad, a_s, u_s, h_s, carry = rest
    else:
        y_ref, st_ref, xpad, a_s, u_s, h_s, carry = rest
    j = pl.program_id(0)
    i = (T // LRU_TB - 1 - j) if reverse else j
    first, last = _lru_seq_pos(i)
    starts = last if reverse else first
    ends = first if reverse else last

    xpad[0:SUBLANES, :] = jnp.where(first, 0.0, prev_ref[...])
    xpad[SUBLANES:SUBLANES + LRU_TB, :] = x_ref[...]
    xpad[SUBLANES + LRU_TB:, :] = jnp.where(last, 0.0, next_ref[...])
    xr = xpad[SUBLANES - 1:SUBLANES - 1 + LRU_TB, :] * cw_ref[0:1, :]
    for t in range(1, 4):
        xr = xr + xpad[SUBLANES - 1 + t:SUBLANES - 1 + t + LRU_TB, :] * cw_ref[t:t + 1, :]
    xr = xr + cb_ref[...]

    xr_b = xr.astype(BF16)
    lam = lam_ref[...]
    neg = -lam
    softplus = jnp.maximum(neg, 0.0) + jnp.log1p(jnp.exp(-jnp.abs(neg)))
    for n in range(LRU_BLOCKS):
        sl = slice(n * LRU_BLK, (n + 1) * LRU_BLK)
        g = jnp.dot(xr_b[:, sl], wg_ref[n], preferred_element_type=F32)
        r = jax.nn.sigmoid(g[:, 0:LRU_BLK] + bg_ref[0:1, sl])
        gi = jax.nn.sigmoid(g[:, LRU_BLK:] + bg_ref[1:2, sl])
        log_a = -LRU_C * r * softplus[:, sl]
        th = jnp.tanh(log_a)
        one_minus_a2 = -2.0 * th / (1.0 - th)
        rows_n = slice(n * LRU_PITCH, n * LRU_PITCH + LRU_TB)
        a_s[rows_n, :] = jnp.exp(log_a)
        root = jnp.where(one_minus_a2 > 0.0, one_minus_a2 * lax.rsqrt(one_minus_a2), 0.0)
        u_s[rows_n, :] = root * (gi * xr[:, sl])

    @pl.when(starts)
    def _():
        carry[...] = h0_ref[...]

    def body(t, h):
        tt = (LRU_TB - 1 - t) if reverse else t
        rows = pl.ds(tt, LRU_BLOCKS, stride=LRU_PITCH)
        h = a_s[rows, :] * h + u_s[rows, :]
        h_s[rows, :] = h
        return h

    h_end = lax.fori_loop(0, LRU_TB, body, carry[...], unroll=8)
    carry[...] = h_end

    for n in range(LRU_BLOCKS):
        sl = slice(n * LRU_BLK, (n + 1) * LRU_BLK)
        hs = h_s[n * LRU_PITCH:n * LRU_PITCH + LRU_TB, :]
        if combine:
            y_ref[:, sl] = ((hf_ref[:, sl] + hs) * jax.nn.gelu(gate_ref[:, sl])).astype(BF16)
        else:
            y_ref[:, sl] = hs

    @pl.when(ends)
    def _():
        st_ref[...] = h_end


def _lru_scan(xr, conv_w, conv_b, w_gate, b_gate, lam, h0, *, reverse, hf=None, gate=None):
    nb = T // LRU_TB
    hb = LRU_TB // SUBLANES
    n_halo = T // SUBLANES

    def blk(j):
        return (nb - 1 - j) if reverse else j

    in_specs = [
        pl.BlockSpec((LRU_TB, D), lambda j: (blk(j), 0)),
        pl.BlockSpec((SUBLANES, D), lambda j: (jnp.maximum(blk(j) * hb - 1, 0), 0)),
        pl.BlockSpec((SUBLANES, D), lambda j: (jnp.minimum((blk(j) + 1) * hb, n_halo - 1), 0)),
        pl.BlockSpec((4, D), lambda j: (0, 0)),
        pl.BlockSpec((1, D), lambda j: (0, 0)),
        pl.BlockSpec((LRU_BLOCKS, LRU_BLK, 2 * LRU_BLK), lambda j: (0, 0, 0)),
        pl.BlockSpec((2, D), lambda j: (0, 0)),
        pl.BlockSpec((1, D), lambda j: (0, 0)),
        pl.BlockSpec((None, LRU_BLOCKS, LRU_BLK), lambda j: (_lru_seq_of_block(blk(j)), 0, 0)),
    ]
    args = [xr, xr, xr, conv_w, conv_b, w_gate, b_gate, lam, h0]
    combine = hf is not None
    if combine:
        in_specs += [pl.BlockSpec((LRU_TB, D), lambda j: (blk(j), 0)),
                     pl.BlockSpec((LRU_TB, D), lambda j: (blk(j), 0))]
        args += [hf, gate]
    n_seq = BATCH + DEC_BATCH
    return pl.pallas_call(
        functools.partial(_lru_scan_kernel, reverse=reverse, combine=combine),
        grid=(nb,),
        in_specs=in_specs,
        out_specs=[pl.BlockSpec((LRU_TB, D), lambda j: (blk(j), 0)),
                   pl.BlockSpec((None, LRU_BLOCKS, LRU_BLK),
                                lambda j: (_lru_seq_of_block(blk(j)), 0, 0))],
        out_shape=[jax.ShapeDtypeStruct((T, D), BF16 if combine else F32),
                   jax.ShapeDtypeStruct((n_seq, LRU_BLOCKS, LRU_BLK), F32)],
        scratch_shapes=[pltpu.VMEM((LRU_TB + 2 * SUBLANES, D), F32),
                        pltpu.VMEM((LRU_BLOCKS * LRU_PITCH, LRU_BLK), F32),
                        pltpu.VMEM((LRU_BLOCKS * LRU_PITCH, LRU_BLK), F32),
                        pltpu.VMEM((LRU_BLOCKS * LRU_PITCH, LRU_BLK), F32),
                        pltpu.VMEM((LRU_BLOCKS, LRU_BLK), F32)],
        compiler_params=_cparams(("arbitrary",)),
        name="lru_scan_bwd" if reverse else "lru_scan_fwd",
    )(*args)


def _post_kernel(*refs, split_x, split_o, split_out):
    it = iter(refs)

    def rows(split):
        if split:
            return jnp.where(_is_ctx_block(), next(it)[...], next(it)[...])
        return next(it)[...]

    x = rows(split_x)
    o = rows(split_o)
    mod_ref, g2_ref, wo_ref, win_ref, wout_ref = (next(it) for _ in range(5))
    y_refs = list(it)
    x1 = x + mod_ref[2:3, :] * jnp.dot(o, wo_ref[...], preferred_element_type=F32)
    h = _rms(x1, g2_ref[...])
    h = (h * (1.0 + mod_ref[4:5, :]) + mod_ref[3:4, :]).astype(BF16)
    acc = jnp.zeros((TM, D), F32)
    for c in range(FFN_H // FFN_CHUNK):
        lo = c * FFN_CHUNK
        g = jnp.dot(h, win_ref[:, lo:lo + FFN_CHUNK], preferred_element_type=F32)
        u = jnp.dot(h, win_ref[:, FFN_H + lo:FFN_H + lo + FFN_CHUNK], preferred_element_type=F32)
        a = (jax.nn.silu(g) * u).astype(BF16)
        acc = acc + jnp.dot(a, wout_ref[lo:lo + FFN_CHUNK, :], preferred_element_type=F32)
    y = x1 + mod_ref[5:6, :] * acc
    if split_out:
        @pl.when(_is_ctx_block())
        def _():
            y_refs[0][...] = y

        @pl.when(jnp.logical_not(_is_ctx_block()))
        def _():
            y_refs[1][...] = y
    else:
        y_refs[0][...] = y


def _layer_weight_spec(shape, layer):
    return pl.BlockSpec((None,) + shape[1:], lambda i: (layer,) + (0,) * (len(shape) - 1),
                        pipeline_mode=pl.Buffered(1))


def _post(xs, os, mod, g2, w_o, w_in, w_out, layer, *, split_out):
    def row_specs(arrays):
        return [_ctx_row_spec(D), _lat_row_spec(D)] if len(arrays) == 2 else [_row_spec(D)]

    if split_out:
        out_specs = [_ctx_row_spec(D), _lat_row_spec(D)]
        out_shape = [jax.ShapeDtypeStruct((N_CTX, D), F32), jax.ShapeDtypeStruct((N_LAT, D), F32)]
    else:
        out_specs = [_row_spec(D)]
        out_shape = [jax.ShapeDtypeStruct((T, D), F32)]
    return pl.pallas_call(
        functools.partial(_post_kernel, split_x=len(xs) == 2, split_o=len(os) == 2,
                          split_out=split_out),
        grid=(T // TM,),
        in_specs=row_specs(xs) + row_specs(os) + [
            _mod_spec(), _full_spec((1, D)), _layer_weight_spec((1,) + w_o.shape, 0),
            _layer_weight_spec(w_in.shape, layer), _layer_weight_spec(w_out.shape, layer)],
        out_specs=out_specs,
        out_shape=out_shape,
        compiler_params=_cparams(("arbitrary",)),
        name="post_mixer_ffn",
    )(*xs, *os, mod, g2, w_o[None], w_in, w_out)


def _axial_tables(rot_dim):
    row = jnp.repeat(jnp.arange(DEC_SEQ // GRID_W), GRID_W).astype(F32)
    col = jnp.tile(jnp.arange(GRID_W), DEC_SEQ // GRID_W).astype(F32)
    n_freq = rot_dim // 4
    inv = ROPE_THETA ** (-jnp.arange(n_freq, dtype=F32) / n_freq)
    ang = jnp.concatenate([row[:, None] * inv, col[:, None] * inv], axis=-1)
    cos, sin = jnp.cos(ang), jnp.sin(ang)
    reps = LANES // rot_dim
    cos_t = jnp.tile(jnp.concatenate([cos, cos], axis=-1), (DEC_BATCH, reps))
    sin_t = jnp.tile(jnp.concatenate([-sin, sin], axis=-1), (DEC_BATCH, reps))
    cos_t = jnp.concatenate([jnp.ones((N_CTX, LANES), F32), cos_t], axis=0)
    sin_t = jnp.concatenate([jnp.zeros((N_CTX, LANES), F32), sin_t], axis=0)
    return cos_t, sin_t


def kernel(x_prompt, x_sample, cache_mla_ckv, cache_mla_krope, cache_diff_k, cache_diff_v, cache_gqa_k, cache_gqa_v, state_lru_h, c, c_ctx, w_mod, b_mod, g_norm1, g_norm2, w_ffn_in, w_ffn_out, mla_w_in, mla_g_cq, mla_g_ckv, mla_w_uq, mla_w_ukv, mla_g_qk, mla_w_o, diff_w_in, diff_g_qk, diff_lambda, diff_g_sub, diff_w_o, gqa_w_in, gqa_g_qk, gqa_w_o, lru_w_in, lru_conv_w, lru_conv_b, lru_w_gate, lru_b_gate, lru_lambda, lru_w_out):
    x_in = (x_prompt.reshape(N_CTX, D), x_sample.reshape(N_LAT, D))
    cond = jnp.concatenate([c_ctx[None, :], c, jnp.zeros((SUBLANES - N_GROUPS, D), F32)], axis=0)
    mod_all = _modulation(cond, w_mod, b_mod)
    cos128, sin128 = _axial_tables(GQA_HD)
    cos64, sin64 = _axial_tables(DIFF_HD)
    w_ffn_in_b = w_ffn_in.astype(BF16)
    w_ffn_out_b = w_ffn_out.astype(BF16)

    def layer_mod(l):
        return mod_all[l, :N_GROUPS].reshape(N_GROUPS, 6, D)

    def post(l, xs, os, w_o):
        out = _post(xs, os, layer_mod(l), g_norm2[l][None, :], w_o.astype(BF16),
                    w_ffn_in_b, w_ffn_out_b, l, split_out=l == DEPTH - 1)
        return out if l == DEPTH - 1 else out[0]

    def attend(q, k, v, *, hq, group, cache, diff, lambda_init):
        common = dict(group=group, diff=diff, lambda_init=lambda_init)
        sub = ATTN_ROWS // group
        o_ctx = _attention(q, k, v, seq_len=SEQ, n_seq=BATCH, row0=0, tq=SEQ, bk=SEQ,
                           sub=min(sub, SEQ), heads_per_step=hq, cache=None, **common)
        o_lat = _attention(q, k, v, seq_len=DEC_SEQ, n_seq=DEC_BATCH, row0=N_CTX, tq=ATTN_TQ,
                           bk=ATTN_BK, sub=sub, heads_per_step=group, cache=cache, **common)
        return o_ctx, o_lat

    l = 0
    w_in = mla_w_in[0]
    kr_cols = w_in[:, MLA_Q_RANK + MLA_KV_RANK:]
    w_in2 = jnp.concatenate([w_in, kr_cols], axis=1).astype(BF16)
    w_uq = mla_w_uq[0].reshape(MLA_Q_RANK, MLA_HEADS, MLA_NOPE + MLA_ROPE)
    wqn = w_uq[:, :, :MLA_NOPE].reshape(MLA_Q_RANK, MLA_HEADS * MLA_NOPE).astype(BF16)
    wqr = w_uq[:, :, MLA_NOPE:].reshape(MLA_Q_RANK, MLA_HEADS * MLA_ROPE).astype(BF16)
    wukv = mla_w_ukv[0].astype(BF16)
    gqk = mla_g_qk[0]
    gqn = gqk[0:1, :MLA_NOPE]
    gqr = jnp.tile(gqk[0:1, MLA_NOPE:], (1, 2))
    gkn = gqk[1:2, :MLA_NOPE]
    gkr = jnp.tile(gqk[1:2, MLA_NOPE:], (1, 2))
    q, k, v, ckv_new, kr_new = _mla_in(*x_in, layer_mod(l), g_norm1[l][None, :], w_in2,
                                       mla_g_cq[0][None, :], mla_g_ckv[0][None, :],
                                       gqn, gqr, gkn, gkr, wqn, wqr, wukv, cos64, sin64)
    kc, vc = _mla_cache(cache_mla_ckv[:, 0].reshape(DEC_BATCH * PAST, MLA_KV_RANK),
                        cache_mla_krope[:, 0].reshape(DEC_BATCH * PAST, MLA_ROPE), wukv, gkn)
    cache = (kc, vc,
             (None, PAST, MLA_DQ), lambda b, hb, qb: (hb, b, 0),
             (None, PAST, MLA_V), lambda b, hb, qb: (hb, b, 0))
    o = attend(q, k, v, hq=MLA_HEADS, group=1, cache=cache, diff=None, lambda_init=0.0)
    x = post(l, x_in, o, mla_w_o[0])
    new_mla_ckv = ckv_new.reshape(BATCH, 1, SEQ, MLA_KV_RANK)
    new_mla_krope = kr_new.reshape(BATCH, 1, SEQ, MLA_ROPE)

    l = 1
    lambda_init = 0.8 - 0.6 * math.exp(-0.3 * l)
    q, k, v, k_new, v_new = _diff_in(x, layer_mod(l), g_norm1[l][None, :], diff_w_in[0].astype(BF16),
                                     jnp.tile(diff_g_qk[0], (1, 2)), cos64, sin64)
    n_diff = DIFF_HEADS * 2 * DIFF_HD
    cache = (cache_diff_k[:, 0].reshape(DEC_BATCH, PAST, n_diff),
             cache_diff_v[:, 0].reshape(DEC_BATCH, PAST, n_diff),
             (None, PAST, LANES), lambda b, hb, qb: (b, 0, hb),
             (None, PAST, LANES), lambda b, hb, qb: (b, 0, hb))
    o = attend(q, k, v, hq=2 * DIFF_HEADS, group=2, cache=cache,
               diff=(diff_lambda[0], diff_g_sub[0][None, :]), lambda_init=lambda_init)
    x = post(l, (x,), o, diff_w_o[0])
    new_diff_k = k_new.reshape(BATCH, 1, SEQ, DIFF_HEADS, 2, DIFF_HD)
    new_diff_v = v_new.reshape(BATCH, 1, SEQ, DIFF_HEADS, 2 * DIFF_HD)

    l = 2
    q, k, v, k_new, v_new = _gqa_in(x, layer_mod(l), g_norm1[l][None, :], gqa_w_in[0].astype(BF16),
                                    gqa_g_qk[0], cos128, sin128)
    n_kv = GQA_KV_HEADS * GQA_HD
    group = GQA_Q_HEADS // GQA_KV_HEADS
    cache = (cache_gqa_k[:, 0].reshape(DEC_BATCH, PAST, n_kv),
             cache_gqa_v[:, 0].reshape(DEC_BATCH, PAST, n_kv),
             (None, PAST, GQA_HD), lambda b, hb, qb: (b, 0, hb),
             (None, PAST, GQA_HD), lambda b, hb, qb: (b, 0, hb))
    o = attend(q, k, v, hq=GQA_Q_HEADS, group=group, cache=cache, diff=None, lambda_init=0.0)
    x = post(l, (x,), o, gqa_w_o[0])
    new_gqa_k = k_new.reshape(BATCH, 1, SEQ, GQA_KV_HEADS, GQA_HD)
    new_gqa_v = v_new.reshape(BATCH, 1, SEQ, GQA_KV_HEADS, GQA_HD)

    l = 3
    gate, xr = _lru_in(x, layer_mod(l), g_norm1[l][None, :], lru_w_in[0].astype(BF16))
    wg = lru_w_gate[0]
    wg = jnp.concatenate([wg[:, 0], wg[:, 1]], axis=-1).astype(BF16)
    h0 = jnp.concatenate([jnp.zeros((BATCH, 2, D), F32), state_lru_h[:, 0]], axis=0)
    h0 = h0.reshape(BATCH + DEC_BATCH, 2, LRU_BLOCKS, LRU_BLK)
    conv_b = lru_conv_b[0][None, :]
    hf, st_f = _lru_scan(xr, lru_conv_w[0], conv_b, wg[0], lru_b_gate[0, 0], lru_lambda[0, 0][None, :],
                         h0[:, 0], reverse=False)
    y, st_b = _lru_scan(xr, lru_conv_w[0], conv_b, wg[1], lru_b_gate[0, 1], lru_lambda[0, 1][None, :],
                        h0[:, 1], reverse=True, hf=hf, gate=gate)
    y_ctx, y_lat = post(l, (x,), (y,), lru_w_out[0])
    new_lru_h = jnp.stack([st_f[:BATCH].reshape(BATCH, D), st_b[:BATCH].reshape(BATCH, D)],
                          axis=1)[:, None]

    y_prompt = y_ctx.reshape(BATCH, SEQ, D)
    y_sample = y_lat.reshape(DEC_BATCH, DEC_SEQ, D)
    return (y_prompt, y_sample, new_mla_ckv, new_mla_krope, new_diff_k, new_diff_v,
            new_gqa_k, new_gqa_v, new_lru_h)
```

```python
import functools
import math

import jax
import jax.numpy as jnp
from jax import lax
from jax.experimental import pallas as pl
from jax.experimental.pallas import tpu as pltpu

F32 = jnp.float32
BF16 = jnp.bfloat16

D = 1024
BATCH = 32
SEQ = 256
DEPTH = 4
DEC_BATCH = 2
DEC_SEQ = 4096
PAST = 256
GRID_W = 64
EPS = 1e-6
ROPE_THETA = 10000.0
FFN_H = 2816
N_CTX = BATCH * SEQ
N_LAT = DEC_BATCH * DEC_SEQ
T = N_CTX + N_LAT
N_GROUPS = 1 + DEC_BATCH

MLA_HEADS = 8
MLA_NOPE = 128
MLA_ROPE = 64
MLA_V = 128
MLA_Q_RANK = 384
MLA_KV_RANK = 256
MLA_DQ = 256
DIFF_HD = 64
DIFF_HEADS = 8
GQA_HD = 128
GQA_Q_HEADS = 8
GQA_KV_HEADS = 2
LRU_BLOCKS = 8
LRU_BLK = 128
LRU_C = 8.0

LANES = 128
SUBLANES = 8
VMEM_LIMIT = 56 * 1024 * 1024

TM = 512
FFN_CHUNK = 256
LRU_TB = 256
ATTN_TQ = 1024
ATTN_BK = 1024
ATTN_ROWS = 1024
LOG2E = 1.4426950408889634


def _cparams(sem):
    return pltpu.CompilerParams(dimension_semantics=sem, vmem_limit_bytes=VMEM_LIMIT)


def _group_of_block(i, rows_per_block):
    n_ctx_blocks = N_CTX // rows_per_block
    per = DEC_SEQ // rows_per_block
    return jnp.maximum(i - (n_ctx_blocks - per), 0) // per


def _rms(x, gain):
    y = x * lax.rsqrt(jnp.mean(x * x, axis=-1, keepdims=True) + EPS)
    return y * gain


def _rms_halves(x, gain):
    lane = lax.broadcasted_iota(jnp.int32, x.shape, 1)
    low = lane < 64
    sq = x * x
    s_low = jnp.sum(jnp.where(low, sq, 0.0), axis=-1, keepdims=True)
    s_all = jnp.sum(sq, axis=-1, keepdims=True)
    ms = jnp.where(low, s_low, s_all - s_low) * (1.0 / 64.0)
    return x * lax.rsqrt(ms + EPS) * gain


def _rope128(x, c, s):
    return x * c + pltpu.roll(x, 64, 1) * s


def _rope64x2(x, c, s):
    lane = lax.broadcasted_iota(jnp.int32, x.shape, 1)
    low = (lane % 64) < 32
    partner = jnp.where(low, pltpu.roll(x, 96, 1), pltpu.roll(x, 32, 1))
    return x * c + partner * s


def _is_ctx_block():
    return pl.program_id(0) < N_CTX // TM


def _norm_mod(x, g_ref, mod_ref, shift_row):
    y = _rms(x, g_ref[...])
    shift = mod_ref[shift_row:shift_row + 1, :]
    scale = mod_ref[shift_row + 1:shift_row + 2, :]
    return (y * (1.0 + scale) + shift).astype(BF16)


def _mod_kernel(cond_ref, w_ref, b_ref, o_ref):
    s = jax.nn.silu(cond_ref[...]).astype(BF16)
    w = w_ref[...].astype(BF16)
    o_ref[...] = jnp.dot(s, w, preferred_element_type=F32) + b_ref[...]


def _modulation(cond, w_mod, b_mod):
    tn = 1536
    return pl.pallas_call(
        _mod_kernel,
        grid=(DEPTH, 6 * D // tn),
        in_specs=[
            pl.BlockSpec((SUBLANES, D), lambda l, j: (0, 0)),
            pl.BlockSpec((None, D, tn), lambda l, j: (l, 0, j)),
            pl.BlockSpec((None, 1, tn), lambda l, j: (l, 0, j)),
        ],
        out_specs=pl.BlockSpec((None, SUBLANES, tn), lambda l, j: (l, 0, j)),
        out_shape=jax.ShapeDtypeStruct((DEPTH, SUBLANES, 6 * D), F32),
        compiler_params=_cparams(("arbitrary", "arbitrary")),
        name="adaln_mod",
    )(cond, w_mod, b_mod.reshape(DEPTH, 1, 6 * D))


def _row_spec(width):
    return pl.BlockSpec((TM, width), lambda i: (i, 0))


def _ctx_row_spec(width):
    return pl.BlockSpec((TM, width), lambda i: (jnp.minimum(i, N_CTX // TM - 1), 0))


def _lat_row_spec(width):
    return pl.BlockSpec((TM, width), lambda i: (jnp.maximum(i - N_CTX // TM, 0), 0))


def _full_spec(shape):
    return pl.BlockSpec(shape, lambda i: (0,) * len(shape))


def _mod_spec():
    return pl.BlockSpec((None, 6, D), lambda i: (_group_of_block(i, TM), 0, 0))


def _heads_spec(heads, width):
    return pl.BlockSpec((heads, TM, width), lambda i: (0, i, 0))


N_BLOCKS = T // TM
N_CTX_BLOCKS = N_CTX // TM


def _proj_block(i):
    return jnp.minimum(i, N_BLOCKS - 1)


def _epi_block(i):
    return jnp.maximum(i - 1, 0)


def _proj_row_spec(width):
    return pl.BlockSpec((TM, width), lambda i: (_proj_block(i), 0))


def _proj_mod_spec():
    return pl.BlockSpec((None, 6, D), lambda i: (_group_of_block(_proj_block(i), TM), 0, 0))


def _epi_row_spec(width):
    return pl.BlockSpec((TM, width), lambda i: (_epi_block(i), 0))


def _epi_heads_spec(heads, width):
    return pl.BlockSpec((heads, TM, width), lambda i: (0, _epi_block(i), 0))


def _epi_ctx_row_spec(width):
    return pl.BlockSpec((TM, width), lambda i: (jnp.minimum(_epi_block(i), N_CTX_BLOCKS - 1), 0))


def _epi_is_ctx():
    i = pl.program_id(0)
    return jnp.logical_and(i >= 1, i <= N_CTX_BLOCKS)


def _two_stage(step, proj_a, proj_b):
    i = pl.program_id(0)

    @pl.when(i == 0)
    def _():
        proj_b[...] = jnp.zeros(proj_b.shape, F32)

    @pl.when(i % 2 == 0)
    def _():
        step(proj_b, proj_a)

    @pl.when(i % 2 == 1)
    def _():
        step(proj_a, proj_b)


def _gqa_in_kernel(x_ref, mod_ref, g_ref, w_ref, gqk_ref, cos_ref, sin_ref,
                   q_ref, k_ref, v_ref, ks_ref, vs_ref, proj_a, proj_b):
    n_q = GQA_Q_HEADS * GQA_HD
    n_kv = GQA_KV_HEADS * GQA_HD

    def step(prev, cur):
        h = _norm_mod(x_ref[...], g_ref, mod_ref, 0)
        cur[...] = jnp.dot(h, w_ref[...], preferred_element_type=F32)
        c = cos_ref[...]
        s = sin_ref[...]
        scale = LOG2E * GQA_HD ** -0.5
        for hd in range(GQA_Q_HEADS):
            q = _rms(prev[:, hd * GQA_HD:(hd + 1) * GQA_HD], gqk_ref[0:1, :])
            q_ref[hd] = (_rope128(q, c, s) * scale).astype(BF16)
        ks = []
        for hd in range(GQA_KV_HEADS):
            lo = n_q + hd * GQA_HD
            ks.append(_rms(prev[:, lo:lo + GQA_HD], gqk_ref[1:2, :]))
            k_ref[hd] = _rope128(ks[-1], c, s).astype(BF16)
            lo = n_q + n_kv + hd * GQA_HD
            v_ref[hd] = _with_ones(prev[:, lo:lo + GQA_HD].astype(BF16))

        @pl.when(_epi_is_ctx())
        def _():
            for hd in range(GQA_KV_HEADS):
                ks_ref[:, hd * GQA_HD:(hd + 1) * GQA_HD] = ks[hd]
            vs_ref[...] = prev[:, n_q + n_kv:]

    _two_stage(step, proj_a, proj_b)


def _proj_scratch(width):
    return [pltpu.VMEM((TM, width), F32), pltpu.VMEM((TM, width), F32)]


def _gqa_in(x, mod, g1, w_in, g_qk, cos, sin):
    n_kv = GQA_KV_HEADS * GQA_HD
    return pl.pallas_call(
        _gqa_in_kernel,
        grid=(N_BLOCKS + 1,),
        in_specs=[_proj_row_spec(D), _proj_mod_spec(), _full_spec((1, D)), _full_spec(w_in.shape),
                  _full_spec((2, GQA_HD)), _epi_row_spec(LANES), _epi_row_spec(LANES)],
        out_specs=[_epi_heads_spec(GQA_Q_HEADS, GQA_HD), _epi_heads_spec(GQA_KV_HEADS, GQA_HD),
                   _epi_heads_spec(GQA_KV_HEADS, 2 * GQA_HD), _epi_ctx_row_spec(n_kv),
                   _epi_ctx_row_spec(n_kv)],
        scratch_shapes=_proj_scratch(w_in.shape[1]),
        out_shape=[jax.ShapeDtypeStruct((GQA_Q_HEADS, T, GQA_HD), BF16),
                   jax.ShapeDtypeStruct((GQA_KV_HEADS, T, GQA_HD), BF16),
                   jax.ShapeDtypeStruct((GQA_KV_HEADS, T, 2 * GQA_HD), BF16),
                   jax.ShapeDtypeStruct((N_CTX, n_kv), F32),
                   jax.ShapeDtypeStruct((N_CTX, n_kv), F32)],
        compiler_params=_cparams(("arbitrary",)),
        name="gqa_in",
    )(x, mod, g1, w_in, g_qk, cos, sin)


def _diff_in_kernel(x_ref, mod_ref, g_ref, w_ref, gqk_ref, cos_ref, sin_ref,
                    q_ref, k_ref, v_ref, ks_ref, vs_ref, proj_a, proj_b):
    n = DIFF_HEADS * 2 * DIFF_HD

    def step(prev, cur):
        h = _norm_mod(x_ref[...], g_ref, mod_ref, 0)
        cur[...] = jnp.dot(h, w_ref[...], preferred_element_type=F32)
        c = cos_ref[...]
        s = sin_ref[...]
        scale = LOG2E * DIFF_HD ** -0.5
        lane = lax.broadcasted_iota(jnp.int32, (TM, LANES), 1)
        low = lane < DIFF_HD
        ks = []
        for hd in range(DIFF_HEADS):
            sl = slice(hd * LANES, (hd + 1) * LANES)
            q = _rope64x2(_rms_halves(prev[:, sl], gqk_ref[0:1, :]), c, s) * scale
            q_ref[2 * hd] = jnp.where(low, q, 0.0).astype(BF16)
            q_ref[2 * hd + 1] = jnp.where(low, 0.0, q).astype(BF16)
            ks.append(_rms_halves(prev[:, n + hd * LANES:n + (hd + 1) * LANES], gqk_ref[1:2, :]))
            k_ref[hd] = _rope64x2(ks[-1], c, s).astype(BF16)
            v_ref[hd] = _with_ones(prev[:, 2 * n + hd * LANES:2 * n + (hd + 1) * LANES].astype(BF16))

        @pl.when(_epi_is_ctx())
        def _():
            for hd in range(DIFF_HEADS):
                ks_ref[:, hd * LANES:(hd + 1) * LANES] = ks[hd]
            vs_ref[...] = prev[:, 2 * n:]

    _two_stage(step, proj_a, proj_b)


def _diff_in(x, mod, g1, w_in, g_qk2, cos, sin):
    n = DIFF_HEADS * 2 * DIFF_HD
    return pl.pallas_call(
        _diff_in_kernel,
        grid=(N_BLOCKS + 1,),
        in_specs=[_proj_row_spec(D), _proj_mod_spec(), _full_spec((1, D)), _full_spec(w_in.shape),
                  _full_spec((2, LANES)), _epi_row_spec(LANES), _epi_row_spec(LANES)],
        out_specs=[_epi_heads_spec(2 * DIFF_HEADS, LANES), _epi_heads_spec(DIFF_HEADS, LANES),
                   _epi_heads_spec(DIFF_HEADS, 2 * LANES), _epi_ctx_row_spec(n), _epi_ctx_row_spec(n)],
        scratch_shapes=_proj_scratch(w_in.shape[1]),
        out_shape=[jax.ShapeDtypeStruct((2 * DIFF_HEADS, T, LANES), BF16),
                   jax.ShapeDtypeStruct((DIFF_HEADS, T, LANES), BF16),
                   jax.ShapeDtypeStruct((DIFF_HEADS, T, 2 * LANES), BF16),
                   jax.ShapeDtypeStruct((N_CTX, n), F32),
                   jax.ShapeDtypeStruct((N_CTX, n), F32)],
        compiler_params=_cparams(("arbitrary",)),
        name="diff_in",
    )(x, mod, g1, w_in, g_qk2, cos, sin)


def _mla_keys_values(ckv_bf16, kr2, w_ukv_ref, gk_ref, k_ref, v_ref):
    kv = jnp.dot(ckv_bf16, w_ukv_ref[...], preferred_element_type=F32)
    lane = lax.broadcasted_iota(jnp.int32, kr2.shape, 1)
    low = lane < MLA_ROPE
    kr_low = jnp.where(low, kr2, 0.0).astype(BF16)
    kr_high = jnp.where(low, 0.0, kr2).astype(BF16)
    width = MLA_NOPE + MLA_V
    for hd in range(MLA_HEADS):
        k_nope = _rms(kv[:, hd * width:hd * width + MLA_NOPE], gk_ref[...])
        k_ref[hd, :, 0:MLA_NOPE] = k_nope.astype(BF16)
        k_ref[hd, :, MLA_NOPE:MLA_DQ] = kr_low if hd % 2 == 0 else kr_high
        v_ref[hd] = _with_ones(kv[:, hd * width + MLA_NOPE:(hd + 1) * width].astype(BF16))


def _mla_in_kernel(xc_ref, xl_ref, mod_ref, g_ref, w_ref, gcq_ref, gckv_ref, gqn_ref, gqr_ref,
                   gkn_ref, gkr_ref, wqn_ref, wqr_ref, wukv_ref, cos_ref, sin_ref,
                   q_ref, k_ref, v_ref, ckv_ref, kr_ref, proj_a, proj_b):
    def step(prev, cur):
        x = jnp.where(_is_ctx_block(), xc_ref[...], xl_ref[...])
        h = _norm_mod(x, g_ref, mod_ref, 0)
        cur[...] = jnp.dot(h, w_ref[...], preferred_element_type=F32)
        c = cos_ref[...]
        s = sin_ref[...]
        cq = _rms(prev[:, 0:MLA_Q_RANK], gcq_ref[...]).astype(BF16)
        ckv = _rms(prev[:, MLA_Q_RANK:MLA_Q_RANK + MLA_KV_RANK], gckv_ref[...])
        kr2 = _rms_halves(prev[:, MLA_Q_RANK + MLA_KV_RANK:], gkr_ref[...])
        _mla_keys_values(ckv.astype(BF16), _rope64x2(kr2, c, s), wukv_ref, gkn_ref, k_ref, v_ref)

        scale = LOG2E * (MLA_NOPE + MLA_ROPE) ** -0.5
        qn = jnp.dot(cq, wqn_ref[...], preferred_element_type=F32)
        qr = jnp.dot(cq, wqr_ref[...], preferred_element_type=F32)
        lane = lax.broadcasted_iota(jnp.int32, (TM, LANES), 1)
        low = lane < MLA_ROPE
        for pair in range(MLA_HEADS // 2):
            r = _rms_halves(qr[:, pair * LANES:(pair + 1) * LANES], gqr_ref[...])
            r = _rope64x2(r, c, s) * scale
            q_ref[2 * pair, :, MLA_NOPE:MLA_DQ] = jnp.where(low, r, 0.0).astype(BF16)
            q_ref[2 * pair + 1, :, MLA_NOPE:MLA_DQ] = jnp.where(low, 0.0, r).astype(BF16)
        for hd in range(MLA_HEADS):
            q = _rms(qn[:, hd * MLA_NOPE:(hd + 1) * MLA_NOPE], gqn_ref[...]) * scale
            q_ref[hd, :, 0:MLA_NOPE] = q.astype(BF16)

        @pl.when(_epi_is_ctx())
        def _():
            ckv_ref[...] = ckv
            kr_ref[...] = kr2[:, 0:MLA_ROPE]

    _two_stage(step, proj_a, proj_b)


def _mla_in(x_ctx, x_lat, mod, g1, w_in, g_cq, g_ckv, gqn, gqr, gkn, gkr, wqn, wqr, wukv, cos, sin):
    n_lat_blocks = N_BLOCKS - N_CTX_BLOCKS
    return pl.pallas_call(
        _mla_in_kernel,
        grid=(N_BLOCKS + 1,),
        in_specs=[_ctx_row_spec(D),
                  pl.BlockSpec((TM, D), lambda i: (jnp.clip(i - N_CTX_BLOCKS, 0, n_lat_blocks - 1), 0)),
                  _proj_mod_spec(), _full_spec((1, D)), _full_spec(w_in.shape),
                  _full_spec((1, MLA_Q_RANK)), _full_spec((1, MLA_KV_RANK)),
                  _full_spec((1, LANES)), _full_spec((1, LANES)), _full_spec((1, LANES)),
                  _full_spec((1, LANES)), _full_spec(wqn.shape), _full_spec(wqr.shape),
                  _full_spec(wukv.shape), _epi_row_spec(LANES), _epi_row_spec(LANES)],
        out_specs=[_epi_heads_spec(MLA_HEADS, MLA_DQ), _epi_heads_spec(MLA_HEADS, MLA_DQ),
                   _epi_heads_spec(MLA_HEADS, 2 * MLA_V), _epi_ctx_row_spec(MLA_KV_RANK),
                   _epi_ctx_row_spec(MLA_ROPE)],
        scratch_shapes=_proj_scratch(w_in.shape[1]),
        out_shape=[jax.ShapeDtypeStruct((MLA_HEADS, T, MLA_DQ), BF16),
                   jax.ShapeDtypeStruct((MLA_HEADS, T, MLA_DQ), BF16),
                   jax.ShapeDtypeStruct((MLA_HEADS, T, 2 * MLA_V), BF16),
                   jax.ShapeDtypeStruct((N_CTX, MLA_KV_RANK), F32),
                   jax.ShapeDtypeStruct((N_CTX, MLA_ROPE), F32)],
        compiler_params=_cparams(("arbitrary",)),
        name="mla_in",
    )(x_ctx, x_lat, mod, g1, w_in, g_cq, g_ckv, gqn, gqr, gkn, gkr, wqn, wqr, wukv, cos, sin)


def _mla_cache_kernel(ckv_ref, kr_ref, wukv_ref, gkn_ref, k_ref, v_ref):
    kr = kr_ref[...]
    kr2 = jnp.concatenate([kr, kr], axis=-1)
    _mla_keys_values(ckv_ref[...].astype(BF16), kr2, wukv_ref, gkn_ref, k_ref, v_ref)


def _mla_cache(ckv, kr, wukv, gkn):
    rows = ckv.shape[0]
    return pl.pallas_call(
        _mla_cache_kernel,
        grid=(1,),
        in_specs=[_full_spec(ckv.shape), _full_spec(kr.shape), _full_spec(wukv.shape),
                  _full_spec((1, LANES))],
        out_specs=[_full_spec((MLA_HEADS, rows, MLA_DQ)), _full_spec((MLA_HEADS, rows, 2 * MLA_V))],
        out_shape=[jax.ShapeDtypeStruct((MLA_HEADS, rows, MLA_DQ), BF16),
                   jax.ShapeDtypeStruct((MLA_HEADS, rows, 2 * MLA_V), BF16)],
        compiler_params=_cparams(("arbitrary",)),
        name="mla_cache_kv",
    )(ckv, kr, wukv, gkn)


def _lru_in_kernel(x_ref, mod_ref, g_ref, w_ref, gate_ref, xr_ref):
    h = _norm_mod(x_ref[...], g_ref, mod_ref, 0)
    proj = jnp.dot(h, w_ref[...], preferred_element_type=F32)
    gate_ref[...] = proj[:, 0:D]
    xr_ref[...] = proj[:, D:2 * D]


def _lru_in(x, mod, g1, w_in):
    return pl.pallas_call(
        _lru_in_kernel,
        grid=(T // TM,),
        in_specs=[_row_spec(D), _mod_spec(), _full_spec((1, D)), _full_spec(w_in.shape)],
        out_specs=[_row_spec(D), _row_spec(D)],
        out_shape=[jax.ShapeDtypeStruct((T, D), F32), jax.ShapeDtypeStruct((T, D), F32)],
        compiler_params=_cparams(("arbitrary",)),
        name="lru_in",
    )(x, mod, g1, w_in)


def _with_ones(v):
    return jnp.concatenate([v, jnp.ones_like(v)], axis=-1)


def _softmax_pv(q, key_blocks, dv):
    nt = (((1,), (1,)), ((), ()))
    m = acc = None
    for k, v1 in key_blocks:
        s = lax.dot_general(q, k, nt, preferred_element_type=F32)
        mb = jnp.max(s, axis=-1, keepdims=True)
        if m is None:
            m = mb
            acc = jnp.dot(jnp.exp2(s - m).astype(BF16), v1, preferred_element_type=F32)
        else:
            m_new = jnp.maximum(m, mb)
            alpha = jnp.exp2(m - m_new)
            acc = alpha * acc + jnp.dot(jnp.exp2(s - m_new).astype(BF16), v1,
                                        preferred_element_type=F32)
            m = m_new
    return acc[:, 0:dv] / acc[:, dv:2 * dv]


def _attn_kernel(*refs, heads, group, sub, bk, has_cache, diff, dv, lambda_init):
    it = iter(refs)
    q_ref, kn_ref, vn_ref = next(it), next(it), next(it)
    kc_ref = vc_ref = lam_ref = gsub_ref = None
    if has_cache:
        kc_ref, vc_ref = next(it), next(it)
    if diff:
        lam_ref, gsub_ref = next(it), next(it)
    o_ref = next(it)

    tq = q_ref.shape[1]
    seq_len = kn_ref.shape[1]
    stacked = min(group, heads)
    cached = []
    if has_cache:
        vc = vc_ref[...].astype(BF16)
        cached = [(kc_ref[...].astype(BF16), vc if vc.shape[-1] == 2 * dv else _with_ones(vc))]
    if diff:
        lam = lam_ref[...]
        lam_full = (jnp.exp(jnp.sum(lam[0:1] * lam[1:2], axis=-1, keepdims=True))
                    - jnp.exp(jnp.sum(lam[2:3] * lam[3:4], axis=-1, keepdims=True)) + lambda_init)

    for kv in range(max(heads // group, 1)):
        def body(i, carry, kv=kv):
            rows = pl.ds(pl.multiple_of(i * sub, sub), sub)
            q = jnp.concatenate([q_ref[kv * stacked + g, rows, :] for g in range(stacked)], axis=0)
            blocks = cached + [(kn_ref[kv, j * bk:(j + 1) * bk, :], vn_ref[kv, j * bk:(j + 1) * bk, :])
                               for j in range(seq_len // bk)]
            o = _softmax_pv(q, blocks, dv)
            if diff:
                od = o[0:sub] - lam_full * o[sub:2 * sub]
                od = _rms(od, gsub_ref[...]) * (1.0 - lambda_init)
                o_ref[rows, kv * dv:(kv + 1) * dv] = od.astype(BF16)
            else:
                for g in range(stacked):
                    hd = kv * stacked + g
                    o_ref[rows, hd * dv:(hd + 1) * dv] = o[g * sub:(g + 1) * sub].astype(BF16)
            return carry
        lax.fori_loop(0, tq // sub, body, 0, unroll=True)


def _attention(q, kn, vn, *, seq_len, n_seq, row0, tq, sub, bk, heads_per_step, group, cache=None,
               diff=None, lambda_init=0.0):
    hq, _, dq = q.shape
    hkv, _, dv1 = vn.shape
    dv = dv1 // 2
    kv_per_step = max(heads_per_step // group, 1)
    n_hblk = hq // heads_per_step
    n_qblk = seq_len // tq
    seq0 = row0 // seq_len
    qblk0 = row0 // tq

    def q_map(b, hb, qb):
        return (hb, qblk0 + b * n_qblk + qb, 0)

    def kv_map(b, hb, qb):
        return ((hb * heads_per_step) // (group * kv_per_step), seq0 + b, 0)

    in_specs = [pl.BlockSpec((heads_per_step, tq, dq), q_map),
                pl.BlockSpec((kv_per_step, seq_len, dq), kv_map),
                pl.BlockSpec((kv_per_step, seq_len, dv1), kv_map)]
    args = [q, kn, vn]
    if cache is not None:
        kc, vc, kc_block, kc_map, vc_block, vc_map = cache
        in_specs += [pl.BlockSpec(kc_block, kc_map), pl.BlockSpec(vc_block, vc_map)]
        args += [kc, vc]
    out_heads = heads_per_step
    if diff is not None:
        lam, gsub = diff
        in_specs += [pl.BlockSpec(lam.shape, lambda b, hb, qb: (0, 0)),
                     pl.BlockSpec(gsub.shape, lambda b, hb, qb: (0, 0))]
        args += [lam, gsub]
        out_heads = heads_per_step // 2
    n_out = (hq // 2 if diff is not None else hq) * dv
    kernel = functools.partial(_attn_kernel, heads=heads_per_step, group=group, sub=sub, bk=bk,
                               has_cache=cache is not None, diff=diff is not None, dv=dv,
                               lambda_init=lambda_init)
    return pl.pallas_call(
        kernel,
        grid=(n_seq, n_hblk, n_qblk),
        in_specs=in_specs,
        out_specs=pl.BlockSpec((tq, out_heads * dv), lambda b, hb, qb: (b * n_qblk + qb, hb)),
        out_shape=jax.ShapeDtypeStruct((n_seq * seq_len, n_out), BF16),
        compiler_params=_cparams(("arbitrary", "arbitrary", "arbitrary")),
        name="attention",
    )(*args)


def _lru_seq_pos(i):
    n_ctx_blocks = N_CTX // LRU_TB
    per_ctx = SEQ // LRU_TB
    per_lat = DEC_SEQ // LRU_TB
    pos = jnp.where(i < n_ctx_blocks, i % per_ctx, (i - n_ctx_blocks) % per_lat)
    length = jnp.where(i < n_ctx_blocks, per_ctx, per_lat)
    return pos == 0, pos == length - 1


def _lru_seq_of_block(i):
    n_ctx_blocks = N_CTX // LRU_TB
    return jnp.where(i < n_ctx_blocks, i // (SEQ // LRU_TB),
                     BATCH + (i - n_ctx_blocks) // (DEC_SEQ // LRU_TB))


LRU_PITCH = LRU_TB + SUBLANES


def _lru_scan_kernel(x_ref, prev_ref, next_ref, cw_ref, cb_ref, wg_ref, bg_ref, lam_ref, h0_ref,
                     *rest, reverse, combine):
    if combine:
        hf_ref, gate_ref, y_ref, st_ref, a_s, u_s, h_s, carry = rest
    else:
        y_ref, st_ref, a_s, u_s, h_s, carry = rest
    j = pl.program_id(0)
    i = (T // LRU_TB - 1 - j) if reverse else j
    first, last = _lru_seq_pos(i)
    starts = last if reverse else first
    ends = first if reverse else last

    x = x_ref[...]
    before = jnp.where(first, 0.0, prev_ref[SUBLANES - 1:SUBLANES, :])
    after = jnp.where(last, 0.0, next_ref[0:2, :])
    row = lax.broadcasted_iota(jnp.int32, (SUBLANES, D), 0)

    def shifted(k, fix):
        y = pltpu.roll(x, (-k) % LRU_TB, 0)
        if k < 0:
            return jnp.concatenate([fix(y[0:SUBLANES]), y[SUBLANES:]], axis=0)
        return jnp.concatenate([y[0:LRU_TB - SUBLANES], fix(y[LRU_TB - SUBLANES:])], axis=0)

    taps = [
        shifted(-1, lambda t: jnp.where(row == 0, before, t)),
        x,
        shifted(1, lambda t: jnp.where(row == SUBLANES - 1, after[0:1], t)),
        shifted(2, lambda t: jnp.where(row == SUBLANES - 2, after[0:1],
                                       jnp.where(row == SUBLANES - 1, after[1:2], t))),
    ]
    xr = taps[0] * cw_ref[0:1, :]
    for t in range(1, 4):
        xr = xr + taps[t] * cw_ref[t:t + 1, :]
    xr = xr + cb_ref[...]

    xr_b = xr.astype(BF16)
    lam = lam_ref[...]
    neg = -lam
    softplus = jnp.maximum(neg, 0.0) + jnp.log1p(jnp.exp(-jnp.abs(neg)))
    for n in range(LRU_BLOCKS):
        sl = slice(n * LRU_BLK, (n + 1) * LRU_BLK)
        g = jnp.dot(xr_b[:, sl], wg_ref[n], preferred_element_type=F32)
        r = jax.nn.sigmoid(g[:, 0:LRU_BLK] + bg_ref[0:1, sl])
        gi = jax.nn.sigmoid(g[:, LRU_BLK:] + bg_ref[1:2, sl])
        log_a = -LRU_C * r * softplus[:, sl]
        th = jnp.tanh(log_a)
        one_minus_a2 = -2.0 * th / (1.0 - th)
        rows_n = slice(n * LRU_PITCH, n * LRU_PITCH + LRU_TB)
        a_s[rows_n, :] = jnp.exp(log_a)
        root = jnp.where(one_minus_a2 > 0.0, one_minus_a2 * lax.rsqrt(one_minus_a2), 0.0)
        u_s[rows_n, :] = root * (gi * xr[:, sl])

    @pl.when(starts)
    def _():
        carry[...] = h0_ref[...]

    def body(t, h):
        tt = (LRU_TB - 1 - t) if reverse else t
        rows = pl.ds(tt, LRU_BLOCKS, stride=LRU_PITCH)
        h = a_s[rows, :] * h + u_s[rows, :]
        h_s[rows, :] = h
        return h

    h_end = lax.fori_loop(0, LRU_TB, body, carry[...], unroll=8)
    carry[...] = h_end

    for n in range(LRU_BLOCKS):
        sl = slice(n * LRU_BLK, (n + 1) * LRU_BLK)
        hs = h_s[n * LRU_PITCH:n * LRU_PITCH + LRU_TB, :]
        if combine:
            y_ref[:, sl] = ((hf_ref[:, sl] + hs) * jax.nn.gelu(gate_ref[:, sl])).astype(BF16)
        else:
            y_ref[:, sl] = hs

    @pl.when(ends)
    def _():
        st_ref[...] = h_end


def _lru_scan(xr, conv_w, conv_b, w_gate, b_gate, lam, h0, *, reverse, hf=None, gate=None):
    nb = T // LRU_TB
    hb = LRU_TB // SUBLANES
    n_halo = T // SUBLANES

    def blk(j):
        return (nb - 1 - j) if reverse else j

    in_specs = [
        pl.BlockSpec((LRU_TB, D), lambda j: (blk(j), 0)),
        pl.BlockSpec((SUBLANES, D), lambda j: (jnp.maximum(blk(j) * hb - 1, 0), 0)),
        pl.BlockSpec((SUBLANES, D), lambda j: (jnp.minimum((blk(j) + 1) * hb, n_halo - 1), 0)),
        pl.BlockSpec((4, D), lambda j: (0, 0)),
        pl.BlockSpec((1, D), lambda j: (0, 0)),
        pl.BlockSpec((LRU_BLOCKS, LRU_BLK, 2 * LRU_BLK), lambda j: (0, 0, 0)),
        pl.BlockSpec((2, D), lambda j: (0, 0)),
        pl.BlockSpec((1, D), lambda j: (0, 0)),
        pl.BlockSpec((None, LRU_BLOCKS, LRU_BLK), lambda j: (_lru_seq_of_block(blk(j)), 0, 0)),
    ]
    args = [xr, xr, xr, conv_w, conv_b, w_gate, b_gate, lam, h0]
    combine = hf is not None
    if combine:
        in_specs += [pl.BlockSpec((LRU_TB, D), lambda j: (blk(j), 0)),
                     pl.BlockSpec((LRU_TB, D), lambda j: (blk(j), 0))]
        args += [hf, gate]
    n_seq = BATCH + DEC_BATCH
    return pl.pallas_call(
        functools.partial(_lru_scan_kernel, reverse=reverse, combine=combine),
        grid=(nb,),
        in_specs=in_specs,
        out_specs=[pl.BlockSpec((LRU_TB, D), lambda j: (blk(j), 0)),
                   pl.BlockSpec((None, LRU_BLOCKS, LRU_BLK),
                                lambda j: (_lru_seq_of_block(blk(j)), 0, 0))],
        out_shape=[jax.ShapeDtypeStruct((T, D), BF16 if combine else F32),
                   jax.ShapeDtypeStruct((n_seq, LRU_BLOCKS, LRU_BLK), F32)],
        scratch_shapes=[pltpu.VMEM((LRU_BLOCKS * LRU_PITCH, LRU_BLK), F32),
                        pltpu.VMEM((LRU_BLOCKS * LRU_PITCH, LRU_BLK), F32),
                        pltpu.VMEM((LRU_BLOCKS * LRU_PITCH, LRU_BLK), F32),
                        pltpu.VMEM((LRU_BLOCKS, LRU_BLK), F32)],
        compiler_params=_cparams(("arbitrary",)),
        name="lru_scan_bwd" if reverse else "lru_scan_fwd",
    )(*args)


def _post_kernel(*refs, split_x, split_o, split_out):
    it = iter(refs)

    def rows(split):
        if split:
            return jnp.where(_is_ctx_block(), next(it)[...], next(it)[...])
        return next(it)[...]

    x = rows(split_x)
    o = rows(split_o)
    mod_ref, g2_ref, wo_ref, win_ref, wout_ref = (next(it) for _ in range(5))
    y_refs = list(it)
    x1 = x + mod_ref[2:3, :] * jnp.dot(o, wo_ref[...], preferred_element_type=F32)
    h = _rms(x1, g2_ref[...])
    h = (h * (1.0 + mod_ref[4:5, :]) + mod_ref[3:4, :]).astype(BF16)
    acc = jnp.zeros((TM, D), F32)
    for c in range(FFN_H // FFN_CHUNK):
        lo = c * FFN_CHUNK
        g = jnp.dot(h, win_ref[:, lo:lo + FFN_CHUNK], preferred_element_type=F32)
        u = jnp.dot(h, win_ref[:, FFN_H + lo:FFN_H + lo + FFN_CHUNK], preferred_element_type=F32)
        a = (jax.nn.silu(g) * u).astype(BF16)
        acc = acc + jnp.dot(a, wout_ref[lo:lo + FFN_CHUNK, :], preferred_element_type=F32)
    y = x1 + mod_ref[5:6, :] * acc
    if split_out:
        @pl.when(_is_ctx_block())
        def _():
            y_refs[0][...] = y

        @pl.when(jnp.logical_not(_is_ctx_block()))
        def _():
            y_refs[1][...] = y
    else:
        y_refs[0][...] = y


def _layer_weight_spec(shape, layer):
    return pl.BlockSpec((None,) + shape[1:], lambda i: (layer,) + (0,) * (len(shape) - 1),
                        pipeline_mode=pl.Buffered(1))


def _post(xs, os, mod, g2, w_o, w_in, w_out, layer, *, split_out):
    def row_specs(arrays):
        return [_ctx_row_spec(D), _lat_row_spec(D)] if len(arrays) == 2 else [_row_spec(D)]

    if split_out:
        out_specs = [_ctx_row_spec(D), _lat_row_spec(D)]
        out_shape = [jax.ShapeDtypeStruct((N_CTX, D), F32), jax.ShapeDtypeStruct((N_LAT, D), F32)]
    else:
        out_specs = [_row_spec(D)]
        out_shape = [jax.ShapeDtypeStruct((T, D), F32)]
    return pl.pallas_call(
        functools.partial(_post_kernel, split_x=len(xs) == 2, split_o=len(os) == 2,
                          split_out=split_out),
        grid=(T // TM,),
        in_specs=row_specs(xs) + row_specs(os) + [
            _mod_spec(), _full_spec((1, D)), _layer_weight_spec((1,) + w_o.shape, 0),
            _layer_weight_spec(w_in.shape, layer), _layer_weight_spec(w_out.shape, layer)],
        out_specs=out_specs,
        out_shape=out_shape,
        compiler_params=_cparams(("arbitrary",)),
        name="post_mixer_ffn",
    )(*xs, *os, mod, g2, w_o[None], w_in, w_out)


def _axial_tables(rot_dim):
    row = jnp.repeat(jnp.arange(DEC_SEQ // GRID_W), GRID_W).astype(F32)
    col = jnp.tile(jnp.arange(GRID_W), DEC_SEQ // GRID_W).astype(F32)
    n_freq = rot_dim // 4
    inv = ROPE_THETA ** (-jnp.arange(n_freq, dtype=F32) / n_freq)
    ang = jnp.concatenate([row[:, None] * inv, col[:, None] * inv], axis=-1)
    cos, sin = jnp.cos(ang), jnp.sin(ang)
    reps = LANES // rot_dim
    cos_t = jnp.tile(jnp.concatenate([cos, cos], axis=-1), (DEC_BATCH, reps))
    sin_t = jnp.tile(jnp.concatenate([-sin, sin], axis=-1), (DEC_BATCH, reps))
    cos_t = jnp.concatenate([jnp.ones((N_CTX, LANES), F32), cos_t], axis=0)
    sin_t = jnp.concatenate([jnp.zeros((N_CTX, LANES), F32), sin_t], axis=0)
    return cos_t, sin_t


def kernel(x_prompt, x_sample, cache_mla_ckv, cache_mla_krope, cache_diff_k, cache_diff_v, cache_gqa_k, cache_gqa_v, state_lru_h, c, c_ctx, w_mod, b_mod, g_norm1, g_norm2, w_ffn_in, w_ffn_out, mla_w_in, mla_g_cq, mla_g_ckv, mla_w_uq, mla_w_ukv, mla_g_qk, mla_w_o, diff_w_in, diff_g_qk, diff_lambda, diff_g_sub, diff_w_o, gqa_w_in, gqa_g_qk, gqa_w_o, lru_w_in, lru_conv_w, lru_conv_b, lru_w_gate, lru_b_gate, lru_lambda, lru_w_out):
    x_in = (x_prompt.reshape(N_CTX, D), x_sample.reshape(N_LAT, D))
    cond = jnp.concatenate([c_ctx[None, :], c, jnp.zeros((SUBLANES - N_GROUPS, D), F32)], axis=0)
    mod_all = _modulation(cond, w_mod, b_mod)
    cos128, sin128 = _axial_tables(GQA_HD)
    cos64, sin64 = _axial_tables(DIFF_HD)
    w_ffn_in_b = w_ffn_in.astype(BF16)
    w_ffn_out_b = w_ffn_out.astype(BF16)

    def layer_mod(l):
        return mod_all[l, :N_GROUPS].reshape(N_GROUPS, 6, D)

    def post(l, xs, os, w_o):
        out = _post(xs, os, layer_mod(l), g_norm2[l][None, :], w_o.astype(BF16),
                    w_ffn_in_b, w_ffn_out_b, l, split_out=l == DEPTH - 1)
        return out if l == DEPTH - 1 else out[0]

    def attend(q, k, v, *, hq, group, cache, diff, lambda_init):
        common = dict(group=group, diff=diff, lambda_init=lambda_init)
        sub = ATTN_ROWS // group
        o_ctx = _attention(q, k, v, seq_len=SEQ, n_seq=BATCH, row0=0, tq=SEQ, bk=SEQ,
                           sub=min(sub, SEQ), heads_per_step=hq, cache=None, **common)
        o_lat = _attention(q, k, v, seq_len=DEC_SEQ, n_seq=DEC_BATCH, row0=N_CTX, tq=ATTN_TQ,
                           bk=ATTN_BK, sub=sub, heads_per_step=group, cache=cache, **common)
        return o_ctx, o_lat

    l = 0
    w_in = mla_w_in[0]
    kr_cols = w_in[:, MLA_Q_RANK + MLA_KV_RANK:]
    w_in2 = jnp.concatenate([w_in, kr_cols], axis=1).astype(BF16)
    w_uq = mla_w_uq[0].reshape(MLA_Q_RANK, MLA_HEADS, MLA_NOPE + MLA_ROPE)
    wqn = w_uq[:, :, :MLA_NOPE].reshape(MLA_Q_RANK, MLA_HEADS * MLA_NOPE).astype(BF16)
    wqr = w_uq[:, :, MLA_NOPE:].reshape(MLA_Q_RANK, MLA_HEADS * MLA_ROPE).astype(BF16)
    wukv = mla_w_ukv[0].astype(BF16)
    gqk = mla_g_qk[0]
    gqn = gqk[0:1, :MLA_NOPE]
    gqr = jnp.tile(gqk[0:1, MLA_NOPE:], (1, 2))
    gkn = gqk[1:2, :MLA_NOPE]
    gkr = jnp.tile(gqk[1:2, MLA_NOPE:], (1, 2))
    q, k, v, ckv_new, kr_new = _mla_in(*x_in, layer_mod(l), g_norm1[l][None, :], w_in2,
                                       mla_g_cq[0][None, :], mla_g_ckv[0][None, :],
                                       gqn, gqr, gkn, gkr, wqn, wqr, wukv, cos64, sin64)
    kc, vc = _mla_cache(cache_mla_ckv[:, 0].reshape(DEC_BATCH * PAST, MLA_KV_RANK),
                        cache_mla_krope[:, 0].reshape(DEC_BATCH * PAST, MLA_ROPE), wukv, gkn)
    cache = (kc, vc,
             (None, PAST, MLA_DQ), lambda b, hb, qb: (hb, b, 0),
             (None, PAST, 2 * MLA_V), lambda b, hb, qb: (hb, b, 0))
    o = attend(q, k, v, hq=MLA_HEADS, group=1, cache=cache, diff=None, lambda_init=0.0)
    x = post(l, x_in, o, mla_w_o[0])
    new_mla_ckv = ckv_new.reshape(BATCH, 1, SEQ, MLA_KV_RANK)
    new_mla_krope = kr_new.reshape(BATCH, 1, SEQ, MLA_ROPE)

    l = 1
    lambda_init = 0.8 - 0.6 * math.exp(-0.3 * l)
    q, k, v, k_new, v_new = _diff_in(x, layer_mod(l), g_norm1[l][None, :], diff_w_in[0].astype(BF16),
                                     jnp.tile(diff_g_qk[0], (1, 2)), cos64, sin64)
    n_diff = DIFF_HEADS * 2 * DIFF_HD
    cache = (cache_diff_k[:, 0].reshape(DEC_BATCH, PAST, n_diff),
             cache_diff_v[:, 0].reshape(DEC_BATCH, PAST, n_diff),
             (None, PAST, LANES), lambda b, hb, qb: (b, 0, hb),
             (None, PAST, LANES), lambda b, hb, qb: (b, 0, hb))
    o = attend(q, k, v, hq=2 * DIFF_HEADS, group=2, cache=cache,
               diff=(diff_lambda[0], diff_g_sub[0][None, :]), lambda_init=lambda_init)
    x = post(l, (x,), o, diff_w_o[0])
    new_diff_k = k_new.reshape(BATCH, 1, SEQ, DIFF_HEADS, 2, DIFF_HD)
    new_diff_v = v_new.reshape(BATCH, 1, SEQ, DIFF_HEADS, 2 * DIFF_HD)

    l = 2
    q, k, v, k_new, v_new = _gqa_in(x, layer_mod(l), g_norm1[l][None, :], gqa_w_in[0].astype(BF16),
                                    gqa_g_qk[0], cos128, sin128)
    n_kv = GQA_KV_HEADS * GQA_HD
    group = GQA_Q_HEADS // GQA_KV_HEADS
    cache = (cache_gqa_k[:, 0].reshape(DEC_BATCH, PAST, n_kv),
             cache_gqa_v[:, 0].reshape(DEC_BATCH, PAST, n_kv),
             (None, PAST, GQA_HD), lambda b, hb, qb: (b, 0, hb),
             (None, PAST, GQA_HD), lambda b, hb, qb: (b, 0, hb))
    o = attend(q, k, v, hq=GQA_Q_HEADS, group=group, cache=cache, diff=None, lambda_init=0.0)
    x = post(l, (x,), o, gqa_w_o[0])
    new_gqa_k = k_new.reshape(BATCH, 1, SEQ, GQA_KV_HEADS, GQA_HD)
    new_gqa_v = v_new.reshape(BATCH, 1, SEQ, GQA_KV_HEADS, GQA_HD)

    l = 3
    gate, xr = _lru_in(x, layer_mod(l), g_norm1[l][None, :], lru_w_in[0].astype(BF16))
    wg = lru_w_gate[0]
    wg = jnp.concatenate([wg[:, 0], wg[:, 1]], axis=-1).astype(BF16)
    h0 = jnp.concatenate([jnp.zeros((BATCH, 2, D), F32), state_lru_h[:, 0]], axis=0)
    h0 = h0.reshape(BATCH + DEC_BATCH, 2, LRU_BLOCKS, LRU_BLK)
    conv_b = lru_conv_b[0][None, :]
    hf, st_f = _lru_scan(xr, lru_conv_w[0], conv_b, wg[0], lru_b_gate[0, 0], lru_lambda[0, 0][None, :],
                         h0[:, 0], reverse=False)
    y, st_b = _lru_scan(xr, lru_conv_w[0], conv_b, wg[1], lru_b_gate[0, 1], lru_lambda[0, 1][None, :],
                        h0[:, 1], reverse=True, hf=hf, gate=gate)
    y_ctx, y_lat = post(l, (x,), (y,), lru_w_out[0])
    new_lru_h = jnp.stack([st_f[:BATCH].reshape(BATCH, D), st_b[:BATCH].reshape(BATCH, D)],
                          axis=1)[:, None]

    y_prompt = y_ctx.reshape(BATCH, SEQ, D)
    y_sample = y_lat.reshape(DEC_BATCH, DEC_SEQ, D)
    return (y_prompt, y_sample, new_mla_ckv, new_mla_krope, new_diff_k, new_diff_v,
            new_gqa_k, new_gqa_v, new_lru_h)
```

```python
import functools
import math

import jax
import jax.numpy as jnp
from jax import lax
from jax.experimental import pallas as pl
from jax.experimental.pallas import tpu as pltpu

F32 = jnp.float32
BF16 = jnp.bfloat16

D = 1024
BATCH = 32
SEQ = 256
DEPTH = 4
DEC_BATCH = 2
DEC_SEQ = 4096
PAST = 256
GRID_W = 64
EPS = 1e-6
ROPE_THETA = 10000.0
FFN_H = 2816
N_CTX = BATCH * SEQ
N_LAT = DEC_BATCH * DEC_SEQ
T = N_CTX + N_LAT
N_GROUPS = 1 + DEC_BATCH

MLA_HEADS = 8
MLA_NOPE = 128
MLA_ROPE = 64
MLA_V = 128
MLA_Q_RANK = 384
MLA_KV_RANK = 256
MLA_DQ = 256
DIFF_HD = 64
DIFF_HEADS = 8
GQA_HD = 128
GQA_Q_HEADS = 8
GQA_KV_HEADS = 2
LRU_BLOCKS = 8
LRU_BLK = 128
LRU_C = 8.0

LANES = 128
SUBLANES = 8
VMEM_LIMIT = 56 * 1024 * 1024

TM = 512
FFN_CHUNK = 256
LRU_TB = 256
ATTN_TILES = 4
ATTN_BK = 1024
ATTN_ROWS = 1024
LOG2E = 1.4426950408889634


def _cparams(sem):
    return pltpu.CompilerParams(dimension_semantics=sem, vmem_limit_bytes=VMEM_LIMIT)


def _group_of_block(i, rows_per_block):
    n_ctx_blocks = N_CTX // rows_per_block
    per = DEC_SEQ // rows_per_block
    return jnp.maximum(i - (n_ctx_blocks - per), 0) // per


def _rms(x, gain):
    y = x * lax.rsqrt(jnp.mean(x * x, axis=-1, keepdims=True) + EPS)
    return y * gain


def _rms_halves(x, gain):
    lane = lax.broadcasted_iota(jnp.int32, x.shape, 1)
    low = lane < 64
    sq = x * x
    s_low = jnp.sum(jnp.where(low, sq, 0.0), axis=-1, keepdims=True)
    s_all = jnp.sum(sq, axis=-1, keepdims=True)
    ms = jnp.where(low, s_low, s_all - s_low) * (1.0 / 64.0)
    return x * lax.rsqrt(ms + EPS) * gain


def _rope128(x, c, s):
    return x * c + pltpu.roll(x, 64, 1) * s


def _rope64x2(x, c, s):
    lane = lax.broadcasted_iota(jnp.int32, x.shape, 1)
    low = (lane % 64) < 32
    partner = jnp.where(low, pltpu.roll(x, 96, 1), pltpu.roll(x, 32, 1))
    return x * c + partner * s


def _is_ctx_block():
    return pl.program_id(0) < N_CTX // TM


def _norm_mod(x, g_ref, mod_ref, shift_row):
    y = _rms(x, g_ref[...])
    shift = mod_ref[shift_row:shift_row + 1, :]
    scale = mod_ref[shift_row + 1:shift_row + 2, :]
    return (y * (1.0 + scale) + shift).astype(BF16)


def _mod_kernel(cond_ref, w_ref, b_ref, o_ref):
    s = jax.nn.silu(cond_ref[...]).astype(BF16)
    w = w_ref[...].astype(BF16)
    o_ref[...] = jnp.dot(s, w, preferred_element_type=F32) + b_ref[...]


def _modulation(cond, w_mod, b_mod):
    tn = 1536
    return pl.pallas_call(
        _mod_kernel,
        grid=(DEPTH, 6 * D // tn),
        in_specs=[
            pl.BlockSpec((SUBLANES, D), lambda l, j: (0, 0)),
            pl.BlockSpec((None, D, tn), lambda l, j: (l, 0, j)),
            pl.BlockSpec((None, 1, tn), lambda l, j: (l, 0, j)),
        ],
        out_specs=pl.BlockSpec((None, SUBLANES, tn), lambda l, j: (l, 0, j)),
        out_shape=jax.ShapeDtypeStruct((DEPTH, SUBLANES, 6 * D), F32),
        compiler_params=_cparams(("arbitrary", "arbitrary")),
        name="adaln_mod",
    )(cond, w_mod, b_mod.reshape(DEPTH, 1, 6 * D))


def _row_spec(width):
    return pl.BlockSpec((TM, width), lambda i: (i, 0))


def _ctx_row_spec(width):
    return pl.BlockSpec((TM, width), lambda i: (jnp.minimum(i, N_CTX // TM - 1), 0))


def _lat_row_spec(width):
    return pl.BlockSpec((TM, width), lambda i: (jnp.maximum(i - N_CTX // TM, 0), 0))


def _full_spec(shape):
    return pl.BlockSpec(shape, lambda i: (0,) * len(shape))


def _mod_spec():
    return pl.BlockSpec((None, 6, D), lambda i: (_group_of_block(i, TM), 0, 0))


def _heads_spec(heads, width):
    return pl.BlockSpec((heads, TM, width), lambda i: (0, i, 0))


N_BLOCKS = T // TM
N_CTX_BLOCKS = N_CTX // TM


def _proj_block(i):
    return jnp.minimum(i, N_BLOCKS - 1)


def _epi_block(i):
    return jnp.maximum(i - 1, 0)


def _proj_row_spec(width):
    return pl.BlockSpec((TM, width), lambda i: (_proj_block(i), 0))


def _proj_mod_spec():
    return pl.BlockSpec((None, 6, D), lambda i: (_group_of_block(_proj_block(i), TM), 0, 0))


def _epi_row_spec(width):
    return pl.BlockSpec((TM, width), lambda i: (_epi_block(i), 0))


def _epi_heads_spec(heads, width):
    return pl.BlockSpec((heads, TM, width), lambda i: (0, _epi_block(i), 0))


def _epi_ctx_row_spec(width):
    return pl.BlockSpec((TM, width), lambda i: (jnp.minimum(_epi_block(i), N_CTX_BLOCKS - 1), 0))


def _epi_is_ctx():
    i = pl.program_id(0)
    return jnp.logical_and(i >= 1, i <= N_CTX_BLOCKS)


def _two_stage(step, proj_a, proj_b):
    i = pl.program_id(0)

    @pl.when(i == 0)
    def _():
        proj_b[...] = jnp.zeros(proj_b.shape, F32)

    @pl.when(i % 2 == 0)
    def _():
        step(proj_b, proj_a)

    @pl.when(i % 2 == 1)
    def _():
        step(proj_a, proj_b)


def _gqa_in_kernel(x_ref, mod_ref, g_ref, w_ref, gqk_ref, cos_ref, sin_ref,
                   q_ref, k_ref, v_ref, ks_ref, vs_ref, proj_a, proj_b):
    n_q = GQA_Q_HEADS * GQA_HD
    n_kv = GQA_KV_HEADS * GQA_HD

    def step(prev, cur):
        h = _norm_mod(x_ref[...], g_ref, mod_ref, 0)
        cur[...] = jnp.dot(h, w_ref[...], preferred_element_type=F32)
        c = cos_ref[...]
        s = sin_ref[...]
        scale = LOG2E * GQA_HD ** -0.5
        for hd in range(GQA_Q_HEADS):
            q = _rms(prev[:, hd * GQA_HD:(hd + 1) * GQA_HD], gqk_ref[0:1, :])
            q_ref[hd] = (_rope128(q, c, s) * scale).astype(BF16)
        ks = []
        for hd in range(GQA_KV_HEADS):
            lo = n_q + hd * GQA_HD
            ks.append(_rms(prev[:, lo:lo + GQA_HD], gqk_ref[1:2, :]))
            k_ref[hd] = _rope128(ks[-1], c, s).astype(BF16)
            lo = n_q + n_kv + hd * GQA_HD
            v_ref[hd] = _with_ones(prev[:, lo:lo + GQA_HD].astype(BF16))

        @pl.when(_epi_is_ctx())
        def _():
            for hd in range(GQA_KV_HEADS):
                ks_ref[:, hd * GQA_HD:(hd + 1) * GQA_HD] = ks[hd]
            vs_ref[...] = prev[:, n_q + n_kv:]

    _two_stage(step, proj_a, proj_b)


def _proj_scratch(width):
    return [pltpu.VMEM((TM, width), F32), pltpu.VMEM((TM, width), F32)]


def _gqa_in(x, mod, g1, w_in, g_qk, cos, sin):
    n_kv = GQA_KV_HEADS * GQA_HD
    return pl.pallas_call(
        _gqa_in_kernel,
        grid=(N_BLOCKS + 1,),
        in_specs=[_proj_row_spec(D), _proj_mod_spec(), _full_spec((1, D)), _full_spec(w_in.shape),
                  _full_spec((2, GQA_HD)), _epi_row_spec(LANES), _epi_row_spec(LANES)],
        out_specs=[_epi_heads_spec(GQA_Q_HEADS, GQA_HD), _epi_heads_spec(GQA_KV_HEADS, GQA_HD),
                   _epi_heads_spec(GQA_KV_HEADS, 2 * GQA_HD), _epi_ctx_row_spec(n_kv),
                   _epi_ctx_row_spec(n_kv)],
        scratch_shapes=_proj_scratch(w_in.shape[1]),
        out_shape=[jax.ShapeDtypeStruct((GQA_Q_HEADS, T, GQA_HD), BF16),
                   jax.ShapeDtypeStruct((GQA_KV_HEADS, T, GQA_HD), BF16),
                   jax.ShapeDtypeStruct((GQA_KV_HEADS, T, 2 * GQA_HD), BF16),
                   jax.ShapeDtypeStruct((N_CTX, n_kv), F32),
                   jax.ShapeDtypeStruct((N_CTX, n_kv), F32)],
        compiler_params=_cparams(("arbitrary",)),
        name="gqa_in",
    )(x, mod, g1, w_in, g_qk, cos, sin)


def _diff_in_kernel(x_ref, mod_ref, g_ref, w_ref, gqk_ref, cos_ref, sin_ref,
                    q_ref, k_ref, v_ref, ks_ref, vs_ref, proj_a, proj_b):
    n = DIFF_HEADS * 2 * DIFF_HD

    def step(prev, cur):
        h = _norm_mod(x_ref[...], g_ref, mod_ref, 0)
        cur[...] = jnp.dot(h, w_ref[...], preferred_element_type=F32)
        c = cos_ref[...]
        s = sin_ref[...]
        scale = LOG2E * DIFF_HD ** -0.5
        lane = lax.broadcasted_iota(jnp.int32, (TM, LANES), 1)
        low = lane < DIFF_HD
        ks = []
        for hd in range(DIFF_HEADS):
            sl = slice(hd * LANES, (hd + 1) * LANES)
            q = _rope64x2(_rms_halves(prev[:, sl], gqk_ref[0:1, :]), c, s) * scale
            q_ref[2 * hd] = jnp.where(low, q, 0.0).astype(BF16)
            q_ref[2 * hd + 1] = jnp.where(low, 0.0, q).astype(BF16)
            ks.append(_rms_halves(prev[:, n + hd * LANES:n + (hd + 1) * LANES], gqk_ref[1:2, :]))
            k_ref[hd] = _rope64x2(ks[-1], c, s).astype(BF16)
            v_ref[hd] = _with_ones(prev[:, 2 * n + hd * LANES:2 * n + (hd + 1) * LANES].astype(BF16))

        @pl.when(_epi_is_ctx())
        def _():
            for hd in range(DIFF_HEADS):
                ks_ref[:, hd * LANES:(hd + 1) * LANES] = ks[hd]
            vs_ref[...] = prev[:, 2 * n:]

    _two_stage(step, proj_a, proj_b)


def _diff_in(x, mod, g1, w_in, g_qk2, cos, sin):
    n = DIFF_HEADS * 2 * DIFF_HD
    return pl.pallas_call(
        _diff_in_kernel,
        grid=(N_BLOCKS + 1,),
        in_specs=[_proj_row_spec(D), _proj_mod_spec(), _full_spec((1, D)), _full_spec(w_in.shape),
                  _full_spec((2, LANES)), _epi_row_spec(LANES), _epi_row_spec(LANES)],
        out_specs=[_epi_heads_spec(2 * DIFF_HEADS, LANES), _epi_heads_spec(DIFF_HEADS, LANES),
                   _epi_heads_spec(DIFF_HEADS, 2 * LANES), _epi_ctx_row_spec(n), _epi_ctx_row_spec(n)],
        scratch_shapes=_proj_scratch(w_in.shape[1]),
        out_shape=[jax.ShapeDtypeStruct((2 * DIFF_HEADS, T, LANES), BF16),
                   jax.ShapeDtypeStruct((DIFF_HEADS, T, LANES), BF16),
                   jax.ShapeDtypeStruct((DIFF_HEADS, T, 2 * LANES), BF16),
                   jax.ShapeDtypeStruct((N_CTX, n), F32),
                   jax.ShapeDtypeStruct((N_CTX, n), F32)],
        compiler_params=_cparams(("arbitrary",)),
        name="diff_in",
    )(x, mod, g1, w_in, g_qk2, cos, sin)


def _mla_keys_values(ckv_bf16, kr2, w_ukv_ref, gk_ref, k_ref, v_ref):
    kv = jnp.dot(ckv_bf16, w_ukv_ref[...], preferred_element_type=F32)
    lane = lax.broadcasted_iota(jnp.int32, kr2.shape, 1)
    low = lane < MLA_ROPE
    kr_low = jnp.where(low, kr2, 0.0).astype(BF16)
    kr_high = jnp.where(low, 0.0, kr2).astype(BF16)
    width = MLA_NOPE + MLA_V
    for hd in range(MLA_HEADS):
        k_nope = _rms(kv[:, hd * width:hd * width + MLA_NOPE], gk_ref[...])
        k_ref[hd, :, 0:MLA_NOPE] = k_nope.astype(BF16)
        k_ref[hd, :, MLA_NOPE:MLA_DQ] = kr_low if hd % 2 == 0 else kr_high
        v_ref[hd] = _with_ones(kv[:, hd * width + MLA_NOPE:(hd + 1) * width].astype(BF16))


def _mla_in_kernel(xc_ref, xl_ref, mod_ref, g_ref, w_ref, gcq_ref, gckv_ref, gqn_ref, gqr_ref,
                   gkn_ref, gkr_ref, wqn_ref, wqr_ref, wukv_ref, cos_ref, sin_ref,
                   q_ref, k_ref, v_ref, ckv_ref, kr_ref, proj_a, proj_b):
    def step(prev, cur):
        x = jnp.where(_is_ctx_block(), xc_ref[...], xl_ref[...])
        h = _norm_mod(x, g_ref, mod_ref, 0)
        cur[...] = jnp.dot(h, w_ref[...], preferred_element_type=F32)
        c = cos_ref[...]
        s = sin_ref[...]
        cq = _rms(prev[:, 0:MLA_Q_RANK], gcq_ref[...]).astype(BF16)
        ckv = _rms(prev[:, MLA_Q_RANK:MLA_Q_RANK + MLA_KV_RANK], gckv_ref[...])
        kr2 = _rms_halves(prev[:, MLA_Q_RANK + MLA_KV_RANK:], gkr_ref[...])
        _mla_keys_values(ckv.astype(BF16), _rope64x2(kr2, c, s), wukv_ref, gkn_ref, k_ref, v_ref)

        scale = LOG2E * (MLA_NOPE + MLA_ROPE) ** -0.5
        qn = jnp.dot(cq, wqn_ref[...], preferred_element_type=F32)
        qr = jnp.dot(cq, wqr_ref[...], preferred_element_type=F32)
        lane = lax.broadcasted_iota(jnp.int32, (TM, LANES), 1)
        low = lane < MLA_ROPE
        for pair in range(MLA_HEADS // 2):
            r = _rms_halves(qr[:, pair * LANES:(pair + 1) * LANES], gqr_ref[...])
            r = _rope64x2(r, c, s) * scale
            q_ref[2 * pair, :, MLA_NOPE:MLA_DQ] = jnp.where(low, r, 0.0).astype(BF16)
            q_ref[2 * pair + 1, :, MLA_NOPE:MLA_DQ] = jnp.where(low, 0.0, r).astype(BF16)
        for hd in range(MLA_HEADS):
            q = _rms(qn[:, hd * MLA_NOPE:(hd + 1) * MLA_NOPE], gqn_ref[...]) * scale
            q_ref[hd, :, 0:MLA_NOPE] = q.astype(BF16)

        @pl.when(_epi_is_ctx())
        def _():
            ckv_ref[...] = ckv
            kr_ref[...] = kr2[:, 0:MLA_ROPE]

    _two_stage(step, proj_a, proj_b)


def _mla_in(x_ctx, x_lat, mod, g1, w_in, g_cq, g_ckv, gqn, gqr, gkn, gkr, wqn, wqr, wukv, cos, sin):
    n_lat_blocks = N_BLOCKS - N_CTX_BLOCKS
    return pl.pallas_call(
        _mla_in_kernel,
        grid=(N_BLOCKS + 1,),
        in_specs=[_ctx_row_spec(D),
                  pl.BlockSpec((TM, D), lambda i: (jnp.clip(i - N_CTX_BLOCKS, 0, n_lat_blocks - 1), 0)),
                  _proj_mod_spec(), _full_spec((1, D)), _full_spec(w_in.shape),
                  _full_spec((1, MLA_Q_RANK)), _full_spec((1, MLA_KV_RANK)),
                  _full_spec((1, LANES)), _full_spec((1, LANES)), _full_spec((1, LANES)),
                  _full_spec((1, LANES)), _full_spec(wqn.shape), _full_spec(wqr.shape),
                  _full_spec(wukv.shape), _epi_row_spec(LANES), _epi_row_spec(LANES)],
        out_specs=[_epi_heads_spec(MLA_HEADS, MLA_DQ), _epi_heads_spec(MLA_HEADS, MLA_DQ),
                   _epi_heads_spec(MLA_HEADS, 2 * MLA_V), _epi_ctx_row_spec(MLA_KV_RANK),
                   _epi_ctx_row_spec(MLA_ROPE)],
        scratch_shapes=_proj_scratch(w_in.shape[1]),
        out_shape=[jax.ShapeDtypeStruct((MLA_HEADS, T, MLA_DQ), BF16),
                   jax.ShapeDtypeStruct((MLA_HEADS, T, MLA_DQ), BF16),
                   jax.ShapeDtypeStruct((MLA_HEADS, T, 2 * MLA_V), BF16),
                   jax.ShapeDtypeStruct((N_CTX, MLA_KV_RANK), F32),
                   jax.ShapeDtypeStruct((N_CTX, MLA_ROPE), F32)],
        compiler_params=_cparams(("arbitrary",)),
        name="mla_in",
    )(x_ctx, x_lat, mod, g1, w_in, g_cq, g_ckv, gqn, gqr, gkn, gkr, wqn, wqr, wukv, cos, sin)


def _mla_cache_kernel(ckv_ref, kr_ref, wukv_ref, gkn_ref, k_ref, v_ref):
    kr = kr_ref[...]
    kr2 = jnp.concatenate([kr, kr], axis=-1)
    _mla_keys_values(ckv_ref[...].astype(BF16), kr2, wukv_ref, gkn_ref, k_ref, v_ref)


def _mla_cache(ckv, kr, wukv, gkn):
    rows = ckv.shape[0]
    return pl.pallas_call(
        _mla_cache_kernel,
        grid=(1,),
        in_specs=[_full_spec(ckv.shape), _full_spec(kr.shape), _full_spec(wukv.shape),
                  _full_spec((1, LANES))],
        out_specs=[_full_spec((MLA_HEADS, rows, MLA_DQ)), _full_spec((MLA_HEADS, rows, 2 * MLA_V))],
        out_shape=[jax.ShapeDtypeStruct((MLA_HEADS, rows, MLA_DQ), BF16),
                   jax.ShapeDtypeStruct((MLA_HEADS, rows, 2 * MLA_V), BF16)],
        compiler_params=_cparams(("arbitrary",)),
        name="mla_cache_kv",
    )(ckv, kr, wukv, gkn)


def _lru_in_kernel(x_ref, mod_ref, g_ref, w_ref, gate_ref, xr_ref):
    h = _norm_mod(x_ref[...], g_ref, mod_ref, 0)
    proj = jnp.dot(h, w_ref[...], preferred_element_type=F32)
    gate_ref[...] = proj[:, 0:D]
    xr_ref[...] = proj[:, D:2 * D]


def _lru_in(x, mod, g1, w_in):
    return pl.pallas_call(
        _lru_in_kernel,
        grid=(T // TM,),
        in_specs=[_row_spec(D), _mod_spec(), _full_spec((1, D)), _full_spec(w_in.shape)],
        out_specs=[_row_spec(D), _row_spec(D)],
        out_shape=[jax.ShapeDtypeStruct((T, D), F32), jax.ShapeDtypeStruct((T, D), F32)],
        compiler_params=_cparams(("arbitrary",)),
        name="lru_in",
    )(x, mod, g1, w_in)


def _with_ones(v):
    return jnp.concatenate([v, jnp.ones_like(v)], axis=-1)


def _softmax_pv(q, key_blocks, dv):
    nt = (((1,), (1,)), ((), ()))
    m = acc = None
    for k, v1 in key_blocks:
        s = lax.dot_general(q, k, nt, preferred_element_type=F32)
        mb = jnp.max(s, axis=-1, keepdims=True)
        if m is None:
            m = mb
            acc = jnp.dot(jnp.exp2(s - m).astype(BF16), v1, preferred_element_type=F32)
        else:
            m_new = jnp.maximum(m, mb)
            alpha = jnp.exp2(m - m_new)
            acc = alpha * acc + jnp.dot(jnp.exp2(s - m_new).astype(BF16), v1,
                                        preferred_element_type=F32)
            m = m_new
    return acc[:, 0:dv] / acc[:, dv:2 * dv]


def _attn_kernel(*refs, heads, group, sub, bk, has_cache, diff, dv, lambda_init):
    it = iter(refs)
    q_ref, kn_ref, vn_ref = next(it), next(it), next(it)
    kc_ref = vc_ref = lam_ref = gsub_ref = None
    if has_cache:
        kc_ref, vc_ref = next(it), next(it)
    if diff:
        lam_ref, gsub_ref = next(it), next(it)
    o_ref = next(it)

    tq = q_ref.shape[1]
    seq_len = kn_ref.shape[1]
    stacked = min(group, heads)
    cached = []
    if has_cache:
        vc = vc_ref[...].astype(BF16)
        cached = [(kc_ref[...].astype(BF16), vc if vc.shape[-1] == 2 * dv else _with_ones(vc))]
    if diff:
        lam = lam_ref[...]
        lam_full = (jnp.exp(jnp.sum(lam[0:1] * lam[1:2], axis=-1, keepdims=True))
                    - jnp.exp(jnp.sum(lam[2:3] * lam[3:4], axis=-1, keepdims=True)) + lambda_init)

    for kv in range(max(heads // group, 1)):
        def body(i, carry, kv=kv):
            rows = pl.ds(pl.multiple_of(i * sub, sub), sub)
            q = jnp.concatenate([q_ref[kv * stacked + g, rows, :] for g in range(stacked)], axis=0)
            blocks = cached + [(kn_ref[kv, j * bk:(j + 1) * bk, :], vn_ref[kv, j * bk:(j + 1) * bk, :])
                               for j in range(seq_len // bk)]
            o = _softmax_pv(q, blocks, dv)
            if diff:
                od = o[0:sub] - lam_full * o[sub:2 * sub]
                od = _rms(od, gsub_ref[...]) * (1.0 - lambda_init)
                o_ref[rows, kv * dv:(kv + 1) * dv] = od.astype(BF16)
            else:
                for g in range(stacked):
                    hd = kv * stacked + g
                    o_ref[rows, hd * dv:(hd + 1) * dv] = o[g * sub:(g + 1) * sub].astype(BF16)
            return carry
        lax.fori_loop(0, tq // sub, body, 0, unroll=True)


def _attention(q, kn, vn, *, seq_len, n_seq, row0, tq, sub, bk, heads_per_step, group, cache=None,
               diff=None, lambda_init=0.0):
    hq, _, dq = q.shape
    hkv, _, dv1 = vn.shape
    dv = dv1 // 2
    kv_per_step = max(heads_per_step // group, 1)
    n_hblk = hq // heads_per_step
    n_qblk = seq_len // tq
    seq0 = row0 // seq_len
    qblk0 = row0 // tq

    def q_map(b, hb, qb):
        return (hb, qblk0 + b * n_qblk + qb, 0)

    def kv_map(b, hb, qb):
        return ((hb * heads_per_step) // (group * kv_per_step), seq0 + b, 0)

    in_specs = [pl.BlockSpec((heads_per_step, tq, dq), q_map),
                pl.BlockSpec((kv_per_step, seq_len, dq), kv_map),
                pl.BlockSpec((kv_per_step, seq_len, dv1), kv_map)]
    args = [q, kn, vn]
    if cache is not None:
        kc, vc, kc_block, kc_map, vc_block, vc_map = cache
        in_specs += [pl.BlockSpec(kc_block, kc_map), pl.BlockSpec(vc_block, vc_map)]
        args += [kc, vc]
    out_heads = heads_per_step
    if diff is not None:
        lam, gsub = diff
        in_specs += [pl.BlockSpec(lam.shape, lambda b, hb, qb: (0, 0)),
                     pl.BlockSpec(gsub.shape, lambda b, hb, qb: (0, 0))]
        args += [lam, gsub]
        out_heads = heads_per_step // 2
    n_out = (hq // 2 if diff is not None else hq) * dv
    kernel = functools.partial(_attn_kernel, heads=heads_per_step, group=group, sub=sub, bk=bk,
                               has_cache=cache is not None, diff=diff is not None, dv=dv,
                               lambda_init=lambda_init)
    return pl.pallas_call(
        kernel,
        grid=(n_seq, n_hblk, n_qblk),
        in_specs=in_specs,
        out_specs=pl.BlockSpec((tq, out_heads * dv), lambda b, hb, qb: (b * n_qblk + qb, hb)),
        out_shape=jax.ShapeDtypeStruct((n_seq * seq_len, n_out), BF16),
        compiler_params=_cparams(("arbitrary", "arbitrary", "arbitrary")),
        name="attention",
    )(*args)


def _lru_seq_pos(i):
    n_ctx_blocks = N_CTX // LRU_TB
    per_ctx = SEQ // LRU_TB
    per_lat = DEC_SEQ // LRU_TB
    pos = jnp.where(i < n_ctx_blocks, i % per_ctx, (i - n_ctx_blocks) % per_lat)
    length = jnp.where(i < n_ctx_blocks, per_ctx, per_lat)
    return pos == 0, pos == length - 1


def _lru_seq_of_block(i):
    n_ctx_blocks = N_CTX // LRU_TB
    return jnp.where(i < n_ctx_blocks, i // (SEQ // LRU_TB),
                     BATCH + (i - n_ctx_blocks) // (DEC_SEQ // LRU_TB))


LRU_PITCH = LRU_TB + SUBLANES


def _lru_scan_kernel(x_ref, prev_ref, next_ref, cw_ref, cb_ref, wg_ref, bg_ref, lam_ref, h0_ref,
                     *rest, reverse, combine):
    if combine:
        hf_ref, gate_ref, y_ref, st_ref, a_s, u_s, h_s, carry = rest
    else:
        y_ref, st_ref, a_s, u_s, h_s, carry = rest
    j = pl.program_id(0)
    i = (T // LRU_TB - 1 - j) if reverse else j
    first, last = _lru_seq_pos(i)
    starts = last if reverse else first
    ends = first if reverse else last

    x = x_ref[...]
    before = jnp.where(first, 0.0, prev_ref[SUBLANES - 1:SUBLANES, :])
    after = jnp.where(last, 0.0, next_ref[0:2, :])
    row = lax.broadcasted_iota(jnp.int32, (SUBLANES, D), 0)

    def shifted(k, fix):
        y = pltpu.roll(x, (-k) % LRU_TB, 0)
        if k < 0:
            return jnp.concatenate([fix(y[0:SUBLANES]), y[SUBLANES:]], axis=0)
        return jnp.concatenate([y[0:LRU_TB - SUBLANES], fix(y[LRU_TB - SUBLANES:])], axis=0)

    taps = [
        shifted(-1, lambda t: jnp.where(row == 0, before, t)),
        x,
        shifted(1, lambda t: jnp.where(row == SUBLANES - 1, after[0:1], t)),
        shifted(2, lambda t: jnp.where(row == SUBLANES - 2, after[0:1],
                                       jnp.where(row == SUBLANES - 1, after[1:2], t))),
    ]
    xr = taps[0] * cw_ref[0:1, :]
    for t in range(1, 4):
        xr = xr + taps[t] * cw_ref[t:t + 1, :]
    xr = xr + cb_ref[...]

    xr_b = xr.astype(BF16)
    lam = lam_ref[...]
    neg = -lam
    softplus = jnp.maximum(neg, 0.0) + jnp.log1p(jnp.exp(-jnp.abs(neg)))
    for n in range(LRU_BLOCKS):
        sl = slice(n * LRU_BLK, (n + 1) * LRU_BLK)
        g = jnp.dot(xr_b[:, sl], wg_ref[n], preferred_element_type=F32)
        r = jax.nn.sigmoid(g[:, 0:LRU_BLK] + bg_ref[0:1, sl])
        gi = jax.nn.sigmoid(g[:, LRU_BLK:] + bg_ref[1:2, sl])
        log_a = -LRU_C * r * softplus[:, sl]
        th = jnp.tanh(log_a)
        one_minus_a2 = -2.0 * th / (1.0 - th)
        rows_n = slice(n * LRU_PITCH, n * LRU_PITCH + LRU_TB)
        a_s[rows_n, :] = jnp.exp(log_a)
        root = jnp.where(one_minus_a2 > 0.0, one_minus_a2 * lax.rsqrt(one_minus_a2), 0.0)
        u_s[rows_n, :] = root * (gi * xr[:, sl])

    @pl.when(starts)
    def _():
        carry[...] = h0_ref[...]

    def body(t, h):
        tt = (LRU_TB - 1 - t) if reverse else t
        rows = pl.ds(tt, LRU_BLOCKS, stride=LRU_PITCH)
        h = a_s[rows, :] * h + u_s[rows, :]
        h_s[rows, :] = h
        return h

    h_end = lax.fori_loop(0, LRU_TB, body, carry[...], unroll=8)
    carry[...] = h_end

    for n in range(LRU_BLOCKS):
        sl = slice(n * LRU_BLK, (n + 1) * LRU_BLK)
        hs = h_s[n * LRU_PITCH:n * LRU_PITCH + LRU_TB, :]
        if combine:
            y_ref[:, sl] = ((hf_ref[:, sl] + hs) * jax.nn.gelu(gate_ref[:, sl])).astype(BF16)
        else:
            y_ref[:, sl] = hs

    @pl.when(ends)
    def _():
        st_ref[...] = h_end


def _lru_scan(xr, conv_w, conv_b, w_gate, b_gate, lam, h0, *, reverse, hf=None, gate=None):
    nb = T // LRU_TB
    hb = LRU_TB // SUBLANES
    n_halo = T // SUBLANES

    def blk(j):
        return (nb - 1 - j) if reverse else j

    in_specs = [
        pl.BlockSpec((LRU_TB, D), lambda j: (blk(j), 0)),
        pl.BlockSpec((SUBLANES, D), lambda j: (jnp.maximum(blk(j) * hb - 1, 0), 0)),
        pl.BlockSpec((SUBLANES, D), lambda j: (jnp.minimum((blk(j) + 1) * hb, n_halo - 1), 0)),
        pl.BlockSpec((4, D), lambda j: (0, 0)),
        pl.BlockSpec((1, D), lambda j: (0, 0)),
        pl.BlockSpec((LRU_BLOCKS, LRU_BLK, 2 * LRU_BLK), lambda j: (0, 0, 0)),
        pl.BlockSpec((2, D), lambda j: (0, 0)),
        pl.BlockSpec((1, D), lambda j: (0, 0)),
        pl.BlockSpec((None, LRU_BLOCKS, LRU_BLK), lambda j: (_lru_seq_of_block(blk(j)), 0, 0)),
    ]
    args = [xr, xr, xr, conv_w, conv_b, w_gate, b_gate, lam, h0]
    combine = hf is not None
    if combine:
        in_specs += [pl.BlockSpec((LRU_TB, D), lambda j: (blk(j), 0)),
                     pl.BlockSpec((LRU_TB, D), lambda j: (blk(j), 0))]
        args += [hf, gate]
    n_seq = BATCH + DEC_BATCH
    return pl.pallas_call(
        functools.partial(_lru_scan_kernel, reverse=reverse, combine=combine),
        grid=(nb,),
        in_specs=in_specs,
        out_specs=[pl.BlockSpec((LRU_TB, D), lambda j: (blk(j), 0)),
                   pl.BlockSpec((None, LRU_BLOCKS, LRU_BLK),
                                lambda j: (_lru_seq_of_block(blk(j)), 0, 0))],
        out_shape=[jax.ShapeDtypeStruct((T, D), BF16 if combine else F32),
                   jax.ShapeDtypeStruct((n_seq, LRU_BLOCKS, LRU_BLK), F32)],
        scratch_shapes=[pltpu.VMEM((LRU_BLOCKS * LRU_PITCH, LRU_BLK), F32),
                        pltpu.VMEM((LRU_BLOCKS * LRU_PITCH, LRU_BLK), F32),
                        pltpu.VMEM((LRU_BLOCKS * LRU_PITCH, LRU_BLK), F32),
                        pltpu.VMEM((LRU_BLOCKS, LRU_BLK), F32)],
        compiler_params=_cparams(("arbitrary",)),
        name="lru_scan_bwd" if reverse else "lru_scan_fwd",
    )(*args)


def _post_kernel(*refs, split_x, split_o, split_out):
    it = iter(refs)

    def rows(split):
        if split:
            return jnp.where(_is_ctx_block(), next(it)[...], next(it)[...])
        return next(it)[...]

    x = rows(split_x)
    o = rows(split_o)
    mod_ref, g2_ref, wo_ref, win_ref, wout_ref = (next(it) for _ in range(5))
    y_refs = list(it)
    x1 = x + mod_ref[2:3, :] * jnp.dot(o, wo_ref[...], preferred_element_type=F32)
    h = _rms(x1, g2_ref[...])
    h = (h * (1.0 + mod_ref[4:5, :]) + mod_ref[3:4, :]).astype(BF16)
    acc = jnp.zeros((TM, D), F32)
    for c in range(FFN_H // FFN_CHUNK):
        lo = c * FFN_CHUNK
        g = jnp.dot(h, win_ref[:, lo:lo + FFN_CHUNK], preferred_element_type=F32)
        u = jnp.dot(h, win_ref[:, FFN_H + lo:FFN_H + lo + FFN_CHUNK], preferred_element_type=F32)
        a = (jax.nn.silu(g) * u).astype(BF16)
        acc = acc + jnp.dot(a, wout_ref[lo:lo + FFN_CHUNK, :], preferred_element_type=F32)
    y = x1 + mod_ref[5:6, :] * acc
    if split_out:
        @pl.when(_is_ctx_block())
        def _():
            y_refs[0][...] = y

        @pl.when(jnp.logical_not(_is_ctx_block()))
        def _():
            y_refs[1][...] = y
    else:
        y_refs[0][...] = y


def _layer_weight_spec(shape, layer):
    return pl.BlockSpec((None,) + shape[1:], lambda i: (layer,) + (0,) * (len(shape) - 1),
                        pipeline_mode=pl.Buffered(1))


def _post(xs, os, mod, g2, w_o, w_in, w_out, layer, *, split_out):
    def row_specs(arrays):
        return [_ctx_row_spec(D), _lat_row_spec(D)] if len(arrays) == 2 else [_row_spec(D)]

    if split_out:
        out_specs = [_ctx_row_spec(D), _lat_row_spec(D)]
        out_shape = [jax.ShapeDtypeStruct((N_CTX, D), F32), jax.ShapeDtypeStruct((N_LAT, D), F32)]
    else:
        out_specs = [_row_spec(D)]
        out_shape = [jax.ShapeDtypeStruct((T, D), F32)]
    return pl.pallas_call(
        functools.partial(_post_kernel, split_x=len(xs) == 2, split_o=len(os) == 2,
                          split_out=split_out),
        grid=(T // TM,),
        in_specs=row_specs(xs) + row_specs(os) + [
            _mod_spec(), _full_spec((1, D)), _layer_weight_spec((1,) + w_o.shape, 0),
            _layer_weight_spec(w_in.shape, layer), _layer_weight_spec(w_out.shape, layer)],
        out_specs=out_specs,
        out_shape=out_shape,
        compiler_params=_cparams(("arbitrary",)),
        name="post_mixer_ffn",
    )(*xs, *os, mod, g2, w_o[None], w_in, w_out)


def _axial_tables(rot_dim):
    row = jnp.repeat(jnp.arange(DEC_SEQ // GRID_W), GRID_W).astype(F32)
    col = jnp.tile(jnp.arange(GRID_W), DEC_SEQ // GRID_W).astype(F32)
    n_freq = rot_dim // 4
    inv = ROPE_THETA ** (-jnp.arange(n_freq, dtype=F32) / n_freq)
    ang = jnp.concatenate([row[:, None] * inv, col[:, None] * inv], axis=-1)
    cos, sin = jnp.cos(ang), jnp.sin(ang)
    reps = LANES // rot_dim
    cos_t = jnp.tile(jnp.concatenate([cos, cos], axis=-1), (DEC_BATCH, reps))
    sin_t = jnp.tile(jnp.concatenate([-sin, sin], axis=-1), (DEC_BATCH, reps))
    cos_t = jnp.concatenate([jnp.ones((N_CTX, LANES), F32), cos_t], axis=0)
    sin_t = jnp.concatenate([jnp.zeros((N_CTX, LANES), F32), sin_t], axis=0)
    return cos_t, sin_t


def kernel(x_prompt, x_sample, cache_mla_ckv, cache_mla_krope, cache_diff_k, cache_diff_v, cache_gqa_k, cache_gqa_v, state_lru_h, c, c_ctx, w_mod, b_mod, g_norm1, g_norm2, w_ffn_in, w_ffn_out, mla_w_in, mla_g_cq, mla_g_ckv, mla_w_uq, mla_w_ukv, mla_g_qk, mla_w_o, diff_w_in, diff_g_qk, diff_lambda, diff_g_sub, diff_w_o, gqa_w_in, gqa_g_qk, gqa_w_o, lru_w_in, lru_conv_w, lru_conv_b, lru_w_gate, lru_b_gate, lru_lambda, lru_w_out):
    x_in = (x_prompt.reshape(N_CTX, D), x_sample.reshape(N_LAT, D))
    cond = jnp.concatenate([c_ctx[None, :], c, jnp.zeros((SUBLANES - N_GROUPS, D), F32)], axis=0)
    mod_all = _modulation(cond, w_mod, b_mod)
    cos128, sin128 = _axial_tables(GQA_HD)
    cos64, sin64 = _axial_tables(DIFF_HD)
    w_ffn_in_b = w_ffn_in.astype(BF16)
    w_ffn_out_b = w_ffn_out.astype(BF16)

    def layer_mod(l):
        return mod_all[l, :N_GROUPS].reshape(N_GROUPS, 6, D)

    def post(l, xs, os, w_o):
        out = _post(xs, os, layer_mod(l), g_norm2[l][None, :], w_o.astype(BF16),
                    w_ffn_in_b, w_ffn_out_b, l, split_out=l == DEPTH - 1)
        return out if l == DEPTH - 1 else out[0]

    def attend(q, k, v, *, hq, group, cache, diff, lambda_init):
        common = dict(group=group, diff=diff, lambda_init=lambda_init)
        sub = ATTN_ROWS // group
        o_ctx = _attention(q, k, v, seq_len=SEQ, n_seq=BATCH, row0=0, tq=SEQ, bk=SEQ,
                           sub=min(sub, SEQ), heads_per_step=hq, cache=None, **common)
        o_lat = _attention(q, k, v, seq_len=DEC_SEQ, n_seq=DEC_BATCH, row0=N_CTX, tq=ATTN_TILES * sub,
                           bk=ATTN_BK, sub=sub, heads_per_step=group, cache=cache, **common)
        return o_ctx, o_lat

    l = 0
    w_in = mla_w_in[0]
    kr_cols = w_in[:, MLA_Q_RANK + MLA_KV_RANK:]
    w_in2 = jnp.concatenate([w_in, kr_cols], axis=1).astype(BF16)
    w_uq = mla_w_uq[0].reshape(MLA_Q_RANK, MLA_HEADS, MLA_NOPE + MLA_ROPE)
    wqn = w_uq[:, :, :MLA_NOPE].reshape(MLA_Q_RANK, MLA_HEADS * MLA_NOPE).astype(BF16)
    wqr = w_uq[:, :, MLA_NOPE:].reshape(MLA_Q_RANK, MLA_HEADS * MLA_ROPE).astype(BF16)
    wukv = mla_w_ukv[0].astype(BF16)
    gqk = mla_g_qk[0]
    gqn = gqk[0:1, :MLA_NOPE]
    gqr = jnp.tile(gqk[0:1, MLA_NOPE:], (1, 2))
    gkn = gqk[1:2, :MLA_NOPE]
    gkr = jnp.tile(gqk[1:2, MLA_NOPE:], (1, 2))
    q, k, v, ckv_new, kr_new = _mla_in(*x_in, layer_mod(l), g_norm1[l][None, :], w_in2,
                                       mla_g_cq[0][None, :], mla_g_ckv[0][None, :],
                                       gqn, gqr, gkn, gkr, wqn, wqr, wukv, cos64, sin64)
    kc, vc = _mla_cache(cache_mla_ckv[:, 0].reshape(DEC_BATCH * PAST, MLA_KV_RANK),
                        cache_mla_krope[:, 0].reshape(DEC_BATCH * PAST, MLA_ROPE), wukv, gkn)
    cache = (kc, vc,
             (None, PAST, MLA_DQ), lambda b, hb, qb: (hb, b, 0),
             (None, PAST, 2 * MLA_V), lambda b, hb, qb: (hb, b, 0))
    o = attend(q, k, v, hq=MLA_HEADS, group=1, cache=cache, diff=None, lambda_init=0.0)
    x = post(l, x_in, o, mla_w_o[0])
    new_mla_ckv = ckv_new.reshape(BATCH, 1, SEQ, MLA_KV_RANK)
    new_mla_krope = kr_new.reshape(BATCH, 1, SEQ, MLA_ROPE)

    l = 1
    lambda_init = 0.8 - 0.6 * math.exp(-0.3 * l)
    q, k, v, k_new, v_new = _diff_in(x, layer_mod(l), g_norm1[l][None, :], diff_w_in[0].astype(BF16),
                                     jnp.tile(diff_g_qk[0], (1, 2)), cos64, sin64)
    n_diff = DIFF_HEADS * 2 * DIFF_HD
    cache = (cache_diff_k[:, 0].reshape(DEC_BATCH, PAST, n_diff),
             cache_diff_v[:, 0].reshape(DEC_BATCH, PAST, n_diff),
             (None, PAST, LANES), lambda b, hb, qb: (b, 0, hb),
             (None, PAST, LANES), lambda b, hb, qb: (b, 0, hb))
    o = attend(q, k, v, hq=2 * DIFF_HEADS, group=2, cache=cache,
               diff=(diff_lambda[0], diff_g_sub[0][None, :]), lambda_init=lambda_init)
    x = post(l, (x,), o, diff_w_o[0])
    new_diff_k = k_new.reshape(BATCH, 1, SEQ, DIFF_HEADS, 2, DIFF_HD)
    new_diff_v = v_new.reshape(BATCH, 1, SEQ, DIFF_HEADS, 2 * DIFF_HD)

    l = 2
    q, k, v, k_new, v_new = _gqa_in(x, layer_mod(l), g_norm1[l][None, :], gqa_w_in[0].astype(BF16),
                                    gqa_g_qk[0], cos128, sin128)
    n_kv = GQA_KV_HEADS * GQA_HD
    group = GQA_Q_HEADS // GQA_KV_HEADS
    cache = (cache_gqa_k[:, 0].reshape(DEC_BATCH, PAST, n_kv),
             cache_gqa_v[:, 0].reshape(DEC_BATCH, PAST, n_kv),
             (None, PAST, GQA_HD), lambda b, hb, qb: (b, 0, hb),
             (None, PAST, GQA_HD), lambda b, hb, qb: (b, 0, hb))
    o = attend(q, k, v, hq=GQA_Q_HEADS, group=group, cache=cache, diff=None, lambda_init=0.0)
    x = post(l, (x,), o, gqa_w_o[0])
    new_gqa_k = k_new.reshape(BATCH, 1, SEQ, GQA_KV_HEADS, GQA_HD)
    new_gqa_v = v_new.reshape(BATCH, 1, SEQ, GQA_KV_HEADS, GQA_HD)

    l = 3
    gate, xr = _lru_in(x, layer_mod(l), g_norm1[l][None, :], lru_w_in[0].astype(BF16))
    wg = lru_w_gate[0]
    wg = jnp.concatenate([wg[:, 0], wg[:, 1]], axis=-1).astype(BF16)
    h0 = jnp.concatenate([jnp.zeros((BATCH, 2, D), F32), state_lru_h[:, 0]], axis=0)
    h0 = h0.reshape(BATCH + DEC_BATCH, 2, LRU_BLOCKS, LRU_BLK)
    conv_b = lru_conv_b[0][None, :]
    hf, st_f = _lru_scan(xr, lru_conv_w[0], conv_b, wg[0], lru_b_gate[0, 0], lru_lambda[0, 0][None, :],
                         h0[:, 0], reverse=False)
    y, st_b = _lru_scan(xr, lru_conv_w[0], conv_b, wg[1], lru_b_gate[0, 1], lru_lambda[0, 1][None, :],
                        h0[:, 1], reverse=True, hf=hf, gate=gate)
    y_ctx, y_lat = post(l, (x,), (y,), lru_w_out[0])
    new_lru_h = jnp.stack([st_f[:BATCH].reshape(BATCH, D), st_b[:BATCH].reshape(BATCH, D)],
                          axis=1)[:, None]

    y_prompt = y_ctx.reshape(BATCH, SEQ, D)
    y_sample = y_lat.reshape(DEC_BATCH, DEC_SEQ, D)
    return (y_prompt, y_sample, new_mla_ckv, new_mla_krope, new_diff_k, new_diff_v,
            new_gqa_k, new_gqa_v, new_lru_h)
```

```python
import functools
import math

import jax
import jax.numpy as jnp
from jax import lax
from jax.experimental import pallas as pl
from jax.experimental.pallas import tpu as pltpu

F32 = jnp.float32
BF16 = jnp.bfloat16

D = 1024
BATCH = 32
SEQ = 256
DEPTH = 4
DEC_BATCH = 2
DEC_SEQ = 4096
PAST = 256
GRID_W = 64
EPS = 1e-6
ROPE_THETA = 10000.0
FFN_H = 2816
N_CTX = BATCH * SEQ
N_LAT = DEC_BATCH * DEC_SEQ
T = N_CTX + N_LAT
N_GROUPS = 1 + DEC_BATCH

MLA_HEADS = 8
MLA_NOPE = 128
MLA_ROPE = 64
MLA_V = 128
MLA_Q_RANK = 384
MLA_KV_RANK = 256
MLA_DQ = 256
DIFF_HD = 64
DIFF_HEADS = 8
GQA_HD = 128
GQA_Q_HEADS = 8
GQA_KV_HEADS = 2
LRU_BLOCKS = 8
LRU_BLK = 128
LRU_C = 8.0

LANES = 128
SUBLANES = 8
VMEM_LIMIT = 56 * 1024 * 1024

TM = 512
FFN_CHUNK = 256
LRU_TB = 256
ATTN_TILES = 4
CTX_SEQS = 4
ATTN_BK = 1024
ATTN_ROWS = 1024
LOG2E = 1.4426950408889634


def _cparams(sem):
    return pltpu.CompilerParams(dimension_semantics=sem, vmem_limit_bytes=VMEM_LIMIT)


def _group_of_block(i, rows_per_block):
    n_ctx_blocks = N_CTX // rows_per_block
    per = DEC_SEQ // rows_per_block
    return jnp.maximum(i - (n_ctx_blocks - per), 0) // per


def _rms(x, gain):
    y = x * lax.rsqrt(jnp.mean(x * x, axis=-1, keepdims=True) + EPS)
    return y * gain


def _rms_halves(x, gain):
    lane = lax.broadcasted_iota(jnp.int32, x.shape, 1)
    low = lane < 64
    sq = x * x
    s_low = jnp.sum(jnp.where(low, sq, 0.0), axis=-1, keepdims=True)
    s_all = jnp.sum(sq, axis=-1, keepdims=True)
    ms = jnp.where(low, s_low, s_all - s_low) * (1.0 / 64.0)
    return x * lax.rsqrt(ms + EPS) * gain


def _rope128(x, c, s):
    return x * c + pltpu.roll(x, 64, 1) * s


def _rope64x2(x, c, s):
    lane = lax.broadcasted_iota(jnp.int32, x.shape, 1)
    low = (lane % 64) < 32
    partner = jnp.where(low, pltpu.roll(x, 96, 1), pltpu.roll(x, 32, 1))
    return x * c + partner * s


def _is_ctx_block():
    return pl.program_id(0) < N_CTX // TM


def _norm_mod(x, g_ref, mod_ref, shift_row):
    y = _rms(x, g_ref[...])
    shift = mod_ref[shift_row:shift_row + 1, :]
    scale = mod_ref[shift_row + 1:shift_row + 2, :]
    return (y * (1.0 + scale) + shift).astype(BF16)


def _mod_kernel(cond_ref, w_ref, b_ref, o_ref):
    s = jax.nn.silu(cond_ref[...]).astype(BF16)
    w = w_ref[...].astype(BF16)
    o_ref[...] = jnp.dot(s, w, preferred_element_type=F32) + b_ref[...]


def _modulation(cond, w_mod, b_mod):
    tn = 1536
    return pl.pallas_call(
        _mod_kernel,
        grid=(DEPTH, 6 * D // tn),
        in_specs=[
            pl.BlockSpec((SUBLANES, D), lambda l, j: (0, 0)),
            pl.BlockSpec((None, D, tn), lambda l, j: (l, 0, j)),
            pl.BlockSpec((None, 1, tn), lambda l, j: (l, 0, j)),
        ],
        out_specs=pl.BlockSpec((None, SUBLANES, tn), lambda l, j: (l, 0, j)),
        out_shape=jax.ShapeDtypeStruct((DEPTH, SUBLANES, 6 * D), F32),
        compiler_params=_cparams(("arbitrary", "arbitrary")),
        name="adaln_mod",
    )(cond, w_mod, b_mod.reshape(DEPTH, 1, 6 * D))


def _row_spec(width):
    return pl.BlockSpec((TM, width), lambda i: (i, 0))


def _ctx_row_spec(width):
    return pl.BlockSpec((TM, width), lambda i: (jnp.minimum(i, N_CTX // TM - 1), 0))


def _lat_row_spec(width):
    return pl.BlockSpec((TM, width), lambda i: (jnp.maximum(i - N_CTX // TM, 0), 0))


def _full_spec(shape):
    return pl.BlockSpec(shape, lambda i: (0,) * len(shape))


def _mod_spec():
    return pl.BlockSpec((None, 6, D), lambda i: (_group_of_block(i, TM), 0, 0))


def _heads_spec(heads, width):
    return pl.BlockSpec((heads, TM, width), lambda i: (0, i, 0))


N_BLOCKS = T // TM
N_CTX_BLOCKS = N_CTX // TM


def _proj_block(i):
    return jnp.minimum(i, N_BLOCKS - 1)


def _epi_block(i):
    return jnp.maximum(i - 1, 0)


def _proj_row_spec(width):
    return pl.BlockSpec((TM, width), lambda i: (_proj_block(i), 0))


def _proj_mod_spec():
    return pl.BlockSpec((None, 6, D), lambda i: (_group_of_block(_proj_block(i), TM), 0, 0))


def _epi_row_spec(width):
    return pl.BlockSpec((TM, width), lambda i: (_epi_block(i), 0))


def _epi_heads_spec(heads, width):
    return pl.BlockSpec((heads, TM, width), lambda i: (0, _epi_block(i), 0))


def _epi_ctx_row_spec(width):
    return pl.BlockSpec((TM, width), lambda i: (jnp.minimum(_epi_block(i), N_CTX_BLOCKS - 1), 0))


def _epi_is_ctx():
    i = pl.program_id(0)
    return jnp.logical_and(i >= 1, i <= N_CTX_BLOCKS)


def _two_stage(step, proj_a, proj_b):
    i = pl.program_id(0)

    @pl.when(i == 0)
    def _():
        proj_b[...] = jnp.zeros(proj_b.shape, F32)

    @pl.when(i % 2 == 0)
    def _():
        step(proj_b, proj_a)

    @pl.when(i % 2 == 1)
    def _():
        step(proj_a, proj_b)


def _gqa_in_kernel(x_ref, mod_ref, g_ref, w_ref, gqk_ref, cos_ref, sin_ref,
                   q_ref, k_ref, v_ref, ks_ref, vs_ref, proj_a, proj_b):
    n_q = GQA_Q_HEADS * GQA_HD
    n_kv = GQA_KV_HEADS * GQA_HD

    def step(prev, cur):
        h = _norm_mod(x_ref[...], g_ref, mod_ref, 0)
        cur[...] = jnp.dot(h, w_ref[...], preferred_element_type=F32)
        c = cos_ref[...]
        s = sin_ref[...]
        scale = LOG2E * GQA_HD ** -0.5
        for hd in range(GQA_Q_HEADS):
            q = _rms(prev[:, hd * GQA_HD:(hd + 1) * GQA_HD], gqk_ref[0:1, :])
            q_ref[hd] = (_rope128(q, c, s) * scale).astype(BF16)
        ks = []
        for hd in range(GQA_KV_HEADS):
            lo = n_q + hd * GQA_HD
            ks.append(_rms(prev[:, lo:lo + GQA_HD], gqk_ref[1:2, :]))
            k_ref[hd] = _rope128(ks[-1], c, s).astype(BF16)
            lo = n_q + n_kv + hd * GQA_HD
            v_ref[hd] = _with_ones(prev[:, lo:lo + GQA_HD].astype(BF16))

        @pl.when(_epi_is_ctx())
        def _():
            for hd in range(GQA_KV_HEADS):
                ks_ref[:, hd * GQA_HD:(hd + 1) * GQA_HD] = ks[hd]
            vs_ref[...] = prev[:, n_q + n_kv:]

    _two_stage(step, proj_a, proj_b)


def _proj_scratch(width):
    return [pltpu.VMEM((TM, width), F32), pltpu.VMEM((TM, width), F32)]


def _gqa_in(x, mod, g1, w_in, g_qk, cos, sin):
    n_kv = GQA_KV_HEADS * GQA_HD
    return pl.pallas_call(
        _gqa_in_kernel,
        grid=(N_BLOCKS + 1,),
        in_specs=[_proj_row_spec(D), _proj_mod_spec(), _full_spec((1, D)), _full_spec(w_in.shape),
                  _full_spec((2, GQA_HD)), _epi_row_spec(LANES), _epi_row_spec(LANES)],
        out_specs=[_epi_heads_spec(GQA_Q_HEADS, GQA_HD), _epi_heads_spec(GQA_KV_HEADS, GQA_HD),
                   _epi_heads_spec(GQA_KV_HEADS, 2 * GQA_HD), _epi_ctx_row_spec(n_kv),
                   _epi_ctx_row_spec(n_kv)],
        scratch_shapes=_proj_scratch(w_in.shape[1]),
        out_shape=[jax.ShapeDtypeStruct((GQA_Q_HEADS, T, GQA_HD), BF16),
                   jax.ShapeDtypeStruct((GQA_KV_HEADS, T, GQA_HD), BF16),
                   jax.ShapeDtypeStruct((GQA_KV_HEADS, T, 2 * GQA_HD), BF16),
                   jax.ShapeDtypeStruct((N_CTX, n_kv), F32),
                   jax.ShapeDtypeStruct((N_CTX, n_kv), F32)],
        compiler_params=_cparams(("arbitrary",)),
        name="gqa_in",
    )(x, mod, g1, w_in, g_qk, cos, sin)


def _diff_in_kernel(x_ref, mod_ref, g_ref, w_ref, gqk_ref, cos_ref, sin_ref,
                    q_ref, k_ref, v_ref, ks_ref, vs_ref, proj_a, proj_b):
    n = DIFF_HEADS * 2 * DIFF_HD

    def step(prev, cur):
        h = _norm_mod(x_ref[...], g_ref, mod_ref, 0)
        cur[...] = jnp.dot(h, w_ref[...], preferred_element_type=F32)
        c = cos_ref[...]
        s = sin_ref[...]
        scale = LOG2E * DIFF_HD ** -0.5
        lane = lax.broadcasted_iota(jnp.int32, (TM, LANES), 1)
        first = (lane % DIFF_HD) < DIFF_HD // 2

        def rms_sub(x, gain):
            sq = x * x
            s0 = jnp.sum(jnp.where(first, sq, 0.0), axis=-1, keepdims=True)
            s_all = jnp.sum(sq, axis=-1, keepdims=True)
            ms = jnp.where(first, s0, s_all - s0) * (1.0 / DIFF_HD)
            return x * lax.rsqrt(ms + EPS) * gain

        ks = []
        for hd in range(DIFF_HEADS):
            sl = slice(hd * LANES, (hd + 1) * LANES)
            q = _rope128(rms_sub(prev[:, sl], gqk_ref[0:1, :]), c, s) * scale
            q_ref[2 * hd] = jnp.where(first, q, 0.0).astype(BF16)
            q_ref[2 * hd + 1] = jnp.where(first, 0.0, q).astype(BF16)
            ks.append(rms_sub(prev[:, n + hd * LANES:n + (hd + 1) * LANES], gqk_ref[1:2, :]))
            k_ref[hd] = _rope128(ks[-1], c, s).astype(BF16)
            v_ref[hd] = _with_ones(prev[:, 2 * n + hd * LANES:2 * n + (hd + 1) * LANES].astype(BF16))

        @pl.when(_epi_is_ctx())
        def _():
            for hd in range(DIFF_HEADS):
                ks_ref[:, hd * LANES:(hd + 1) * LANES] = ks[hd]
            vs_ref[...] = prev[:, 2 * n:]

    _two_stage(step, proj_a, proj_b)


def _diff_in(x, mod, g1, w_in, g_qk2, cos, sin):
    n = DIFF_HEADS * 2 * DIFF_HD
    return pl.pallas_call(
        _diff_in_kernel,
        grid=(N_BLOCKS + 1,),
        in_specs=[_proj_row_spec(D), _proj_mod_spec(), _full_spec((1, D)), _full_spec(w_in.shape),
                  _full_spec((2, LANES)), _epi_row_spec(LANES), _epi_row_spec(LANES)],
        out_specs=[_epi_heads_spec(2 * DIFF_HEADS, LANES), _epi_heads_spec(DIFF_HEADS, LANES),
                   _epi_heads_spec(DIFF_HEADS, 2 * LANES), _epi_ctx_row_spec(n), _epi_ctx_row_spec(n)],
        scratch_shapes=_proj_scratch(w_in.shape[1]),
        out_shape=[jax.ShapeDtypeStruct((2 * DIFF_HEADS, T, LANES), BF16),
                   jax.ShapeDtypeStruct((DIFF_HEADS, T, LANES), BF16),
                   jax.ShapeDtypeStruct((DIFF_HEADS, T, 2 * LANES), BF16),
                   jax.ShapeDtypeStruct((N_CTX, n), F32),
                   jax.ShapeDtypeStruct((N_CTX, n), F32)],
        compiler_params=_cparams(("arbitrary",)),
        name="diff_in",
    )(x, mod, g1, w_in, g_qk2, cos, sin)


def _mla_keys_values(ckv_bf16, kr2, w_ukv_ref, gk_ref, k_ref, v_ref):
    kv = jnp.dot(ckv_bf16, w_ukv_ref[...], preferred_element_type=F32)
    lane = lax.broadcasted_iota(jnp.int32, kr2.shape, 1)
    low = lane < MLA_ROPE
    kr_low = jnp.where(low, kr2, 0.0).astype(BF16)
    kr_high = jnp.where(low, 0.0, kr2).astype(BF16)
    width = MLA_NOPE + MLA_V
    for hd in range(MLA_HEADS):
        k_nope = _rms(kv[:, hd * width:hd * width + MLA_NOPE], gk_ref[...])
        k_ref[hd, :, 0:MLA_NOPE] = k_nope.astype(BF16)
        k_ref[hd, :, MLA_NOPE:MLA_DQ] = kr_low if hd % 2 == 0 else kr_high
        v_ref[hd] = _with_ones(kv[:, hd * width + MLA_NOPE:(hd + 1) * width].astype(BF16))


def _mla_in_kernel(xc_ref, xl_ref, mod_ref, g_ref, w_ref, gcq_ref, gckv_ref, gqn_ref, gqr_ref,
                   gkn_ref, gkr_ref, wqn_ref, wqr_ref, wukv_ref, cos_ref, sin_ref,
                   q_ref, k_ref, v_ref, ckv_ref, kr_ref, proj_a, proj_b):
    def step(prev, cur):
        x = jnp.where(_is_ctx_block(), xc_ref[...], xl_ref[...])
        h = _norm_mod(x, g_ref, mod_ref, 0)
        cur[...] = jnp.dot(h, w_ref[...], preferred_element_type=F32)
        c = cos_ref[...]
        s = sin_ref[...]
        cq = _rms(prev[:, 0:MLA_Q_RANK], gcq_ref[...]).astype(BF16)
        ckv = _rms(prev[:, MLA_Q_RANK:MLA_Q_RANK + MLA_KV_RANK], gckv_ref[...])
        kr2 = _rms_halves(prev[:, MLA_Q_RANK + MLA_KV_RANK:], gkr_ref[...])
        _mla_keys_values(ckv.astype(BF16), _rope64x2(kr2, c, s), wukv_ref, gkn_ref, k_ref, v_ref)

        scale = LOG2E * (MLA_NOPE + MLA_ROPE) ** -0.5
        qn = jnp.dot(cq, wqn_ref[...], preferred_element_type=F32)
        qr = jnp.dot(cq, wqr_ref[...], preferred_element_type=F32)
        lane = lax.broadcasted_iota(jnp.int32, (TM, LANES), 1)
        low = lane < MLA_ROPE
        for pair in range(MLA_HEADS // 2):
            r = _rms_halves(qr[:, pair * LANES:(pair + 1) * LANES], gqr_ref[...])
            r = _rope64x2(r, c, s) * scale
            q_ref[2 * pair, :, MLA_NOPE:MLA_DQ] = jnp.where(low, r, 0.0).astype(BF16)
            q_ref[2 * pair + 1, :, MLA_NOPE:MLA_DQ] = jnp.where(low, 0.0, r).astype(BF16)
        for hd in range(MLA_HEADS):
            q = _rms(qn[:, hd * MLA_NOPE:(hd + 1) * MLA_NOPE], gqn_ref[...]) * scale
            q_ref[hd, :, 0:MLA_NOPE] = q.astype(BF16)

        @pl.when(_epi_is_ctx())
        def _():
            ckv_ref[...] = ckv
            kr_ref[...] = kr2[:, 0:MLA_ROPE]

    _two_stage(step, proj_a, proj_b)


def _mla_in(x_ctx, x_lat, mod, g1, w_in, g_cq, g_ckv, gqn, gqr, gkn, gkr, wqn, wqr, wukv, cos, sin):
    n_lat_blocks = N_BLOCKS - N_CTX_BLOCKS
    return pl.pallas_call(
        _mla_in_kernel,
        grid=(N_BLOCKS + 1,),
        in_specs=[_ctx_row_spec(D),
                  pl.BlockSpec((TM, D), lambda i: (jnp.clip(i - N_CTX_BLOCKS, 0, n_lat_blocks - 1), 0)),
                  _proj_mod_spec(), _full_spec((1, D)), _full_spec(w_in.shape),
                  _full_spec((1, MLA_Q_RANK)), _full_spec((1, MLA_KV_RANK)),
                  _full_spec((1, LANES)), _full_spec((1, LANES)), _full_spec((1, LANES)),
                  _full_spec((1, LANES)), _full_spec(wqn.shape), _full_spec(wqr.shape),
                  _full_spec(wukv.shape), _epi_row_spec(LANES), _epi_row_spec(LANES)],
        out_specs=[_epi_heads_spec(MLA_HEADS, MLA_DQ), _epi_heads_spec(MLA_HEADS, MLA_DQ),
                   _epi_heads_spec(MLA_HEADS, 2 * MLA_V), _epi_ctx_row_spec(MLA_KV_RANK),
                   _epi_ctx_row_spec(MLA_ROPE)],
        scratch_shapes=_proj_scratch(w_in.shape[1]),
        out_shape=[jax.ShapeDtypeStruct((MLA_HEADS, T, MLA_DQ), BF16),
                   jax.ShapeDtypeStruct((MLA_HEADS, T, MLA_DQ), BF16),
                   jax.ShapeDtypeStruct((MLA_HEADS, T, 2 * MLA_V), BF16),
                   jax.ShapeDtypeStruct((N_CTX, MLA_KV_RANK), F32),
                   jax.ShapeDtypeStruct((N_CTX, MLA_ROPE), F32)],
        compiler_params=_cparams(("arbitrary",)),
        name="mla_in",
    )(x_ctx, x_lat, mod, g1, w_in, g_cq, g_ckv, gqn, gqr, gkn, gkr, wqn, wqr, wukv, cos, sin)


def _mla_cache_kernel(ckv_ref, kr_ref, wukv_ref, gkn_ref, k_ref, v_ref):
    kr = kr_ref[...]
    kr2 = jnp.concatenate([kr, kr], axis=-1)
    _mla_keys_values(ckv_ref[...].astype(BF16), kr2, wukv_ref, gkn_ref, k_ref, v_ref)


def _mla_cache(ckv, kr, wukv, gkn):
    rows = ckv.shape[0]
    return pl.pallas_call(
        _mla_cache_kernel,
        grid=(1,),
        in_specs=[_full_spec(ckv.shape), _full_spec(kr.shape), _full_spec(wukv.shape),
                  _full_spec((1, LANES))],
        out_specs=[_full_spec((MLA_HEADS, rows, MLA_DQ)), _full_spec((MLA_HEADS, rows, 2 * MLA_V))],
        out_shape=[jax.ShapeDtypeStruct((MLA_HEADS, rows, MLA_DQ), BF16),
                   jax.ShapeDtypeStruct((MLA_HEADS, rows, 2 * MLA_V), BF16)],
        compiler_params=_cparams(("arbitrary",)),
        name="mla_cache_kv",
    )(ckv, kr, wukv, gkn)


def _lru_in_kernel(x_ref, mod_ref, g_ref, w_ref, gate_ref, xr_ref):
    h = _norm_mod(x_ref[...], g_ref, mod_ref, 0)
    proj = jnp.dot(h, w_ref[...], preferred_element_type=F32)
    gate_ref[...] = proj[:, 0:D]
    xr_ref[...] = proj[:, D:2 * D]


def _lru_in(x, mod, g1, w_in):
    return pl.pallas_call(
        _lru_in_kernel,
        grid=(T // TM,),
        in_specs=[_row_spec(D), _mod_spec(), _full_spec((1, D)), _full_spec(w_in.shape)],
        out_specs=[_row_spec(D), _row_spec(D)],
        out_shape=[jax.ShapeDtypeStruct((T, D), F32), jax.ShapeDtypeStruct((T, D), F32)],
        compiler_params=_cparams(("arbitrary",)),
        name="lru_in",
    )(x, mod, g1, w_in)


def _with_ones(v):
    return jnp.concatenate([v, jnp.ones_like(v)], axis=-1)


def _softmax_pv(q, key_blocks, dv):
    nt = (((1,), (1,)), ((), ()))
    m = acc = None
    for k, v1 in key_blocks:
        s = lax.dot_general(q, k, nt, preferred_element_type=F32)
        mb = jnp.max(s, axis=-1, keepdims=True)
        if m is None:
            m = mb
            acc = jnp.dot(jnp.exp2(s - m).astype(BF16), v1, preferred_element_type=F32)
        else:
            m_new = jnp.maximum(m, mb)
            alpha = jnp.exp2(m - m_new)
            acc = alpha * acc + jnp.dot(jnp.exp2(s - m_new).astype(BF16), v1,
                                        preferred_element_type=F32)
            m = m_new
    return acc[:, 0:dv] / acc[:, dv:2 * dv]


def _attn_kernel(*refs, heads, group, seqs, sub, bk, has_cache, diff, dv, lambda_init):
    it = iter(refs)
    q_ref, kn_ref, vn_ref = next(it), next(it), next(it)
    kc_ref = vc_ref = lam_ref = gsub_ref = None
    if has_cache:
        kc_ref, vc_ref = next(it), next(it)
    if diff:
        lam_ref, gsub_ref = next(it), next(it)
    o_ref = next(it)

    tq = q_ref.shape[1] // seqs
    seq_len = kn_ref.shape[1] // seqs
    stacked = min(group, heads)
    cached = []
    if has_cache:
        vc = vc_ref[...].astype(BF16)
        cached = [(kc_ref[...].astype(BF16), vc if vc.shape[-1] == 2 * dv else _with_ones(vc))]
    if diff:
        lam = lam_ref[...]
        lam_full = (jnp.exp(jnp.sum(lam[0:1] * lam[1:2], axis=-1, keepdims=True))
                    - jnp.exp(jnp.sum(lam[2:3] * lam[3:4], axis=-1, keepdims=True)) + lambda_init)

    for sq, kv in [(sq, kv) for sq in range(seqs) for kv in range(max(heads // group, 1))]:
        def body(i, carry, sq=sq, kv=kv):
            rows = pl.ds(pl.multiple_of(sq * tq + i * sub, sub), sub)
            q = jnp.concatenate([q_ref[kv * stacked + g, rows, :] for g in range(stacked)], axis=0)
            keys = [slice(sq * seq_len + j * bk, sq * seq_len + (j + 1) * bk)
                    for j in range(seq_len // bk)]
            blocks = cached + [(kn_ref[kv, ks, :], vn_ref[kv, ks, :]) for ks in keys]
            o = _softmax_pv(q, blocks, dv)
            if diff:
                od = o[0:sub] - lam_full * o[sub:2 * sub]
                od = _rms(od, gsub_ref[...]) * (1.0 - lambda_init)
                o_ref[rows, kv * dv:(kv + 1) * dv] = od.astype(BF16)
            else:
                for g in range(stacked):
                    hd = kv * stacked + g
                    o_ref[rows, hd * dv:(hd + 1) * dv] = o[g * sub:(g + 1) * sub].astype(BF16)
            return carry
        lax.fori_loop(0, tq // sub, body, 0, unroll=True)


def _attention(q, kn, vn, *, seq_len, n_seq, row0, tq, sub, bk, heads_per_step, group, cache=None,
               diff=None, lambda_init=0.0, seqs=1):
    hq, _, dq = q.shape
    hkv, _, dv1 = vn.shape
    dv = dv1 // 2
    kv_per_step = max(heads_per_step // group, 1)
    n_hblk = hq // heads_per_step
    n_qblk = seq_len // tq
    assert seqs == 1 or (n_qblk == 1 and cache is None and row0 % (seqs * seq_len) == 0)
    n_seq //= seqs
    seq0 = row0 // (seqs * seq_len)
    qblk0 = row0 // (seqs * tq)

    def q_map(b, hb, qb):
        return (hb, qblk0 + b * n_qblk + qb, 0)

    def kv_map(b, hb, qb):
        return ((hb * heads_per_step) // (group * kv_per_step), seq0 + b, 0)

    in_specs = [pl.BlockSpec((heads_per_step, seqs * tq, dq), q_map),
                pl.BlockSpec((kv_per_step, seqs * seq_len, dq), kv_map),
                pl.BlockSpec((kv_per_step, seqs * seq_len, dv1), kv_map)]
    args = [q, kn, vn]
    if cache is not None:
        kc, vc, kc_block, kc_map, vc_block, vc_map = cache
        in_specs += [pl.BlockSpec(kc_block, kc_map), pl.BlockSpec(vc_block, vc_map)]
        args += [kc, vc]
    out_heads = heads_per_step
    if diff is not None:
        lam, gsub = diff
        in_specs += [pl.BlockSpec(lam.shape, lambda b, hb, qb: (0, 0)),
                     pl.BlockSpec(gsub.shape, lambda b, hb, qb: (0, 0))]
        args += [lam, gsub]
        out_heads = heads_per_step // 2
    n_out = (hq // 2 if diff is not None else hq) * dv
    kernel = functools.partial(_attn_kernel, heads=heads_per_step, group=group, seqs=seqs, sub=sub,
                               bk=bk, has_cache=cache is not None, diff=diff is not None, dv=dv,
                               lambda_init=lambda_init)
    return pl.pallas_call(
        kernel,
        grid=(n_seq, n_hblk, n_qblk),
        in_specs=in_specs,
        out_specs=pl.BlockSpec((seqs * tq, out_heads * dv),
                               lambda b, hb, qb: (b * n_qblk + qb, hb)),
        out_shape=jax.ShapeDtypeStruct((n_seq * seqs * seq_len, n_out), BF16),
        compiler_params=_cparams(("arbitrary", "arbitrary", "arbitrary")),
        name="attention",
    )(*args)


def _lru_seq_pos(i):
    n_ctx_blocks = N_CTX // LRU_TB
    per_ctx = SEQ // LRU_TB
    per_lat = DEC_SEQ // LRU_TB
    pos = jnp.where(i < n_ctx_blocks, i % per_ctx, (i - n_ctx_blocks) % per_lat)
    length = jnp.where(i < n_ctx_blocks, per_ctx, per_lat)
    return pos == 0, pos == length - 1


def _lru_seq_of_block(i):
    n_ctx_blocks = N_CTX // LRU_TB
    return jnp.where(i < n_ctx_blocks, i // (SEQ // LRU_TB),
                     BATCH + (i - n_ctx_blocks) // (DEC_SEQ // LRU_TB))


LRU_PITCH = LRU_TB + SUBLANES


def _lru_scan_kernel(x_ref, prev_ref, next_ref, cw_ref, cb_ref, wg_ref, bg_ref, lam_ref, h0_ref,
                     *rest, reverse, combine):
    if combine:
        hf_ref, gate_ref, y_ref, st_ref, a_s, u_s, h_s, carry = rest
    else:
        y_ref, st_ref, a_s, u_s, h_s, carry = rest
    j = pl.program_id(0)
    i = (T // LRU_TB - 1 - j) if reverse else j
    first, last = _lru_seq_pos(i)
    starts = last if reverse else first
    ends = first if reverse else last

    x = x_ref[...]
    before = jnp.where(first, 0.0, prev_ref[SUBLANES - 1:SUBLANES, :])
    after = jnp.where(last, 0.0, next_ref[0:2, :])
    row = lax.broadcasted_iota(jnp.int32, (SUBLANES, D), 0)

    def shifted(k, fix):
        y = pltpu.roll(x, (-k) % LRU_TB, 0)
        if k < 0:
            return jnp.concatenate([fix(y[0:SUBLANES]), y[SUBLANES:]], axis=0)
        return jnp.concatenate([y[0:LRU_TB - SUBLANES], fix(y[LRU_TB - SUBLANES:])], axis=0)

    taps = [
        shifted(-1, lambda t: jnp.where(row == 0, before, t)),
        x,
        shifted(1, lambda t: jnp.where(row == SUBLANES - 1, after[0:1], t)),
        shifted(2, lambda t: jnp.where(row == SUBLANES - 2, after[0:1],
                                       jnp.where(row == SUBLANES - 1, after[1:2], t))),
    ]
    xr = taps[0] * cw_ref[0:1, :]
    for t in range(1, 4):
        xr = xr + taps[t] * cw_ref[t:t + 1, :]
    xr = xr + cb_ref[...]

    xr_b = xr.astype(BF16)
    lam = lam_ref[...]
    neg = -lam
    softplus = jnp.maximum(neg, 0.0) + jnp.log1p(jnp.exp(-jnp.abs(neg)))
    for n in range(LRU_BLOCKS):
        sl = slice(n * LRU_BLK, (n + 1) * LRU_BLK)
        g = jnp.dot(xr_b[:, sl], wg_ref[n], preferred_element_type=F32)
        r = jax.nn.sigmoid(g[:, 0:LRU_BLK] + bg_ref[0:1, sl])
        gi = jax.nn.sigmoid(g[:, LRU_BLK:] + bg_ref[1:2, sl])
        log_a = -LRU_C * r * softplus[:, sl]
        th = jnp.tanh(log_a)
        one_minus_a2 = -2.0 * th / (1.0 - th)
        rows_n = slice(n * LRU_PITCH, n * LRU_PITCH + LRU_TB)
        a_s[rows_n, :] = jnp.exp(log_a)
        root = jnp.where(one_minus_a2 > 0.0, one_minus_a2 * lax.rsqrt(one_minus_a2), 0.0)
        u_s[rows_n, :] = root * (gi * xr[:, sl])

    @pl.when(starts)
    def _():
        carry[...] = h0_ref[...]

    def body(t, h):
        tt = (LRU_TB - 1 - t) if reverse else t
        rows = pl.ds(tt, LRU_BLOCKS, stride=LRU_PITCH)
        h = a_s[rows, :] * h + u_s[rows, :]
        h_s[rows, :] = h
        return h

    h_end = lax.fori_loop(0, LRU_TB, body, carry[...], unroll=8)
    carry[...] = h_end

    for n in range(LRU_BLOCKS):
        sl = slice(n * LRU_BLK, (n + 1) * LRU_BLK)
        hs = h_s[n * LRU_PITCH:n * LRU_PITCH + LRU_TB, :]
        if combine:
            y_ref[:, sl] = ((hf_ref[:, sl] + hs) * jax.nn.gelu(gate_ref[:, sl])).astype(BF16)
        else:
            y_ref[:, sl] = hs

    @pl.when(ends)
    def _():
        st_ref[...] = h_end


def _lru_scan(xr, conv_w, conv_b, w_gate, b_gate, lam, h0, *, reverse, hf=None, gate=None):
    nb = T // LRU_TB
    hb = LRU_TB // SUBLANES
    n_halo = T // SUBLANES

    def blk(j):
        return (nb - 1 - j) if reverse else j

    in_specs = [
        pl.BlockSpec((LRU_TB, D), lambda j: (blk(j), 0)),
        pl.BlockSpec((SUBLANES, D), lambda j: (jnp.maximum(blk(j) * hb - 1, 0), 0)),
        pl.BlockSpec((SUBLANES, D), lambda j: (jnp.minimum((blk(j) + 1) * hb, n_halo - 1), 0)),
        pl.BlockSpec((4, D), lambda j: (0, 0)),
        pl.BlockSpec((1, D), lambda j: (0, 0)),
        pl.BlockSpec((LRU_BLOCKS, LRU_BLK, 2 * LRU_BLK), lambda j: (0, 0, 0)),
        pl.BlockSpec((2, D), lambda j: (0, 0)),
        pl.BlockSpec((1, D), lambda j: (0, 0)),
        pl.BlockSpec((None, LRU_BLOCKS, LRU_BLK), lambda j: (_lru_seq_of_block(blk(j)), 0, 0)),
    ]
    args = [xr, xr, xr, conv_w, conv_b, w_gate, b_gate, lam, h0]
    combine = hf is not None
    if combine:
        in_specs += [pl.BlockSpec((LRU_TB, D), lambda j: (blk(j), 0)),
                     pl.BlockSpec((LRU_TB, D), lambda j: (blk(j), 0))]
        args += [hf, gate]
    n_seq = BATCH + DEC_BATCH
    return pl.pallas_call(
        functools.partial(_lru_scan_kernel, reverse=reverse, combine=combine),
        grid=(nb,),
        in_specs=in_specs,
        out_specs=[pl.BlockSpec((LRU_TB, D), lambda j: (blk(j), 0)),
                   pl.BlockSpec((None, LRU_BLOCKS, LRU_BLK),
                                lambda j: (_lru_seq_of_block(blk(j)), 0, 0))],
        out_shape=[jax.ShapeDtypeStruct((T, D), BF16 if combine else F32),
                   jax.ShapeDtypeStruct((n_seq, LRU_BLOCKS, LRU_BLK), F32)],
        scratch_shapes=[pltpu.VMEM((LRU_BLOCKS * LRU_PITCH, LRU_BLK), F32),
                        pltpu.VMEM((LRU_BLOCKS * LRU_PITCH, LRU_BLK), F32),
                        pltpu.VMEM((LRU_BLOCKS * LRU_PITCH, LRU_BLK), F32),
                        pltpu.VMEM((LRU_BLOCKS, LRU_BLK), F32)],
        compiler_params=_cparams(("arbitrary",)),
        name="lru_scan_bwd" if reverse else "lru_scan_fwd",
    )(*args)


def _post_kernel(*refs, split_x, split_o, split_out):
    it = iter(refs)

    def rows(split):
        if split:
            return jnp.where(_is_ctx_block(), next(it)[...], next(it)[...])
        return next(it)[...]

    x = rows(split_x)
    o = rows(split_o)
    mod_ref, g2_ref, wo_ref, win_ref, wout_ref = (next(it) for _ in range(5))
    y_refs = list(it)
    x1 = x + mod_ref[2:3, :] * jnp.dot(o, wo_ref[...], preferred_element_type=F32)
    h = _rms(x1, g2_ref[...])
    h = (h * (1.0 + mod_ref[4:5, :]) + mod_ref[3:4, :]).astype(BF16)
    acc = jnp.zeros((TM, D), F32)
    for c in range(FFN_H // FFN_CHUNK):
        lo = c * FFN_CHUNK
        g = jnp.dot(h, win_ref[:, lo:lo + FFN_CHUNK], preferred_element_type=F32)
        u = jnp.dot(h, win_ref[:, FFN_H + lo:FFN_H + lo + FFN_CHUNK], preferred_element_type=F32)
        a = (jax.nn.silu(g) * u).astype(BF16)
        acc = acc + jnp.dot(a, wout_ref[lo:lo + FFN_CHUNK, :], preferred_element_type=F32)
    y = x1 + mod_ref[5:6, :] * acc
    if split_out:
        @pl.when(_is_ctx_block())
        def _():
            y_refs[0][...] = y

        @pl.when(jnp.logical_not(_is_ctx_block()))
        def _():
            y_refs[1][...] = y
    else:
        y_refs[0][...] = y


def _layer_weight_spec(shape, layer):
    return pl.BlockSpec((None,) + shape[1:], lambda i: (layer,) + (0,) * (len(shape) - 1),
                        pipeline_mode=pl.Buffered(1))


def _post(xs, os, mod, g2, w_o, w_in, w_out, layer, *, split_out):
    def row_specs(arrays):
        return [_ctx_row_spec(D), _lat_row_spec(D)] if len(arrays) == 2 else [_row_spec(D)]

    if split_out:
        out_specs = [_ctx_row_spec(D), _lat_row_spec(D)]
        out_shape = [jax.ShapeDtypeStruct((N_CTX, D), F32), jax.ShapeDtypeStruct((N_LAT, D), F32)]
    else:
        out_specs = [_row_spec(D)]
        out_shape = [jax.ShapeDtypeStruct((T, D), F32)]
    return pl.pallas_call(
        functools.partial(_post_kernel, split_x=len(xs) == 2, split_o=len(os) == 2,
                          split_out=split_out),
        grid=(T // TM,),
        in_specs=row_specs(xs) + row_specs(os) + [
            _mod_spec(), _full_spec((1, D)), _layer_weight_spec((1,) + w_o.shape, 0),
            _layer_weight_spec(w_in.shape, layer), _layer_weight_spec(w_out.shape, layer)],
        out_specs=out_specs,
        out_shape=out_shape,
        compiler_params=_cparams(("arbitrary",)),
        name="post_mixer_ffn",
    )(*xs, *os, mod, g2, w_o[None], w_in, w_out)


def _axial_tables(rot_dim):
    row = jnp.repeat(jnp.arange(DEC_SEQ // GRID_W), GRID_W).astype(F32)
    col = jnp.tile(jnp.arange(GRID_W), DEC_SEQ // GRID_W).astype(F32)
    n_freq = rot_dim // 4
    inv = ROPE_THETA ** (-jnp.arange(n_freq, dtype=F32) / n_freq)
    ang = jnp.concatenate([row[:, None] * inv, col[:, None] * inv], axis=-1)
    cos, sin = jnp.cos(ang), jnp.sin(ang)
    reps = LANES // rot_dim
    cos_t = jnp.tile(jnp.concatenate([cos, cos], axis=-1), (DEC_BATCH, reps))
    sin_t = jnp.tile(jnp.concatenate([-sin, sin], axis=-1), (DEC_BATCH, reps))
    cos_t = jnp.concatenate([jnp.ones((N_CTX, LANES), F32), cos_t], axis=0)
    sin_t = jnp.concatenate([jnp.zeros((N_CTX, LANES), F32), sin_t], axis=0)
    return cos_t, sin_t


def kernel(x_prompt, x_sample, cache_mla_ckv, cache_mla_krope, cache_diff_k, cache_diff_v, cache_gqa_k, cache_gqa_v, state_lru_h, c, c_ctx, w_mod, b_mod, g_norm1, g_norm2, w_ffn_in, w_ffn_out, mla_w_in, mla_g_cq, mla_g_ckv, mla_w_uq, mla_w_ukv, mla_g_qk, mla_w_o, diff_w_in, diff_g_qk, diff_lambda, diff_g_sub, diff_w_o, gqa_w_in, gqa_g_qk, gqa_w_o, lru_w_in, lru_conv_w, lru_conv_b, lru_w_gate, lru_b_gate, lru_lambda, lru_w_out):
    x_in = (x_prompt.reshape(N_CTX, D), x_sample.reshape(N_LAT, D))
    cond = jnp.concatenate([c_ctx[None, :], c, jnp.zeros((SUBLANES - N_GROUPS, D), F32)], axis=0)
    mod_all = _modulation(cond, w_mod, b_mod)
    cos128, sin128 = _axial_tables(GQA_HD)
    cos64, sin64 = _axial_tables(DIFF_HD)
    w_ffn_in_b = w_ffn_in.astype(BF16)
    w_ffn_out_b = w_ffn_out.astype(BF16)

    def layer_mod(l):
        return mod_all[l, :N_GROUPS].reshape(N_GROUPS, 6, D)

    def post(l, xs, os, w_o):
        out = _post(xs, os, layer_mod(l), g_norm2[l][None, :], w_o.astype(BF16),
                    w_ffn_in_b, w_ffn_out_b, l, split_out=l == DEPTH - 1)
        return out if l == DEPTH - 1 else out[0]

    def attend(q, k, v, *, hq, group, cache, diff, lambda_init):
        common = dict(group=group, diff=diff, lambda_init=lambda_init)
        sub = ATTN_ROWS // group
        o_ctx = _attention(q, k, v, seq_len=SEQ, n_seq=BATCH, row0=0, tq=SEQ, bk=SEQ,
                           sub=min(sub, SEQ), heads_per_step=hq, cache=None, seqs=CTX_SEQS,
                           **common)
        o_lat = _attention(q, k, v, seq_len=DEC_SEQ, n_seq=DEC_BATCH, row0=N_CTX, tq=ATTN_TILES * sub,
                           bk=ATTN_BK, sub=sub, heads_per_step=group, cache=cache, **common)
        return o_ctx, o_lat

    l = 0
    w_in = mla_w_in[0]
    kr_cols = w_in[:, MLA_Q_RANK + MLA_KV_RANK:]
    w_in2 = jnp.concatenate([w_in, kr_cols], axis=1).astype(BF16)
    w_uq = mla_w_uq[0].reshape(MLA_Q_RANK, MLA_HEADS, MLA_NOPE + MLA_ROPE)
    wqn = w_uq[:, :, :MLA_NOPE].reshape(MLA_Q_RANK, MLA_HEADS * MLA_NOPE).astype(BF16)
    wqr = w_uq[:, :, MLA_NOPE:].reshape(MLA_Q_RANK, MLA_HEADS * MLA_ROPE).astype(BF16)
    wukv = mla_w_ukv[0].astype(BF16)
    gqk = mla_g_qk[0]
    gqn = gqk[0:1, :MLA_NOPE]
    gqr = jnp.tile(gqk[0:1, MLA_NOPE:], (1, 2))
    gkn = gqk[1:2, :MLA_NOPE]
    gkr = jnp.tile(gqk[1:2, MLA_NOPE:], (1, 2))
    q, k, v, ckv_new, kr_new = _mla_in(*x_in, layer_mod(l), g_norm1[l][None, :], w_in2,
                                       mla_g_cq[0][None, :], mla_g_ckv[0][None, :],
                                       gqn, gqr, gkn, gkr, wqn, wqr, wukv, cos64, sin64)
    kc, vc = _mla_cache(cache_mla_ckv[:, 0].reshape(DEC_BATCH * PAST, MLA_KV_RANK),
                        cache_mla_krope[:, 0].reshape(DEC_BATCH * PAST, MLA_ROPE), wukv, gkn)
    cache = (kc, vc,
             (None, PAST, MLA_DQ), lambda b, hb, qb: (hb, b, 0),
             (None, PAST, 2 * MLA_V), lambda b, hb, qb: (hb, b, 0))
    o = attend(q, k, v, hq=MLA_HEADS, group=1, cache=cache, diff=None, lambda_init=0.0)
    x = post(l, x_in, o, mla_w_o[0])
    new_mla_ckv = ckv_new.reshape(BATCH, 1, SEQ, MLA_KV_RANK)
    new_mla_krope = kr_new.reshape(BATCH, 1, SEQ, MLA_ROPE)

    l = 1
    lambda_init = 0.8 - 0.6 * math.exp(-0.3 * l)
    n_diff = DIFF_HEADS * 2 * DIFF_HD

    def interleave(a):
        shape = a.shape
        a = a.reshape(shape[:-1] + (shape[-1] // LANES, 2, 2, DIFF_HD // 2))
        return jnp.swapaxes(a, -3, -2).reshape(shape)

    w_in = diff_w_in[0]
    w_in = jnp.concatenate([interleave(w_in[:, :2 * n_diff]), w_in[:, 2 * n_diff:]], axis=1)
    q, k, v, k_new, v_new = _diff_in(x, layer_mod(l), g_norm1[l][None, :], w_in.astype(BF16),
                                     interleave(jnp.tile(diff_g_qk[0], (1, 2))), cos64,
                                     interleave(sin64))
    k_new = interleave(k_new)
    cache = (interleave(cache_diff_k[:, 0].reshape(DEC_BATCH, PAST, n_diff)),
             cache_diff_v[:, 0].reshape(DEC_BATCH, PAST, n_diff),
             (None, PAST, LANES), lambda b, hb, qb: (b, 0, hb),
             (None, PAST, LANES), lambda b, hb, qb: (b, 0, hb))
    o = attend(q, k, v, hq=2 * DIFF_HEADS, group=2, cache=cache,
               diff=(diff_lambda[0], diff_g_sub[0][None, :]), lambda_init=lambda_init)
    x = post(l, (x,), o, diff_w_o[0])
    new_diff_k = k_new.reshape(BATCH, 1, SEQ, DIFF_HEADS, 2, DIFF_HD)
    new_diff_v = v_new.reshape(BATCH, 1, SEQ, DIFF_HEADS, 2 * DIFF_HD)

    l = 2
    q, k, v, k_new, v_new = _gqa_in(x, layer_mod(l), g_norm1[l][None, :], gqa_w_in[0].astype(BF16),
                                    gqa_g_qk[0], cos128, sin128)
    n_kv = GQA_KV_HEADS * GQA_HD
    group = GQA_Q_HEADS // GQA_KV_HEADS
    cache = (cache_gqa_k[:, 0].reshape(DEC_BATCH, PAST, n_kv),
             cache_gqa_v[:, 0].reshape(DEC_BATCH, PAST, n_kv),
             (None, PAST, GQA_HD), lambda b, hb, qb: (b, 0, hb),
             (None, PAST, GQA_HD), lambda b, hb, qb: (b, 0, hb))
    o = attend(q, k, v, hq=GQA_Q_HEADS, group=group, cache=cache, diff=None, lambda_init=0.0)
    x = post(l, (x,), o, gqa_w_o[0])
    new_gqa_k = k_new.reshape(BATCH, 1, SEQ, GQA_KV_HEADS, GQA_HD)
    new_gqa_v = v_new.reshape(BATCH, 1, SEQ, GQA_KV_HEADS, GQA_HD)

    l = 3
    gate, xr = _lru_in(x, layer_mod(l), g_norm1[l][None, :], lru_w_in[0].astype(BF16))
    wg = lru_w_gate[0]
    wg = jnp.concatenate([wg[:, 0], wg[:, 1]], axis=-1).astype(BF16)
    h0 = jnp.concatenate([jnp.zeros((BATCH, 2, D), F32), state_lru_h[:, 0]], axis=0)
    h0 = h0.reshape(BATCH + DEC_BATCH, 2, LRU_BLOCKS, LRU_BLK)
    conv_b = lru_conv_b[0][None, :]
    hf, st_f = _lru_scan(xr, lru_conv_w[0], conv_b, wg[0], lru_b_gate[0, 0], lru_lambda[0, 0][None, :],
                         h0[:, 0], reverse=False)
    y, st_b = _lru_scan(xr, lru_conv_w[0], conv_b, wg[1], lru_b_gate[0, 1], lru_lambda[0, 1][None, :],
                        h0[:, 1], reverse=True, hf=hf, gate=gate)
    y_ctx, y_lat = post(l, (x,), (y,), lru_w_out[0])
    new_lru_h = jnp.stack([st_f[:BATCH].reshape(BATCH, D), st_b[:BATCH].reshape(BATCH, D)],
                          axis=1)[:, None]

    y_prompt = y_ctx.reshape(BATCH, SEQ, D)
    y_sample = y_lat.reshape(DEC_BATCH, DEC_SEQ, D)
    return (y_prompt, y_sample, new_mla_ckv, new_mla_krope, new_diff_k, new_diff_v,
            new_gqa_k, new_gqa_v, new_lru_h)
```

```python
import functools
import math

import jax
import jax.numpy as jnp
from jax import lax
from jax.experimental import pallas as pl
from jax.experimental.pallas import tpu as pltpu

F32 = jnp.float32
BF16 = jnp.bfloat16

D = 1024
BATCH = 32
SEQ = 256
DEPTH = 4
DEC_BATCH = 2
DEC_SEQ = 4096
PAST = 256
GRID_W = 64
EPS = 1e-6
ROPE_THETA = 10000.0
FFN_H = 2816
N_CTX = BATCH * SEQ
N_LAT = DEC_BATCH * DEC_SEQ
T = N_CTX + N_LAT
N_GROUPS = 1 + DEC_BATCH

MLA_HEADS = 8
MLA_NOPE = 128
MLA_ROPE = 64
MLA_V = 128
MLA_Q_RANK = 384
MLA_KV_RANK = 256
MLA_DQ = 256
DIFF_HD = 64
DIFF_HEADS = 8
GQA_HD = 128
GQA_Q_HEADS = 8
GQA_KV_HEADS = 2
LRU_BLOCKS = 8
LRU_BLK = 128
LRU_C = 8.0

LANES = 128
SUBLANES = 8
VMEM_LIMIT = 56 * 1024 * 1024

TM = 512
FFN_CHUNK = 256
LRU_TB = 256
ATTN_TILES = 4
CTX_SEQS = 4
ATTN_BK = 1024
ATTN_ROWS = 1024
LOG2E = 1.4426950408889634


def _cparams(sem):
    return pltpu.CompilerParams(dimension_semantics=sem, vmem_limit_bytes=VMEM_LIMIT)


def _group_of_block(i, rows_per_block):
    n_ctx_blocks = N_CTX // rows_per_block
    per = DEC_SEQ // rows_per_block
    return jnp.maximum(i - (n_ctx_blocks - per), 0) // per


def _rms(x, gain):
    y = x * lax.rsqrt(jnp.mean(x * x, axis=-1, keepdims=True) + EPS)
    return y * gain


def _rms_halves(x, gain):
    lane = lax.broadcasted_iota(jnp.int32, x.shape, 1)
    low = lane < 64
    sq = x * x
    s_low = jnp.sum(jnp.where(low, sq, 0.0), axis=-1, keepdims=True)
    s_all = jnp.sum(sq, axis=-1, keepdims=True)
    ms = jnp.where(low, s_low, s_all - s_low) * (1.0 / 64.0)
    return x * lax.rsqrt(ms + EPS) * gain


def _rope128(x, c, s):
    return x * c + pltpu.roll(x, 64, 1) * s


def _rope64x2(x, c, s):
    lane = lax.broadcasted_iota(jnp.int32, x.shape, 1)
    low = (lane % 64) < 32
    partner = jnp.where(low, pltpu.roll(x, 96, 1), pltpu.roll(x, 32, 1))
    return x * c + partner * s


def _is_ctx_block():
    return pl.program_id(0) < N_CTX // TM


def _norm_mod(x, g_ref, mod_ref, shift_row):
    y = _rms(x, g_ref[...])
    shift = mod_ref[shift_row:shift_row + 1, :]
    scale = mod_ref[shift_row + 1:shift_row + 2, :]
    return (y * (1.0 + scale) + shift).astype(BF16)


def _mod_kernel(cond_ref, w_ref, b_ref, o_ref):
    s = jax.nn.silu(cond_ref[...]).astype(BF16)
    w = w_ref[...].astype(BF16)
    o_ref[...] = jnp.dot(s, w, preferred_element_type=F32) + b_ref[...]


def _modulation(cond, w_mod, b_mod):
    tn = 1536
    return pl.pallas_call(
        _mod_kernel,
        grid=(DEPTH, 6 * D // tn),
        in_specs=[
            pl.BlockSpec((SUBLANES, D), lambda l, j: (0, 0)),
            pl.BlockSpec((None, D, tn), lambda l, j: (l, 0, j)),
            pl.BlockSpec((None, 1, tn), lambda l, j: (l, 0, j)),
        ],
        out_specs=pl.BlockSpec((None, SUBLANES, tn), lambda l, j: (l, 0, j)),
        out_shape=jax.ShapeDtypeStruct((DEPTH, SUBLANES, 6 * D), F32),
        compiler_params=_cparams(("arbitrary", "arbitrary")),
        name="adaln_mod",
    )(cond, w_mod, b_mod.reshape(DEPTH, 1, 6 * D))


def _row_spec(width):
    return pl.BlockSpec((TM, width), lambda i: (i, 0))


def _ctx_row_spec(width):
    return pl.BlockSpec((TM, width), lambda i: (jnp.minimum(i, N_CTX // TM - 1), 0))


def _full_spec(shape):
    return pl.BlockSpec(shape, lambda i: (0,) * len(shape))


def _mod_spec():
    return pl.BlockSpec((None, 6, D), lambda i: (_group_of_block(i, TM), 0, 0))


N_BLOCKS = T // TM
N_CTX_BLOCKS = N_CTX // TM


def _proj_block(i):
    return jnp.minimum(i, N_BLOCKS - 1)


def _epi_block(i):
    return jnp.maximum(i - 1, 0)


def _proj_row_spec(width):
    return pl.BlockSpec((TM, width), lambda i: (_proj_block(i), 0))


def _proj_mod_spec():
    return pl.BlockSpec((None, 6, D), lambda i: (_group_of_block(_proj_block(i), TM), 0, 0))


def _epi_row_spec(width):
    return pl.BlockSpec((TM, width), lambda i: (_epi_block(i), 0))


def _epi_heads_spec(heads, width):
    return pl.BlockSpec((heads, TM, width), lambda i: (0, _epi_block(i), 0))


def _epi_ctx_row_spec(width):
    return pl.BlockSpec((TM, width), lambda i: (jnp.minimum(_epi_block(i), N_CTX_BLOCKS - 1), 0))


def _epi_is_ctx():
    i = pl.program_id(0)
    return jnp.logical_and(i >= 1, i <= N_CTX_BLOCKS)


def _rope_spec():
    per = DEC_SEQ // TM

    def index(i):
        j = _epi_block(i)
        return (jnp.where(j < N_CTX_BLOCKS, 0, 1 + (j - N_CTX_BLOCKS) % per), 0)

    return pl.BlockSpec((TM, LANES), index)


def _proj_lat_row_spec(width):
    n_lat = N_BLOCKS - N_CTX_BLOCKS
    return pl.BlockSpec((TM, width), lambda i: (jnp.clip(i - N_CTX_BLOCKS, 0, n_lat - 1), 0))


def _two_stage(step, bufs_a, bufs_b):
    i = pl.program_id(0)

    @pl.when(i == 0)
    def _():
        for buf in bufs_b:
            buf[...] = jnp.zeros(buf.shape, buf.dtype)

    @pl.when(i % 2 == 0)
    def _():
        step(bufs_b, bufs_a)

    @pl.when(i % 2 == 1)
    def _():
        step(bufs_a, bufs_b)


def _gqa_in_kernel(x_ref, mod_ref, g_ref, w_ref, gqk_ref, cos_ref, sin_ref,
                   q_ref, k_ref, v_ref, ks_ref, vs_ref, proj_a, proj_b):
    n_q = GQA_Q_HEADS * GQA_HD
    n_kv = GQA_KV_HEADS * GQA_HD

    def step(prev, cur):
        h = _norm_mod(x_ref[...], g_ref, mod_ref, 0)
        cur[...] = jnp.dot(h, w_ref[...], preferred_element_type=F32)
        c = cos_ref[...]
        s = sin_ref[...]
        scale = LOG2E * GQA_HD ** -0.5
        for hd in range(GQA_Q_HEADS):
            q = _rms(prev[:, hd * GQA_HD:(hd + 1) * GQA_HD], gqk_ref[0:1, :])
            q_ref[hd] = (_rope128(q, c, s) * scale).astype(BF16)
        ks = []
        for hd in range(GQA_KV_HEADS):
            lo = n_q + hd * GQA_HD
            ks.append(_rms(prev[:, lo:lo + GQA_HD], gqk_ref[1:2, :]))
            k_ref[hd] = _rope128(ks[-1], c, s).astype(BF16)
            lo = n_q + n_kv + hd * GQA_HD
            v_ref[hd] = _with_ones(prev[:, lo:lo + GQA_HD].astype(BF16))

        @pl.when(_epi_is_ctx())
        def _():
            for hd in range(GQA_KV_HEADS):
                ks_ref[:, hd * GQA_HD:(hd + 1) * GQA_HD] = ks[hd]
            vs_ref[...] = prev[:, n_q + n_kv:]

    _two_stage(lambda prev, cur: step(prev[0], cur[0]), (proj_a,), (proj_b,))


def _proj_scratch(width):
    return [pltpu.VMEM((TM, width), F32), pltpu.VMEM((TM, width), F32)]


def _gqa_in(x, mod, g1, w_in, g_qk, cos, sin):
    n_kv = GQA_KV_HEADS * GQA_HD
    return pl.pallas_call(
        _gqa_in_kernel,
        grid=(N_BLOCKS + 1,),
        in_specs=[_proj_row_spec(D), _proj_mod_spec(), _full_spec((1, D)), _full_spec(w_in.shape),
                  _full_spec((2, GQA_HD)), _rope_spec(), _rope_spec()],
        out_specs=[_epi_heads_spec(GQA_Q_HEADS, GQA_HD), _epi_heads_spec(GQA_KV_HEADS, GQA_HD),
                   _epi_heads_spec(GQA_KV_HEADS, 2 * GQA_HD), _epi_ctx_row_spec(n_kv),
                   _epi_ctx_row_spec(n_kv)],
        scratch_shapes=_proj_scratch(w_in.shape[1]),
        out_shape=[jax.ShapeDtypeStruct((GQA_Q_HEADS, T, GQA_HD), BF16),
                   jax.ShapeDtypeStruct((GQA_KV_HEADS, T, GQA_HD), BF16),
                   jax.ShapeDtypeStruct((GQA_KV_HEADS, T, 2 * GQA_HD), BF16),
                   jax.ShapeDtypeStruct((N_CTX, n_kv), F32),
                   jax.ShapeDtypeStruct((N_CTX, n_kv), F32)],
        compiler_params=_cparams(("arbitrary",)),
        name="gqa_in",
    )(x, mod, g1, w_in, g_qk, cos, sin)


def _diff_in_kernel(x_ref, mod_ref, g_ref, w_ref, gqk_ref, cos_ref, sin_ref,
                    q_ref, k_ref, v_ref, ks_ref, vs_ref, proj_a, proj_b):
    n = DIFF_HEADS * 2 * DIFF_HD

    def step(prev, cur):
        h = _norm_mod(x_ref[...], g_ref, mod_ref, 0)
        cur[...] = jnp.dot(h, w_ref[...], preferred_element_type=F32)
        c = cos_ref[...]
        s = sin_ref[...]
        scale = LOG2E * DIFF_HD ** -0.5
        lane = lax.broadcasted_iota(jnp.int32, (TM, LANES), 1)
        first = (lane % DIFF_HD) < DIFF_HD // 2

        def rms_sub(x, gain):
            sq = x * x
            s0 = jnp.sum(jnp.where(first, sq, 0.0), axis=-1, keepdims=True)
            s_all = jnp.sum(sq, axis=-1, keepdims=True)
            ms = jnp.where(first, s0, s_all - s0) * (1.0 / DIFF_HD)
            return x * lax.rsqrt(ms + EPS) * gain

        ks = []
        for hd in range(DIFF_HEADS):
            sl = slice(hd * LANES, (hd + 1) * LANES)
            q = _rope128(rms_sub(prev[:, sl], gqk_ref[0:1, :]), c, s) * scale
            q_ref[2 * hd] = jnp.where(first, q, 0.0).astype(BF16)
            q_ref[2 * hd + 1] = jnp.where(first, 0.0, q).astype(BF16)
            ks.append(rms_sub(prev[:, n + hd * LANES:n + (hd + 1) * LANES], gqk_ref[1:2, :]))
            k_ref[hd] = _rope128(ks[-1], c, s).astype(BF16)
            v_ref[hd] = _with_ones(prev[:, 2 * n + hd * LANES:2 * n + (hd + 1) * LANES].astype(BF16))

        @pl.when(_epi_is_ctx())
        def _():
            for hd in range(DIFF_HEADS):
                ks_ref[:, hd * LANES:(hd + 1) * LANES] = ks[hd]
            vs_ref[...] = prev[:, 2 * n:]

    _two_stage(lambda prev, cur: step(prev[0], cur[0]), (proj_a,), (proj_b,))


def _diff_in(x, mod, g1, w_in, g_qk2, cos, sin):
    n = DIFF_HEADS * 2 * DIFF_HD
    return pl.pallas_call(
        _diff_in_kernel,
        grid=(N_BLOCKS + 1,),
        in_specs=[_proj_row_spec(D), _proj_mod_spec(), _full_spec((1, D)), _full_spec(w_in.shape),
                  _full_spec((2, LANES)), _rope_spec(), _rope_spec()],
        out_specs=[_epi_heads_spec(2 * DIFF_HEADS, LANES), _epi_heads_spec(DIFF_HEADS, LANES),
                   _epi_heads_spec(DIFF_HEADS, 2 * LANES), _epi_ctx_row_spec(n), _epi_ctx_row_spec(n)],
        scratch_shapes=_proj_scratch(w_in.shape[1]),
        out_shape=[jax.ShapeDtypeStruct((2 * DIFF_HEADS, T, LANES), BF16),
                   jax.ShapeDtypeStruct((DIFF_HEADS, T, LANES), BF16),
                   jax.ShapeDtypeStruct((DIFF_HEADS, T, 2 * LANES), BF16),
                   jax.ShapeDtypeStruct((N_CTX, n), F32),
                   jax.ShapeDtypeStruct((N_CTX, n), F32)],
        compiler_params=_cparams(("arbitrary",)),
        name="diff_in",
    )(x, mod, g1, w_in, g_qk2, cos, sin)


def _mla_keys_values(ckv_bf16, kr2, w_ukv_ref, gk_ref, k_ref, v_ref):
    kv = jnp.dot(ckv_bf16, w_ukv_ref[...], preferred_element_type=F32)
    lane = lax.broadcasted_iota(jnp.int32, kr2.shape, 1)
    low = lane < MLA_ROPE
    kr_low = jnp.where(low, kr2, 0.0).astype(BF16)
    kr_high = jnp.where(low, 0.0, kr2).astype(BF16)
    width = MLA_NOPE + MLA_V
    for hd in range(MLA_HEADS):
        k_nope = _rms(kv[:, hd * width:hd * width + MLA_NOPE], gk_ref[...])
        k_ref[hd, :, 0:MLA_NOPE] = k_nope.astype(BF16)
        k_ref[hd, :, MLA_NOPE:MLA_DQ] = kr_low if hd % 2 == 0 else kr_high
        v_ref[hd] = _with_ones(kv[:, hd * width + MLA_NOPE:(hd + 1) * width].astype(BF16))


def _mla_in_kernel(xc_ref, xl_ref, mod_ref, g_ref, w_ref, gcq_ref, gckv_ref, gqn_ref, gqr_ref,
                   gkn_ref, gkr_ref, wqn_ref, wqr_ref, wukv_ref, cos_ref, sin_ref,
                   q_ref, k_ref, v_ref, ckv_ref, kr_ref, proj_a, proj_b):
    def step(prev, cur):
        x = jnp.where(_is_ctx_block(), xc_ref[...], xl_ref[...])
        h = _norm_mod(x, g_ref, mod_ref, 0)
        cur[...] = jnp.dot(h, w_ref[...], preferred_element_type=F32)
        c = cos_ref[...]
        s = sin_ref[...]
        cq = _rms(prev[:, 0:MLA_Q_RANK], gcq_ref[...]).astype(BF16)
        ckv = _rms(prev[:, MLA_Q_RANK:MLA_Q_RANK + MLA_KV_RANK], gckv_ref[...])
        kr2 = _rms_halves(prev[:, MLA_Q_RANK + MLA_KV_RANK:], gkr_ref[...])
        _mla_keys_values(ckv.astype(BF16), _rope64x2(kr2, c, s), wukv_ref, gkn_ref, k_ref, v_ref)

        scale = LOG2E * (MLA_NOPE + MLA_ROPE) ** -0.5
        qn = jnp.dot(cq, wqn_ref[...], preferred_element_type=F32)
        qr = jnp.dot(cq, wqr_ref[...], preferred_element_type=F32)
        lane = lax.broadcasted_iota(jnp.int32, (TM, LANES), 1)
        low = lane < MLA_ROPE
        for pair in range(MLA_HEADS // 2):
            r = _rms_halves(qr[:, pair * LANES:(pair + 1) * LANES], gqr_ref[...])
            r = _rope64x2(r, c, s) * scale
            q_ref[2 * pair, :, MLA_NOPE:MLA_DQ] = jnp.where(low, r, 0.0).astype(BF16)
            q_ref[2 * pair + 1, :, MLA_NOPE:MLA_DQ] = jnp.where(low, 0.0, r).astype(BF16)
        for hd in range(MLA_HEADS):
            q = _rms(qn[:, hd * MLA_NOPE:(hd + 1) * MLA_NOPE], gqn_ref[...]) * scale
            q_ref[hd, :, 0:MLA_NOPE] = q.astype(BF16)

        @pl.when(_epi_is_ctx())
        def _():
            ckv_ref[...] = ckv
            kr_ref[...] = kr2[:, 0:MLA_ROPE]

    _two_stage(lambda prev, cur: step(prev[0], cur[0]), (proj_a,), (proj_b,))


def _mla_in(x_ctx, x_lat, mod, g1, w_in, g_cq, g_ckv, gqn, gqr, gkn, gkr, wqn, wqr, wukv, cos, sin):
    n_lat_blocks = N_BLOCKS - N_CTX_BLOCKS
    return pl.pallas_call(
        _mla_in_kernel,
        grid=(N_BLOCKS + 1,),
        in_specs=[_ctx_row_spec(D),
                  pl.BlockSpec((TM, D), lambda i: (jnp.clip(i - N_CTX_BLOCKS, 0, n_lat_blocks - 1), 0)),
                  _proj_mod_spec(), _full_spec((1, D)), _full_spec(w_in.shape),
                  _full_spec((1, MLA_Q_RANK)), _full_spec((1, MLA_KV_RANK)),
                  _full_spec((1, LANES)), _full_spec((1, LANES)), _full_spec((1, LANES)),
                  _full_spec((1, LANES)), _full_spec(wqn.shape), _full_spec(wqr.shape),
                  _full_spec(wukv.shape), _rope_spec(), _rope_spec()],
        out_specs=[_epi_heads_spec(MLA_HEADS, MLA_DQ), _epi_heads_spec(MLA_HEADS, MLA_DQ),
                   _epi_heads_spec(MLA_HEADS, 2 * MLA_V), _epi_ctx_row_spec(MLA_KV_RANK),
                   _epi_ctx_row_spec(MLA_ROPE)],
        scratch_shapes=_proj_scratch(w_in.shape[1]),
        out_shape=[jax.ShapeDtypeStruct((MLA_HEADS, T, MLA_DQ), BF16),
                   jax.ShapeDtypeStruct((MLA_HEADS, T, MLA_DQ), BF16),
                   jax.ShapeDtypeStruct((MLA_HEADS, T, 2 * MLA_V), BF16),
                   jax.ShapeDtypeStruct((N_CTX, MLA_KV_RANK), F32),
                   jax.ShapeDtypeStruct((N_CTX, MLA_ROPE), F32)],
        compiler_params=_cparams(("arbitrary",)),
        name="mla_in",
    )(x_ctx, x_lat, mod, g1, w_in, g_cq, g_ckv, gqn, gqr, gkn, gkr, wqn, wqr, wukv, cos, sin)


def _mla_cache_kernel(ckv_ref, kr_ref, wukv_ref, gkn_ref, k_ref, v_ref):
    kr = kr_ref[...]
    kr2 = jnp.concatenate([kr, kr], axis=-1)
    _mla_keys_values(ckv_ref[...].astype(BF16), kr2, wukv_ref, gkn_ref, k_ref, v_ref)


def _mla_cache(ckv, kr, wukv, gkn):
    rows = ckv.shape[0]
    return pl.pallas_call(
        _mla_cache_kernel,
        grid=(1,),
        in_specs=[_full_spec(ckv.shape), _full_spec(kr.shape), _full_spec(wukv.shape),
                  _full_spec((1, LANES))],
        out_specs=[_full_spec((MLA_HEADS, rows, MLA_DQ)), _full_spec((MLA_HEADS, rows, 2 * MLA_V))],
        out_shape=[jax.ShapeDtypeStruct((MLA_HEADS, rows, MLA_DQ), BF16),
                   jax.ShapeDtypeStruct((MLA_HEADS, rows, 2 * MLA_V), BF16)],
        compiler_params=_cparams(("arbitrary",)),
        name="mla_cache_kv",
    )(ckv, kr, wukv, gkn)


def _lru_in_kernel(x_ref, mod_ref, g_ref, w_ref, gate_ref, xr_ref):
    h = _norm_mod(x_ref[...], g_ref, mod_ref, 0)
    proj = jnp.dot(h, w_ref[...], preferred_element_type=F32)
    gate_ref[...] = proj[:, 0:D]
    xr_ref[...] = proj[:, D:2 * D]


def _lru_in(x, mod, g1, w_in):
    return pl.pallas_call(
        _lru_in_kernel,
        grid=(T // TM,),
        in_specs=[_row_spec(D), _mod_spec(), _full_spec((1, D)), _full_spec(w_in.shape)],
        out_specs=[_row_spec(D), _row_spec(D)],
        out_shape=[jax.ShapeDtypeStruct((T, D), F32), jax.ShapeDtypeStruct((T, D), F32)],
        compiler_params=_cparams(("arbitrary",)),
        name="lru_in",
    )(x, mod, g1, w_in)


def _with_ones(v):
    return jnp.concatenate([v, jnp.ones_like(v)], axis=-1)


def _softmax_pv(q, key_blocks, dv):
    nt = (((1,), (1,)), ((), ()))
    m = acc = None
    for k, v1 in key_blocks:
        s = lax.dot_general(q, k, nt, preferred_element_type=F32)
        mb = jnp.max(s, axis=-1, keepdims=True)
        if m is None:
            m = mb
            acc = jnp.dot(jnp.exp2(s - m).astype(BF16), v1, preferred_element_type=F32)
        else:
            m_new = jnp.maximum(m, mb)
            alpha = jnp.exp2(m - m_new)
            acc = alpha * acc + jnp.dot(jnp.exp2(s - m_new).astype(BF16), v1,
                                        preferred_element_type=F32)
            m = m_new
    return acc[:, 0:dv] / acc[:, dv:2 * dv]


def _attn_kernel(*refs, heads, group, seqs, sub, bk, has_cache, diff, dv, lambda_init):
    it = iter(refs)
    q_ref, kn_ref, vn_ref = next(it), next(it), next(it)
    kc_ref = vc_ref = lam_ref = gsub_ref = None
    if has_cache:
        kc_ref, vc_ref = next(it), next(it)
    if diff:
        lam_ref, gsub_ref = next(it), next(it)
    o_ref = next(it)

    tq = q_ref.shape[1] // seqs
    seq_len = kn_ref.shape[1] // seqs
    stacked = min(group, heads)
    cached = []
    if has_cache:
        vc = vc_ref[...].astype(BF16)
        cached = [(kc_ref[...].astype(BF16), vc if vc.shape[-1] == 2 * dv else _with_ones(vc))]
    if diff:
        lam = lam_ref[...]
        lam_full = (jnp.exp(jnp.sum(lam[0:1] * lam[1:2], axis=-1, keepdims=True))
                    - jnp.exp(jnp.sum(lam[2:3] * lam[3:4], axis=-1, keepdims=True)) + lambda_init)

    for sq, kv in [(sq, kv) for sq in range(seqs) for kv in range(max(heads // group, 1))]:
        def body(i, carry, sq=sq, kv=kv):
            rows = pl.ds(pl.multiple_of(sq * tq + i * sub, sub), sub)
            q = jnp.concatenate([q_ref[kv * stacked + g, rows, :] for g in range(stacked)], axis=0)
            keys = [slice(sq * seq_len + j * bk, sq * seq_len + (j + 1) * bk)
                    for j in range(seq_len // bk)]
            blocks = cached + [(kn_ref[kv, ks, :], vn_ref[kv, ks, :]) for ks in keys]
            o = _softmax_pv(q, blocks, dv)
            if diff:
                od = o[0:sub] - lam_full * o[sub:2 * sub]
                od = _rms(od, gsub_ref[...]) * (1.0 - lambda_init)
                o_ref[rows, kv * dv:(kv + 1) * dv] = od.astype(BF16)
            else:
                for g in range(stacked):
                    hd = kv * stacked + g
                    o_ref[rows, hd * dv:(hd + 1) * dv] = o[g * sub:(g + 1) * sub].astype(BF16)
            return carry
        lax.fori_loop(0, tq // sub, body, 0, unroll=True)


def _attention(q, kn, vn, *, seq_len, n_seq, row0, tq, sub, bk, heads_per_step, group, cache=None,
               diff=None, lambda_init=0.0, seqs=1):
    hq, _, dq = q.shape
    hkv, _, dv1 = vn.shape
    dv = dv1 // 2
    kv_per_step = max(heads_per_step // group, 1)
    n_hblk = hq // heads_per_step
    n_qblk = seq_len // tq
    assert seqs == 1 or (n_qblk == 1 and cache is None and row0 % (seqs * seq_len) == 0)
    n_seq //= seqs
    seq0 = row0 // (seqs * seq_len)
    qblk0 = row0 // (seqs * tq)

    def q_map(b, hb, qb):
        return (hb, qblk0 + b * n_qblk + qb, 0)

    def kv_map(b, hb, qb):
        return ((hb * heads_per_step) // (group * kv_per_step), seq0 + b, 0)

    in_specs = [pl.BlockSpec((heads_per_step, seqs * tq, dq), q_map),
                pl.BlockSpec((kv_per_step, seqs * seq_len, dq), kv_map),
                pl.BlockSpec((kv_per_step, seqs * seq_len, dv1), kv_map)]
    args = [q, kn, vn]
    if cache is not None:
        kc, vc, kc_block, kc_map, vc_block, vc_map = cache
        in_specs += [pl.BlockSpec(kc_block, kc_map), pl.BlockSpec(vc_block, vc_map)]
        args += [kc, vc]
    out_heads = heads_per_step
    if diff is not None:
        lam, gsub = diff
        in_specs += [pl.BlockSpec(lam.shape, lambda b, hb, qb: (0, 0)),
                     pl.BlockSpec(gsub.shape, lambda b, hb, qb: (0, 0))]
        args += [lam, gsub]
        out_heads = heads_per_step // 2
    n_out = (hq // 2 if diff is not None else hq) * dv
    kernel = functools.partial(_attn_kernel, heads=heads_per_step, group=group, seqs=seqs, sub=sub,
                               bk=bk, has_cache=cache is not None, diff=diff is not None, dv=dv,
                               lambda_init=lambda_init)
    return pl.pallas_call(
        kernel,
        grid=(n_seq, n_hblk, n_qblk),
        in_specs=in_specs,
        out_specs=pl.BlockSpec((seqs * tq, out_heads * dv),
                               lambda b, hb, qb: (b * n_qblk + qb, hb)),
        out_shape=jax.ShapeDtypeStruct((n_seq * seqs * seq_len, n_out), BF16),
        compiler_params=_cparams(("arbitrary", "arbitrary", "arbitrary")),
        name="attention",
    )(*args)


def _lru_seq_pos(i):
    n_ctx_blocks = N_CTX // LRU_TB
    per_ctx = SEQ // LRU_TB
    per_lat = DEC_SEQ // LRU_TB
    pos = jnp.where(i < n_ctx_blocks, i % per_ctx, (i - n_ctx_blocks) % per_lat)
    length = jnp.where(i < n_ctx_blocks, per_ctx, per_lat)
    return pos == 0, pos == length - 1


def _lru_seq_of_block(i):
    n_ctx_blocks = N_CTX // LRU_TB
    return jnp.where(i < n_ctx_blocks, i // (SEQ // LRU_TB),
                     BATCH + (i - n_ctx_blocks) // (DEC_SEQ // LRU_TB))


LRU_PITCH = LRU_TB + SUBLANES


def _lru_scan_kernel(x_ref, prev_ref, next_ref, cw_ref, cb_ref, wg_ref, bg_ref, lam_ref, h0_ref,
                     *rest, reverse, combine):
    if combine:
        hf_ref, gate_ref, y_ref, st_ref, a_s, u_s, h_s, carry = rest
    else:
        y_ref, st_ref, a_s, u_s, h_s, carry = rest
    j = pl.program_id(0)
    i = (T // LRU_TB - 1 - j) if reverse else j
    first, last = _lru_seq_pos(i)
    starts = last if reverse else first
    ends = first if reverse else last

    x = x_ref[...]
    before = jnp.where(first, 0.0, prev_ref[SUBLANES - 1:SUBLANES, :])
    after = jnp.where(last, 0.0, next_ref[0:2, :])
    row = lax.broadcasted_iota(jnp.int32, (SUBLANES, D), 0)

    def shifted(k, fix):
        y = pltpu.roll(x, (-k) % LRU_TB, 0)
        if k < 0:
            return jnp.concatenate([fix(y[0:SUBLANES]), y[SUBLANES:]], axis=0)
        return jnp.concatenate([y[0:LRU_TB - SUBLANES], fix(y[LRU_TB - SUBLANES:])], axis=0)

    taps = [
        shifted(-1, lambda t: jnp.where(row == 0, before, t)),
        x,
        shifted(1, lambda t: jnp.where(row == SUBLANES - 1, after[0:1], t)),
        shifted(2, lambda t: jnp.where(row == SUBLANES - 2, after[0:1],
                                       jnp.where(row == SUBLANES - 1, after[1:2], t))),
    ]
    xr = taps[0] * cw_ref[0:1, :]
    for t in range(1, 4):
        xr = xr + taps[t] * cw_ref[t:t + 1, :]
    xr = xr + cb_ref[...]

    xr_b = xr.astype(BF16)
    lam = lam_ref[...]
    neg = -lam
    softplus = jnp.maximum(neg, 0.0) + jnp.log1p(jnp.exp(-jnp.abs(neg)))
    for n in range(LRU_BLOCKS):
        sl = slice(n * LRU_BLK, (n + 1) * LRU_BLK)
        g = jnp.dot(xr_b[:, sl], wg_ref[n], preferred_element_type=F32)
        r = jax.nn.sigmoid(g[:, 0:LRU_BLK] + bg_ref[0:1, sl])
        gi = jax.nn.sigmoid(g[:, LRU_BLK:] + bg_ref[1:2, sl])
        log_a = -LRU_C * r * softplus[:, sl]
        th = jnp.tanh(log_a)
        one_minus_a2 = -2.0 * th / (1.0 - th)
        rows_n = slice(n * LRU_PITCH, n * LRU_PITCH + LRU_TB)
        a_s[rows_n, :] = jnp.exp(log_a)
        root = jnp.where(one_minus_a2 > 0.0, one_minus_a2 * lax.rsqrt(one_minus_a2), 0.0)
        u_s[rows_n, :] = root * (gi * xr[:, sl])

    @pl.when(starts)
    def _():
        carry[...] = h0_ref[...]

    def body(t, h):
        tt = (LRU_TB - 1 - t) if reverse else t
        rows = pl.ds(tt, LRU_BLOCKS, stride=LRU_PITCH)
        h = a_s[rows, :] * h + u_s[rows, :]
        h_s[rows, :] = h
        return h

    h_end = lax.fori_loop(0, LRU_TB, body, carry[...], unroll=8)
    carry[...] = h_end

    for n in range(LRU_BLOCKS):
        sl = slice(n * LRU_BLK, (n + 1) * LRU_BLK)
        hs = h_s[n * LRU_PITCH:n * LRU_PITCH + LRU_TB, :]
        if combine:
            y_ref[:, sl] = ((hf_ref[:, sl] + hs) * jax.nn.gelu(gate_ref[:, sl])).astype(BF16)
        else:
            y_ref[:, sl] = hs

    @pl.when(ends)
    def _():
        st_ref[...] = h_end


def _lru_scan(xr, conv_w, conv_b, w_gate, b_gate, lam, h0, *, reverse, hf=None, gate=None):
    nb = T // LRU_TB
    hb = LRU_TB // SUBLANES
    n_halo = T // SUBLANES

    def blk(j):
        return (nb - 1 - j) if reverse else j

    in_specs = [
        pl.BlockSpec((LRU_TB, D), lambda j: (blk(j), 0)),
        pl.BlockSpec((SUBLANES, D), lambda j: (jnp.maximum(blk(j) * hb - 1, 0), 0)),
        pl.BlockSpec((SUBLANES, D), lambda j: (jnp.minimum((blk(j) + 1) * hb, n_halo - 1), 0)),
        pl.BlockSpec((4, D), lambda j: (0, 0)),
        pl.BlockSpec((1, D), lambda j: (0, 0)),
        pl.BlockSpec((LRU_BLOCKS, LRU_BLK, 2 * LRU_BLK), lambda j: (0, 0, 0)),
        pl.BlockSpec((2, D), lambda j: (0, 0)),
        pl.BlockSpec((1, D), lambda j: (0, 0)),
        pl.BlockSpec((None, LRU_BLOCKS, LRU_BLK), lambda j: (_lru_seq_of_block(blk(j)), 0, 0)),
    ]
    args = [xr, xr, xr, conv_w, conv_b, w_gate, b_gate, lam, h0]
    combine = hf is not None
    if combine:
        in_specs += [pl.BlockSpec((LRU_TB, D), lambda j: (blk(j), 0)),
                     pl.BlockSpec((LRU_TB, D), lambda j: (blk(j), 0))]
        args += [hf, gate]
    n_seq = BATCH + DEC_BATCH
    return pl.pallas_call(
        functools.partial(_lru_scan_kernel, reverse=reverse, combine=combine),
        grid=(nb,),
        in_specs=in_specs,
        out_specs=[pl.BlockSpec((LRU_TB, D), lambda j: (blk(j), 0)),
                   pl.BlockSpec((None, LRU_BLOCKS, LRU_BLK),
                                lambda j: (_lru_seq_of_block(blk(j)), 0, 0))],
        out_shape=[jax.ShapeDtypeStruct((T, D), BF16 if combine else F32),
                   jax.ShapeDtypeStruct((n_seq, LRU_BLOCKS, LRU_BLK), F32)],
        scratch_shapes=[pltpu.VMEM((LRU_BLOCKS * LRU_PITCH, LRU_BLK), F32),
                        pltpu.VMEM((LRU_BLOCKS * LRU_PITCH, LRU_BLK), F32),
                        pltpu.VMEM((LRU_BLOCKS * LRU_PITCH, LRU_BLK), F32),
                        pltpu.VMEM((LRU_BLOCKS, LRU_BLK), F32)],
        compiler_params=_cparams(("arbitrary",)),
        name="lru_scan_bwd" if reverse else "lru_scan_fwd",
    )(*args)


def _post_kernel(*refs, split_x, split_o, split_out):
    it = iter(refs)
    x_refs = [next(it) for _ in range(2 if split_x else 1)]
    o_refs = [next(it) for _ in range(2 if split_o else 1)]
    mod_a, mod_b, g2_ref, wo_ref, win_ref, wout_ref = (next(it) for _ in range(6))
    y_refs = [next(it) for _ in range(2 if split_out else 1)]
    x1_a, h_a, x1_b, h_b = it

    def rows(rs):
        if len(rs) == 2:
            return jnp.where(_is_ctx_block(), rs[0][...], rs[1][...])
        return rs[0][...]

    def step(prev, cur):
        x1_p, h_p = prev
        x1_c, h_c = cur
        x1 = rows(x_refs) + mod_a[2:3, :] * jnp.dot(rows(o_refs), wo_ref[...],
                                                   preferred_element_type=F32)
        x1_c[...] = x1
        hn = _rms(x1, g2_ref[...])
        h_c[...] = (hn * (1.0 + mod_a[4:5, :]) + mod_a[3:4, :]).astype(BF16)

        h = h_p[...]
        acc = jnp.zeros((TM, D), F32)
        for c in range(FFN_H // FFN_CHUNK):
            lo = c * FFN_CHUNK
            g = jnp.dot(h, win_ref[:, lo:lo + FFN_CHUNK], preferred_element_type=F32)
            u = jnp.dot(h, win_ref[:, FFN_H + lo:FFN_H + lo + FFN_CHUNK],
                        preferred_element_type=F32)
            a = (jax.nn.silu(g) * u).astype(BF16)
            acc = acc + jnp.dot(a, wout_ref[lo:lo + FFN_CHUNK, :], preferred_element_type=F32)
        y = x1_p[...] + mod_b[5:6, :] * acc
        if split_out:
            @pl.when(_epi_is_ctx())
            def _():
                y_refs[0][...] = y

            @pl.when(pl.program_id(0) > N_CTX_BLOCKS)
            def _():
                y_refs[1][...] = y
        else:
            y_refs[0][...] = y

    _two_stage(step, (x1_a, h_a), (x1_b, h_b))


def _layer_weight_spec(shape, layer):
    return pl.BlockSpec((None,) + shape[1:], lambda i: (layer,) + (0,) * (len(shape) - 1),
                        pipeline_mode=pl.Buffered(1))


def _post(xs, os, mod, g2, w_o, w_in, w_out, layer, *, split_out):
    def row_specs(arrays):
        if len(arrays) == 2:
            return [_ctx_row_spec(D), _proj_lat_row_spec(D)]
        return [_proj_row_spec(D)]

    n_lat = N_BLOCKS - N_CTX_BLOCKS
    if split_out:
        out_specs = [_epi_ctx_row_spec(D),
                     pl.BlockSpec((TM, D),
                                  lambda i: (jnp.clip(i - 1 - N_CTX_BLOCKS, 0, n_lat - 1), 0))]
        out_shape = [jax.ShapeDtypeStruct((N_CTX, D), F32), jax.ShapeDtypeStruct((N_LAT, D), F32)]
    else:
        out_specs = [_epi_row_spec(D)]
        out_shape = [jax.ShapeDtypeStruct((T, D), F32)]
    epi_mod_spec = pl.BlockSpec((None, 6, D),
                                lambda i: (_group_of_block(_epi_block(i), TM), 0, 0))
    return pl.pallas_call(
        functools.partial(_post_kernel, split_x=len(xs) == 2, split_o=len(os) == 2,
                          split_out=split_out),
        grid=(N_BLOCKS + 1,),
        in_specs=row_specs(xs) + row_specs(os) + [
            _proj_mod_spec(), epi_mod_spec, _full_spec((1, D)),
            _layer_weight_spec((1,) + w_o.shape, 0),
            _layer_weight_spec(w_in.shape, layer), _layer_weight_spec(w_out.shape, layer)],
        out_specs=out_specs,
        out_shape=out_shape,
        scratch_shapes=[pltpu.VMEM((TM, D), F32), pltpu.VMEM((TM, D), BF16),
                        pltpu.VMEM((TM, D), F32), pltpu.VMEM((TM, D), BF16)],
        compiler_params=_cparams(("arbitrary",)),
        name="post_mixer_ffn",
    )(*xs, *os, mod, mod, g2, w_o[None], w_in, w_out)


def _axial_tables(rot_dim):
    row = jnp.repeat(jnp.arange(DEC_SEQ // GRID_W), GRID_W).astype(F32)
    col = jnp.tile(jnp.arange(GRID_W), DEC_SEQ // GRID_W).astype(F32)
    n_freq = rot_dim // 4
    inv = ROPE_THETA ** (-jnp.arange(n_freq, dtype=F32) / n_freq)
    ang = jnp.concatenate([row[:, None] * inv, col[:, None] * inv], axis=-1)
    cos, sin = jnp.cos(ang), jnp.sin(ang)
    reps = LANES // rot_dim
    cos_t = jnp.tile(jnp.concatenate([cos, cos], axis=-1), (1, reps))
    sin_t = jnp.tile(jnp.concatenate([-sin, sin], axis=-1), (1, reps))
    cos_t = jnp.concatenate([jnp.ones((TM, LANES), F32), cos_t], axis=0)
    sin_t = jnp.concatenate([jnp.zeros((TM, LANES), F32), sin_t], axis=0)
    return cos_t, sin_t


def kernel(x_prompt, x_sample, cache_mla_ckv, cache_mla_krope, cache_diff_k, cache_diff_v, cache_gqa_k, cache_gqa_v, state_lru_h, c, c_ctx, w_mod, b_mod, g_norm1, g_norm2, w_ffn_in, w_ffn_out, mla_w_in, mla_g_cq, mla_g_ckv, mla_w_uq, mla_w_ukv, mla_g_qk, mla_w_o, diff_w_in, diff_g_qk, diff_lambda, diff_g_sub, diff_w_o, gqa_w_in, gqa_g_qk, gqa_w_o, lru_w_in, lru_conv_w, lru_conv_b, lru_w_gate, lru_b_gate, lru_lambda, lru_w_out):
    x_in = (x_prompt.reshape(N_CTX, D), x_sample.reshape(N_LAT, D))
    cond = jnp.concatenate([c_ctx[None, :], c, jnp.zeros((SUBLANES - N_GROUPS, D), F32)], axis=0)
    mod_all = _modulation(cond, w_mod, b_mod)
    cos128, sin128 = _axial_tables(GQA_HD)
    cos64, sin64 = _axial_tables(DIFF_HD)
    w_ffn_in_b = w_ffn_in.astype(BF16)
    w_ffn_out_b = w_ffn_out.astype(BF16)

    def layer_mod(l):
        return mod_all[l, :N_GROUPS].reshape(N_GROUPS, 6, D)

    def post(l, xs, os, w_o):
        out = _post(xs, os, layer_mod(l), g_norm2[l][None, :], w_o.astype(BF16),
                    w_ffn_in_b, w_ffn_out_b, l, split_out=l == DEPTH - 1)
        return out if l == DEPTH - 1 else out[0]

    def attend(q, k, v, *, hq, group, cache, diff, lambda_init):
        common = dict(group=group, diff=diff, lambda_init=lambda_init)
        sub = ATTN_ROWS // group
        o_ctx = _attention(q, k, v, seq_len=SEQ, n_seq=BATCH, row0=0, tq=SEQ, bk=SEQ,
                           sub=min(sub, SEQ), heads_per_step=hq, cache=None, seqs=CTX_SEQS,
                           **common)
        o_lat = _attention(q, k, v, seq_len=DEC_SEQ, n_seq=DEC_BATCH, row0=N_CTX, tq=ATTN_TILES * sub,
                           bk=ATTN_BK, sub=sub, heads_per_step=group, cache=cache, **common)
        return o_ctx, o_lat

    l = 0
    w_in = mla_w_in[0]
    kr_cols = w_in[:, MLA_Q_RANK + MLA_KV_RANK:]
    w_in2 = jnp.concatenate([w_in, kr_cols], axis=1).astype(BF16)
    w_uq = mla_w_uq[0].reshape(MLA_Q_RANK, MLA_HEADS, MLA_NOPE + MLA_ROPE)
    wqn = w_uq[:, :, :MLA_NOPE].reshape(MLA_Q_RANK, MLA_HEADS * MLA_NOPE).astype(BF16)
    wqr = w_uq[:, :, MLA_NOPE:].reshape(MLA_Q_RANK, MLA_HEADS * MLA_ROPE).astype(BF16)
    wukv = mla_w_ukv[0].astype(BF16)
    gqk = mla_g_qk[0]
    gqn = gqk[0:1, :MLA_NOPE]
    gqr = jnp.tile(gqk[0:1, MLA_NOPE:], (1, 2))
    gkn = gqk[1:2, :MLA_NOPE]
    gkr = jnp.tile(gqk[1:2, MLA_NOPE:], (1, 2))
    q, k, v, ckv_new, kr_new = _mla_in(*x_in, layer_mod(l), g_norm1[l][None, :], w_in2,
                                       mla_g_cq[0][None, :], mla_g_ckv[0][None, :],
                                       gqn, gqr, gkn, gkr, wqn, wqr, wukv, cos64, sin64)
    kc, vc = _mla_cache(cache_mla_ckv[:, 0].reshape(DEC_BATCH * PAST, MLA_KV_RANK),
                        cache_mla_krope[:, 0].reshape(DEC_BATCH * PAST, MLA_ROPE), wukv, gkn)
    cache = (kc, vc,
             (None, PAST, MLA_DQ), lambda b, hb, qb: (hb, b, 0),
             (None, PAST, 2 * MLA_V), lambda b, hb, qb: (hb, b, 0))
    o = attend(q, k, v, hq=MLA_HEADS, group=1, cache=cache, diff=None, lambda_init=0.0)
    x = post(l, x_in, o, mla_w_o[0])
    new_mla_ckv = ckv_new.reshape(BATCH, 1, SEQ, MLA_KV_RANK)
    new_mla_krope = kr_new.reshape(BATCH, 1, SEQ, MLA_ROPE)

    l = 1
    lambda_init = 0.8 - 0.6 * math.exp(-0.3 * l)
    n_diff = DIFF_HEADS * 2 * DIFF_HD

    def interleave(a):
        shape = a.shape
        a = a.reshape(shape[:-1] + (shape[-1] // LANES, 2, 2, DIFF_HD // 2))
        return jnp.swapaxes(a, -3, -2).reshape(shape)

    w_in = diff_w_in[0]
    w_in = jnp.concatenate([interleave(w_in[:, :2 * n_diff]), w_in[:, 2 * n_diff:]], axis=1)
    q, k, v, k_new, v_new = _diff_in(x, layer_mod(l), g_norm1[l][None, :], w_in.astype(BF16),
                                     interleave(jnp.tile(diff_g_qk[0], (1, 2))), cos64,
                                     interleave(sin64))
    k_new = interleave(k_new)
    cache = (interleave(cache_diff_k[:, 0].reshape(DEC_BATCH, PAST, n_diff)),
             cache_diff_v[:, 0].reshape(DEC_BATCH, PAST, n_diff),
             (None, PAST, LANES), lambda b, hb, qb: (b, 0, hb),
             (None, PAST, LANES), lambda b, hb, qb: (b, 0, hb))
    o = attend(q, k, v, hq=2 * DIFF_HEADS, group=2, cache=cache,
               diff=(diff_lambda[0], diff_g_sub[0][None, :]), lambda_init=lambda_init)
    x = post(l, (x,), o, diff_w_o[0])
    new_diff_k = k_new.reshape(BATCH, 1, SEQ, DIFF_HEADS, 2, DIFF_HD)
    new_diff_v = v_new.reshape(BATCH, 1, SEQ, DIFF_HEADS, 2 * DIFF_HD)

    l = 2
    q, k, v, k_new, v_new = _gqa_in(x, layer_mod(l), g_norm1[l][None, :], gqa_w_in[0].astype(BF16),
                                    gqa_g_qk[0], cos128, sin128)
    n_kv = GQA_KV_HEADS * GQA_HD
    group = GQA_Q_HEADS // GQA_KV_HEADS
    cache = (cache_gqa_k[:, 0].reshape(DEC_BATCH, PAST, n_kv),
             cache_gqa_v[:, 0].reshape(DEC_BATCH, PAST, n_kv),
             (None, PAST, GQA_HD), lambda b, hb, qb: (b, 0, hb),
             (None, PAST, GQA_HD), lambda b, hb, qb: (b, 0, hb))
    o = attend(q, k, v, hq=GQA_Q_HEADS, group=group, cache=cache, diff=None, lambda_init=0.0)
    x = post(l, (x,), o, gqa_w_o[0])
    new_gqa_k = k_new.reshape(BATCH, 1, SEQ, GQA_KV_HEADS, GQA_HD)
    new_gqa_v = v_new.reshape(BATCH, 1, SEQ, GQA_KV_HEADS, GQA_HD)

    l = 3
    gate, xr = _lru_in(x, layer_mod(l), g_norm1[l][None, :], lru_w_in[0].astype(BF16))
    wg = lru_w_gate[0]
    wg = jnp.concatenate([wg[:, 0], wg[:, 1]], axis=-1).astype(BF16)
    h0 = jnp.concatenate([jnp.zeros((BATCH, 2, D), F32), state_lru_h[:, 0]], axis=0)
    h0 = h0.reshape(BATCH + DEC_BATCH, 2, LRU_BLOCKS, LRU_BLK)
    conv_b = lru_conv_b[0][None, :]
    hf, st_f = _lru_scan(xr, lru_conv_w[0], conv_b, wg[0], lru_b_gate[0, 0], lru_lambda[0, 0][None, :],
                         h0[:, 0], reverse=False)
    y, st_b = _lru_scan(xr, lru_conv_w[0], conv_b, wg[1], lru_b_gate[0, 1], lru_lambda[0, 1][None, :],
                        h0[:, 1], reverse=True, hf=hf, gate=gate)
    y_ctx, y_lat = post(l, (x,), (y,), lru_w_out[0])
    new_lru_h = jnp.stack([st_f[:BATCH].reshape(BATCH, D), st_b[:BATCH].reshape(BATCH, D)],
                          axis=1)[:, None]

    y_prompt = y_ctx.reshape(BATCH, SEQ, D)
    y_sample = y_lat.reshape(DEC_BATCH, DEC_SEQ, D)
    return (y_prompt, y_sample, new_mla_ckv, new_mla_krope, new_diff_k, new_diff_v,
            new_gqa_k, new_gqa_v, new_lru_h)
```

```python
import functools
import math

import jax
import jax.numpy as jnp
from jax import lax
from jax.experimental import pallas as pl
from jax.experimental.pallas import tpu as pltpu

F32 = jnp.float32
BF16 = jnp.bfloat16

D = 1024
BATCH = 32
SEQ = 256
DEPTH = 4
DEC_BATCH = 2
DEC_SEQ = 4096
PAST = 256
GRID_W = 64
EPS = 1e-6
ROPE_THETA = 10000.0
FFN_H = 2816
N_CTX = BATCH * SEQ
N_LAT = DEC_BATCH * DEC_SEQ
T = N_CTX + N_LAT
N_GROUPS = 1 + DEC_BATCH

MLA_HEADS = 8
MLA_NOPE = 128
MLA_ROPE = 64
MLA_V = 128
MLA_Q_RANK = 384
MLA_KV_RANK = 256
MLA_DQ = 256
DIFF_HD = 64
DIFF_HEADS = 8
GQA_HD = 128
GQA_Q_HEADS = 8
GQA_KV_HEADS = 2
LRU_BLOCKS = 8
LRU_BLK = 128
LRU_C = 8.0

LANES = 128
SUBLANES = 8
VMEM_LIMIT = 56 * 1024 * 1024

TM = 512
FFN_CHUNK = 256
LRU_TB = 256
ATTN_TILES = 4
CTX_SEQS = 4
LAT_KV_HEADS = 2
ATTN_BK = 1024
ATTN_ROWS = 1024
LOG2E = 1.4426950408889634


def _cparams(sem):
    return pltpu.CompilerParams(dimension_semantics=sem, vmem_limit_bytes=VMEM_LIMIT)


def _group_of_block(i, rows_per_block):
    n_ctx_blocks = N_CTX // rows_per_block
    per = DEC_SEQ // rows_per_block
    return jnp.maximum(i - (n_ctx_blocks - per), 0) // per


def _rms(x, gain):
    y = x * lax.rsqrt(jnp.mean(x * x, axis=-1, keepdims=True) + EPS)
    return y * gain


def _rms_halves(x, gain):
    lane = lax.broadcasted_iota(jnp.int32, x.shape, 1)
    low = lane < 64
    sq = x * x
    s_low = jnp.sum(jnp.where(low, sq, 0.0), axis=-1, keepdims=True)
    s_all = jnp.sum(sq, axis=-1, keepdims=True)
    ms = jnp.where(low, s_low, s_all - s_low) * (1.0 / 64.0)
    return x * lax.rsqrt(ms + EPS) * gain


def _rope128(x, c, s):
    return x * c + pltpu.roll(x, 64, 1) * s


def _rope64x2(x, c, s):
    lane = lax.broadcasted_iota(jnp.int32, x.shape, 1)
    low = (lane % 64) < 32
    partner = jnp.where(low, pltpu.roll(x, 96, 1), pltpu.roll(x, 32, 1))
    return x * c + partner * s


def _is_ctx_block():
    return pl.program_id(0) < N_CTX // TM


def _norm_mod(x, g_ref, mod_ref, shift_row):
    y = _rms(x, g_ref[...])
    shift = mod_ref[shift_row:shift_row + 1, :]
    scale = mod_ref[shift_row + 1:shift_row + 2, :]
    return (y * (1.0 + scale) + shift).astype(BF16)


def _mod_kernel(cond_ref, w_ref, b_ref, o_ref):
    s = jax.nn.silu(cond_ref[...]).astype(BF16)
    w = w_ref[...].astype(BF16)
    o_ref[...] = jnp.dot(s, w, preferred_element_type=F32) + b_ref[...]


def _modulation(cond, w_mod, b_mod):
    tn = 1536
    return pl.pallas_call(
        _mod_kernel,
        grid=(DEPTH, 6 * D // tn),
        in_specs=[
            pl.BlockSpec((SUBLANES, D), lambda l, j: (0, 0)),
            pl.BlockSpec((None, D, tn), lambda l, j: (l, 0, j)),
            pl.BlockSpec((None, 1, tn), lambda l, j: (l, 0, j)),
        ],
        out_specs=pl.BlockSpec((None, SUBLANES, tn), lambda l, j: (l, 0, j)),
        out_shape=jax.ShapeDtypeStruct((DEPTH, SUBLANES, 6 * D), F32),
        compiler_params=_cparams(("arbitrary", "arbitrary")),
        name="adaln_mod",
    )(cond, w_mod, b_mod.reshape(DEPTH, 1, 6 * D))


def _row_spec(width):
    return pl.BlockSpec((TM, width), lambda i: (i, 0))


def _ctx_row_spec(width):
    return pl.BlockSpec((TM, width), lambda i: (jnp.minimum(i, N_CTX // TM - 1), 0))


def _full_spec(shape):
    return pl.BlockSpec(shape, lambda i: (0,) * len(shape))


def _mod_spec():
    return pl.BlockSpec((None, 6, D), lambda i: (_group_of_block(i, TM), 0, 0))


N_BLOCKS = T // TM
N_CTX_BLOCKS = N_CTX // TM


def _proj_block(i):
    return jnp.minimum(i, N_BLOCKS - 1)


def _epi_block(i):
    return jnp.maximum(i - 1, 0)


def _proj_row_spec(width):
    return pl.BlockSpec((TM, width), lambda i: (_proj_block(i), 0))


def _proj_mod_spec():
    return pl.BlockSpec((None, 6, D), lambda i: (_group_of_block(_proj_block(i), TM), 0, 0))


def _epi_row_spec(width):
    return pl.BlockSpec((TM, width), lambda i: (_epi_block(i), 0))


def _epi_heads_spec(heads, width):
    return pl.BlockSpec((heads, TM, width), lambda i: (0, _epi_block(i), 0))


def _epi_ctx_row_spec(width):
    return pl.BlockSpec((TM, width), lambda i: (jnp.minimum(_epi_block(i), N_CTX_BLOCKS - 1), 0))


def _epi_is_ctx():
    i = pl.program_id(0)
    return jnp.logical_and(i >= 1, i <= N_CTX_BLOCKS)


def _rope_spec():
    per = DEC_SEQ // TM

    def index(i):
        j = _epi_block(i)
        return (jnp.where(j < N_CTX_BLOCKS, 0, 1 + (j - N_CTX_BLOCKS) % per), 0)

    return pl.BlockSpec((TM, LANES), index)


def _proj_lat_row_spec(width):
    n_lat = N_BLOCKS - N_CTX_BLOCKS
    return pl.BlockSpec((TM, width), lambda i: (jnp.clip(i - N_CTX_BLOCKS, 0, n_lat - 1), 0))


def _two_stage(step, bufs_a, bufs_b):
    i = pl.program_id(0)

    @pl.when(i == 0)
    def _():
        for buf in bufs_b:
            buf[...] = jnp.zeros(buf.shape, buf.dtype)

    @pl.when(i % 2 == 0)
    def _():
        step(bufs_b, bufs_a)

    @pl.when(i % 2 == 1)
    def _():
        step(bufs_a, bufs_b)


def _gqa_in_kernel(x_ref, mod_ref, g_ref, w_ref, gqk_ref, cos_ref, sin_ref,
                   q_ref, k_ref, v_ref, ks_ref, vs_ref, proj_a, proj_b):
    n_q = GQA_Q_HEADS * GQA_HD
    n_kv = GQA_KV_HEADS * GQA_HD

    def step(prev, cur):
        h = _norm_mod(x_ref[...], g_ref, mod_ref, 0)
        cur[...] = jnp.dot(h, w_ref[...], preferred_element_type=F32)
        c = cos_ref[...]
        s = sin_ref[...]
        scale = LOG2E * GQA_HD ** -0.5
        for hd in range(GQA_Q_HEADS):
            q = _rms(prev[:, hd * GQA_HD:(hd + 1) * GQA_HD], gqk_ref[0:1, :])
            q_ref[hd] = (_rope128(q, c, s) * scale).astype(BF16)
        ks = []
        for hd in range(GQA_KV_HEADS):
            lo = n_q + hd * GQA_HD
            ks.append(_rms(prev[:, lo:lo + GQA_HD], gqk_ref[1:2, :]))
            k_ref[hd] = _rope128(ks[-1], c, s).astype(BF16)
            lo = n_q + n_kv + hd * GQA_HD
            v_ref[hd] = _with_ones(prev[:, lo:lo + GQA_HD].astype(BF16))

        @pl.when(_epi_is_ctx())
        def _():
            for hd in range(GQA_KV_HEADS):
                ks_ref[:, hd * GQA_HD:(hd + 1) * GQA_HD] = ks[hd]
            vs_ref[...] = prev[:, n_q + n_kv:]

    _two_stage(lambda prev, cur: step(prev[0], cur[0]), (proj_a,), (proj_b,))


def _proj_scratch(width):
    return [pltpu.VMEM((TM, width), F32), pltpu.VMEM((TM, width), F32)]


def _gqa_in(x, mod, g1, w_in, g_qk, cos, sin):
    n_kv = GQA_KV_HEADS * GQA_HD
    return pl.pallas_call(
        _gqa_in_kernel,
        grid=(N_BLOCKS + 1,),
        in_specs=[_proj_row_spec(D), _proj_mod_spec(), _full_spec((1, D)), _full_spec(w_in.shape),
                  _full_spec((2, GQA_HD)), _rope_spec(), _rope_spec()],
        out_specs=[_epi_heads_spec(GQA_Q_HEADS, GQA_HD), _epi_heads_spec(GQA_KV_HEADS, GQA_HD),
                   _epi_heads_spec(GQA_KV_HEADS, 2 * GQA_HD), _epi_ctx_row_spec(n_kv),
                   _epi_ctx_row_spec(n_kv)],
        scratch_shapes=_proj_scratch(w_in.shape[1]),
        out_shape=[jax.ShapeDtypeStruct((GQA_Q_HEADS, T, GQA_HD), BF16),
                   jax.ShapeDtypeStruct((GQA_KV_HEADS, T, GQA_HD), BF16),
                   jax.ShapeDtypeStruct((GQA_KV_HEADS, T, 2 * GQA_HD), BF16),
                   jax.ShapeDtypeStruct((N_CTX, n_kv), F32),
                   jax.ShapeDtypeStruct((N_CTX, n_kv), F32)],
        compiler_params=_cparams(("arbitrary",)),
        name="gqa_in",
    )(x, mod, g1, w_in, g_qk, cos, sin)


def _diff_in_kernel(x_ref, mod_ref, g_ref, w_ref, gqk_ref, cos_ref, sin_ref,
                    q_ref, k_ref, v_ref, ks_ref, vs_ref, proj_a, proj_b):
    n = DIFF_HEADS * 2 * DIFF_HD

    def step(prev, cur):
        h = _norm_mod(x_ref[...], g_ref, mod_ref, 0)
        cur[...] = jnp.dot(h, w_ref[...], preferred_element_type=F32)
        c = cos_ref[...]
        s = sin_ref[...]
        scale = LOG2E * DIFF_HD ** -0.5
        lane = lax.broadcasted_iota(jnp.int32, (TM, LANES), 1)
        first = (lane % DIFF_HD) < DIFF_HD // 2

        def rms_sub(x, gain):
            sq = x * x
            s0 = jnp.sum(jnp.where(first, sq, 0.0), axis=-1, keepdims=True)
            s_all = jnp.sum(sq, axis=-1, keepdims=True)
            ms = jnp.where(first, s0, s_all - s0) * (1.0 / DIFF_HD)
            return x * lax.rsqrt(ms + EPS) * gain

        ks = []
        for hd in range(DIFF_HEADS):
            sl = slice(hd * LANES, (hd + 1) * LANES)
            q = _rope128(rms_sub(prev[:, sl], gqk_ref[0:1, :]), c, s) * scale
            q_ref[2 * hd] = jnp.where(first, q, 0.0).astype(BF16)
            q_ref[2 * hd + 1] = jnp.where(first, 0.0, q).astype(BF16)
            ks.append(rms_sub(prev[:, n + hd * LANES:n + (hd + 1) * LANES], gqk_ref[1:2, :]))
            k_ref[hd] = _rope128(ks[-1], c, s).astype(BF16)
            v_ref[hd] = _with_ones(prev[:, 2 * n + hd * LANES:2 * n + (hd + 1) * LANES].astype(BF16))

        @pl.when(_epi_is_ctx())
        def _():
            for hd in range(DIFF_HEADS):
                ks_ref[:, hd * LANES:(hd + 1) * LANES] = ks[hd]
            vs_ref[...] = prev[:, 2 * n:]

    _two_stage(lambda prev, cur: step(prev[0], cur[0]), (proj_a,), (proj_b,))


def _diff_in(x, mod, g1, w_in, g_qk2, cos, sin):
    n = DIFF_HEADS * 2 * DIFF_HD
    return pl.pallas_call(
        _diff_in_kernel,
        grid=(N_BLOCKS + 1,),
        in_specs=[_proj_row_spec(D), _proj_mod_spec(), _full_spec((1, D)), _full_spec(w_in.shape),
                  _full_spec((2, LANES)), _rope_spec(), _rope_spec()],
        out_specs=[_epi_heads_spec(2 * DIFF_HEADS, LANES), _epi_heads_spec(DIFF_HEADS, LANES),
                   _epi_heads_spec(DIFF_HEADS, 2 * LANES), _epi_ctx_row_spec(n), _epi_ctx_row_spec(n)],
        scratch_shapes=_proj_scratch(w_in.shape[1]),
        out_shape=[jax.ShapeDtypeStruct((2 * DIFF_HEADS, T, LANES), BF16),
                   jax.ShapeDtypeStruct((DIFF_HEADS, T, LANES), BF16),
                   jax.ShapeDtypeStruct((DIFF_HEADS, T, 2 * LANES), BF16),
                   jax.ShapeDtypeStruct((N_CTX, n), F32),
                   jax.ShapeDtypeStruct((N_CTX, n), F32)],
        compiler_params=_cparams(("arbitrary",)),
        name="diff_in",
    )(x, mod, g1, w_in, g_qk2, cos, sin)


def _mla_keys_values(ckv_bf16, kr2, w_ukv_ref, gk_ref, k_ref, v_ref):
    kv = jnp.dot(ckv_bf16, w_ukv_ref[...], preferred_element_type=F32)
    lane = lax.broadcasted_iota(jnp.int32, kr2.shape, 1)
    low = lane < MLA_ROPE
    kr_low = jnp.where(low, kr2, 0.0).astype(BF16)
    kr_high = jnp.where(low, 0.0, kr2).astype(BF16)
    width = MLA_NOPE + MLA_V
    for hd in range(MLA_HEADS):
        k_nope = _rms(kv[:, hd * width:hd * width + MLA_NOPE], gk_ref[...])
        k_ref[hd, :, 0:MLA_NOPE] = k_nope.astype(BF16)
        k_ref[hd, :, MLA_NOPE:MLA_DQ] = kr_low if hd % 2 == 0 else kr_high
        v_ref[hd] = _with_ones(kv[:, hd * width + MLA_NOPE:(hd + 1) * width].astype(BF16))


def _mla_in_kernel(xc_ref, xl_ref, mod_ref, g_ref, w_ref, gcq_ref, gckv_ref, gqn_ref, gqr_ref,
                   gkn_ref, gkr_ref, wqn_ref, wqr_ref, wukv_ref, cos_ref, sin_ref,
                   q_ref, k_ref, v_ref, ckv_ref, kr_ref, proj_a, proj_b):
    def step(prev, cur):
        x = jnp.where(_is_ctx_block(), xc_ref[...], xl_ref[...])
        h = _norm_mod(x, g_ref, mod_ref, 0)
        cur[...] = jnp.dot(h, w_ref[...], preferred_element_type=F32)
        c = cos_ref[...]
        s = sin_ref[...]
        cq = _rms(prev[:, 0:MLA_Q_RANK], gcq_ref[...]).astype(BF16)
        ckv = _rms(prev[:, MLA_Q_RANK:MLA_Q_RANK + MLA_KV_RANK], gckv_ref[...])
        kr2 = _rms_halves(prev[:, MLA_Q_RANK + MLA_KV_RANK:], gkr_ref[...])
        _mla_keys_values(ckv.astype(BF16), _rope64x2(kr2, c, s), wukv_ref, gkn_ref, k_ref, v_ref)

        scale = LOG2E * (MLA_NOPE + MLA_ROPE) ** -0.5
        qn = jnp.dot(cq, wqn_ref[...], preferred_element_type=F32)
        qr = jnp.dot(cq, wqr_ref[...], preferred_element_type=F32)
        lane = lax.broadcasted_iota(jnp.int32, (TM, LANES), 1)
        low = lane < MLA_ROPE
        for pair in range(MLA_HEADS // 2):
            r = _rms_halves(qr[:, pair * LANES:(pair + 1) * LANES], gqr_ref[...])
            r = _rope64x2(r, c, s) * scale
            q_ref[2 * pair, :, MLA_NOPE:MLA_DQ] = jnp.where(low, r, 0.0).astype(BF16)
            q_ref[2 * pair + 1, :, MLA_NOPE:MLA_DQ] = jnp.where(low, 0.0, r).astype(BF16)
        for hd in range(MLA_HEADS):
            q = _rms(qn[:, hd * MLA_NOPE:(hd + 1) * MLA_NOPE], gqn_ref[...]) * scale
            q_ref[hd, :, 0:MLA_NOPE] = q.astype(BF16)

        @pl.when(_epi_is_ctx())
        def _():
            ckv_ref[...] = ckv
            kr_ref[...] = kr2[:, 0:MLA_ROPE]

    _two_stage(lambda prev, cur: step(prev[0], cur[0]), (proj_a,), (proj_b,))


def _mla_in(x_ctx, x_lat, mod, g1, w_in, g_cq, g_ckv, gqn, gqr, gkn, gkr, wqn, wqr, wukv, cos, sin):
    n_lat_blocks = N_BLOCKS - N_CTX_BLOCKS
    return pl.pallas_call(
        _mla_in_kernel,
        grid=(N_BLOCKS + 1,),
        in_specs=[_ctx_row_spec(D),
                  pl.BlockSpec((TM, D), lambda i: (jnp.clip(i - N_CTX_BLOCKS, 0, n_lat_blocks - 1), 0)),
                  _proj_mod_spec(), _full_spec((1, D)), _full_spec(w_in.shape),
                  _full_spec((1, MLA_Q_RANK)), _full_spec((1, MLA_KV_RANK)),
                  _full_spec((1, LANES)), _full_spec((1, LANES)), _full_spec((1, LANES)),
                  _full_spec((1, LANES)), _full_spec(wqn.shape), _full_spec(wqr.shape),
                  _full_spec(wukv.shape), _rope_spec(), _rope_spec()],
        out_specs=[_epi_heads_spec(MLA_HEADS, MLA_DQ), _epi_heads_spec(MLA_HEADS, MLA_DQ),
                   _epi_heads_spec(MLA_HEADS, 2 * MLA_V), _epi_ctx_row_spec(MLA_KV_RANK),
                   _epi_ctx_row_spec(MLA_ROPE)],
        scratch_shapes=_proj_scratch(w_in.shape[1]),
        out_shape=[jax.ShapeDtypeStruct((MLA_HEADS, T, MLA_DQ), BF16),
                   jax.ShapeDtypeStruct((MLA_HEADS, T, MLA_DQ), BF16),
                   jax.ShapeDtypeStruct((MLA_HEADS, T, 2 * MLA_V), BF16),
                   jax.ShapeDtypeStruct((N_CTX, MLA_KV_RANK), F32),
                   jax.ShapeDtypeStruct((N_CTX, MLA_ROPE), F32)],
        compiler_params=_cparams(("arbitrary",)),
        name="mla_in",
    )(x_ctx, x_lat, mod, g1, w_in, g_cq, g_ckv, gqn, gqr, gkn, gkr, wqn, wqr, wukv, cos, sin)


def _mla_cache_kernel(ckv_ref, kr_ref, wukv_ref, gkn_ref, k_ref, v_ref):
    kr = kr_ref[...]
    kr2 = jnp.concatenate([kr, kr], axis=-1)
    _mla_keys_values(ckv_ref[...].astype(BF16), kr2, wukv_ref, gkn_ref, k_ref, v_ref)


def _mla_cache(ckv, kr, wukv, gkn):
    rows = ckv.shape[0]
    return pl.pallas_call(
        _mla_cache_kernel,
        grid=(1,),
        in_specs=[_full_spec(ckv.shape), _full_spec(kr.shape), _full_spec(wukv.shape),
                  _full_spec((1, LANES))],
        out_specs=[_full_spec((MLA_HEADS, rows, MLA_DQ)), _full_spec((MLA_HEADS, rows, 2 * MLA_V))],
        out_shape=[jax.ShapeDtypeStruct((MLA_HEADS, rows, MLA_DQ), BF16),
                   jax.ShapeDtypeStruct((MLA_HEADS, rows, 2 * MLA_V), BF16)],
        compiler_params=_cparams(("arbitrary",)),
        name="mla_cache_kv",
    )(ckv, kr, wukv, gkn)


def _lru_in_kernel(x_ref, mod_ref, g_ref, w_ref, gate_ref, xr_ref):
    h = _norm_mod(x_ref[...], g_ref, mod_ref, 0)
    proj = jnp.dot(h, w_ref[...], preferred_element_type=F32)
    gate_ref[...] = proj[:, 0:D]
    xr_ref[...] = proj[:, D:2 * D]


def _lru_in(x, mod, g1, w_in):
    return pl.pallas_call(
        _lru_in_kernel,
        grid=(T // TM,),
        in_specs=[_row_spec(D), _mod_spec(), _full_spec((1, D)), _full_spec(w_in.shape)],
        out_specs=[_row_spec(D), _row_spec(D)],
        out_shape=[jax.ShapeDtypeStruct((T, D), F32), jax.ShapeDtypeStruct((T, D), F32)],
        compiler_params=_cparams(("arbitrary",)),
        name="lru_in",
    )(x, mod, g1, w_in)


def _with_ones(v):
    return jnp.concatenate([v, jnp.ones_like(v)], axis=-1)


def _softmax_pv(q, key_blocks, dv):
    nt = (((1,), (1,)), ((), ()))
    m = acc = None
    for k, v1 in key_blocks:
        s = lax.dot_general(q, k, nt, preferred_element_type=F32)
        mb = jnp.max(s, axis=-1, keepdims=True)
        if m is None:
            m = mb
            acc = jnp.dot(jnp.exp2(s - m).astype(BF16), v1, preferred_element_type=F32)
        else:
            m_new = jnp.maximum(m, mb)
            alpha = jnp.exp2(m - m_new)
            acc = alpha * acc + jnp.dot(jnp.exp2(s - m_new).astype(BF16), v1,
                                        preferred_element_type=F32)
            m = m_new
    return acc[:, 0:dv] / acc[:, dv:2 * dv]


def _attn_kernel(*refs, heads, group, seqs, sub, bk, has_cache, diff, dv, lambda_init):
    it = iter(refs)
    q_ref, kn_ref, vn_ref = next(it), next(it), next(it)
    kc_ref = vc_ref = lam_ref = gsub_ref = None
    if has_cache:
        kc_ref, vc_ref = next(it), next(it)
    if diff:
        lam_ref, gsub_ref = next(it), next(it)
    o_ref = next(it)

    tq = q_ref.shape[1] // seqs
    seq_len = kn_ref.shape[1] // seqs
    stacked = min(group, heads)
    def cached_block(kv):
        if not has_cache:
            return []
        if len(kc_ref.shape) == 2:
            wk = kc_ref.shape[1] // n_kv
            wv = vc_ref.shape[1] // n_kv
            kc, vc = kc_ref[:, kv * wk:(kv + 1) * wk], vc_ref[:, kv * wv:(kv + 1) * wv]
        else:
            kc, vc = kc_ref[kv], vc_ref[kv]
        vc = vc.astype(BF16)
        return [(kc.astype(BF16), vc if vc.shape[-1] == 2 * dv else _with_ones(vc))]

    n_kv = max(heads // group, 1)
    if diff:
        lam = lam_ref[...]
        lam_full = (jnp.exp(jnp.sum(lam[0:1] * lam[1:2], axis=-1, keepdims=True))
                    - jnp.exp(jnp.sum(lam[2:3] * lam[3:4], axis=-1, keepdims=True)) + lambda_init)

    for sq, kv in [(sq, kv) for sq in range(seqs) for kv in range(n_kv)]:
        def body(i, carry, sq=sq, kv=kv):
            rows = pl.ds(pl.multiple_of(sq * tq + i * sub, sub), sub)
            q = jnp.concatenate([q_ref[kv * stacked + g, rows, :] for g in range(stacked)], axis=0)
            keys = [slice(sq * seq_len + j * bk, sq * seq_len + (j + 1) * bk)
                    for j in range(seq_len // bk)]
            blocks = cached_block(kv) + [(kn_ref[kv, ks, :], vn_ref[kv, ks, :]) for ks in keys]
            o = _softmax_pv(q, blocks, dv)
            if diff:
                od = o[0:sub] - lam_full * o[sub:2 * sub]
                od = _rms(od, gsub_ref[...]) * (1.0 - lambda_init)
                o_ref[rows, kv * dv:(kv + 1) * dv] = od.astype(BF16)
            else:
                for g in range(stacked):
                    hd = kv * stacked + g
                    o_ref[rows, hd * dv:(hd + 1) * dv] = o[g * sub:(g + 1) * sub].astype(BF16)
            return carry
        lax.fori_loop(0, tq // sub, body, 0, unroll=True)


def _attention(q, kn, vn, *, seq_len, n_seq, row0, tq, sub, bk, heads_per_step, group, cache=None,
               diff=None, lambda_init=0.0, seqs=1):
    hq, _, dq = q.shape
    hkv, _, dv1 = vn.shape
    dv = dv1 // 2
    kv_per_step = max(heads_per_step // group, 1)
    n_hblk = hq // heads_per_step
    n_qblk = seq_len // tq
    assert seqs == 1 or (n_qblk == 1 and cache is None and row0 % (seqs * seq_len) == 0)
    n_seq //= seqs
    seq0 = row0 // (seqs * seq_len)
    qblk0 = row0 // (seqs * tq)

    def q_map(b, hb, qb):
        return (hb, qblk0 + b * n_qblk + qb, 0)

    def kv_map(b, hb, qb):
        return ((hb * heads_per_step) // (group * kv_per_step), seq0 + b, 0)

    in_specs = [pl.BlockSpec((heads_per_step, seqs * tq, dq), q_map),
                pl.BlockSpec((kv_per_step, seqs * seq_len, dq), kv_map),
                pl.BlockSpec((kv_per_step, seqs * seq_len, dv1), kv_map)]
    args = [q, kn, vn]
    if cache is not None:
        kc, vc, kc_block, kc_map, vc_block, vc_map = cache
        in_specs += [pl.BlockSpec(kc_block, kc_map), pl.BlockSpec(vc_block, vc_map)]
        args += [kc, vc]
    out_heads = heads_per_step
    if diff is not None:
        lam, gsub = diff
        in_specs += [pl.BlockSpec(lam.shape, lambda b, hb, qb: (0, 0)),
                     pl.BlockSpec(gsub.shape, lambda b, hb, qb: (0, 0))]
        args += [lam, gsub]
        out_heads = heads_per_step // 2
    n_out = (hq // 2 if diff is not None else hq) * dv
    kernel = functools.partial(_attn_kernel, heads=heads_per_step, group=group, seqs=seqs, sub=sub,
                               bk=bk, has_cache=cache is not None, diff=diff is not None, dv=dv,
                               lambda_init=lambda_init)
    return pl.pallas_call(
        kernel,
        grid=(n_seq, n_hblk, n_qblk),
        in_specs=in_specs,
        out_specs=pl.BlockSpec((seqs * tq, out_heads * dv),
                               lambda b, hb, qb: (b * n_qblk + qb, hb)),
        out_shape=jax.ShapeDtypeStruct((n_seq * seqs * seq_len, n_out), BF16),
        compiler_params=_cparams(("arbitrary", "arbitrary", "arbitrary")),
        name="attention",
    )(*args)


def _lru_seq_pos(i):
    n_ctx_blocks = N_CTX // LRU_TB
    per_ctx = SEQ // LRU_TB
    per_lat = DEC_SEQ // LRU_TB
    pos = jnp.where(i < n_ctx_blocks, i % per_ctx, (i - n_ctx_blocks) % per_lat)
    length = jnp.where(i < n_ctx_blocks, per_ctx, per_lat)
    return pos == 0, pos == length - 1


def _lru_seq_of_block(i):
    n_ctx_blocks = N_CTX // LRU_TB
    return jnp.where(i < n_ctx_blocks, i // (SEQ // LRU_TB),
                     BATCH + (i - n_ctx_blocks) // (DEC_SEQ // LRU_TB))


LRU_PITCH = LRU_TB + SUBLANES


def _lru_scan_kernel(x_ref, prev_ref, next_ref, cw_ref, cb_ref, wg_ref, bg_ref, lam_ref, h0_ref,
                     *rest, reverse, combine):
    if combine:
        hf_ref, gate_ref, y_ref, st_ref, a_s, u_s, h_s, carry = rest
    else:
        y_ref, st_ref, a_s, u_s, h_s, carry = rest
    j = pl.program_id(0)
    i = (T // LRU_TB - 1 - j) if reverse else j
    first, last = _lru_seq_pos(i)
    starts = last if reverse else first
    ends = first if reverse else last

    x = x_ref[...]
    before = jnp.where(first, 0.0, prev_ref[SUBLANES - 1:SUBLANES, :])
    after = jnp.where(last, 0.0, next_ref[0:2, :])
    row = lax.broadcasted_iota(jnp.int32, (SUBLANES, D), 0)

    def shifted(k, fix):
        y = pltpu.roll(x, (-k) % LRU_TB, 0)
        if k < 0:
            return jnp.concatenate([fix(y[0:SUBLANES]), y[SUBLANES:]], axis=0)
        return jnp.concatenate([y[0:LRU_TB - SUBLANES], fix(y[LRU_TB - SUBLANES:])], axis=0)

    taps = [
        shifted(-1, lambda t: jnp.where(row == 0, before, t)),
        x,
        shifted(1, lambda t: jnp.where(row == SUBLANES - 1, after[0:1], t)),
        shifted(2, lambda t: jnp.where(row == SUBLANES - 2, after[0:1],
                                       jnp.where(row == SUBLANES - 1, after[1:2], t))),
    ]
    xr = taps[0] * cw_ref[0:1, :]
    for t in range(1, 4):
        xr = xr + taps[t] * cw_ref[t:t + 1, :]
    xr = xr + cb_ref[...]

    xr_b = xr.astype(BF16)
    lam = lam_ref[...]
    neg = -lam
    softplus = jnp.maximum(neg, 0.0) + jnp.log1p(jnp.exp(-jnp.abs(neg)))
    for n in range(LRU_BLOCKS):
        sl = slice(n * LRU_BLK, (n + 1) * LRU_BLK)
        g = jnp.dot(xr_b[:, sl], wg_ref[n], preferred_element_type=F32)
        r = jax.nn.sigmoid(g[:, 0:LRU_BLK] + bg_ref[0:1, sl])
        gi = jax.nn.sigmoid(g[:, LRU_BLK:] + bg_ref[1:2, sl])
        log_a = -LRU_C * r * softplus[:, sl]
        th = jnp.tanh(log_a)
        one_minus_a2 = -2.0 * th / (1.0 - th)
        rows_n = slice(n * LRU_PITCH, n * LRU_PITCH + LRU_TB)
        a_s[rows_n, :] = jnp.exp(log_a)
        root = jnp.where(one_minus_a2 > 0.0, one_minus_a2 * lax.rsqrt(one_minus_a2), 0.0)
        u_s[rows_n, :] = root * (gi * xr[:, sl])

    @pl.when(starts)
    def _():
        carry[...] = h0_ref[...]

    def body(t, h):
        tt = (LRU_TB - 1 - t) if reverse else t
        rows = pl.ds(tt, LRU_BLOCKS, stride=LRU_PITCH)
        h = a_s[rows, :] * h + u_s[rows, :]
        h_s[rows, :] = h
        return h

    h_end = lax.fori_loop(0, LRU_TB, body, carry[...], unroll=8)
    carry[...] = h_end

    for n in range(LRU_BLOCKS):
        sl = slice(n * LRU_BLK, (n + 1) * LRU_BLK)
        hs = h_s[n * LRU_PITCH:n * LRU_PITCH + LRU_TB, :]
        if combine:
            y_ref[:, sl] = ((hf_ref[:, sl] + hs) * jax.nn.gelu(gate_ref[:, sl])).astype(BF16)
        else:
            y_ref[:, sl] = hs

    @pl.when(ends)
    def _():
        st_ref[...] = h_end


def _lru_scan(xr, conv_w, conv_b, w_gate, b_gate, lam, h0, *, reverse, hf=None, gate=None):
    nb = T // LRU_TB
    hb = LRU_TB // SUBLANES
    n_halo = T // SUBLANES

    def blk(j):
        return (nb - 1 - j) if reverse else j

    in_specs = [
        pl.BlockSpec((LRU_TB, D), lambda j: (blk(j), 0)),
        pl.BlockSpec((SUBLANES, D), lambda j: (jnp.maximum(blk(j) * hb - 1, 0), 0)),
        pl.BlockSpec((SUBLANES, D), lambda j: (jnp.minimum((blk(j) + 1) * hb, n_halo - 1), 0)),
        pl.BlockSpec((4, D), lambda j: (0, 0)),
        pl.BlockSpec((1, D), lambda j: (0, 0)),
        pl.BlockSpec((LRU_BLOCKS, LRU_BLK, 2 * LRU_BLK), lambda j: (0, 0, 0)),
        pl.BlockSpec((2, D), lambda j: (0, 0)),
        pl.BlockSpec((1, D), lambda j: (0, 0)),
        pl.BlockSpec((None, LRU_BLOCKS, LRU_BLK), lambda j: (_lru_seq_of_block(blk(j)), 0, 0)),
    ]
    args = [xr, xr, xr, conv_w, conv_b, w_gate, b_gate, lam, h0]
    combine = hf is not None
    if combine:
        in_specs += [pl.BlockSpec((LRU_TB, D), lambda j: (blk(j), 0)),
                     pl.BlockSpec((LRU_TB, D), lambda j: (blk(j), 0))]
        args += [hf, gate]
    n_seq = BATCH + DEC_BATCH
    return pl.pallas_call(
        functools.partial(_lru_scan_kernel, reverse=reverse, combine=combine),
        grid=(nb,),
        in_specs=in_specs,
        out_specs=[pl.BlockSpec((LRU_TB, D), lambda j: (blk(j), 0)),
                   pl.BlockSpec((None, LRU_BLOCKS, LRU_BLK),
                                lambda j: (_lru_seq_of_block(blk(j)), 0, 0))],
        out_shape=[jax.ShapeDtypeStruct((T, D), BF16 if combine else F32),
                   jax.ShapeDtypeStruct((n_seq, LRU_BLOCKS, LRU_BLK), F32)],
        scratch_shapes=[pltpu.VMEM((LRU_BLOCKS * LRU_PITCH, LRU_BLK), F32),
                        pltpu.VMEM((LRU_BLOCKS * LRU_PITCH, LRU_BLK), F32),
                        pltpu.VMEM((LRU_BLOCKS * LRU_PITCH, LRU_BLK), F32),
                        pltpu.VMEM((LRU_BLOCKS, LRU_BLK), F32)],
        compiler_params=_cparams(("arbitrary",)),
        name="lru_scan_bwd" if reverse else "lru_scan_fwd",
    )(*args)


def _post_kernel(*refs, split_x, split_o, split_out):
    it = iter(refs)

    def rows(split):
        if split:
            return jnp.where(_is_ctx_block(), next(it)[...], next(it)[...])
        return next(it)[...]

    x = rows(split_x)
    o = rows(split_o)
    mod_ref, g2_ref, wo_ref, win_ref, wout_ref = (next(it) for _ in range(5))
    y_refs = list(it)
    x1 = x + mod_ref[2:3, :] * jnp.dot(o, wo_ref[...], preferred_element_type=F32)
    h = _rms(x1, g2_ref[...])
    h = (h * (1.0 + mod_ref[4:5, :]) + mod_ref[3:4, :]).astype(BF16)
    acc = jnp.zeros((TM, D), F32)
    for c in range(FFN_H // FFN_CHUNK):
        lo = c * FFN_CHUNK
        g = jnp.dot(h, win_ref[:, lo:lo + FFN_CHUNK], preferred_element_type=F32)
        u = jnp.dot(h, win_ref[:, FFN_H + lo:FFN_H + lo + FFN_CHUNK], preferred_element_type=F32)
        a = (jax.nn.silu(g) * u).astype(BF16)
        acc = acc + jnp.dot(a, wout_ref[lo:lo + FFN_CHUNK, :], preferred_element_type=F32)
    y = x1 + mod_ref[5:6, :] * acc
    if split_out:
        @pl.when(_is_ctx_block())
        def _():
            y_refs[0][...] = y

        @pl.when(jnp.logical_not(_is_ctx_block()))
        def _():
            y_refs[1][...] = y
    else:
        y_refs[0][...] = y


def _layer_weight_spec(shape, layer):
    return pl.BlockSpec((None,) + shape[1:], lambda i: (layer,) + (0,) * (len(shape) - 1),
                        pipeline_mode=pl.Buffered(1))


def _post(xs, os, mod, g2, w_o, w_in, w_out, layer, *, split_out):
    def row_specs(arrays):
        return [_ctx_row_spec(D), _proj_lat_row_spec(D)] if len(arrays) == 2 else [_row_spec(D)]

    if split_out:
        out_specs = [_ctx_row_spec(D), _proj_lat_row_spec(D)]
        out_shape = [jax.ShapeDtypeStruct((N_CTX, D), F32), jax.ShapeDtypeStruct((N_LAT, D), F32)]
    else:
        out_specs = [_row_spec(D)]
        out_shape = [jax.ShapeDtypeStruct((T, D), F32)]
    return pl.pallas_call(
        functools.partial(_post_kernel, split_x=len(xs) == 2, split_o=len(os) == 2,
                          split_out=split_out),
        grid=(N_BLOCKS,),
        in_specs=row_specs(xs) + row_specs(os) + [
            _mod_spec(), _full_spec((1, D)), _layer_weight_spec((1,) + w_o.shape, 0),
            _layer_weight_spec(w_in.shape, layer), _layer_weight_spec(w_out.shape, layer)],
        out_specs=out_specs,
        out_shape=out_shape,
        compiler_params=_cparams(("arbitrary",)),
        name="post_mixer_ffn",
    )(*xs, *os, mod, g2, w_o[None], w_in, w_out)


def _axial_tables(rot_dim):
    row = jnp.repeat(jnp.arange(DEC_SEQ // GRID_W), GRID_W).astype(F32)
    col = jnp.tile(jnp.arange(GRID_W), DEC_SEQ // GRID_W).astype(F32)
    n_freq = rot_dim // 4
    inv = ROPE_THETA ** (-jnp.arange(n_freq, dtype=F32) / n_freq)
    ang = jnp.concatenate([row[:, None] * inv, col[:, None] * inv], axis=-1)
    cos, sin = jnp.cos(ang), jnp.sin(ang)
    reps = LANES // rot_dim
    cos_t = jnp.tile(jnp.concatenate([cos, cos], axis=-1), (1, reps))
    sin_t = jnp.tile(jnp.concatenate([-sin, sin], axis=-1), (1, reps))
    cos_t = jnp.concatenate([jnp.ones((TM, LANES), F32), cos_t], axis=0)
    sin_t = jnp.concatenate([jnp.zeros((TM, LANES), F32), sin_t], axis=0)
    return cos_t, sin_t


def kernel(x_prompt, x_sample, cache_mla_ckv, cache_mla_krope, cache_diff_k, cache_diff_v, cache_gqa_k, cache_gqa_v, state_lru_h, c, c_ctx, w_mod, b_mod, g_norm1, g_norm2, w_ffn_in, w_ffn_out, mla_w_in, mla_g_cq, mla_g_ckv, mla_w_uq, mla_w_ukv, mla_g_qk, mla_w_o, diff_w_in, diff_g_qk, diff_lambda, diff_g_sub, diff_w_o, gqa_w_in, gqa_g_qk, gqa_w_o, lru_w_in, lru_conv_w, lru_conv_b, lru_w_gate, lru_b_gate, lru_lambda, lru_w_out):
    x_in = (x_prompt.reshape(N_CTX, D), x_sample.reshape(N_LAT, D))
    cond = jnp.concatenate([c_ctx[None, :], c, jnp.zeros((SUBLANES - N_GROUPS, D), F32)], axis=0)
    mod_all = _modulation(cond, w_mod, b_mod)
    cos128, sin128 = _axial_tables(GQA_HD)
    cos64, sin64 = _axial_tables(DIFF_HD)
    w_ffn_in_b = w_ffn_in.astype(BF16)
    w_ffn_out_b = w_ffn_out.astype(BF16)

    def layer_mod(l):
        return mod_all[l, :N_GROUPS].reshape(N_GROUPS, 6, D)

    def post(l, xs, os, w_o):
        out = _post(xs, os, layer_mod(l), g_norm2[l][None, :], w_o.astype(BF16),
                    w_ffn_in_b, w_ffn_out_b, l, split_out=l == DEPTH - 1)
        return out if l == DEPTH - 1 else out[0]

    def attend(q, k, v, *, hq, group, cache, diff, lambda_init):
        common = dict(group=group, diff=diff, lambda_init=lambda_init)
        sub = ATTN_ROWS // group
        o_ctx = _attention(q, k, v, seq_len=SEQ, n_seq=BATCH, row0=0, tq=SEQ, bk=SEQ,
                           sub=min(sub, SEQ), heads_per_step=hq, cache=None, seqs=CTX_SEQS,
                           **common)
        o_lat = _attention(q, k, v, seq_len=DEC_SEQ, n_seq=DEC_BATCH, row0=N_CTX, tq=ATTN_TILES * sub,
                           bk=ATTN_BK, sub=sub, heads_per_step=LAT_KV_HEADS * group, cache=cache,
                           **common)
        return o_ctx, o_lat

    l = 0
    w_in = mla_w_in[0]
    kr_cols = w_in[:, MLA_Q_RANK + MLA_KV_RANK:]
    w_in2 = jnp.concatenate([w_in, kr_cols], axis=1).astype(BF16)
    w_uq = mla_w_uq[0].reshape(MLA_Q_RANK, MLA_HEADS, MLA_NOPE + MLA_ROPE)
    wqn = w_uq[:, :, :MLA_NOPE].reshape(MLA_Q_RANK, MLA_HEADS * MLA_NOPE).astype(BF16)
    wqr = w_uq[:, :, MLA_NOPE:].reshape(MLA_Q_RANK, MLA_HEADS * MLA_ROPE).astype(BF16)
    wukv = mla_w_ukv[0].astype(BF16)
    gqk = mla_g_qk[0]
    gqn = gqk[0:1, :MLA_NOPE]
    gqr = jnp.tile(gqk[0:1, MLA_NOPE:], (1, 2))
    gkn = gqk[1:2, :MLA_NOPE]
    gkr = jnp.tile(gqk[1:2, MLA_NOPE:], (1, 2))
    q, k, v, ckv_new, kr_new = _mla_in(*x_in, layer_mod(l), g_norm1[l][None, :], w_in2,
                                       mla_g_cq[0][None, :], mla_g_ckv[0][None, :],
                                       gqn, gqr, gkn, gkr, wqn, wqr, wukv, cos64, sin64)
    kc, vc = _mla_cache(cache_mla_ckv[:, 0].reshape(DEC_BATCH * PAST, MLA_KV_RANK),
                        cache_mla_krope[:, 0].reshape(DEC_BATCH * PAST, MLA_ROPE), wukv, gkn)
    cache = (kc, vc,
             (LAT_KV_HEADS, PAST, MLA_DQ), lambda b, hb, qb: (hb, b, 0),
             (LAT_KV_HEADS, PAST, 2 * MLA_V), lambda b, hb, qb: (hb, b, 0))
    o = attend(q, k, v, hq=MLA_HEADS, group=1, cache=cache, diff=None, lambda_init=0.0)
    x = post(l, x_in, o, mla_w_o[0])
    new_mla_ckv = ckv_new.reshape(BATCH, 1, SEQ, MLA_KV_RANK)
    new_mla_krope = kr_new.reshape(BATCH, 1, SEQ, MLA_ROPE)

    l = 1
    lambda_init = 0.8 - 0.6 * math.exp(-0.3 * l)
    n_diff = DIFF_HEADS * 2 * DIFF_HD

    def interleave(a):
        shape = a.shape
        a = a.reshape(shape[:-1] + (shape[-1] // LANES, 2, 2, DIFF_HD // 2))
        return jnp.swapaxes(a, -3, -2).reshape(shape)

    w_in = diff_w_in[0]
    w_in = jnp.concatenate([interleave(w_in[:, :2 * n_diff]), w_in[:, 2 * n_diff:]], axis=1)
    q, k, v, k_new, v_new = _diff_in(x, layer_mod(l), g_norm1[l][None, :], w_in.astype(BF16),
                                     interleave(jnp.tile(diff_g_qk[0], (1, 2))), cos64,
                                     interleave(sin64))
    k_new = interleave(k_new)
    cache = (interleave(cache_diff_k[:, 0].reshape(DEC_BATCH, PAST, n_diff)),
             cache_diff_v[:, 0].reshape(DEC_BATCH, PAST, n_diff),
             (None, PAST, LAT_KV_HEADS * LANES), lambda b, hb, qb: (b, 0, hb),
             (None, PAST, LAT_KV_HEADS * LANES), lambda b, hb, qb: (b, 0, hb))
    o = attend(q, k, v, hq=2 * DIFF_HEADS, group=2, cache=cache,
               diff=(diff_lambda[0], diff_g_sub[0][None, :]), lambda_init=lambda_init)
    x = post(l, (x,), o, diff_w_o[0])
    new_diff_k = k_new.reshape(BATCH, 1, SEQ, DIFF_HEADS, 2, DIFF_HD)
    new_diff_v = v_new.reshape(BATCH, 1, SEQ, DIFF_HEADS, 2 * DIFF_HD)

    l = 2
    q, k, v, k_new, v_new = _gqa_in(x, layer_mod(l), g_norm1[l][None, :], gqa_w_in[0].astype(BF16),
                                    gqa_g_qk[0], cos128, sin128)
    n_kv = GQA_KV_HEADS * GQA_HD
    group = GQA_Q_HEADS // GQA_KV_HEADS
    cache = (cache_gqa_k[:, 0].reshape(DEC_BATCH, PAST, n_kv),
             cache_gqa_v[:, 0].reshape(DEC_BATCH, PAST, n_kv),
             (None, PAST, LAT_KV_HEADS * GQA_HD), lambda b, hb, qb: (b, 0, hb),
             (None, PAST, LAT_KV_HEADS * GQA_HD), lambda b, hb, qb: (b, 0, hb))
    o = attend(q, k, v, hq=GQA_Q_HEADS, group=group, cache=cache, diff=None, lambda_init=0.0)
    x = post(l, (x,), o, gqa_w_o[0])
    new_gqa_k = k_new.reshape(BATCH, 1, SEQ, GQA_KV_HEADS, GQA_HD)
    new_gqa_v = v_new.reshape(BATCH, 1, SEQ, GQA_KV_HEADS, GQA_HD)

    l = 3
    gate, xr = _lru_in(x, layer_mod(l), g_norm1[l][None, :], lru_w_in[0].astype(BF16))
    wg = lru_w_gate[0]
    wg = jnp.concatenate([wg[:, 0], wg[:, 1]], axis=-1).astype(BF16)
    h0 = jnp.concatenate([jnp.zeros((BATCH, 2, D), F32), state_lru_h[:, 0]], axis=0)
    h0 = h0.reshape(BATCH + DEC_BATCH, 2, LRU_BLOCKS, LRU_BLK)
    conv_b = lru_conv_b[0][None, :]
    hf, st_f = _lru_scan(xr, lru_conv_w[0], conv_b, wg[0], lru_b_gate[0, 0], lru_lambda[0, 0][None, :],
                         h0[:, 0], reverse=False)
    y, st_b = _lru_scan(xr, lru_conv_w[0], conv_b, wg[1], lru_b_gate[0, 1], lru_lambda[0, 1][None, :],
                        h0[:, 1], reverse=True, hf=hf, gate=gate)
    y_ctx, y_lat = post(l, (x,), (y,), lru_w_out[0])
    new_lru_h = jnp.stack([st_f[:BATCH].reshape(BATCH, D), st_b[:BATCH].reshape(BATCH, D)],
                          axis=1)[:, None]

    y_prompt = y_ctx.reshape(BATCH, SEQ, D)
    y_sample = y_lat.reshape(DEC_BATCH, DEC_SEQ, D)
    return (y_prompt, y_sample, new_mla_ckv, new_mla_krope, new_diff_k, new_diff_v,
            new_gqa_k, new_gqa_v, new_lru_h)
```

```python
import functools
import math

import jax
import jax.numpy as jnp
from jax import lax
from jax.experimental import pallas as pl
from jax.experimental.pallas import tpu as pltpu

F32 = jnp.float32
BF16 = jnp.bfloat16

D = 1024
BATCH = 32
SEQ = 256
DEPTH = 4
DEC_BATCH = 2
DEC_SEQ = 4096
PAST = 256
GRID_W = 64
EPS = 1e-6
ROPE_THETA = 10000.0
FFN_H = 2816
N_CTX = BATCH * SEQ
N_LAT = DEC_BATCH * DEC_SEQ
T = N_CTX + N_LAT
N_GROUPS = 1 + DEC_BATCH

MLA_HEADS = 8
MLA_NOPE = 128
MLA_ROPE = 64
MLA_V = 128
MLA_Q_RANK = 384
MLA_KV_RANK = 256
MLA_DQ = 256
DIFF_HD = 64
DIFF_HEADS = 8
GQA_HD = 128
GQA_Q_HEADS = 8
GQA_KV_HEADS = 2
LRU_BLOCKS = 8
LRU_BLK = 128
LRU_C = 8.0

LANES = 128
SUBLANES = 8
VMEM_LIMIT = 56 * 1024 * 1024

TM = 512
FFN_CHUNK = 256
LRU_TB = 256
ATTN_TILES = 4
CTX_SEQS = 4
LAT_KV_HEADS = 1
ATTN_BK = 1024
ATTN_ROWS = 1024
LOG2E = 1.4426950408889634


def _cparams(sem):
    return pltpu.CompilerParams(dimension_semantics=sem, vmem_limit_bytes=VMEM_LIMIT)


def _group_of_block(i, rows_per_block):
    n_ctx_blocks = N_CTX // rows_per_block
    per = DEC_SEQ // rows_per_block
    return jnp.maximum(i - (n_ctx_blocks - per), 0) // per


def _rms(x, gain):
    y = x * lax.rsqrt(jnp.mean(x * x, axis=-1, keepdims=True) + EPS)
    return y * gain


def _first_of_pair(shape):
    return (lax.broadcasted_iota(jnp.int32, shape, 1) % 64) < 32


def _interleave(a):
    shape = a.shape
    a = a.reshape(shape[:-1] + (shape[-1] // LANES, 2, 2, 32))
    return jnp.swapaxes(a, -3, -2).reshape(shape)


def _rms_pair(x, gain):
    first = _first_of_pair(x.shape)
    sq = x * x
    s0 = jnp.sum(jnp.where(first, sq, 0.0), axis=-1, keepdims=True)
    s_all = jnp.sum(sq, axis=-1, keepdims=True)
    ms = jnp.where(first, s0, s_all - s0) * (1.0 / 64.0)
    return x * lax.rsqrt(ms + EPS) * gain


def _rope128(x, c, s):
    return x * c + pltpu.roll(x, 64, 1) * s


def _is_ctx_block():
    return pl.program_id(0) < N_CTX // TM


def _norm_mod(x, g_ref, mod_ref, shift_row):
    y = _rms(x, g_ref[...])
    shift = mod_ref[shift_row:shift_row + 1, :]
    scale = mod_ref[shift_row + 1:shift_row + 2, :]
    return (y * (1.0 + scale) + shift).astype(BF16)


def _mod_kernel(cond_ref, w_ref, b_ref, o_ref):
    s = jax.nn.silu(cond_ref[...]).astype(BF16)
    w = w_ref[...].astype(BF16)
    o_ref[...] = jnp.dot(s, w, preferred_element_type=F32) + b_ref[...]


def _modulation(cond, w_mod, b_mod):
    tn = 1536
    return pl.pallas_call(
        _mod_kernel,
        grid=(DEPTH, 6 * D // tn),
        in_specs=[
            pl.BlockSpec((SUBLANES, D), lambda l, j: (0, 0)),
            pl.BlockSpec((None, D, tn), lambda l, j: (l, 0, j)),
            pl.BlockSpec((None, 1, tn), lambda l, j: (l, 0, j)),
        ],
        out_specs=pl.BlockSpec((None, SUBLANES, tn), lambda l, j: (l, 0, j)),
        out_shape=jax.ShapeDtypeStruct((DEPTH, SUBLANES, 6 * D), F32),
        compiler_params=_cparams(("arbitrary", "arbitrary")),
        name="adaln_mod",
    )(cond, w_mod, b_mod.reshape(DEPTH, 1, 6 * D))


def _row_spec(width):
    return pl.BlockSpec((TM, width), lambda i: (i, 0))


def _ctx_row_spec(width):
    return pl.BlockSpec((TM, width), lambda i: (jnp.minimum(i, N_CTX // TM - 1), 0))


def _full_spec(shape):
    return pl.BlockSpec(shape, lambda i: (0,) * len(shape))


def _mod_spec():
    return pl.BlockSpec((None, 6, D), lambda i: (_group_of_block(i, TM), 0, 0))


N_BLOCKS = T // TM
N_CTX_BLOCKS = N_CTX // TM


def _proj_block(i):
    return jnp.minimum(i, N_BLOCKS - 1)


def _epi_block(i):
    return jnp.maximum(i - 1, 0)


def _proj_row_spec(width):
    return pl.BlockSpec((TM, width), lambda i: (_proj_block(i), 0))


def _proj_mod_spec():
    return pl.BlockSpec((None, 6, D), lambda i: (_group_of_block(_proj_block(i), TM), 0, 0))


def _epi_row_spec(width):
    return pl.BlockSpec((TM, width), lambda i: (_epi_block(i), 0))


def _epi_heads_spec(heads, width):
    return pl.BlockSpec((heads, TM, width), lambda i: (0, _epi_block(i), 0))


def _epi_ctx_row_spec(width):
    return pl.BlockSpec((TM, width), lambda i: (jnp.minimum(_epi_block(i), N_CTX_BLOCKS - 1), 0))


def _epi_is_ctx():
    i = pl.program_id(0)
    return jnp.logical_and(i >= 1, i <= N_CTX_BLOCKS)


def _rope_spec():
    per = DEC_SEQ // TM

    def index(i):
        j = _epi_block(i)
        return (jnp.where(j < N_CTX_BLOCKS, 0, 1 + (j - N_CTX_BLOCKS) % per), 0)

    return pl.BlockSpec((TM, LANES), index)


def _proj_lat_row_spec(width):
    n_lat = N_BLOCKS - N_CTX_BLOCKS
    return pl.BlockSpec((TM, width), lambda i: (jnp.clip(i - N_CTX_BLOCKS, 0, n_lat - 1), 0))


def _two_stage(step, bufs_a, bufs_b):
    i = pl.program_id(0)

    @pl.when(i == 0)
    def _():
        for buf in bufs_b:
            buf[...] = jnp.zeros(buf.shape, buf.dtype)

    @pl.when(i % 2 == 0)
    def _():
        step(bufs_b, bufs_a)

    @pl.when(i % 2 == 1)
    def _():
        step(bufs_a, bufs_b)


def _gqa_in_kernel(x_ref, mod_ref, g_ref, w_ref, gqk_ref, cos_ref, sin_ref,
                   q_ref, k_ref, v_ref, ks_ref, vs_ref, proj_a, proj_b):
    n_q = GQA_Q_HEADS * GQA_HD
    n_kv = GQA_KV_HEADS * GQA_HD

    def step(prev, cur):
        h = _norm_mod(x_ref[...], g_ref, mod_ref, 0)
        cur[...] = jnp.dot(h, w_ref[...], preferred_element_type=F32)
        c = cos_ref[...]
        s = sin_ref[...]
        scale = LOG2E * GQA_HD ** -0.5
        for hd in range(GQA_Q_HEADS):
            q = _rms(prev[:, hd * GQA_HD:(hd + 1) * GQA_HD], gqk_ref[0:1, :])
            q_ref[hd] = (_rope128(q, c, s) * scale).astype(BF16)
        ks = []
        for hd in range(GQA_KV_HEADS):
            lo = n_q + hd * GQA_HD
            ks.append(_rms(prev[:, lo:lo + GQA_HD], gqk_ref[1:2, :]))
            k_ref[hd] = _rope128(ks[-1], c, s).astype(BF16)
            lo = n_q + n_kv + hd * GQA_HD
            v_ref[hd] = _with_ones(prev[:, lo:lo + GQA_HD].astype(BF16))

        @pl.when(_epi_is_ctx())
        def _():
            for hd in range(GQA_KV_HEADS):
                ks_ref[:, hd * GQA_HD:(hd + 1) * GQA_HD] = ks[hd]
            vs_ref[...] = prev[:, n_q + n_kv:]

    _two_stage(lambda prev, cur: step(prev[0], cur[0]), (proj_a,), (proj_b,))


def _proj_scratch(width):
    return [pltpu.VMEM((TM, width), F32), pltpu.VMEM((TM, width), F32)]


def _gqa_in(x, mod, g1, w_in, g_qk, cos, sin):
    n_kv = GQA_KV_HEADS * GQA_HD
    return pl.pallas_call(
        _gqa_in_kernel,
        grid=(N_BLOCKS + 1,),
        in_specs=[_proj_row_spec(D), _proj_mod_spec(), _full_spec((1, D)), _full_spec(w_in.shape),
                  _full_spec((2, GQA_HD)), _rope_spec(), _rope_spec()],
        out_specs=[_epi_heads_spec(GQA_Q_HEADS, GQA_HD), _epi_heads_spec(GQA_KV_HEADS, GQA_HD),
                   _epi_heads_spec(GQA_KV_HEADS, 2 * GQA_HD), _epi_ctx_row_spec(n_kv),
                   _epi_ctx_row_spec(n_kv)],
        scratch_shapes=_proj_scratch(w_in.shape[1]),
        out_shape=[jax.ShapeDtypeStruct((GQA_Q_HEADS, T, GQA_HD), BF16),
                   jax.ShapeDtypeStruct((GQA_KV_HEADS, T, GQA_HD), BF16),
                   jax.ShapeDtypeStruct((GQA_KV_HEADS, T, 2 * GQA_HD), BF16),
                   jax.ShapeDtypeStruct((N_CTX, n_kv), F32),
                   jax.ShapeDtypeStruct((N_CTX, n_kv), F32)],
        compiler_params=_cparams(("arbitrary",)),
        name="gqa_in",
    )(x, mod, g1, w_in, g_qk, cos, sin)


def _diff_in_kernel(x_ref, mod_ref, g_ref, w_ref, gqk_ref, cos_ref, sin_ref,
                    q_ref, k_ref, v_ref, ks_ref, vs_ref, proj_a, proj_b):
    n = DIFF_HEADS * 2 * DIFF_HD

    def step(prev, cur):
        h = _norm_mod(x_ref[...], g_ref, mod_ref, 0)
        cur[...] = jnp.dot(h, w_ref[...], preferred_element_type=F32)
        c = cos_ref[...]
        s = sin_ref[...]
        scale = LOG2E * DIFF_HD ** -0.5
        first = _first_of_pair((TM, LANES))
        ks = []
        for hd in range(DIFF_HEADS):
            sl = slice(hd * LANES, (hd + 1) * LANES)
            q = _rope128(_rms_pair(prev[:, sl], gqk_ref[0:1, :]), c, s) * scale
            q_ref[2 * hd] = jnp.where(first, q, 0.0).astype(BF16)
            q_ref[2 * hd + 1] = jnp.where(first, 0.0, q).astype(BF16)
            ks.append(_rms_pair(prev[:, n + hd * LANES:n + (hd + 1) * LANES], gqk_ref[1:2, :]))
            k_ref[hd] = _rope128(ks[-1], c, s).astype(BF16)
            v_ref[hd] = _with_ones(prev[:, 2 * n + hd * LANES:2 * n + (hd + 1) * LANES].astype(BF16))

        @pl.when(_epi_is_ctx())
        def _():
            for hd in range(DIFF_HEADS):
                ks_ref[:, hd * LANES:(hd + 1) * LANES] = ks[hd]
            vs_ref[...] = prev[:, 2 * n:]

    _two_stage(lambda prev, cur: step(prev[0], cur[0]), (proj_a,), (proj_b,))


def _diff_in(x, mod, g1, w_in, g_qk2, cos, sin):
    n = DIFF_HEADS * 2 * DIFF_HD
    return pl.pallas_call(
        _diff_in_kernel,
        grid=(N_BLOCKS + 1,),
        in_specs=[_proj_row_spec(D), _proj_mod_spec(), _full_spec((1, D)), _full_spec(w_in.shape),
                  _full_spec((2, LANES)), _rope_spec(), _rope_spec()],
        out_specs=[_epi_heads_spec(2 * DIFF_HEADS, LANES), _epi_heads_spec(DIFF_HEADS, LANES),
                   _epi_heads_spec(DIFF_HEADS, 2 * LANES), _epi_ctx_row_spec(n), _epi_ctx_row_spec(n)],
        scratch_shapes=_proj_scratch(w_in.shape[1]),
        out_shape=[jax.ShapeDtypeStruct((2 * DIFF_HEADS, T, LANES), BF16),
                   jax.ShapeDtypeStruct((DIFF_HEADS, T, LANES), BF16),
                   jax.ShapeDtypeStruct((DIFF_HEADS, T, 2 * LANES), BF16),
                   jax.ShapeDtypeStruct((N_CTX, n), F32),
                   jax.ShapeDtypeStruct((N_CTX, n), F32)],
        compiler_params=_cparams(("arbitrary",)),
        name="diff_in",
    )(x, mod, g1, w_in, g_qk2, cos, sin)


def _mla_keys_values(ckv_bf16, kr2, w_ukv_ref, gk_ref, k_ref, v_ref):
    kv = jnp.dot(ckv_bf16, w_ukv_ref[...], preferred_element_type=F32)
    low = _first_of_pair(kr2.shape)
    kr_low = jnp.where(low, kr2, 0.0).astype(BF16)
    kr_high = jnp.where(low, 0.0, kr2).astype(BF16)
    width = MLA_NOPE + MLA_V
    for hd in range(MLA_HEADS):
        k_nope = _rms(kv[:, hd * width:hd * width + MLA_NOPE], gk_ref[...])
        k_ref[hd, :, 0:MLA_NOPE] = k_nope.astype(BF16)
        k_ref[hd, :, MLA_NOPE:MLA_DQ] = kr_low if hd % 2 == 0 else kr_high
        v_ref[hd] = _with_ones(kv[:, hd * width + MLA_NOPE:(hd + 1) * width].astype(BF16))


def _mla_in_kernel(xc_ref, xl_ref, mod_ref, g_ref, w_ref, gcq_ref, gckv_ref, gqn_ref, gqr_ref,
                   gkn_ref, gkr_ref, wqn_ref, wqr_ref, wukv_ref, cos_ref, sin_ref,
                   q_ref, k_ref, v_ref, ckv_ref, kr_ref, proj_a, proj_b):
    def step(prev, cur):
        x = jnp.where(_is_ctx_block(), xc_ref[...], xl_ref[...])
        h = _norm_mod(x, g_ref, mod_ref, 0)
        cur[...] = jnp.dot(h, w_ref[...], preferred_element_type=F32)
        c = cos_ref[...]
        s = sin_ref[...]
        cq = _rms(prev[:, 0:MLA_Q_RANK], gcq_ref[...]).astype(BF16)
        ckv = _rms(prev[:, MLA_Q_RANK:MLA_Q_RANK + MLA_KV_RANK], gckv_ref[...])
        kr2 = _rms_pair(prev[:, MLA_Q_RANK + MLA_KV_RANK:], gkr_ref[...])
        _mla_keys_values(ckv.astype(BF16), _rope128(kr2, c, s), wukv_ref, gkn_ref, k_ref, v_ref)

        scale = LOG2E * (MLA_NOPE + MLA_ROPE) ** -0.5
        qn = jnp.dot(cq, wqn_ref[...], preferred_element_type=F32)
        qr = jnp.dot(cq, wqr_ref[...], preferred_element_type=F32)
        low = _first_of_pair((TM, LANES))
        for pair in range(MLA_HEADS // 2):
            r = _rms_pair(qr[:, pair * LANES:(pair + 1) * LANES], gqr_ref[...])
            r = _rope128(r, c, s) * scale
            q_ref[2 * pair, :, MLA_NOPE:MLA_DQ] = jnp.where(low, r, 0.0).astype(BF16)
            q_ref[2 * pair + 1, :, MLA_NOPE:MLA_DQ] = jnp.where(low, 0.0, r).astype(BF16)
        for hd in range(MLA_HEADS):
            q = _rms(qn[:, hd * MLA_NOPE:(hd + 1) * MLA_NOPE], gqn_ref[...]) * scale
            q_ref[hd, :, 0:MLA_NOPE] = q.astype(BF16)

        @pl.when(_epi_is_ctx())
        def _():
            ckv_ref[...] = ckv
            kr_ref[...] = jnp.concatenate([kr2[:, 0:MLA_ROPE // 2],
                                           kr2[:, MLA_ROPE:MLA_ROPE + MLA_ROPE // 2]], axis=1)

    _two_stage(lambda prev, cur: step(prev[0], cur[0]), (proj_a,), (proj_b,))


def _mla_in(x_ctx, x_lat, mod, g1, w_in, g_cq, g_ckv, gqn, gqr, gkn, gkr, wqn, wqr, wukv, cos, sin):
    n_lat_blocks = N_BLOCKS - N_CTX_BLOCKS
    return pl.pallas_call(
        _mla_in_kernel,
        grid=(N_BLOCKS + 1,),
        in_specs=[_ctx_row_spec(D),
                  pl.BlockSpec((TM, D), lambda i: (jnp.clip(i - N_CTX_BLOCKS, 0, n_lat_blocks - 1), 0)),
                  _proj_mod_spec(), _full_spec((1, D)), _full_spec(w_in.shape),
                  _full_spec((1, MLA_Q_RANK)), _full_spec((1, MLA_KV_RANK)),
                  _full_spec((1, LANES)), _full_spec((1, LANES)), _full_spec((1, LANES)),
                  _full_spec((1, LANES)), _full_spec(wqn.shape), _full_spec(wqr.shape),
                  _full_spec(wukv.shape), _rope_spec(), _rope_spec()],
        out_specs=[_epi_heads_spec(MLA_HEADS, MLA_DQ), _epi_heads_spec(MLA_HEADS, MLA_DQ),
                   _epi_heads_spec(MLA_HEADS, 2 * MLA_V), _epi_ctx_row_spec(MLA_KV_RANK),
                   _epi_ctx_row_spec(MLA_ROPE)],
        scratch_shapes=_proj_scratch(w_in.shape[1]),
        out_shape=[jax.ShapeDtypeStruct((MLA_HEADS, T, MLA_DQ), BF16),
                   jax.ShapeDtypeStruct((MLA_HEADS, T, MLA_DQ), BF16),
                   jax.ShapeDtypeStruct((MLA_HEADS, T, 2 * MLA_V), BF16),
                   jax.ShapeDtypeStruct((N_CTX, MLA_KV_RANK), F32),
                   jax.ShapeDtypeStruct((N_CTX, MLA_ROPE), F32)],
        compiler_params=_cparams(("arbitrary",)),
        name="mla_in",
    )(x_ctx, x_lat, mod, g1, w_in, g_cq, g_ckv, gqn, gqr, gkn, gkr, wqn, wqr, wukv, cos, sin)


def _mla_cache_kernel(ckv_ref, kr_ref, wukv_ref, gkn_ref, k_ref, v_ref):
    _mla_keys_values(ckv_ref[...].astype(BF16), kr_ref[...], wukv_ref, gkn_ref, k_ref, v_ref)


def _mla_cache(ckv, kr, wukv, gkn):
    rows = ckv.shape[0]
    return pl.pallas_call(
        _mla_cache_kernel,
        grid=(1,),
        in_specs=[_full_spec(ckv.shape), _full_spec(kr.shape), _full_spec(wukv.shape),
                  _full_spec((1, LANES))],
        out_specs=[_full_spec((MLA_HEADS, rows, MLA_DQ)), _full_spec((MLA_HEADS, rows, 2 * MLA_V))],
        out_shape=[jax.ShapeDtypeStruct((MLA_HEADS, rows, MLA_DQ), BF16),
                   jax.ShapeDtypeStruct((MLA_HEADS, rows, 2 * MLA_V), BF16)],
        compiler_params=_cparams(("arbitrary",)),
        name="mla_cache_kv",
    )(ckv, kr, wukv, gkn)


def _lru_in_kernel(x_ref, mod_ref, g_ref, w_ref, gate_ref, xr_ref):
    h = _norm_mod(x_ref[...], g_ref, mod_ref, 0)
    proj = jnp.dot(h, w_ref[...], preferred_element_type=F32)
    gate_ref[...] = proj[:, 0:D]
    xr_ref[...] = proj[:, D:2 * D]


def _lru_in(x, mod, g1, w_in):
    return pl.pallas_call(
        _lru_in_kernel,
        grid=(T // TM,),
        in_specs=[_row_spec(D), _mod_spec(), _full_spec((1, D)), _full_spec(w_in.shape)],
        out_specs=[_row_spec(D), _row_spec(D)],
        out_shape=[jax.ShapeDtypeStruct((T, D), F32), jax.ShapeDtypeStruct((T, D), F32)],
        compiler_params=_cparams(("arbitrary",)),
        name="lru_in",
    )(x, mod, g1, w_in)


def _with_ones(v):
    return jnp.concatenate([v, jnp.ones_like(v)], axis=-1)


def _softmax_pv(q, key_blocks, dv):
    nt = (((1,), (1,)), ((), ()))
    m = acc = None
    for k, v1 in key_blocks:
        s = lax.dot_general(q, k, nt, preferred_element_type=F32)
        mb = jnp.max(s, axis=-1, keepdims=True)
        if m is None:
            m = mb
            acc = jnp.dot(jnp.exp2(s - m).astype(BF16), v1, preferred_element_type=F32)
        else:
            m_new = jnp.maximum(m, mb)
            alpha = jnp.exp2(m - m_new)
            acc = alpha * acc + jnp.dot(jnp.exp2(s - m_new).astype(BF16), v1,
                                        preferred_element_type=F32)
            m = m_new
    return acc[:, 0:dv] / acc[:, dv:2 * dv]


def _attn_kernel(*refs, heads, group, seqs, sub, bk, has_cache, diff, dv, lambda_init):
    it = iter(refs)
    q_ref, kn_ref, vn_ref = next(it), next(it), next(it)
    kc_ref = vc_ref = lam_ref = gsub_ref = None
    if has_cache:
        kc_ref, vc_ref = next(it), next(it)
    if diff:
        lam_ref, gsub_ref = next(it), next(it)
    o_ref = next(it)

    tq = q_ref.shape[1] // seqs
    seq_len = kn_ref.shape[1] // seqs
    stacked = min(group, heads)
    def cached_block(kv):
        if not has_cache:
            return []
        if len(kc_ref.shape) == 2:
            wk = kc_ref.shape[1] // n_kv
            wv = vc_ref.shape[1] // n_kv
            kc, vc = kc_ref[:, kv * wk:(kv + 1) * wk], vc_ref[:, kv * wv:(kv + 1) * wv]
        else:
            kc, vc = kc_ref[kv], vc_ref[kv]
        vc = vc.astype(BF16)
        return [(kc.astype(BF16), vc if vc.shape[-1] == 2 * dv else _with_ones(vc))]

    n_kv = max(heads // group, 1)
    if diff:
        lam = lam_ref[...]
        lam_full = (jnp.exp(jnp.sum(lam[0:1] * lam[1:2], axis=-1, keepdims=True))
                    - jnp.exp(jnp.sum(lam[2:3] * lam[3:4], axis=-1, keepdims=True)) + lambda_init)

    for sq, kv in [(sq, kv) for sq in range(seqs) for kv in range(n_kv)]:
        def body(i, carry, sq=sq, kv=kv):
            rows = pl.ds(pl.multiple_of(sq * tq + i * sub, sub), sub)
            q = jnp.concatenate([q_ref[kv * stacked + g, rows, :] for g in range(stacked)], axis=0)
            keys = [slice(sq * seq_len + j * bk, sq * seq_len + (j + 1) * bk)
                    for j in range(seq_len // bk)]
            blocks = cached_block(kv) + [(kn_ref[kv, ks, :], vn_ref[kv, ks, :]) for ks in keys]
            o = _softmax_pv(q, blocks, dv)
            if diff:
                od = o[0:sub] - lam_full * o[sub:2 * sub]
                od = _rms(od, gsub_ref[...]) * (1.0 - lambda_init)
                o_ref[rows, kv * dv:(kv + 1) * dv] = od.astype(BF16)
            else:
                for g in range(stacked):
                    hd = kv * stacked + g
                    o_ref[rows, hd * dv:(hd + 1) * dv] = o[g * sub:(g + 1) * sub].astype(BF16)
            return carry
        lax.fori_loop(0, tq // sub, body, 0, unroll=True)


def _attention(q, kn, vn, *, seq_len, n_seq, row0, tq, sub, bk, heads_per_step, group, cache=None,
               diff=None, lambda_init=0.0, seqs=1):
    hq, _, dq = q.shape
    hkv, _, dv1 = vn.shape
    dv = dv1 // 2
    kv_per_step = max(heads_per_step // group, 1)
    n_hblk = hq // heads_per_step
    n_qblk = seq_len // tq
    assert seqs == 1 or (n_qblk == 1 and cache is None and row0 % (seqs * seq_len) == 0)
    n_seq //= seqs
    seq0 = row0 // (seqs * seq_len)
    qblk0 = row0 // (seqs * tq)

    def q_map(b, hb, qb):
        return (hb, qblk0 + b * n_qblk + qb, 0)

    def kv_map(b, hb, qb):
        return ((hb * heads_per_step) // (group * kv_per_step), seq0 + b, 0)

    in_specs = [pl.BlockSpec((heads_per_step, seqs * tq, dq), q_map),
                pl.BlockSpec((kv_per_step, seqs * seq_len, dq), kv_map),
                pl.BlockSpec((kv_per_step, seqs * seq_len, dv1), kv_map)]
    args = [q, kn, vn]
    if cache is not None:
        kc, vc, kc_block, kc_map, vc_block, vc_map = cache
        in_specs += [pl.BlockSpec(kc_block, kc_map), pl.BlockSpec(vc_block, vc_map)]
        args += [kc, vc]
    out_heads = heads_per_step
    if diff is not None:
        lam, gsub = diff
        in_specs += [pl.BlockSpec(lam.shape, lambda b, hb, qb: (0, 0)),
                     pl.BlockSpec(gsub.shape, lambda b, hb, qb: (0, 0))]
        args += [lam, gsub]
        out_heads = heads_per_step // 2
    n_out = (hq // 2 if diff is not None else hq) * dv
    kernel = functools.partial(_attn_kernel, heads=heads_per_step, group=group, seqs=seqs, sub=sub,
                               bk=bk, has_cache=cache is not None, diff=diff is not None, dv=dv,
                               lambda_init=lambda_init)
    return pl.pallas_call(
        kernel,
        grid=(n_seq, n_hblk, n_qblk),
        in_specs=in_specs,
        out_specs=pl.BlockSpec((seqs * tq, out_heads * dv),
                               lambda b, hb, qb: (b * n_qblk + qb, hb)),
        out_shape=jax.ShapeDtypeStruct((n_seq * seqs * seq_len, n_out), BF16),
        compiler_params=_cparams(("arbitrary", "arbitrary", "arbitrary")),
        name="attention",
    )(*args)


def _lru_seq_pos(i):
    n_ctx_blocks = N_CTX // LRU_TB
    per_ctx = SEQ // LRU_TB
    per_lat = DEC_SEQ // LRU_TB
    pos = jnp.where(i < n_ctx_blocks, i % per_ctx, (i - n_ctx_blocks) % per_lat)
    length = jnp.where(i < n_ctx_blocks, per_ctx, per_lat)
    return pos == 0, pos == length - 1


def _lru_seq_of_block(i):
    n_ctx_blocks = N_CTX // LRU_TB
    return jnp.where(i < n_ctx_blocks, i // (SEQ // LRU_TB),
                     BATCH + (i - n_ctx_blocks) // (DEC_SEQ // LRU_TB))


LRU_PITCH = LRU_TB + SUBLANES


def _lru_scan_kernel(x_ref, prev_ref, next_ref, cw_ref, cb_ref, wg_ref, bg_ref, lam_ref, h0_ref,
                     *rest, reverse, combine):
    if combine:
        hf_ref, gate_ref, y_ref, st_ref, a_s, u_s, h_s, carry = rest
    else:
        y_ref, st_ref, a_s, u_s, h_s, carry = rest
    j = pl.program_id(0)
    i = (T // LRU_TB - 1 - j) if reverse else j
    first, last = _lru_seq_pos(i)
    starts = last if reverse else first
    ends = first if reverse else last

    x = x_ref[...]
    before = jnp.where(first, 0.0, prev_ref[SUBLANES - 1:SUBLANES, :])
    after = jnp.where(last, 0.0, next_ref[0:2, :])
    row = lax.broadcasted_iota(jnp.int32, (SUBLANES, D), 0)

    def shifted(k, fix):
        y = pltpu.roll(x, (-k) % LRU_TB, 0)
        if k < 0:
            return jnp.concatenate([fix(y[0:SUBLANES]), y[SUBLANES:]], axis=0)
        return jnp.concatenate([y[0:LRU_TB - SUBLANES], fix(y[LRU_TB - SUBLANES:])], axis=0)

    taps = [
        shifted(-1, lambda t: jnp.where(row == 0, before, t)),
        x,
        shifted(1, lambda t: jnp.where(row == SUBLANES - 1, after[0:1], t)),
        shifted(2, lambda t: jnp.where(row == SUBLANES - 2, after[0:1],
                                       jnp.where(row == SUBLANES - 1, after[1:2], t))),
    ]
    xr = taps[0] * cw_ref[0:1, :]
    for t in range(1, 4):
        xr = xr + taps[t] * cw_ref[t:t + 1, :]
    xr = xr + cb_ref[...]

    xr_b = xr.astype(BF16)
    lam = lam_ref[...]
    neg = -lam
    softplus = jnp.maximum(neg, 0.0) + jnp.log1p(jnp.exp(-jnp.abs(neg)))
    for n in range(LRU_BLOCKS):
        sl = slice(n * LRU_BLK, (n + 1) * LRU_BLK)
        g = jnp.dot(xr_b[:, sl], wg_ref[n], preferred_element_type=F32)
        r = jax.nn.sigmoid(g[:, 0:LRU_BLK] + bg_ref[0:1, sl])
        gi = jax.nn.sigmoid(g[:, LRU_BLK:] + bg_ref[1:2, sl])
        log_a = -LRU_C * r * softplus[:, sl]
        th = jnp.tanh(log_a)
        one_minus_a2 = -2.0 * th / (1.0 - th)
        rows_n = slice(n * LRU_PITCH, n * LRU_PITCH + LRU_TB)
        a_s[rows_n, :] = jnp.exp(log_a)
        root = jnp.where(one_minus_a2 > 0.0, one_minus_a2 * lax.rsqrt(one_minus_a2), 0.0)
        u_s[rows_n, :] = root * (gi * xr[:, sl])

    @pl.when(starts)
    def _():
        carry[...] = h0_ref[...]

    def body(t, h):
        tt = (LRU_TB - 1 - t) if reverse else t
        rows = pl.ds(tt, LRU_BLOCKS, stride=LRU_PITCH)
        h = a_s[rows, :] * h + u_s[rows, :]
        h_s[rows, :] = h
        return h

    h_end = lax.fori_loop(0, LRU_TB, body, carry[...], unroll=8)
    carry[...] = h_end

    for n in range(LRU_BLOCKS):
        sl = slice(n * LRU_BLK, (n + 1) * LRU_BLK)
        hs = h_s[n * LRU_PITCH:n * LRU_PITCH + LRU_TB, :]
        if combine:
            y_ref[:, sl] = ((hf_ref[:, sl] + hs) * jax.nn.gelu(gate_ref[:, sl])).astype(BF16)
        else:
            y_ref[:, sl] = hs

    @pl.when(ends)
    def _():
        st_ref[...] = h_end


def _lru_scan(xr, conv_w, conv_b, w_gate, b_gate, lam, h0, *, reverse, hf=None, gate=None):
    nb = T // LRU_TB
    hb = LRU_TB // SUBLANES
    n_halo = T // SUBLANES

    def blk(j):
        return (nb - 1 - j) if reverse else j

    in_specs = [
        pl.BlockSpec((LRU_TB, D), lambda j: (blk(j), 0)),
        pl.BlockSpec((SUBLANES, D), lambda j: (jnp.maximum(blk(j) * hb - 1, 0), 0)),
        pl.BlockSpec((SUBLANES, D), lambda j: (jnp.minimum((blk(j) + 1) * hb, n_halo - 1), 0)),
        pl.BlockSpec((4, D), lambda j: (0, 0)),
        pl.BlockSpec((1, D), lambda j: (0, 0)),
        pl.BlockSpec((LRU_BLOCKS, LRU_BLK, 2 * LRU_BLK), lambda j: (0, 0, 0)),
        pl.BlockSpec((2, D), lambda j: (0, 0)),
        pl.BlockSpec((1, D), lambda j: (0, 0)),
        pl.BlockSpec((None, LRU_BLOCKS, LRU_BLK), lambda j: (_lru_seq_of_block(blk(j)), 0, 0)),
    ]
    args = [xr, xr, xr, conv_w, conv_b, w_gate, b_gate, lam, h0]
    combine = hf is not None
    if combine:
        in_specs += [pl.BlockSpec((LRU_TB, D), lambda j: (blk(j), 0)),
                     pl.BlockSpec((LRU_TB, D), lambda j: (blk(j), 0))]
        args += [hf, gate]
    n_seq = BATCH + DEC_BATCH
    return pl.pallas_call(
        functools.partial(_lru_scan_kernel, reverse=reverse, combine=combine),
        grid=(nb,),
        in_specs=in_specs,
        out_specs=[pl.BlockSpec((LRU_TB, D), lambda j: (blk(j), 0)),
                   pl.BlockSpec((None, LRU_BLOCKS, LRU_BLK),
                                lambda j: (_lru_seq_of_block(blk(j)), 0, 0))],
        out_shape=[jax.ShapeDtypeStruct((T, D), BF16 if combine else F32),
                   jax.ShapeDtypeStruct((n_seq, LRU_BLOCKS, LRU_BLK), F32)],
        scratch_shapes=[pltpu.VMEM((LRU_BLOCKS * LRU_PITCH, LRU_BLK), F32),
                        pltpu.VMEM((LRU_BLOCKS * LRU_PITCH, LRU_BLK), F32),
                        pltpu.VMEM((LRU_BLOCKS * LRU_PITCH, LRU_BLK), F32),
                        pltpu.VMEM((LRU_BLOCKS, LRU_BLK), F32)],
        compiler_params=_cparams(("arbitrary",)),
        name="lru_scan_bwd" if reverse else "lru_scan_fwd",
    )(*args)


def _post_kernel(*refs, split_x, split_o, split_out):
    it = iter(refs)

    def rows(split):
        if split:
            return jnp.where(_is_ctx_block(), next(it)[...], next(it)[...])
        return next(it)[...]

    x = rows(split_x)
    o = rows(split_o)
    mod_ref, g2_ref, wo_ref, win_ref, wout_ref = (next(it) for _ in range(5))
    y_refs = list(it)
    x1 = x + mod_ref[2:3, :] * jnp.dot(o, wo_ref[...], preferred_element_type=F32)
    h = _rms(x1, g2_ref[...])
    h = (h * (1.0 + mod_ref[4:5, :]) + mod_ref[3:4, :]).astype(BF16)
    acc = jnp.zeros((TM, D), F32)
    for c in range(FFN_H // FFN_CHUNK):
        lo = c * FFN_CHUNK
        g = jnp.dot(h, win_ref[:, lo:lo + FFN_CHUNK], preferred_element_type=F32)
        u = jnp.dot(h, win_ref[:, FFN_H + lo:FFN_H + lo + FFN_CHUNK], preferred_element_type=F32)
        a = (jax.nn.silu(g) * u).astype(BF16)
        acc = acc + jnp.dot(a, wout_ref[lo:lo + FFN_CHUNK, :], preferred_element_type=F32)
    y = x1 + mod_ref[5:6, :] * acc
    if split_out:
        @pl.when(_is_ctx_block())
        def _():
            y_refs[0][...] = y

        @pl.when(jnp.logical_not(_is_ctx_block()))
        def _():
            y_refs[1][...] = y
    else:
        y_refs[0][...] = y


def _layer_weight_spec(shape, layer):
    return pl.BlockSpec((None,) + shape[1:], lambda i: (layer,) + (0,) * (len(shape) - 1),
                        pipeline_mode=pl.Buffered(1))


def _post(xs, os, mod, g2, w_o, w_in, w_out, layer, *, split_out):
    def row_specs(arrays):
        return [_ctx_row_spec(D), _proj_lat_row_spec(D)] if len(arrays) == 2 else [_row_spec(D)]

    if split_out:
        out_specs = [_ctx_row_spec(D), _proj_lat_row_spec(D)]
        out_shape = [jax.ShapeDtypeStruct((N_CTX, D), F32), jax.ShapeDtypeStruct((N_LAT, D), F32)]
    else:
        out_specs = [_row_spec(D)]
        out_shape = [jax.ShapeDtypeStruct((T, D), F32)]
    return pl.pallas_call(
        functools.partial(_post_kernel, split_x=len(xs) == 2, split_o=len(os) == 2,
                          split_out=split_out),
        grid=(N_BLOCKS,),
        in_specs=row_specs(xs) + row_specs(os) + [
            _mod_spec(), _full_spec((1, D)), _layer_weight_spec((1,) + w_o.shape, 0),
            _layer_weight_spec(w_in.shape, layer), _layer_weight_spec(w_out.shape, layer)],
        out_specs=out_specs,
        out_shape=out_shape,
        compiler_params=_cparams(("arbitrary",)),
        name="post_mixer_ffn",
    )(*xs, *os, mod, g2, w_o[None], w_in, w_out)


def _axial_tables(rot_dim):
    row = jnp.repeat(jnp.arange(DEC_SEQ // GRID_W), GRID_W).astype(F32)
    col = jnp.tile(jnp.arange(GRID_W), DEC_SEQ // GRID_W).astype(F32)
    n_freq = rot_dim // 4
    inv = ROPE_THETA ** (-jnp.arange(n_freq, dtype=F32) / n_freq)
    ang = jnp.concatenate([row[:, None] * inv, col[:, None] * inv], axis=-1)
    cos, sin = jnp.cos(ang), jnp.sin(ang)
    reps = LANES // rot_dim
    cos_t = jnp.tile(jnp.concatenate([cos, cos], axis=-1), (1, reps))
    sin_t = jnp.tile(jnp.concatenate([-sin, sin], axis=-1), (1, reps))
    cos_t = jnp.concatenate([jnp.ones((TM, LANES), F32), cos_t], axis=0)
    sin_t = jnp.concatenate([jnp.zeros((TM, LANES), F32), sin_t], axis=0)
    return cos_t, sin_t


def kernel(x_prompt, x_sample, cache_mla_ckv, cache_mla_krope, cache_diff_k, cache_diff_v, cache_gqa_k, cache_gqa_v, state_lru_h, c, c_ctx, w_mod, b_mod, g_norm1, g_norm2, w_ffn_in, w_ffn_out, mla_w_in, mla_g_cq, mla_g_ckv, mla_w_uq, mla_w_ukv, mla_g_qk, mla_w_o, diff_w_in, diff_g_qk, diff_lambda, diff_g_sub, diff_w_o, gqa_w_in, gqa_g_qk, gqa_w_o, lru_w_in, lru_conv_w, lru_conv_b, lru_w_gate, lru_b_gate, lru_lambda, lru_w_out):
    x_in = (x_prompt.reshape(N_CTX, D), x_sample.reshape(N_LAT, D))
    cond = jnp.concatenate([c_ctx[None, :], c, jnp.zeros((SUBLANES - N_GROUPS, D), F32)], axis=0)
    mod_all = _modulation(cond, w_mod, b_mod)
    cos128, sin128 = _axial_tables(GQA_HD)
    cos64, sin64 = _axial_tables(DIFF_HD)
    w_ffn_in_b = w_ffn_in.astype(BF16)
    w_ffn_out_b = w_ffn_out.astype(BF16)

    def layer_mod(l):
        return mod_all[l, :N_GROUPS].reshape(N_GROUPS, 6, D)

    def post(l, xs, os, w_o):
        out = _post(xs, os, layer_mod(l), g_norm2[l][None, :], w_o.astype(BF16),
                    w_ffn_in_b, w_ffn_out_b, l, split_out=l == DEPTH - 1)
        return out if l == DEPTH - 1 else out[0]

    def attend(q, k, v, *, hq, group, cache, diff, lambda_init):
        common = dict(group=group, diff=diff, lambda_init=lambda_init)
        sub = ATTN_ROWS // group
        o_ctx = _attention(q, k, v, seq_len=SEQ, n_seq=BATCH, row0=0, tq=SEQ, bk=SEQ,
                           sub=min(sub, SEQ), heads_per_step=hq, cache=None, seqs=CTX_SEQS,
                           **common)
        o_lat = _attention(q, k, v, seq_len=DEC_SEQ, n_seq=DEC_BATCH, row0=N_CTX, tq=ATTN_TILES * sub,
                           bk=ATTN_BK, sub=sub, heads_per_step=LAT_KV_HEADS * group, cache=cache,
                           **common)
        return o_ctx, o_lat

    l = 0
    w_in = mla_w_in[0]
    sin64i = _interleave(sin64)

    def twice(a):
        return _interleave(jnp.concatenate([a, a], axis=-1))

    n_lat = MLA_Q_RANK + MLA_KV_RANK
    w_in2 = jnp.concatenate([w_in[:, :n_lat], twice(w_in[:, n_lat:])], axis=1).astype(BF16)
    w_uq = mla_w_uq[0].reshape(MLA_Q_RANK, MLA_HEADS, MLA_NOPE + MLA_ROPE)
    wqn = w_uq[:, :, :MLA_NOPE].reshape(MLA_Q_RANK, MLA_HEADS * MLA_NOPE).astype(BF16)
    wqr = _interleave(w_uq[:, :, MLA_NOPE:].reshape(MLA_Q_RANK, MLA_HEADS * MLA_ROPE)).astype(BF16)
    wukv = mla_w_ukv[0].astype(BF16)
    gqk = mla_g_qk[0]
    gqn = gqk[0:1, :MLA_NOPE]
    gqr = twice(gqk[0:1, MLA_NOPE:])
    gkn = gqk[1:2, :MLA_NOPE]
    gkr = twice(gqk[1:2, MLA_NOPE:])
    q, k, v, ckv_new, kr_new = _mla_in(*x_in, layer_mod(l), g_norm1[l][None, :], w_in2,
                                       mla_g_cq[0][None, :], mla_g_ckv[0][None, :],
                                       gqn, gqr, gkn, gkr, wqn, wqr, wukv, cos64, sin64i)
    kc, vc = _mla_cache(cache_mla_ckv[:, 0].reshape(DEC_BATCH * PAST, MLA_KV_RANK),
                        twice(cache_mla_krope[:, 0].reshape(DEC_BATCH * PAST, MLA_ROPE)),
                        wukv, gkn)
    cache = (kc, vc,
             (LAT_KV_HEADS, PAST, MLA_DQ), lambda b, hb, qb: (hb, b, 0),
             (LAT_KV_HEADS, PAST, 2 * MLA_V), lambda b, hb, qb: (hb, b, 0))
    o = attend(q, k, v, hq=MLA_HEADS, group=1, cache=cache, diff=None, lambda_init=0.0)
    x = post(l, x_in, o, mla_w_o[0])
    new_mla_ckv = ckv_new.reshape(BATCH, 1, SEQ, MLA_KV_RANK)
    new_mla_krope = kr_new.reshape(BATCH, 1, SEQ, MLA_ROPE)

    l = 1
    lambda_init = 0.8 - 0.6 * math.exp(-0.3 * l)
    n_diff = DIFF_HEADS * 2 * DIFF_HD

    w_in = diff_w_in[0]
    w_in = jnp.concatenate([_interleave(w_in[:, :2 * n_diff]), w_in[:, 2 * n_diff:]], axis=1)
    q, k, v, k_new, v_new = _diff_in(x, layer_mod(l), g_norm1[l][None, :], w_in.astype(BF16),
                                     _interleave(jnp.tile(diff_g_qk[0], (1, 2))), cos64, sin64i)
    k_new = _interleave(k_new)
    cache = (_interleave(cache_diff_k[:, 0].reshape(DEC_BATCH, PAST, n_diff)),
             cache_diff_v[:, 0].reshape(DEC_BATCH, PAST, n_diff),
             (None, PAST, LAT_KV_HEADS * LANES), lambda b, hb, qb: (b, 0, hb),
             (None, PAST, LAT_KV_HEADS * LANES), lambda b, hb, qb: (b, 0, hb))
    o = attend(q, k, v, hq=2 * DIFF_HEADS, group=2, cache=cache,
               diff=(diff_lambda[0], diff_g_sub[0][None, :]), lambda_init=lambda_init)
    x = post(l, (x,), o, diff_w_o[0])
    new_diff_k = k_new.reshape(BATCH, 1, SEQ, DIFF_HEADS, 2, DIFF_HD)
    new_diff_v = v_new.reshape(BATCH, 1, SEQ, DIFF_HEADS, 2 * DIFF_HD)

    l = 2
    q, k, v, k_new, v_new = _gqa_in(x, layer_mod(l), g_norm1[l][None, :], gqa_w_in[0].astype(BF16),
                                    gqa_g_qk[0], cos128, sin128)
    n_kv = GQA_KV_HEADS * GQA_HD
    group = GQA_Q_HEADS // GQA_KV_HEADS
    cache = (cache_gqa_k[:, 0].reshape(DEC_BATCH, PAST, n_kv),
             cache_gqa_v[:, 0].reshape(DEC_BATCH, PAST, n_kv),
             (None, PAST, LAT_KV_HEADS * GQA_HD), lambda b, hb, qb: (b, 0, hb),
             (None, PAST, LAT_KV_HEADS * GQA_HD), lambda b, hb, qb: (b, 0, hb))
    o = attend(q, k, v, hq=GQA_Q_HEADS, group=group, cache=cache, diff=None, lambda_init=0.0)
    x = post(l, (x,), o, gqa_w_o[0])
    new_gqa_k = k_new.reshape(BATCH, 1, SEQ, GQA_KV_HEADS, GQA_HD)
    new_gqa_v = v_new.reshape(BATCH, 1, SEQ, GQA_KV_HEADS, GQA_HD)

    l = 3
    gate, xr = _lru_in(x, layer_mod(l), g_norm1[l][None, :], lru_w_in[0].astype(BF16))
    wg = lru_w_gate[0]
    wg = jnp.concatenate([wg[:, 0], wg[:, 1]], axis=-1).astype(BF16)
    h0 = jnp.concatenate([jnp.zeros((BATCH, 2, D), F32), state_lru_h[:, 0]], axis=0)
    h0 = h0.reshape(BATCH + DEC_BATCH, 2, LRU_BLOCKS, LRU_BLK)
    conv_b = lru_conv_b[0][None, :]
    hf, st_f = _lru_scan(xr, lru_conv_w[0], conv_b, wg[0], lru_b_gate[0, 0], lru_lambda[0, 0][None, :],
                         h0[:, 0], reverse=False)
    y, st_b = _lru_scan(xr, lru_conv_w[0], conv_b, wg[1], lru_b_gate[0, 1], lru_lambda[0, 1][None, :],
                        h0[:, 1], reverse=True, hf=hf, gate=gate)
    y_ctx, y_lat = post(l, (x,), (y,), lru_w_out[0])
    new_lru_h = jnp.stack([st_f[:BATCH].reshape(BATCH, D), st_b[:BATCH].reshape(BATCH, D)],
                          axis=1)[:, None]

    y_prompt = y_ctx.reshape(BATCH, SEQ, D)
    y_sample = y_lat.reshape(DEC_BATCH, DEC_SEQ, D)
    return (y_prompt, y_sample, new_mla_ckv, new_mla_krope, new_diff_k, new_diff_v,
            new_gqa_k, new_gqa_v, new_lru_h)
```

```python
import functools
import math

import jax
import jax.numpy as jnp
from jax import lax
from jax.experimental import pallas as pl
from jax.experimental.pallas import tpu as pltpu

F32 = jnp.float32
BF16 = jnp.bfloat16

D = 1024
BATCH = 32
SEQ = 256
DEPTH = 4
DEC_BATCH = 2
DEC_SEQ = 4096
PAST = 256
GRID_W = 64
EPS = 1e-6
ROPE_THETA = 10000.0
FFN_H = 2816
N_CTX = BATCH * SEQ
N_LAT = DEC_BATCH * DEC_SEQ
T = N_CTX + N_LAT
N_GROUPS = 1 + DEC_BATCH

MLA_HEADS = 8
MLA_NOPE = 128
MLA_ROPE = 64
MLA_V = 128
MLA_Q_RANK = 384
MLA_KV_RANK = 256
MLA_DQ = 256
DIFF_HD = 64
DIFF_HEADS = 8
GQA_HD = 128
GQA_Q_HEADS = 8
GQA_KV_HEADS = 2
LRU_BLOCKS = 8
LRU_BLK = 128
LRU_C = 8.0

LANES = 128
SUBLANES = 8
VMEM_LIMIT = 56 * 1024 * 1024

TM = 512
FFN_CHUNK = 1408
LRU_TB = 256
ATTN_TILES = 4
CTX_SEQS = 4
LAT_KV_HEADS = 1
ATTN_BK = 1024
ATTN_ROWS = 1024
LOG2E = 1.4426950408889634


def _cparams(sem):
    return pltpu.CompilerParams(dimension_semantics=sem, vmem_limit_bytes=VMEM_LIMIT)


def _group_of_block(i, rows_per_block):
    n_ctx_blocks = N_CTX // rows_per_block
    per = DEC_SEQ // rows_per_block
    return jnp.maximum(i - (n_ctx_blocks - per), 0) // per


def _rms(x, gain):
    y = x * lax.rsqrt(jnp.mean(x * x, axis=-1, keepdims=True) + EPS)
    return y * gain


def _first_of_pair(shape):
    return (lax.broadcasted_iota(jnp.int32, shape, 1) % 64) < 32


def _interleave(a):
    shape = a.shape
    a = a.reshape(shape[:-1] + (shape[-1] // LANES, 2, 2, 32))
    return jnp.swapaxes(a, -3, -2).reshape(shape)


def _rms_pair(x, gain):
    first = _first_of_pair(x.shape)
    sq = x * x
    s0 = jnp.sum(jnp.where(first, sq, 0.0), axis=-1, keepdims=True)
    s_all = jnp.sum(sq, axis=-1, keepdims=True)
    ms = jnp.where(first, s0, s_all - s0) * (1.0 / 64.0)
    return x * lax.rsqrt(ms + EPS) * gain


def _rope128(x, c, s):
    return x * c + pltpu.roll(x, 64, 1) * s


def _is_ctx_block():
    return pl.program_id(0) < N_CTX // TM


def _norm_mod(x, g_ref, mod_ref, shift_row):
    y = _rms(x, g_ref[...])
    shift = mod_ref[shift_row:shift_row + 1, :]
    scale = mod_ref[shift_row + 1:shift_row + 2, :]
    return (y * (1.0 + scale) + shift).astype(BF16)


def _mod_kernel(cond_ref, w_ref, b_ref, o_ref):
    s = jax.nn.silu(cond_ref[...]).astype(BF16)
    w = w_ref[...].astype(BF16)
    o_ref[...] = jnp.dot(s, w, preferred_element_type=F32) + b_ref[...]


def _modulation(cond, w_mod, b_mod):
    tn = 1536
    return pl.pallas_call(
        _mod_kernel,
        grid=(DEPTH, 6 * D // tn),
        in_specs=[
            pl.BlockSpec((SUBLANES, D), lambda l, j: (0, 0)),
            pl.BlockSpec((None, D, tn), lambda l, j: (l, 0, j)),
            pl.BlockSpec((None, 1, tn), lambda l, j: (l, 0, j)),
        ],
        out_specs=pl.BlockSpec((None, SUBLANES, tn), lambda l, j: (l, 0, j)),
        out_shape=jax.ShapeDtypeStruct((DEPTH, SUBLANES, 6 * D), F32),
        compiler_params=_cparams(("arbitrary", "arbitrary")),
        name="adaln_mod",
    )(cond, w_mod, b_mod.reshape(DEPTH, 1, 6 * D))


def _row_spec(width):
    return pl.BlockSpec((TM, width), lambda i: (i, 0))


def _ctx_row_spec(width):
    return pl.BlockSpec((TM, width), lambda i: (jnp.minimum(i, N_CTX // TM - 1), 0))


def _full_spec(shape):
    return pl.BlockSpec(shape, lambda i: (0,) * len(shape))


def _mod_spec():
    return pl.BlockSpec((None, 6, D), lambda i: (_group_of_block(i, TM), 0, 0))


N_BLOCKS = T // TM
N_CTX_BLOCKS = N_CTX // TM


def _proj_block(i):
    return jnp.minimum(i, N_BLOCKS - 1)


def _epi_block(i):
    return jnp.maximum(i - 1, 0)


def _proj_row_spec(width):
    return pl.BlockSpec((TM, width), lambda i: (_proj_block(i), 0))


def _proj_mod_spec():
    return pl.BlockSpec((None, 6, D), lambda i: (_group_of_block(_proj_block(i), TM), 0, 0))


def _epi_row_spec(width):
    return pl.BlockSpec((TM, width), lambda i: (_epi_block(i), 0))


def _epi_heads_spec(heads, width):
    return pl.BlockSpec((heads, TM, width), lambda i: (0, _epi_block(i), 0))


def _epi_ctx_row_spec(width):
    return pl.BlockSpec((TM, width), lambda i: (jnp.minimum(_epi_block(i), N_CTX_BLOCKS - 1), 0))


def _epi_is_ctx():
    i = pl.program_id(0)
    return jnp.logical_and(i >= 1, i <= N_CTX_BLOCKS)


def _rope_spec():
    per = DEC_SEQ // TM

    def index(i):
        j = _epi_block(i)
        return (jnp.where(j < N_CTX_BLOCKS, 0, 1 + (j - N_CTX_BLOCKS) % per), 0)

    return pl.BlockSpec((TM, LANES), index)


def _proj_lat_row_spec(width):
    n_lat = N_BLOCKS - N_CTX_BLOCKS
    return pl.BlockSpec((TM, width), lambda i: (jnp.clip(i - N_CTX_BLOCKS, 0, n_lat - 1), 0))


def _two_stage(step, bufs_a, bufs_b):
    i = pl.program_id(0)

    @pl.when(i == 0)
    def _():
        for buf in bufs_b:
            buf[...] = jnp.zeros(buf.shape, buf.dtype)

    @pl.when(i % 2 == 0)
    def _():
        step(bufs_b, bufs_a)

    @pl.when(i % 2 == 1)
    def _():
        step(bufs_a, bufs_b)


def _gqa_in_kernel(x_ref, mod_ref, g_ref, w_ref, gqk_ref, cos_ref, sin_ref,
                   q_ref, k_ref, v_ref, ks_ref, vs_ref, proj_a, proj_b):
    n_q = GQA_Q_HEADS * GQA_HD
    n_kv = GQA_KV_HEADS * GQA_HD

    def step(prev, cur):
        h = _norm_mod(x_ref[...], g_ref, mod_ref, 0)
        cur[...] = jnp.dot(h, w_ref[...], preferred_element_type=F32)
        c = cos_ref[...]
        s = sin_ref[...]
        scale = LOG2E * GQA_HD ** -0.5
        for hd in range(GQA_Q_HEADS):
            q = _rms(prev[:, hd * GQA_HD:(hd + 1) * GQA_HD], gqk_ref[0:1, :])
            q_ref[hd] = (_rope128(q, c, s) * scale).astype(BF16)
        ks = []
        for hd in range(GQA_KV_HEADS):
            lo = n_q + hd * GQA_HD
            ks.append(_rms(prev[:, lo:lo + GQA_HD], gqk_ref[1:2, :]))
            k_ref[hd] = _rope128(ks[-1], c, s).astype(BF16)
            lo = n_q + n_kv + hd * GQA_HD
            v_ref[hd] = _with_ones(prev[:, lo:lo + GQA_HD].astype(BF16))

        @pl.when(_epi_is_ctx())
        def _():
            for hd in range(GQA_KV_HEADS):
                ks_ref[:, hd * GQA_HD:(hd + 1) * GQA_HD] = ks[hd]
            vs_ref[...] = prev[:, n_q + n_kv:]

    _two_stage(lambda prev, cur: step(prev[0], cur[0]), (proj_a,), (proj_b,))


def _proj_scratch(width):
    return [pltpu.VMEM((TM, width), F32), pltpu.VMEM((TM, width), F32)]


def _gqa_in(x, mod, g1, w_in, g_qk, cos, sin):
    n_kv = GQA_KV_HEADS * GQA_HD
    return pl.pallas_call(
        _gqa_in_kernel,
        grid=(N_BLOCKS + 1,),
        in_specs=[_proj_row_spec(D), _proj_mod_spec(), _full_spec((1, D)), _full_spec(w_in.shape),
                  _full_spec((2, GQA_HD)), _rope_spec(), _rope_spec()],
        out_specs=[_epi_heads_spec(GQA_Q_HEADS, GQA_HD), _epi_heads_spec(GQA_KV_HEADS, GQA_HD),
                   _epi_heads_spec(GQA_KV_HEADS, 2 * GQA_HD), _epi_ctx_row_spec(n_kv),
                   _epi_ctx_row_spec(n_kv)],
        scratch_shapes=_proj_scratch(w_in.shape[1]),
        out_shape=[jax.ShapeDtypeStruct((GQA_Q_HEADS, T, GQA_HD), BF16),
                   jax.ShapeDtypeStruct((GQA_KV_HEADS, T, GQA_HD), BF16),
                   jax.ShapeDtypeStruct((GQA_KV_HEADS, T, 2 * GQA_HD), BF16),
                   jax.ShapeDtypeStruct((N_CTX, n_kv), F32),
                   jax.ShapeDtypeStruct((N_CTX, n_kv), F32)],
        compiler_params=_cparams(("arbitrary",)),
        name="gqa_in",
    )(x, mod, g1, w_in, g_qk, cos, sin)


def _diff_in_kernel(x_ref, mod_ref, g_ref, w_ref, gqk_ref, cos_ref, sin_ref,
                    q_ref, k_ref, v_ref, ks_ref, vs_ref, proj_a, proj_b):
    n = DIFF_HEADS * 2 * DIFF_HD

    def step(prev, cur):
        h = _norm_mod(x_ref[...], g_ref, mod_ref, 0)
        cur[...] = jnp.dot(h, w_ref[...], preferred_element_type=F32)
        c = cos_ref[...]
        s = sin_ref[...]
        scale = LOG2E * DIFF_HD ** -0.5
        first = _first_of_pair((TM, LANES))
        ks = []
        for hd in range(DIFF_HEADS):
            sl = slice(hd * LANES, (hd + 1) * LANES)
            q = _rope128(_rms_pair(prev[:, sl], gqk_ref[0:1, :]), c, s) * scale
            q_ref[2 * hd] = jnp.where(first, q, 0.0).astype(BF16)
            q_ref[2 * hd + 1] = jnp.where(first, 0.0, q).astype(BF16)
            ks.append(_rms_pair(prev[:, n + hd * LANES:n + (hd + 1) * LANES], gqk_ref[1:2, :]))
            k_ref[hd] = _rope128(ks[-1], c, s).astype(BF16)
            v_ref[hd] = _with_ones(prev[:, 2 * n + hd * LANES:2 * n + (hd + 1) * LANES].astype(BF16))

        @pl.when(_epi_is_ctx())
        def _():
            for hd in range(DIFF_HEADS):
                ks_ref[:, hd * LANES:(hd + 1) * LANES] = ks[hd]
            vs_ref[...] = prev[:, 2 * n:]

    _two_stage(lambda prev, cur: step(prev[0], cur[0]), (proj_a,), (proj_b,))


def _diff_in(x, mod, g1, w_in, g_qk2, cos, sin):
    n = DIFF_HEADS * 2 * DIFF_HD
    return pl.pallas_call(
        _diff_in_kernel,
        grid=(N_BLOCKS + 1,),
        in_specs=[_proj_row_spec(D), _proj_mod_spec(), _full_spec((1, D)), _full_spec(w_in.shape),
                  _full_spec((2, LANES)), _rope_spec(), _rope_spec()],
        out_specs=[_epi_heads_spec(2 * DIFF_HEADS, LANES), _epi_heads_spec(DIFF_HEADS, LANES),
                   _epi_heads_spec(DIFF_HEADS, 2 * LANES), _epi_ctx_row_spec(n), _epi_ctx_row_spec(n)],
        scratch_shapes=_proj_scratch(w_in.shape[1]),
        out_shape=[jax.ShapeDtypeStruct((2 * DIFF_HEADS, T, LANES), BF16),
                   jax.ShapeDtypeStruct((DIFF_HEADS, T, LANES), BF16),
                   jax.ShapeDtypeStruct((DIFF_HEADS, T, 2 * LANES), BF16),
                   jax.ShapeDtypeStruct((N_CTX, n), F32),
                   jax.ShapeDtypeStruct((N_CTX, n), F32)],
        compiler_params=_cparams(("arbitrary",)),
        name="diff_in",
    )(x, mod, g1, w_in, g_qk2, cos, sin)


def _mla_keys_values(ckv_bf16, kr2, w_ukv_ref, gk_ref, k_ref, v_ref):
    kv = jnp.dot(ckv_bf16, w_ukv_ref[...], preferred_element_type=F32)
    low = _first_of_pair(kr2.shape)
    kr_low = jnp.where(low, kr2, 0.0).astype(BF16)
    kr_high = jnp.where(low, 0.0, kr2).astype(BF16)
    width = MLA_NOPE + MLA_V
    for hd in range(MLA_HEADS):
        k_nope = _rms(kv[:, hd * width:hd * width + MLA_NOPE], gk_ref[...])
        k_ref[hd, :, 0:MLA_NOPE] = k_nope.astype(BF16)
        k_ref[hd, :, MLA_NOPE:MLA_DQ] = kr_low if hd % 2 == 0 else kr_high
        v_ref[hd] = _with_ones(kv[:, hd * width + MLA_NOPE:(hd + 1) * width].astype(BF16))


def _mla_in_kernel(xc_ref, xl_ref, mod_ref, g_ref, w_ref, gcq_ref, gckv_ref, gqn_ref, gqr_ref,
                   gkn_ref, gkr_ref, wqn_ref, wqr_ref, wukv_ref, cos_ref, sin_ref,
                   q_ref, k_ref, v_ref, ckv_ref, kr_ref, proj_a, proj_b):
    def step(prev, cur):
        x = jnp.where(_is_ctx_block(), xc_ref[...], xl_ref[...])
        h = _norm_mod(x, g_ref, mod_ref, 0)
        cur[...] = jnp.dot(h, w_ref[...], preferred_element_type=F32)
        c = cos_ref[...]
        s = sin_ref[...]
        cq = _rms(prev[:, 0:MLA_Q_RANK], gcq_ref[...]).astype(BF16)
        ckv = _rms(prev[:, MLA_Q_RANK:MLA_Q_RANK + MLA_KV_RANK], gckv_ref[...])
        kr2 = _rms_pair(prev[:, MLA_Q_RANK + MLA_KV_RANK:], gkr_ref[...])
        _mla_keys_values(ckv.astype(BF16), _rope128(kr2, c, s), wukv_ref, gkn_ref, k_ref, v_ref)

        scale = LOG2E * (MLA_NOPE + MLA_ROPE) ** -0.5
        qn = jnp.dot(cq, wqn_ref[...], preferred_element_type=F32)
        qr = jnp.dot(cq, wqr_ref[...], preferred_element_type=F32)
        low = _first_of_pair((TM, LANES))
        for pair in range(MLA_HEADS // 2):
            r = _rms_pair(qr[:, pair * LANES:(pair + 1) * LANES], gqr_ref[...])
            r = _rope128(r, c, s) * scale
            q_ref[2 * pair, :, MLA_NOPE:MLA_DQ] = jnp.where(low, r, 0.0).astype(BF16)
            q_ref[2 * pair + 1, :, MLA_NOPE:MLA_DQ] = jnp.where(low, 0.0, r).astype(BF16)
        for hd in range(MLA_HEADS):
            q = _rms(qn[:, hd * MLA_NOPE:(hd + 1) * MLA_NOPE], gqn_ref[...]) * scale
            q_ref[hd, :, 0:MLA_NOPE] = q.astype(BF16)

        @pl.when(_epi_is_ctx())
        def _():
            ckv_ref[...] = ckv
            kr_ref[...] = jnp.concatenate([kr2[:, 0:MLA_ROPE // 2],
                                           kr2[:, MLA_ROPE:MLA_ROPE + MLA_ROPE // 2]], axis=1)

    _two_stage(lambda prev, cur: step(prev[0], cur[0]), (proj_a,), (proj_b,))


def _mla_in(x_ctx, x_lat, mod, g1, w_in, g_cq, g_ckv, gqn, gqr, gkn, gkr, wqn, wqr, wukv, cos, sin):
    n_lat_blocks = N_BLOCKS - N_CTX_BLOCKS
    return pl.pallas_call(
        _mla_in_kernel,
        grid=(N_BLOCKS + 1,),
        in_specs=[_ctx_row_spec(D),
                  pl.BlockSpec((TM, D), lambda i: (jnp.clip(i - N_CTX_BLOCKS, 0, n_lat_blocks - 1), 0)),
                  _proj_mod_spec(), _full_spec((1, D)), _full_spec(w_in.shape),
                  _full_spec((1, MLA_Q_RANK)), _full_spec((1, MLA_KV_RANK)),
                  _full_spec((1, LANES)), _full_spec((1, LANES)), _full_spec((1, LANES)),
                  _full_spec((1, LANES)), _full_spec(wqn.shape), _full_spec(wqr.shape),
                  _full_spec(wukv.shape), _rope_spec(), _rope_spec()],
        out_specs=[_epi_heads_spec(MLA_HEADS, MLA_DQ), _epi_heads_spec(MLA_HEADS, MLA_DQ),
                   _epi_heads_spec(MLA_HEADS, 2 * MLA_V), _epi_ctx_row_spec(MLA_KV_RANK),
                   _epi_ctx_row_spec(MLA_ROPE)],
        scratch_shapes=_proj_scratch(w_in.shape[1]),
        out_shape=[jax.ShapeDtypeStruct((MLA_HEADS, T, MLA_DQ), BF16),
                   jax.ShapeDtypeStruct((MLA_HEADS, T, MLA_DQ), BF16),
                   jax.ShapeDtypeStruct((MLA_HEADS, T, 2 * MLA_V), BF16),
                   jax.ShapeDtypeStruct((N_CTX, MLA_KV_RANK), F32),
                   jax.ShapeDtypeStruct((N_CTX, MLA_ROPE), F32)],
        compiler_params=_cparams(("arbitrary",)),
        name="mla_in",
    )(x_ctx, x_lat, mod, g1, w_in, g_cq, g_ckv, gqn, gqr, gkn, gkr, wqn, wqr, wukv, cos, sin)


def _mla_cache_kernel(ckv_ref, kr_ref, wukv_ref, gkn_ref, k_ref, v_ref):
    _mla_keys_values(ckv_ref[...].astype(BF16), kr_ref[...], wukv_ref, gkn_ref, k_ref, v_ref)


def _mla_cache(ckv, kr, wukv, gkn):
    rows = ckv.shape[0]
    return pl.pallas_call(
        _mla_cache_kernel,
        grid=(1,),
        in_specs=[_full_spec(ckv.shape), _full_spec(kr.shape), _full_spec(wukv.shape),
                  _full_spec((1, LANES))],
        out_specs=[_full_spec((MLA_HEADS, rows, MLA_DQ)), _full_spec((MLA_HEADS, rows, 2 * MLA_V))],
        out_shape=[jax.ShapeDtypeStruct((MLA_HEADS, rows, MLA_DQ), BF16),
                   jax.ShapeDtypeStruct((MLA_HEADS, rows, 2 * MLA_V), BF16)],
        compiler_params=_cparams(("arbitrary",)),
        name="mla_cache_kv",
    )(ckv, kr, wukv, gkn)


def _lru_in_kernel(x_ref, mod_ref, g_ref, w_ref, gate_ref, xr_ref):
    h = _norm_mod(x_ref[...], g_ref, mod_ref, 0)
    proj = jnp.dot(h, w_ref[...], preferred_element_type=F32)
    gate_ref[...] = proj[:, 0:D]
    xr_ref[...] = proj[:, D:2 * D]


def _lru_in(x, mod, g1, w_in):
    return pl.pallas_call(
        _lru_in_kernel,
        grid=(T // TM,),
        in_specs=[_row_spec(D), _mod_spec(), _full_spec((1, D)), _full_spec(w_in.shape)],
        out_specs=[_row_spec(D), _row_spec(D)],
        out_shape=[jax.ShapeDtypeStruct((T, D), F32), jax.ShapeDtypeStruct((T, D), F32)],
        compiler_params=_cparams(("arbitrary",)),
        name="lru_in",
    )(x, mod, g1, w_in)


def _with_ones(v):
    return jnp.concatenate([v, jnp.ones_like(v)], axis=-1)


def _softmax_pv(q, key_blocks, dv):
    nt = (((1,), (1,)), ((), ()))
    m = acc = None
    for k, v1 in key_blocks:
        s = lax.dot_general(q, k, nt, preferred_element_type=F32)
        mb = jnp.max(s, axis=-1, keepdims=True)
        if m is None:
            m = mb
            acc = jnp.dot(jnp.exp2(s - m).astype(BF16), v1, preferred_element_type=F32)
        else:
            m_new = jnp.maximum(m, mb)
            alpha = jnp.exp2(m - m_new)
            acc = alpha * acc + jnp.dot(jnp.exp2(s - m_new).astype(BF16), v1,
                                        preferred_element_type=F32)
            m = m_new
    return acc[:, 0:dv] / acc[:, dv:2 * dv]


def _attn_kernel(*refs, heads, group, seqs, sub, bk, has_cache, diff, dv, lambda_init):
    it = iter(refs)
    q_ref, kn_ref, vn_ref = next(it), next(it), next(it)
    kc_ref = vc_ref = lam_ref = gsub_ref = None
    if has_cache:
        kc_ref, vc_ref = next(it), next(it)
    if diff:
        lam_ref, gsub_ref = next(it), next(it)
    o_ref = next(it)

    tq = q_ref.shape[1] // seqs
    seq_len = kn_ref.shape[1] // seqs
    stacked = min(group, heads)
    def cached_block(kv):
        if not has_cache:
            return []
        if len(kc_ref.shape) == 2:
            wk = kc_ref.shape[1] // n_kv
            wv = vc_ref.shape[1] // n_kv
            kc, vc = kc_ref[:, kv * wk:(kv + 1) * wk], vc_ref[:, kv * wv:(kv + 1) * wv]
        else:
            kc, vc = kc_ref[kv], vc_ref[kv]
        vc = vc.astype(BF16)
        return [(kc.astype(BF16), vc if vc.shape[-1] == 2 * dv else _with_ones(vc))]

    n_kv = max(heads // group, 1)
    if diff:
        lam = lam_ref[...]
        lam_full = (jnp.exp(jnp.sum(lam[0:1] * lam[1:2], axis=-1, keepdims=True))
                    - jnp.exp(jnp.sum(lam[2:3] * lam[3:4], axis=-1, keepdims=True)) + lambda_init)

    for sq, kv in [(sq, kv) for sq in range(seqs) for kv in range(n_kv)]:
        def body(i, carry, sq=sq, kv=kv):
            rows = pl.ds(pl.multiple_of(sq * tq + i * sub, sub), sub)
            q = jnp.concatenate([q_ref[kv * stacked + g, rows, :] for g in range(stacked)], axis=0)
            keys = [slice(sq * seq_len + j * bk, sq * seq_len + (j + 1) * bk)
                    for j in range(seq_len // bk)]
            blocks = cached_block(kv) + [(kn_ref[kv, ks, :], vn_ref[kv, ks, :]) for ks in keys]
            o = _softmax_pv(q, blocks, dv)
            if diff:
                od = o[0:sub] - lam_full * o[sub:2 * sub]
                od = _rms(od, gsub_ref[...]) * (1.0 - lambda_init)
                o_ref[rows, kv * dv:(kv + 1) * dv] = od.astype(BF16)
            else:
                for g in range(stacked):
                    hd = kv * stacked + g
                    o_ref[rows, hd * dv:(hd + 1) * dv] = o[g * sub:(g + 1) * sub].astype(BF16)
            return carry
        lax.fori_loop(0, tq // sub, body, 0, unroll=True)


def _attention(q, kn, vn, *, seq_len, n_seq, row0, tq, sub, bk, heads_per_step, group, cache=None,
               diff=None, lambda_init=0.0, seqs=1):
    hq, _, dq = q.shape
    hkv, _, dv1 = vn.shape
    dv = dv1 // 2
    kv_per_step = max(heads_per_step // group, 1)
    n_hblk = hq // heads_per_step
    n_qblk = seq_len // tq
    assert seqs == 1 or (n_qblk == 1 and cache is None and row0 % (seqs * seq_len) == 0)
    n_seq //= seqs
    seq0 = row0 // (seqs * seq_len)
    qblk0 = row0 // (seqs * tq)

    def q_map(b, hb, qb):
        return (hb, qblk0 + b * n_qblk + qb, 0)

    def kv_map(b, hb, qb):
        return ((hb * heads_per_step) // (group * kv_per_step), seq0 + b, 0)

    in_specs = [pl.BlockSpec((heads_per_step, seqs * tq, dq), q_map),
                pl.BlockSpec((kv_per_step, seqs * seq_len, dq), kv_map),
                pl.BlockSpec((kv_per_step, seqs * seq_len, dv1), kv_map)]
    args = [q, kn, vn]
    if cache is not None:
        kc, vc, kc_block, kc_map, vc_block, vc_map = cache
        in_specs += [pl.BlockSpec(kc_block, kc_map), pl.BlockSpec(vc_block, vc_map)]
        args += [kc, vc]
    out_heads = heads_per_step
    if diff is not None:
        lam, gsub = diff
        in_specs += [pl.BlockSpec(lam.shape, lambda b, hb, qb: (0, 0)),
                     pl.BlockSpec(gsub.shape, lambda b, hb, qb: (0, 0))]
        args += [lam, gsub]
        out_heads = heads_per_step // 2
    n_out = (hq // 2 if diff is not None else hq) * dv
    kernel = functools.partial(_attn_kernel, heads=heads_per_step, group=group, seqs=seqs, sub=sub,
                               bk=bk, has_cache=cache is not None, diff=diff is not None, dv=dv,
                               lambda_init=lambda_init)
    return pl.pallas_call(
        kernel,
        grid=(n_seq, n_hblk, n_qblk),
        in_specs=in_specs,
        out_specs=pl.BlockSpec((seqs * tq, out_heads * dv),
                               lambda b, hb, qb: (b * n_qblk + qb, hb)),
        out_shape=jax.ShapeDtypeStruct((n_seq * seqs * seq_len, n_out), BF16),
        compiler_params=_cparams(("arbitrary", "arbitrary", "arbitrary")),
        name="attention",
    )(*args)


def _lru_seq_pos(i):
    n_ctx_blocks = N_CTX // LRU_TB
    per_ctx = SEQ // LRU_TB
    per_lat = DEC_SEQ // LRU_TB
    pos = jnp.where(i < n_ctx_blocks, i % per_ctx, (i - n_ctx_blocks) % per_lat)
    length = jnp.where(i < n_ctx_blocks, per_ctx, per_lat)
    return pos == 0, pos == length - 1


def _lru_seq_of_block(i):
    n_ctx_blocks = N_CTX // LRU_TB
    return jnp.where(i < n_ctx_blocks, i // (SEQ // LRU_TB),
                     BATCH + (i - n_ctx_blocks) // (DEC_SEQ // LRU_TB))


LRU_PITCH = LRU_TB + SUBLANES


def _lru_scan_kernel(x_ref, prev_ref, next_ref, cw_ref, cb_ref, wg_ref, bg_ref, lam_ref, h0_ref,
                     *rest, reverse, combine):
    if combine:
        hf_ref, gate_ref, y_ref, st_ref, a_s, u_s, h_s, carry = rest
    else:
        y_ref, st_ref, a_s, u_s, h_s, carry = rest
    j = pl.program_id(0)
    i = (T // LRU_TB - 1 - j) if reverse else j
    first, last = _lru_seq_pos(i)
    starts = last if reverse else first
    ends = first if reverse else last

    x = x_ref[...]
    before = jnp.where(first, 0.0, prev_ref[SUBLANES - 1:SUBLANES, :])
    after = jnp.where(last, 0.0, next_ref[0:2, :])
    row = lax.broadcasted_iota(jnp.int32, (SUBLANES, D), 0)

    def shifted(k, fix):
        y = pltpu.roll(x, (-k) % LRU_TB, 0)
        if k < 0:
            return jnp.concatenate([fix(y[0:SUBLANES]), y[SUBLANES:]], axis=0)
        return jnp.concatenate([y[0:LRU_TB - SUBLANES], fix(y[LRU_TB - SUBLANES:])], axis=0)

    taps = [
        shifted(-1, lambda t: jnp.where(row == 0, before, t)),
        x,
        shifted(1, lambda t: jnp.where(row == SUBLANES - 1, after[0:1], t)),
        shifted(2, lambda t: jnp.where(row == SUBLANES - 2, after[0:1],
                                       jnp.where(row == SUBLANES - 1, after[1:2], t))),
    ]
    xr = taps[0] * cw_ref[0:1, :]
    for t in range(1, 4):
        xr = xr + taps[t] * cw_ref[t:t + 1, :]
    xr = xr + cb_ref[...]

    xr_b = xr.astype(BF16)
    lam = lam_ref[...]
    neg = -lam
    softplus = jnp.maximum(neg, 0.0) + jnp.log1p(jnp.exp(-jnp.abs(neg)))
    for n in range(LRU_BLOCKS):
        sl = slice(n * LRU_BLK, (n + 1) * LRU_BLK)
        g = jnp.dot(xr_b[:, sl], wg_ref[n], preferred_element_type=F32)
        r = jax.nn.sigmoid(g[:, 0:LRU_BLK] + bg_ref[0:1, sl])
        gi = jax.nn.sigmoid(g[:, LRU_BLK:] + bg_ref[1:2, sl])
        log_a = -LRU_C * r * softplus[:, sl]
        th = jnp.tanh(log_a)
        one_minus_a2 = -2.0 * th / (1.0 - th)
        rows_n = slice(n * LRU_PITCH, n * LRU_PITCH + LRU_TB)
        a_s[rows_n, :] = jnp.exp(log_a)
        root = jnp.where(one_minus_a2 > 0.0, one_minus_a2 * lax.rsqrt(one_minus_a2), 0.0)
        u_s[rows_n, :] = root * (gi * xr[:, sl])

    @pl.when(starts)
    def _():
        carry[...] = h0_ref[...]

    def body(t, h):
        tt = (LRU_TB - 1 - t) if reverse else t
        rows = pl.ds(tt, LRU_BLOCKS, stride=LRU_PITCH)
        h = a_s[rows, :] * h + u_s[rows, :]
        h_s[rows, :] = h
        return h

    h_end = lax.fori_loop(0, LRU_TB, body, carry[...], unroll=8)
    carry[...] = h_end

    for n in range(LRU_BLOCKS):
        sl = slice(n * LRU_BLK, (n + 1) * LRU_BLK)
        hs = h_s[n * LRU_PITCH:n * LRU_PITCH + LRU_TB, :]
        if combine:
            y_ref[:, sl] = ((hf_ref[:, sl] + hs) * jax.nn.gelu(gate_ref[:, sl])).astype(BF16)
        else:
            y_ref[:, sl] = hs

    @pl.when(ends)
    def _():
        st_ref[...] = h_end


def _lru_scan(xr, conv_w, conv_b, w_gate, b_gate, lam, h0, *, reverse, hf=None, gate=None):
    nb = T // LRU_TB
    hb = LRU_TB // SUBLANES
    n_halo = T // SUBLANES

    def blk(j):
        return (nb - 1 - j) if reverse else j

    in_specs = [
        pl.BlockSpec((LRU_TB, D), lambda j: (blk(j), 0)),
        pl.BlockSpec((SUBLANES, D), lambda j: (jnp.maximum(blk(j) * hb - 1, 0), 0)),
        pl.BlockSpec((SUBLANES, D), lambda j: (jnp.minimum((blk(j) + 1) * hb, n_halo - 1), 0)),
        pl.BlockSpec((4, D), lambda j: (0, 0)),
        pl.BlockSpec((1, D), lambda j: (0, 0)),
        pl.BlockSpec((LRU_BLOCKS, LRU_BLK, 2 * LRU_BLK), lambda j: (0, 0, 0)),
        pl.BlockSpec((2, D), lambda j: (0, 0)),
        pl.BlockSpec((1, D), lambda j: (0, 0)),
        pl.BlockSpec((None, LRU_BLOCKS, LRU_BLK), lambda j: (_lru_seq_of_block(blk(j)), 0, 0)),
    ]
    args = [xr, xr, xr, conv_w, conv_b, w_gate, b_gate, lam, h0]
    combine = hf is not None
    if combine:
        in_specs += [pl.BlockSpec((LRU_TB, D), lambda j: (blk(j), 0)),
                     pl.BlockSpec((LRU_TB, D), lambda j: (blk(j), 0))]
        args += [hf, gate]
    n_seq = BATCH + DEC_BATCH
    return pl.pallas_call(
        functools.partial(_lru_scan_kernel, reverse=reverse, combine=combine),
        grid=(nb,),
        in_specs=in_specs,
        out_specs=[pl.BlockSpec((LRU_TB, D), lambda j: (blk(j), 0)),
                   pl.BlockSpec((None, LRU_BLOCKS, LRU_BLK),
                                lambda j: (_lru_seq_of_block(blk(j)), 0, 0))],
        out_shape=[jax.ShapeDtypeStruct((T, D), BF16 if combine else F32),
                   jax.ShapeDtypeStruct((n_seq, LRU_BLOCKS, LRU_BLK), F32)],
        scratch_shapes=[pltpu.VMEM((LRU_BLOCKS * LRU_PITCH, LRU_BLK), F32),
                        pltpu.VMEM((LRU_BLOCKS * LRU_PITCH, LRU_BLK), F32),
                        pltpu.VMEM((LRU_BLOCKS * LRU_PITCH, LRU_BLK), F32),
                        pltpu.VMEM((LRU_BLOCKS, LRU_BLK), F32)],
        compiler_params=_cparams(("arbitrary",)),
        name="lru_scan_bwd" if reverse else "lru_scan_fwd",
    )(*args)


def _post_kernel(*refs, split_x, split_o, split_out):
    it = iter(refs)

    def rows(split):
        if split:
            return jnp.where(_is_ctx_block(), next(it)[...], next(it)[...])
        return next(it)[...]

    x = rows(split_x)
    o = rows(split_o)
    mod_ref, g2_ref, wo_ref, win_ref, wout_ref = (next(it) for _ in range(5))
    y_refs = list(it)
    x1 = x + mod_ref[2:3, :] * jnp.dot(o, wo_ref[...], preferred_element_type=F32)
    h = _rms(x1, g2_ref[...])
    h = (h * (1.0 + mod_ref[4:5, :]) + mod_ref[3:4, :]).astype(BF16)
    acc = jnp.zeros((TM, D), F32)
    for c in range(FFN_H // FFN_CHUNK):
        lo = c * FFN_CHUNK
        g = jnp.dot(h, win_ref[:, lo:lo + FFN_CHUNK], preferred_element_type=F32)
        u = jnp.dot(h, win_ref[:, FFN_H + lo:FFN_H + lo + FFN_CHUNK], preferred_element_type=F32)
        a = (jax.nn.silu(g) * u).astype(BF16)
        acc = acc + jnp.dot(a, wout_ref[lo:lo + FFN_CHUNK, :], preferred_element_type=F32)
    y = x1 + mod_ref[5:6, :] * acc
    if split_out:
        @pl.when(_is_ctx_block())
        def _():
            y_refs[0][...] = y

        @pl.when(jnp.logical_not(_is_ctx_block()))
        def _():
            y_refs[1][...] = y
    else:
        y_refs[0][...] = y


def _layer_weight_spec(shape, layer):
    return pl.BlockSpec((None,) + shape[1:], lambda i: (layer,) + (0,) * (len(shape) - 1),
                        pipeline_mode=pl.Buffered(1))


def _post(xs, os, mod, g2, w_o, w_in, w_out, layer, *, split_out):
    def row_specs(arrays):
        return [_ctx_row_spec(D), _proj_lat_row_spec(D)] if len(arrays) == 2 else [_row_spec(D)]

    if split_out:
        out_specs = [_ctx_row_spec(D), _proj_lat_row_spec(D)]
        out_shape = [jax.ShapeDtypeStruct((N_CTX, D), F32), jax.ShapeDtypeStruct((N_LAT, D), F32)]
    else:
        out_specs = [_row_spec(D)]
        out_shape = [jax.ShapeDtypeStruct((T, D), F32)]
    return pl.pallas_call(
        functools.partial(_post_kernel, split_x=len(xs) == 2, split_o=len(os) == 2,
                          split_out=split_out),
        grid=(N_BLOCKS,),
        in_specs=row_specs(xs) + row_specs(os) + [
            _mod_spec(), _full_spec((1, D)), _layer_weight_spec((1,) + w_o.shape, 0),
            _layer_weight_spec(w_in.shape, layer), _layer_weight_spec(w_out.shape, layer)],
        out_specs=out_specs,
        out_shape=out_shape,
        compiler_params=_cparams(("arbitrary",)),
        name="post_mixer_ffn",
    )(*xs, *os, mod, g2, w_o[None], w_in, w_out)


def _axial_tables(rot_dim):
    row = jnp.repeat(jnp.arange(DEC_SEQ // GRID_W), GRID_W).astype(F32)
    col = jnp.tile(jnp.arange(GRID_W), DEC_SEQ // GRID_W).astype(F32)
    n_freq = rot_dim // 4
    inv = ROPE_THETA ** (-jnp.arange(n_freq, dtype=F32) / n_freq)
    ang = jnp.concatenate([row[:, None] * inv, col[:, None] * inv], axis=-1)
    cos, sin = jnp.cos(ang), jnp.sin(ang)
    reps = LANES // rot_dim
    cos_t = jnp.tile(jnp.concatenate([cos, cos], axis=-1), (1, reps))
    sin_t = jnp.tile(jnp.concatenate([-sin, sin], axis=-1), (1, reps))
    cos_t = jnp.concatenate([jnp.ones((TM, LANES), F32), cos_t], axis=0)
    sin_t = jnp.concatenate([jnp.zeros((TM, LANES), F32), sin_t], axis=0)
    return cos_t, sin_t


def kernel(x_prompt, x_sample, cache_mla_ckv, cache_mla_krope, cache_diff_k, cache_diff_v, cache_gqa_k, cache_gqa_v, state_lru_h, c, c_ctx, w_mod, b_mod, g_norm1, g_norm2, w_ffn_in, w_ffn_out, mla_w_in, mla_g_cq, mla_g_ckv, mla_w_uq, mla_w_ukv, mla_g_qk, mla_w_o, diff_w_in, diff_g_qk, diff_lambda, diff_g_sub, diff_w_o, gqa_w_in, gqa_g_qk, gqa_w_o, lru_w_in, lru_conv_w, lru_conv_b, lru_w_gate, lru_b_gate, lru_lambda, lru_w_out):
    x_in = (x_prompt.reshape(N_CTX, D), x_sample.reshape(N_LAT, D))
    cond = jnp.concatenate([c_ctx[None, :], c, jnp.zeros((SUBLANES - N_GROUPS, D), F32)], axis=0)
    mod_all = _modulation(cond, w_mod, b_mod)
    cos128, sin128 = _axial_tables(GQA_HD)
    cos64, sin64 = _axial_tables(DIFF_HD)
    w_ffn_in_b = w_ffn_in.astype(BF16)
    w_ffn_out_b = w_ffn_out.astype(BF16)

    def layer_mod(l):
        return mod_all[l, :N_GROUPS].reshape(N_GROUPS, 6, D)

    def post(l, xs, os, w_o):
        out = _post(xs, os, layer_mod(l), g_norm2[l][None, :], w_o.astype(BF16),
                    w_ffn_in_b, w_ffn_out_b, l, split_out=l == DEPTH - 1)
        return out if l == DEPTH - 1 else out[0]

    def attend(q, k, v, *, hq, group, cache, diff, lambda_init):
        common = dict(group=group, diff=diff, lambda_init=lambda_init)
        sub = ATTN_ROWS // group
        o_ctx = _attention(q, k, v, seq_len=SEQ, n_seq=BATCH, row0=0, tq=SEQ, bk=SEQ,
                           sub=min(sub, SEQ), heads_per_step=hq, cache=None, seqs=CTX_SEQS,
                           **common)
        o_lat = _attention(q, k, v, seq_len=DEC_SEQ, n_seq=DEC_BATCH, row0=N_CTX, tq=ATTN_TILES * sub,
                           bk=ATTN_BK, sub=sub, heads_per_step=LAT_KV_HEADS * group, cache=cache,
                           **common)
        return o_ctx, o_lat

    l = 0
    w_in = mla_w_in[0]
    sin64i = _interleave(sin64)

    def twice(a):
        return _interleave(jnp.concatenate([a, a], axis=-1))

    n_lat = MLA_Q_RANK + MLA_KV_RANK
    w_in2 = jnp.concatenate([w_in[:, :n_lat], twice(w_in[:, n_lat:])], axis=1).astype(BF16)
    w_uq = mla_w_uq[0].reshape(MLA_Q_RANK, MLA_HEADS, MLA_NOPE + MLA_ROPE)
    wqn = w_uq[:, :, :MLA_NOPE].reshape(MLA_Q_RANK, MLA_HEADS * MLA_NOPE).astype(BF16)
    wqr = _interleave(w_uq[:, :, MLA_NOPE:].reshape(MLA_Q_RANK, MLA_HEADS * MLA_ROPE)).astype(BF16)
    wukv = mla_w_ukv[0].astype(BF16)
    gqk = mla_g_qk[0]
    gqn = gqk[0:1, :MLA_NOPE]
    gqr = twice(gqk[0:1, MLA_NOPE:])
    gkn = gqk[1:2, :MLA_NOPE]
    gkr = twice(gqk[1:2, MLA_NOPE:])
    q, k, v, ckv_new, kr_new = _mla_in(*x_in, layer_mod(l), g_norm1[l][None, :], w_in2,
                                       mla_g_cq[0][None, :], mla_g_ckv[0][None, :],
                                       gqn, gqr, gkn, gkr, wqn, wqr, wukv, cos64, sin64i)
    kc, vc = _mla_cache(cache_mla_ckv[:, 0].reshape(DEC_BATCH * PAST, MLA_KV_RANK),
                        twice(cache_mla_krope[:, 0].reshape(DEC_BATCH * PAST, MLA_ROPE)),
                        wukv, gkn)
    cache = (kc, vc,
             (LAT_KV_HEADS, PAST, MLA_DQ), lambda b, hb, qb: (hb, b, 0),
             (LAT_KV_HEADS, PAST, 2 * MLA_V), lambda b, hb, qb: (hb, b, 0))
    o = attend(q, k, v, hq=MLA_HEADS, group=1, cache=cache, diff=None, lambda_init=0.0)
    x = post(l, x_in, o, mla_w_o[0])
    new_mla_ckv = ckv_new.reshape(BATCH, 1, SEQ, MLA_KV_RANK)
    new_mla_krope = kr_new.reshape(BATCH, 1, SEQ, MLA_ROPE)

    l = 1
    lambda_init = 0.8 - 0.6 * math.exp(-0.3 * l)
    n_diff = DIFF_HEADS * 2 * DIFF_HD

    w_in = diff_w_in[0]
    w_in = jnp.concatenate([_interleave(w_in[:, :2 * n_diff]), w_in[:, 2 * n_diff:]], axis=1)
    q, k, v, k_new, v_new = _diff_in(x, layer_mod(l), g_norm1[l][None, :], w_in.astype(BF16),
                                     _interleave(jnp.tile(diff_g_qk[0], (1, 2))), cos64, sin64i)
    k_new = _interleave(k_new)
    cache = (_interleave(cache_diff_k[:, 0].reshape(DEC_BATCH, PAST, n_diff)),
             cache_diff_v[:, 0].reshape(DEC_BATCH, PAST, n_diff),
             (None, PAST, LAT_KV_HEADS * LANES), lambda b, hb, qb: (b, 0, hb),
             (None, PAST, LAT_KV_HEADS * LANES), lambda b, hb, qb: (b, 0, hb))
    o = attend(q, k, v, hq=2 * DIFF_HEADS, group=2, cache=cache,
               diff=(diff_lambda[0], diff_g_sub[0][None, :]), lambda_init=lambda_init)
    x = post(l, (x,), o, diff_w_o[0])
    new_diff_k = k_new.reshape(BATCH, 1, SEQ, DIFF_HEADS, 2, DIFF_HD)
    new_diff_v = v_new.reshape(BATCH, 1, SEQ, DIFF_HEADS, 2 * DIFF_HD)

    l = 2
    q, k, v, k_new, v_new = _gqa_in(x, layer_mod(l), g_norm1[l][None, :], gqa_w_in[0].astype(BF16),
                                    gqa_g_qk[0], cos128, sin128)
    n_kv = GQA_KV_HEADS * GQA_HD
    group = GQA_Q_HEADS // GQA_KV_HEADS
    cache = (cache_gqa_k[:, 0].reshape(DEC_BATCH, PAST, n_kv),
             cache_gqa_v[:, 0].reshape(DEC_BATCH, PAST, n_kv),
             (None, PAST, LAT_KV_HEADS * GQA_HD), lambda b, hb, qb: (b, 0, hb),
             (None, PAST, LAT_KV_HEADS * GQA_HD), lambda b, hb, qb: (b, 0, hb))
    o = attend(q, k, v, hq=GQA_Q_HEADS, group=group, cache=cache, diff=None, lambda_init=0.0)
    x = post(l, (x,), o, gqa_w_o[0])
    new_gqa_k = k_new.reshape(BATCH, 1, SEQ, GQA_KV_HEADS, GQA_HD)
    new_gqa_v = v_new.reshape(BATCH, 1, SEQ, GQA_KV_HEADS, GQA_HD)

    l = 3
    gate, xr = _lru_in(x, layer_mod(l), g_norm1[l][None, :], lru_w_in[0].astype(BF16))
    wg = lru_w_gate[0]
    wg = jnp.concatenate([wg[:, 0], wg[:, 1]], axis=-1).astype(BF16)
    h0 = jnp.concatenate([jnp.zeros((BATCH, 2, D), F32), state_lru_h[:, 0]], axis=0)
    h0 = h0.reshape(BATCH + DEC_BATCH, 2, LRU_BLOCKS, LRU_BLK)
    conv_b = lru_conv_b[0][None, :]
    hf, st_f = _lru_scan(xr, lru_conv_w[0], conv_b, wg[0], lru_b_gate[0, 0], lru_lambda[0, 0][None, :],
                         h0[:, 0], reverse=False)
    y, st_b = _lru_scan(xr, lru_conv_w[0], conv_b, wg[1], lru_b_gate[0, 1], lru_lambda[0, 1][None, :],
                        h0[:, 1], reverse=True, hf=hf, gate=gate)
    y_ctx, y_lat = post(l, (x,), (y,), lru_w_out[0])
    new_lru_h = jnp.stack([st_f[:BATCH].reshape(BATCH, D), st_b[:BATCH].reshape(BATCH, D)],
                          axis=1)[:, None]

    y_prompt = y_ctx.reshape(BATCH, SEQ, D)
    y_sample = y_lat.reshape(DEC_BATCH, DEC_SEQ, D)
    return (y_prompt, y_sample, new_mla_ckv, new_mla_krope, new_diff_k, new_diff_v,
            new_gqa_k, new_gqa_v, new_lru_h)
```

```python
import functools
import math

import jax
import jax.numpy as jnp
from jax import lax
from jax.experimental import pallas as pl
from jax.experimental.pallas import tpu as pltpu

F32 = jnp.float32
BF16 = jnp.bfloat16

D = 1024
BATCH = 32
SEQ = 256
DEPTH = 4
DEC_BATCH = 2
DEC_SEQ = 4096
PAST = 256
GRID_W = 64
EPS = 1e-6
ROPE_THETA = 10000.0
FFN_H = 2816
N_CTX = BATCH * SEQ
N_LAT = DEC_BATCH * DEC_SEQ
T = N_CTX + N_LAT
N_GROUPS = 1 + DEC_BATCH

MLA_HEADS = 8
MLA_NOPE = 128
MLA_ROPE = 64
MLA_V = 128
MLA_Q_RANK = 384
MLA_KV_RANK = 256
MLA_DQ = 256
DIFF_HD = 64
DIFF_HEADS = 8
GQA_HD = 128
GQA_Q_HEADS = 8
GQA_KV_HEADS = 2
LRU_BLOCKS = 8
LRU_BLK = 128
LRU_C = 8.0

LANES = 128
SUBLANES = 8
VMEM_LIMIT = 56 * 1024 * 1024

TM = 512
FFN_CHUNK = 256
LRU_TB = 256
ATTN_TILES = 4
CTX_SEQS = 4
LAT_KV_HEADS = 1
ATTN_BK = 1024
ATTN_ROWS = 1024
LOG2E = 1.4426950408889634


def _cparams(sem):
    return pltpu.CompilerParams(dimension_semantics=sem, vmem_limit_bytes=VMEM_LIMIT)


def _group_of_block(i, rows_per_block):
    n_ctx_blocks = N_CTX // rows_per_block
    per = DEC_SEQ // rows_per_block
    return jnp.maximum(i - (n_ctx_blocks - per), 0) // per


def _rms(x, gain):
    y = x * lax.rsqrt(jnp.mean(x * x, axis=-1, keepdims=True) + EPS)
    return y * gain


PAIR_HD = LANES // 2
PAIR_HALF = PAIR_HD // 2


def _first_of_pair(shape):
    return (lax.broadcasted_iota(jnp.int32, shape, 1) % PAIR_HD) < PAIR_HALF


def _interleave(a):
    shape = a.shape
    a = a.reshape(shape[:-1] + (shape[-1] // LANES, 2, 2, PAIR_HALF))
    return jnp.swapaxes(a, -3, -2).reshape(shape)


def _rms_pair(x, gain):
    first = _first_of_pair(x.shape)
    sq = x * x
    s0 = jnp.sum(jnp.where(first, sq, 0.0), axis=-1, keepdims=True)
    s_all = jnp.sum(sq, axis=-1, keepdims=True)
    ms = jnp.where(first, s0, s_all - s0) * (1.0 / PAIR_HD)
    return x * lax.rsqrt(ms + EPS) * gain


def _rope128(x, c, s):
    return x * c + pltpu.roll(x, 64, 1) * s


def _is_ctx_block():
    return pl.program_id(0) < N_CTX // TM


def _norm_mod(x, g_ref, mod_ref, shift_row):
    y = _rms(x, g_ref[...])
    shift = mod_ref[shift_row:shift_row + 1, :]
    scale = mod_ref[shift_row + 1:shift_row + 2, :]
    return (y * (1.0 + scale) + shift).astype(BF16)


def _mod_kernel(cond_ref, w_ref, b_ref, o_ref):
    s = jax.nn.silu(cond_ref[...]).astype(BF16)
    w = w_ref[...].astype(BF16)
    o_ref[...] = jnp.dot(s, w, preferred_element_type=F32) + b_ref[...]


def _modulation(cond, w_mod, b_mod):
    tn = 1536
    return pl.pallas_call(
        _mod_kernel,
        grid=(DEPTH, 6 * D // tn),
        in_specs=[
            pl.BlockSpec((SUBLANES, D), lambda l, j: (0, 0)),
            pl.BlockSpec((None, D, tn), lambda l, j: (l, 0, j)),
            pl.BlockSpec((None, 1, tn), lambda l, j: (l, 0, j)),
        ],
        out_specs=pl.BlockSpec((None, SUBLANES, tn), lambda l, j: (l, 0, j)),
        out_shape=jax.ShapeDtypeStruct((DEPTH, SUBLANES, 6 * D), F32),
        compiler_params=_cparams(("arbitrary", "arbitrary")),
        name="adaln_mod",
    )(cond, w_mod, b_mod.reshape(DEPTH, 1, 6 * D))


def _row_spec(width):
    return pl.BlockSpec((TM, width), lambda i: (i, 0))


def _ctx_row_spec(width):
    return pl.BlockSpec((TM, width), lambda i: (jnp.minimum(i, N_CTX // TM - 1), 0))


def _full_spec(shape):
    return pl.BlockSpec(shape, lambda i: (0,) * len(shape))


def _mod_spec():
    return pl.BlockSpec((None, 6, D), lambda i: (_group_of_block(i, TM), 0, 0))


N_BLOCKS = T // TM
N_CTX_BLOCKS = N_CTX // TM


def _proj_block(i):
    return jnp.minimum(i, N_BLOCKS - 1)


def _epi_block(i):
    return jnp.maximum(i - 1, 0)


def _proj_row_spec(width):
    return pl.BlockSpec((TM, width), lambda i: (_proj_block(i), 0))


def _proj_mod_spec():
    return pl.BlockSpec((None, 6, D), lambda i: (_group_of_block(_proj_block(i), TM), 0, 0))


def _epi_row_spec(width):
    return pl.BlockSpec((TM, width), lambda i: (_epi_block(i), 0))


def _epi_heads_spec(heads, width):
    return pl.BlockSpec((heads, TM, width), lambda i: (0, _epi_block(i), 0))


def _epi_ctx_row_spec(width):
    return pl.BlockSpec((TM, width), lambda i: (jnp.minimum(_epi_block(i), N_CTX_BLOCKS - 1), 0))


def _epi_is_ctx():
    i = pl.program_id(0)
    return jnp.logical_and(i >= 1, i <= N_CTX_BLOCKS)


def _rope_spec():
    per = DEC_SEQ // TM

    def index(i):
        j = _epi_block(i)
        return (jnp.where(j < N_CTX_BLOCKS, 0, 1 + (j - N_CTX_BLOCKS) % per), 0)

    return pl.BlockSpec((TM, LANES), index)


def _proj_lat_row_spec(width):
    n_lat = N_BLOCKS - N_CTX_BLOCKS
    return pl.BlockSpec((TM, width), lambda i: (jnp.clip(i - N_CTX_BLOCKS, 0, n_lat - 1), 0))


def _two_stage(step, bufs_a, bufs_b):
    i = pl.program_id(0)

    @pl.when(i == 0)
    def _():
        for buf in bufs_b:
            buf[...] = jnp.zeros(buf.shape, buf.dtype)

    @pl.when(i % 2 == 0)
    def _():
        step(bufs_b, bufs_a)

    @pl.when(i % 2 == 1)
    def _():
        step(bufs_a, bufs_b)


def _gqa_in_kernel(x_ref, mod_ref, g_ref, w_ref, gqk_ref, cos_ref, sin_ref,
                   q_ref, k_ref, v_ref, ks_ref, vs_ref, proj_a, proj_b):
    n_q = GQA_Q_HEADS * GQA_HD
    n_kv = GQA_KV_HEADS * GQA_HD

    def step(prev, cur):
        h = _norm_mod(x_ref[...], g_ref, mod_ref, 0)
        cur[...] = jnp.dot(h, w_ref[...], preferred_element_type=F32)
        c = cos_ref[...]
        s = sin_ref[...]
        scale = LOG2E * GQA_HD ** -0.5
        for hd in range(GQA_Q_HEADS):
            q = _rms(prev[:, hd * GQA_HD:(hd + 1) * GQA_HD], gqk_ref[0:1, :])
            q_ref[hd] = (_rope128(q, c, s) * scale).astype(BF16)
        ks = []
        for hd in range(GQA_KV_HEADS):
            lo = n_q + hd * GQA_HD
            ks.append(_rms(prev[:, lo:lo + GQA_HD], gqk_ref[1:2, :]))
            k_ref[hd] = _rope128(ks[-1], c, s).astype(BF16)
            lo = n_q + n_kv + hd * GQA_HD
            v_ref[hd] = _with_ones(prev[:, lo:lo + GQA_HD].astype(BF16))

        @pl.when(_epi_is_ctx())
        def _():
            for hd in range(GQA_KV_HEADS):
                ks_ref[:, hd * GQA_HD:(hd + 1) * GQA_HD] = ks[hd]
            vs_ref[...] = prev[:, n_q + n_kv:]

    _two_stage(lambda prev, cur: step(prev[0], cur[0]), (proj_a,), (proj_b,))


def _proj_scratch(width):
    return [pltpu.VMEM((TM, width), F32), pltpu.VMEM((TM, width), F32)]


def _gqa_in(x, mod, g1, w_in, g_qk, cos, sin):
    n_kv = GQA_KV_HEADS * GQA_HD
    return pl.pallas_call(
        _gqa_in_kernel,
        grid=(N_BLOCKS + 1,),
        in_specs=[_proj_row_spec(D), _proj_mod_spec(), _full_spec((1, D)), _full_spec(w_in.shape),
                  _full_spec((2, GQA_HD)), _rope_spec(), _rope_spec()],
        out_specs=[_epi_heads_spec(GQA_Q_HEADS, GQA_HD), _epi_heads_spec(GQA_KV_HEADS, GQA_HD),
                   _epi_heads_spec(GQA_KV_HEADS, 2 * GQA_HD), _epi_ctx_row_spec(n_kv),
                   _epi_ctx_row_spec(n_kv)],
        scratch_shapes=_proj_scratch(w_in.shape[1]),
        out_shape=[jax.ShapeDtypeStruct((GQA_Q_HEADS, T, GQA_HD), BF16),
                   jax.ShapeDtypeStruct((GQA_KV_HEADS, T, GQA_HD), BF16),
                   jax.ShapeDtypeStruct((GQA_KV_HEADS, T, 2 * GQA_HD), BF16),
                   jax.ShapeDtypeStruct((N_CTX, n_kv), F32),
                   jax.ShapeDtypeStruct((N_CTX, n_kv), F32)],
        compiler_params=_cparams(("arbitrary",)),
        name="gqa_in",
    )(x, mod, g1, w_in, g_qk, cos, sin)


def _diff_in_kernel(x_ref, mod_ref, g_ref, w_ref, gqk_ref, cos_ref, sin_ref,
                    q_ref, k_ref, v_ref, ks_ref, vs_ref, proj_a, proj_b):
    n = DIFF_HEADS * 2 * DIFF_HD

    def step(prev, cur):
        h = _norm_mod(x_ref[...], g_ref, mod_ref, 0)
        cur[...] = jnp.dot(h, w_ref[...], preferred_element_type=F32)
        c = cos_ref[...]
        s = sin_ref[...]
        scale = LOG2E * DIFF_HD ** -0.5
        ks = []
        for hd in range(DIFF_HEADS):
            sl = slice(hd * LANES, (hd + 1) * LANES)
            q = _rope128(_rms_pair(prev[:, sl], gqk_ref[0:1, :]), c, s) * scale
            q_ref[hd] = q.astype(BF16)
            ks.append(_rms_pair(prev[:, n + hd * LANES:n + (hd + 1) * LANES], gqk_ref[1:2, :]))
            k_ref[hd] = _rope128(ks[-1], c, s).astype(BF16)
            v_ref[hd] = _with_ones(prev[:, 2 * n + hd * LANES:2 * n + (hd + 1) * LANES].astype(BF16))

        @pl.when(_epi_is_ctx())
        def _():
            for hd in range(DIFF_HEADS):
                ks_ref[:, hd * LANES:(hd + 1) * LANES] = ks[hd]
            vs_ref[...] = prev[:, 2 * n:]

    _two_stage(lambda prev, cur: step(prev[0], cur[0]), (proj_a,), (proj_b,))


def _diff_in(x, mod, g1, w_in, g_qk2, cos, sin):
    n = DIFF_HEADS * 2 * DIFF_HD
    return pl.pallas_call(
        _diff_in_kernel,
        grid=(N_BLOCKS + 1,),
        in_specs=[_proj_row_spec(D), _proj_mod_spec(), _full_spec((1, D)), _full_spec(w_in.shape),
                  _full_spec((2, LANES)), _rope_spec(), _rope_spec()],
        out_specs=[_epi_heads_spec(DIFF_HEADS, LANES), _epi_heads_spec(DIFF_HEADS, LANES),
                   _epi_heads_spec(DIFF_HEADS, 2 * LANES), _epi_ctx_row_spec(n), _epi_ctx_row_spec(n)],
        scratch_shapes=_proj_scratch(w_in.shape[1]),
        out_shape=[jax.ShapeDtypeStruct((DIFF_HEADS, T, LANES), BF16),
                   jax.ShapeDtypeStruct((DIFF_HEADS, T, LANES), BF16),
                   jax.ShapeDtypeStruct((DIFF_HEADS, T, 2 * LANES), BF16),
                   jax.ShapeDtypeStruct((N_CTX, n), F32),
                   jax.ShapeDtypeStruct((N_CTX, n), F32)],
        compiler_params=_cparams(("arbitrary",)),
        name="diff_in",
    )(x, mod, g1, w_in, g_qk2, cos, sin)


def _mla_keys_values(ckv_bf16, kr2, w_ukv_ref, gk_ref, k_ref, v_ref):
    kv = jnp.dot(ckv_bf16, w_ukv_ref[...], preferred_element_type=F32)
    low = _first_of_pair(kr2.shape)
    kr_low = jnp.where(low, kr2, 0.0).astype(BF16)
    kr_high = jnp.where(low, 0.0, kr2).astype(BF16)
    width = MLA_NOPE + MLA_V
    for hd in range(MLA_HEADS):
        k_nope = _rms(kv[:, hd * width:hd * width + MLA_NOPE], gk_ref[...])
        k_ref[hd, :, 0:MLA_NOPE] = k_nope.astype(BF16)
        k_ref[hd, :, MLA_NOPE:MLA_DQ] = kr_low if hd % 2 == 0 else kr_high
        v_ref[hd] = _with_ones(kv[:, hd * width + MLA_NOPE:(hd + 1) * width].astype(BF16))


def _mla_in_kernel(xc_ref, xl_ref, mod_ref, g_ref, w_ref, gcq_ref, gckv_ref, gqn_ref, gqr_ref,
                   gkn_ref, gkr_ref, wqn_ref, wqr_ref, wukv_ref, cos_ref, sin_ref,
                   q_ref, k_ref, v_ref, ckv_ref, kr_ref, proj_a, proj_b):
    def step(prev, cur):
        x = jnp.where(_is_ctx_block(), xc_ref[...], xl_ref[...])
        h = _norm_mod(x, g_ref, mod_ref, 0)
        cur[...] = jnp.dot(h, w_ref[...], preferred_element_type=F32)
        c = cos_ref[...]
        s = sin_ref[...]
        cq = _rms(prev[:, 0:MLA_Q_RANK], gcq_ref[...]).astype(BF16)
        ckv = _rms(prev[:, MLA_Q_RANK:MLA_Q_RANK + MLA_KV_RANK], gckv_ref[...])
        kr2 = _rms_pair(prev[:, MLA_Q_RANK + MLA_KV_RANK:], gkr_ref[...])
        _mla_keys_values(ckv.astype(BF16), _rope128(kr2, c, s), wukv_ref, gkn_ref, k_ref, v_ref)

        scale = LOG2E * (MLA_NOPE + MLA_ROPE) ** -0.5
        qn = jnp.dot(cq, wqn_ref[...], preferred_element_type=F32)
        qr = jnp.dot(cq, wqr_ref[...], preferred_element_type=F32)
        low = _first_of_pair((TM, LANES))
        for pair in range(MLA_HEADS // 2):
            r = _rms_pair(qr[:, pair * LANES:(pair + 1) * LANES], gqr_ref[...])
            r = _rope128(r, c, s) * scale
            q_ref[2 * pair, :, MLA_NOPE:MLA_DQ] = jnp.where(low, r, 0.0).astype(BF16)
            q_ref[2 * pair + 1, :, MLA_NOPE:MLA_DQ] = jnp.where(low, 0.0, r).astype(BF16)
        for hd in range(MLA_HEADS):
            q = _rms(qn[:, hd * MLA_NOPE:(hd + 1) * MLA_NOPE], gqn_ref[...]) * scale
            q_ref[hd, :, 0:MLA_NOPE] = q.astype(BF16)

        @pl.when(_epi_is_ctx())
        def _():
            ckv_ref[...] = ckv
            kr_ref[...] = jnp.concatenate([kr2[:, 0:MLA_ROPE // 2],
                                           kr2[:, MLA_ROPE:MLA_ROPE + MLA_ROPE // 2]], axis=1)

    _two_stage(lambda prev, cur: step(prev[0], cur[0]), (proj_a,), (proj_b,))


def _mla_in(x_ctx, x_lat, mod, g1, w_in, g_cq, g_ckv, gqn, gqr, gkn, gkr, wqn, wqr, wukv, cos, sin):
    n_lat_blocks = N_BLOCKS - N_CTX_BLOCKS
    return pl.pallas_call(
        _mla_in_kernel,
        grid=(N_BLOCKS + 1,),
        in_specs=[_ctx_row_spec(D),
                  pl.BlockSpec((TM, D), lambda i: (jnp.clip(i - N_CTX_BLOCKS, 0, n_lat_blocks - 1), 0)),
                  _proj_mod_spec(), _full_spec((1, D)), _full_spec(w_in.shape),
                  _full_spec((1, MLA_Q_RANK)), _full_spec((1, MLA_KV_RANK)),
                  _full_spec((1, LANES)), _full_spec((1, LANES)), _full_spec((1, LANES)),
                  _full_spec((1, LANES)), _full_spec(wqn.shape), _full_spec(wqr.shape),
                  _full_spec(wukv.shape), _rope_spec(), _rope_spec()],
        out_specs=[_epi_heads_spec(MLA_HEADS, MLA_DQ), _epi_heads_spec(MLA_HEADS, MLA_DQ),
                   _epi_heads_spec(MLA_HEADS, 2 * MLA_V), _epi_ctx_row_spec(MLA_KV_RANK),
                   _epi_ctx_row_spec(MLA_ROPE)],
        scratch_shapes=_proj_scratch(w_in.shape[1]),
        out_shape=[jax.ShapeDtypeStruct((MLA_HEADS, T, MLA_DQ), BF16),
                   jax.ShapeDtypeStruct((MLA_HEADS, T, MLA_DQ), BF16),
                   jax.ShapeDtypeStruct((MLA_HEADS, T, 2 * MLA_V), BF16),
                   jax.ShapeDtypeStruct((N_CTX, MLA_KV_RANK), F32),
                   jax.ShapeDtypeStruct((N_CTX, MLA_ROPE), F32)],
        compiler_params=_cparams(("arbitrary",)),
        name="mla_in",
    )(x_ctx, x_lat, mod, g1, w_in, g_cq, g_ckv, gqn, gqr, gkn, gkr, wqn, wqr, wukv, cos, sin)


def _mla_cache_kernel(ckv_ref, kr_ref, wukv_ref, gkn_ref, k_ref, v_ref):
    _mla_keys_values(ckv_ref[...].astype(BF16), kr_ref[...], wukv_ref, gkn_ref, k_ref, v_ref)


def _mla_cache(ckv, kr, wukv, gkn):
    rows = ckv.shape[0]
    return pl.pallas_call(
        _mla_cache_kernel,
        grid=(1,),
        in_specs=[_full_spec(ckv.shape), _full_spec(kr.shape), _full_spec(wukv.shape),
                  _full_spec((1, LANES))],
        out_specs=[_full_spec((MLA_HEADS, rows, MLA_DQ)), _full_spec((MLA_HEADS, rows, 2 * MLA_V))],
        out_shape=[jax.ShapeDtypeStruct((MLA_HEADS, rows, MLA_DQ), BF16),
                   jax.ShapeDtypeStruct((MLA_HEADS, rows, 2 * MLA_V), BF16)],
        compiler_params=_cparams(("arbitrary",)),
        name="mla_cache_kv",
    )(ckv, kr, wukv, gkn)


def _lru_in_kernel(x_ref, mod_ref, g_ref, w_ref, gate_ref, xr_ref):
    h = _norm_mod(x_ref[...], g_ref, mod_ref, 0)
    proj = jnp.dot(h, w_ref[...], preferred_element_type=F32)
    gate_ref[...] = proj[:, 0:D]
    xr_ref[...] = proj[:, D:2 * D]


def _lru_in(x, mod, g1, w_in):
    return pl.pallas_call(
        _lru_in_kernel,
        grid=(T // TM,),
        in_specs=[_row_spec(D), _mod_spec(), _full_spec((1, D)), _full_spec(w_in.shape)],
        out_specs=[_row_spec(D), _row_spec(D)],
        out_shape=[jax.ShapeDtypeStruct((T, D), F32), jax.ShapeDtypeStruct((T, D), F32)],
        compiler_params=_cparams(("arbitrary",)),
        name="lru_in",
    )(x, mod, g1, w_in)


def _with_ones(v):
    return jnp.concatenate([v, jnp.ones_like(v)], axis=-1)


def _softmax_pv(q, key_blocks, dv):
    nt = (((1,), (1,)), ((), ()))
    m = acc = None
    for k, v1 in key_blocks:
        s = lax.dot_general(q, k, nt, preferred_element_type=F32)
        mb = jnp.max(s, axis=-1, keepdims=True)
        if m is None:
            m = mb
            acc = jnp.dot(jnp.exp2(s - m).astype(BF16), v1, preferred_element_type=F32)
        else:
            m_new = jnp.maximum(m, mb)
            alpha = jnp.exp2(m - m_new)
            acc = alpha * acc + jnp.dot(jnp.exp2(s - m_new).astype(BF16), v1,
                                        preferred_element_type=F32)
            m = m_new
    return acc[:, 0:dv] / acc[:, dv:2 * dv]


def _attn_kernel(*refs, heads, group, seqs, sub, bk, has_cache, diff, dv, lambda_init):
    it = iter(refs)
    q_ref, kn_ref, vn_ref = next(it), next(it), next(it)
    kc_ref = vc_ref = lam_ref = gsub_ref = None
    if has_cache:
        kc_ref, vc_ref = next(it), next(it)
    if diff:
        lam_ref, gsub_ref = next(it), next(it)
    o_ref = next(it)

    tq = q_ref.shape[1] // seqs
    seq_len = kn_ref.shape[1] // seqs
    stacked = min(group, heads)
    def cached_block(kv):
        if not has_cache:
            return []
        if len(kc_ref.shape) == 2:
            wk = kc_ref.shape[1] // n_kv
            wv = vc_ref.shape[1] // n_kv
            kc, vc = kc_ref[:, kv * wk:(kv + 1) * wk], vc_ref[:, kv * wv:(kv + 1) * wv]
        else:
            kc, vc = kc_ref[kv], vc_ref[kv]
        vc = vc.astype(BF16)
        return [(kc.astype(BF16), vc if vc.shape[-1] == 2 * dv else _with_ones(vc))]

    n_kv = max(heads // group, 1)
    if diff:
        lam = lam_ref[...]
        lam_full = (jnp.exp(jnp.sum(lam[0:1] * lam[1:2], axis=-1, keepdims=True))
                    - jnp.exp(jnp.sum(lam[2:3] * lam[3:4], axis=-1, keepdims=True)) + lambda_init)

    for sq, kv in [(sq, kv) for sq in range(seqs) for kv in range(n_kv)]:
        def body(i, carry, sq=sq, kv=kv):
            rows = pl.ds(pl.multiple_of(sq * tq + i * sub, sub), sub)
            if diff:
                q2 = q_ref[kv, rows, :]
                first = _first_of_pair(q2.shape)
                zero = jnp.zeros_like(q2)
                q = jnp.concatenate([jnp.where(first, q2, zero), jnp.where(first, zero, q2)],
                                    axis=0)
            else:
                q = jnp.concatenate([q_ref[kv * stacked + g, rows, :] for g in range(stacked)],
                                    axis=0)
            keys = [slice(sq * seq_len + j * bk, sq * seq_len + (j + 1) * bk)
                    for j in range(seq_len // bk)]
            blocks = cached_block(kv) + [(kn_ref[kv, ks, :], vn_ref[kv, ks, :]) for ks in keys]
            o = _softmax_pv(q, blocks, dv)
            if diff:
                od = o[0:sub] - lam_full * o[sub:2 * sub]
                od = _rms(od, gsub_ref[...]) * (1.0 - lambda_init)
                o_ref[rows, kv * dv:(kv + 1) * dv] = od.astype(BF16)
            else:
                for g in range(stacked):
                    hd = kv * stacked + g
                    o_ref[rows, hd * dv:(hd + 1) * dv] = o[g * sub:(g + 1) * sub].astype(BF16)
            return carry
        lax.fori_loop(0, tq // sub, body, 0, unroll=True)


def _attention(q, kn, vn, *, seq_len, n_seq, row0, tq, sub, bk, heads_per_step, group, cache=None,
               diff=None, lambda_init=0.0, seqs=1):
    q_per_row = 2 if diff is not None else 1
    hq, _, dq = q.shape
    hq *= q_per_row
    hkv, _, dv1 = vn.shape
    dv = dv1 // 2
    kv_per_step = max(heads_per_step // group, 1)
    n_hblk = hq // heads_per_step
    n_qblk = seq_len // tq
    assert seqs == 1 or (n_qblk == 1 and cache is None and row0 % (seqs * seq_len) == 0)
    n_seq //= seqs
    seq0 = row0 // (seqs * seq_len)
    qblk0 = row0 // (seqs * tq)

    def q_map(b, hb, qb):
        return (hb, qblk0 + b * n_qblk + qb, 0)

    def kv_map(b, hb, qb):
        return ((hb * heads_per_step) // (group * kv_per_step), seq0 + b, 0)

    in_specs = [pl.BlockSpec((heads_per_step // q_per_row, seqs * tq, dq), q_map),
                pl.BlockSpec((kv_per_step, seqs * seq_len, dq), kv_map),
                pl.BlockSpec((kv_per_step, seqs * seq_len, dv1), kv_map)]
    args = [q, kn, vn]
    if cache is not None:
        kc, vc, kc_block, kc_map, vc_block, vc_map = cache
        in_specs += [pl.BlockSpec(kc_block, kc_map), pl.BlockSpec(vc_block, vc_map)]
        args += [kc, vc]
    out_heads = heads_per_step
    if diff is not None:
        lam, gsub = diff
        in_specs += [pl.BlockSpec(lam.shape, lambda b, hb, qb: (0, 0)),
                     pl.BlockSpec(gsub.shape, lambda b, hb, qb: (0, 0))]
        args += [lam, gsub]
        out_heads = heads_per_step // 2
    n_out = (hq // 2 if diff is not None else hq) * dv
    kernel = functools.partial(_attn_kernel, heads=heads_per_step, group=group, seqs=seqs, sub=sub,
                               bk=bk, has_cache=cache is not None, diff=diff is not None, dv=dv,
                               lambda_init=lambda_init)
    return pl.pallas_call(
        kernel,
        grid=(n_seq, n_hblk, n_qblk),
        in_specs=in_specs,
        out_specs=pl.BlockSpec((seqs * tq, out_heads * dv),
                               lambda b, hb, qb: (b * n_qblk + qb, hb)),
        out_shape=jax.ShapeDtypeStruct((n_seq * seqs * seq_len, n_out), BF16),
        compiler_params=_cparams(("arbitrary", "arbitrary", "arbitrary")),
        name="attention",
    )(*args)


def _lru_seq_pos(i):
    n_ctx_blocks = N_CTX // LRU_TB
    per_ctx = SEQ // LRU_TB
    per_lat = DEC_SEQ // LRU_TB
    pos = jnp.where(i < n_ctx_blocks, i % per_ctx, (i - n_ctx_blocks) % per_lat)
    length = jnp.where(i < n_ctx_blocks, per_ctx, per_lat)
    return pos == 0, pos == length - 1


def _lru_seq_of_block(i):
    n_ctx_blocks = N_CTX // LRU_TB
    return jnp.where(i < n_ctx_blocks, i // (SEQ // LRU_TB),
                     BATCH + (i - n_ctx_blocks) // (DEC_SEQ // LRU_TB))


LRU_PITCH = LRU_TB + SUBLANES


def _lru_scan_kernel(x_ref, prev_ref, next_ref, cw_ref, cb_ref, wg_ref, bg_ref, lam_ref, h0_ref,
                     *rest, reverse, combine):
    if combine:
        hf_ref, gate_ref, y_ref, st_ref, a_s, u_s, h_s, carry = rest
    else:
        y_ref, st_ref, a_s, u_s, h_s, carry = rest
    j = pl.program_id(0)
    i = (T // LRU_TB - 1 - j) if reverse else j
    first, last = _lru_seq_pos(i)
    starts = last if reverse else first
    ends = first if reverse else last

    x = x_ref[...]
    before = jnp.where(first, 0.0, prev_ref[SUBLANES - 1:SUBLANES, :])
    after = jnp.where(last, 0.0, next_ref[0:2, :])
    row = lax.broadcasted_iota(jnp.int32, (SUBLANES, D), 0)

    def shifted(k, fix):
        y = pltpu.roll(x, (-k) % LRU_TB, 0)
        if k < 0:
            return jnp.concatenate([fix(y[0:SUBLANES]), y[SUBLANES:]], axis=0)
        return jnp.concatenate([y[0:LRU_TB - SUBLANES], fix(y[LRU_TB - SUBLANES:])], axis=0)

    taps = [
        shifted(-1, lambda t: jnp.where(row == 0, before, t)),
        x,
        shifted(1, lambda t: jnp.where(row == SUBLANES - 1, after[0:1], t)),
        shifted(2, lambda t: jnp.where(row == SUBLANES - 2, after[0:1],
                                       jnp.where(row == SUBLANES - 1, after[1:2], t))),
    ]
    xr = taps[0] * cw_ref[0:1, :]
    for t in range(1, 4):
        xr = xr + taps[t] * cw_ref[t:t + 1, :]
    xr = xr + cb_ref[...]

    xr_b = xr.astype(BF16)
    lam = lam_ref[...]
    neg = -lam
    softplus = jnp.maximum(neg, 0.0) + jnp.log1p(jnp.exp(-jnp.abs(neg)))
    for n in range(LRU_BLOCKS):
        sl = slice(n * LRU_BLK, (n + 1) * LRU_BLK)
        g = jnp.dot(xr_b[:, sl], wg_ref[n], preferred_element_type=F32)
        r = jax.nn.sigmoid(g[:, 0:LRU_BLK] + bg_ref[0:1, sl])
        gi = jax.nn.sigmoid(g[:, LRU_BLK:] + bg_ref[1:2, sl])
        log_a = -LRU_C * r * softplus[:, sl]
        th = jnp.tanh(log_a)
        one_minus_a2 = -2.0 * th / (1.0 - th)
        rows_n = slice(n * LRU_PITCH, n * LRU_PITCH + LRU_TB)
        a_s[rows_n, :] = jnp.exp(log_a)
        root = jnp.where(one_minus_a2 > 0.0, one_minus_a2 * lax.rsqrt(one_minus_a2), 0.0)
        u_s[rows_n, :] = root * (gi * xr[:, sl])

    @pl.when(starts)
    def _():
        carry[...] = h0_ref[...]

    def body(t, h):
        tt = (LRU_TB - 1 - t) if reverse else t
        rows = pl.ds(tt, LRU_BLOCKS, stride=LRU_PITCH)
        h = a_s[rows, :] * h + u_s[rows, :]
        h_s[rows, :] = h
        return h

    h_end = lax.fori_loop(0, LRU_TB, body, carry[...], unroll=8)
    carry[...] = h_end

    for n in range(LRU_BLOCKS):
        sl = slice(n * LRU_BLK, (n + 1) * LRU_BLK)
        hs = h_s[n * LRU_PITCH:n * LRU_PITCH + LRU_TB, :]
        if combine:
            y_ref[:, sl] = ((hf_ref[:, sl] + hs) * jax.nn.gelu(gate_ref[:, sl])).astype(BF16)
        else:
            y_ref[:, sl] = hs

    @pl.when(ends)
    def _():
        st_ref[...] = h_end


def _lru_scan(xr, conv_w, conv_b, w_gate, b_gate, lam, h0, *, reverse, hf=None, gate=None):
    nb = T // LRU_TB
    hb = LRU_TB // SUBLANES
    n_halo = T // SUBLANES

    def blk(j):
        return (nb - 1 - j) if reverse else j

    in_specs = [
        pl.BlockSpec((LRU_TB, D), lambda j: (blk(j), 0)),
        pl.BlockSpec((SUBLANES, D), lambda j: (jnp.maximum(blk(j) * hb - 1, 0), 0)),
        pl.BlockSpec((SUBLANES, D), lambda j: (jnp.minimum((blk(j) + 1) * hb, n_halo - 1), 0)),
        pl.BlockSpec((4, D), lambda j: (0, 0)),
        pl.BlockSpec((1, D), lambda j: (0, 0)),
        pl.BlockSpec((LRU_BLOCKS, LRU_BLK, 2 * LRU_BLK), lambda j: (0, 0, 0)),
        pl.BlockSpec((2, D), lambda j: (0, 0)),
        pl.BlockSpec((1, D), lambda j: (0, 0)),
        pl.BlockSpec((None, LRU_BLOCKS, LRU_BLK), lambda j: (_lru_seq_of_block(blk(j)), 0, 0)),
    ]
    args = [xr, xr, xr, conv_w, conv_b, w_gate, b_gate, lam, h0]
    combine = hf is not None
    if combine:
        in_specs += [pl.BlockSpec((LRU_TB, D), lambda j: (blk(j), 0)),
                     pl.BlockSpec((LRU_TB, D), lambda j: (blk(j), 0))]
        args += [hf, gate]
    n_seq = BATCH + DEC_BATCH
    return pl.pallas_call(
        functools.partial(_lru_scan_kernel, reverse=reverse, combine=combine),
        grid=(nb,),
        in_specs=in_specs,
        out_specs=[pl.BlockSpec((LRU_TB, D), lambda j: (blk(j), 0)),
                   pl.BlockSpec((None, LRU_BLOCKS, LRU_BLK),
                                lambda j: (_lru_seq_of_block(blk(j)), 0, 0))],
        out_shape=[jax.ShapeDtypeStruct((T, D), BF16 if combine else F32),
                   jax.ShapeDtypeStruct((n_seq, LRU_BLOCKS, LRU_BLK), F32)],
        scratch_shapes=[pltpu.VMEM((LRU_BLOCKS * LRU_PITCH, LRU_BLK), F32),
                        pltpu.VMEM((LRU_BLOCKS * LRU_PITCH, LRU_BLK), F32),
                        pltpu.VMEM((LRU_BLOCKS * LRU_PITCH, LRU_BLK), F32),
                        pltpu.VMEM((LRU_BLOCKS, LRU_BLK), F32)],
        compiler_params=_cparams(("arbitrary",)),
        name="lru_scan_bwd" if reverse else "lru_scan_fwd",
    )(*args)


def _post_kernel(*refs, split_x, split_o, split_out):
    it = iter(refs)

    def rows(split):
        if split:
            return jnp.where(_is_ctx_block(), next(it)[...], next(it)[...])
        return next(it)[...]

    x = rows(split_x)
    o = rows(split_o)
    mod_ref, g2_ref, wo_ref, win_ref, wout_ref = (next(it) for _ in range(5))
    y_refs = list(it)
    x1 = x + mod_ref[2:3, :] * jnp.dot(o, wo_ref[...], preferred_element_type=F32)
    h = _rms(x1, g2_ref[...])
    h = (h * (1.0 + mod_ref[4:5, :]) + mod_ref[3:4, :]).astype(BF16)
    acc = jnp.zeros((TM, D), F32)
    for c in range(FFN_H // FFN_CHUNK):
        lo = c * FFN_CHUNK
        g = jnp.dot(h, win_ref[:, lo:lo + FFN_CHUNK], preferred_element_type=F32)
        u = jnp.dot(h, win_ref[:, FFN_H + lo:FFN_H + lo + FFN_CHUNK], preferred_element_type=F32)
        a = (jax.nn.silu(g) * u).astype(BF16)
        acc = acc + jnp.dot(a, wout_ref[lo:lo + FFN_CHUNK, :], preferred_element_type=F32)
    y = x1 + mod_ref[5:6, :] * acc
    if split_out:
        @pl.when(_is_ctx_block())
        def _():
            y_refs[0][...] = y

        @pl.when(jnp.logical_not(_is_ctx_block()))
        def _():
            y_refs[1][...] = y
    else:
        y_refs[0][...] = y


def _layer_weight_spec(shape, layer):
    return pl.BlockSpec((None,) + shape[1:], lambda i: (layer,) + (0,) * (len(shape) - 1),
                        pipeline_mode=pl.Buffered(1))


def _post(xs, os, mod, g2, w_o, w_in, w_out, layer, *, split_out):
    def row_specs(arrays):
        return [_ctx_row_spec(D), _proj_lat_row_spec(D)] if len(arrays) == 2 else [_row_spec(D)]

    if split_out:
        out_specs = [_ctx_row_spec(D), _proj_lat_row_spec(D)]
        out_shape = [jax.ShapeDtypeStruct((N_CTX, D), F32), jax.ShapeDtypeStruct((N_LAT, D), F32)]
    else:
        out_specs = [_row_spec(D)]
        out_shape = [jax.ShapeDtypeStruct((T, D), F32)]
    return pl.pallas_call(
        functools.partial(_post_kernel, split_x=len(xs) == 2, split_o=len(os) == 2,
                          split_out=split_out),
        grid=(N_BLOCKS,),
        in_specs=row_specs(xs) + row_specs(os) + [
            _mod_spec(), _full_spec((1, D)), _layer_weight_spec((1,) + w_o.shape, 0),
            _layer_weight_spec(w_in.shape, layer), _layer_weight_spec(w_out.shape, layer)],
        out_specs=out_specs,
        out_shape=out_shape,
        compiler_params=_cparams(("arbitrary",)),
        name="post_mixer_ffn",
    )(*xs, *os, mod, g2, w_o[None], w_in, w_out)


def _axial_tables(rot_dim):
    row = jnp.repeat(jnp.arange(DEC_SEQ // GRID_W), GRID_W).astype(F32)
    col = jnp.tile(jnp.arange(GRID_W), DEC_SEQ // GRID_W).astype(F32)
    n_freq = rot_dim // 4
    inv = ROPE_THETA ** (-jnp.arange(n_freq, dtype=F32) / n_freq)
    ang = jnp.concatenate([row[:, None] * inv, col[:, None] * inv], axis=-1)
    cos, sin = jnp.cos(ang), jnp.sin(ang)
    reps = LANES // rot_dim
    cos_t = jnp.tile(jnp.concatenate([cos, cos], axis=-1), (1, reps))
    sin_t = jnp.tile(jnp.concatenate([-sin, sin], axis=-1), (1, reps))
    cos_t = jnp.concatenate([jnp.ones((TM, LANES), F32), cos_t], axis=0)
    sin_t = jnp.concatenate([jnp.zeros((TM, LANES), F32), sin_t], axis=0)
    return cos_t, sin_t


def kernel(x_prompt, x_sample, cache_mla_ckv, cache_mla_krope, cache_diff_k, cache_diff_v, cache_gqa_k, cache_gqa_v, state_lru_h, c, c_ctx, w_mod, b_mod, g_norm1, g_norm2, w_ffn_in, w_ffn_out, mla_w_in, mla_g_cq, mla_g_ckv, mla_w_uq, mla_w_ukv, mla_g_qk, mla_w_o, diff_w_in, diff_g_qk, diff_lambda, diff_g_sub, diff_w_o, gqa_w_in, gqa_g_qk, gqa_w_o, lru_w_in, lru_conv_w, lru_conv_b, lru_w_gate, lru_b_gate, lru_lambda, lru_w_out):
    x_in = (x_prompt.reshape(N_CTX, D), x_sample.reshape(N_LAT, D))
    cond = jnp.concatenate([c_ctx[None, :], c, jnp.zeros((SUBLANES - N_GROUPS, D), F32)], axis=0)
    mod_all = _modulation(cond, w_mod, b_mod)
    cos128, sin128 = _axial_tables(GQA_HD)
    cos64, sin64 = _axial_tables(DIFF_HD)
    w_ffn_in_b = w_ffn_in.astype(BF16)
    w_ffn_out_b = w_ffn_out.astype(BF16)

    def layer_mod(l):
        return mod_all[l, :N_GROUPS].reshape(N_GROUPS, 6, D)

    def post(l, xs, os, w_o):
        out = _post(xs, os, layer_mod(l), g_norm2[l][None, :], w_o.astype(BF16),
                    w_ffn_in_b, w_ffn_out_b, l, split_out=l == DEPTH - 1)
        return out if l == DEPTH - 1 else out[0]

    def attend(q, k, v, *, hq, group, cache, diff, lambda_init):
        common = dict(group=group, diff=diff, lambda_init=lambda_init)
        sub = ATTN_ROWS // group
        o_ctx = _attention(q, k, v, seq_len=SEQ, n_seq=BATCH, row0=0, tq=SEQ, bk=SEQ,
                           sub=min(sub, SEQ), heads_per_step=hq, cache=None, seqs=CTX_SEQS,
                           **common)
        o_lat = _attention(q, k, v, seq_len=DEC_SEQ, n_seq=DEC_BATCH, row0=N_CTX, tq=ATTN_TILES * sub,
                           bk=ATTN_BK, sub=sub, heads_per_step=LAT_KV_HEADS * group, cache=cache,
                           **common)
        return o_ctx, o_lat

    l = 0
    w_in = mla_w_in[0]
    sin64i = _interleave(sin64)

    def twice(a):
        return _interleave(jnp.concatenate([a, a], axis=-1))

    n_lat = MLA_Q_RANK + MLA_KV_RANK
    w_in2 = jnp.concatenate([w_in[:, :n_lat], twice(w_in[:, n_lat:])], axis=1).astype(BF16)
    w_uq = mla_w_uq[0].reshape(MLA_Q_RANK, MLA_HEADS, MLA_NOPE + MLA_ROPE)
    wqn = w_uq[:, :, :MLA_NOPE].reshape(MLA_Q_RANK, MLA_HEADS * MLA_NOPE).astype(BF16)
    wqr = _interleave(w_uq[:, :, MLA_NOPE:].reshape(MLA_Q_RANK, MLA_HEADS * MLA_ROPE)).astype(BF16)
    wukv = mla_w_ukv[0].astype(BF16)
    gqk = mla_g_qk[0]
    gqn = gqk[0:1, :MLA_NOPE]
    gqr = twice(gqk[0:1, MLA_NOPE:])
    gkn = gqk[1:2, :MLA_NOPE]
    gkr = twice(gqk[1:2, MLA_NOPE:])
    q, k, v, ckv_new, kr_new = _mla_in(*x_in, layer_mod(l), g_norm1[l][None, :], w_in2,
                                       mla_g_cq[0][None, :], mla_g_ckv[0][None, :],
                                       gqn, gqr, gkn, gkr, wqn, wqr, wukv, cos64, sin64i)
    kc, vc = _mla_cache(cache_mla_ckv[:, 0].reshape(DEC_BATCH * PAST, MLA_KV_RANK),
                        twice(cache_mla_krope[:, 0].reshape(DEC_BATCH * PAST, MLA_ROPE)),
                        wukv, gkn)
    cache = (kc, vc,
             (LAT_KV_HEADS, PAST, MLA_DQ), lambda b, hb, qb: (hb, b, 0),
             (LAT_KV_HEADS, PAST, 2 * MLA_V), lambda b, hb, qb: (hb, b, 0))
    o = attend(q, k, v, hq=MLA_HEADS, group=1, cache=cache, diff=None, lambda_init=0.0)
    x = post(l, x_in, o, mla_w_o[0])
    new_mla_ckv = ckv_new.reshape(BATCH, 1, SEQ, MLA_KV_RANK)
    new_mla_krope = kr_new.reshape(BATCH, 1, SEQ, MLA_ROPE)

    l = 1
    lambda_init = 0.8 - 0.6 * math.exp(-0.3 * l)
    n_diff = DIFF_HEADS * 2 * DIFF_HD

    w_in = diff_w_in[0]
    w_in = jnp.concatenate([_interleave(w_in[:, :2 * n_diff]), w_in[:, 2 * n_diff:]], axis=1)
    q, k, v, k_new, v_new = _diff_in(x, layer_mod(l), g_norm1[l][None, :], w_in.astype(BF16),
                                     _interleave(jnp.tile(diff_g_qk[0], (1, 2))), cos64, sin64i)
    k_new = _interleave(k_new)
    cache = (_interleave(cache_diff_k[:, 0].reshape(DEC_BATCH, PAST, n_diff)),
             cache_diff_v[:, 0].reshape(DEC_BATCH, PAST, n_diff),
             (None, PAST, LAT_KV_HEADS * LANES), lambda b, hb, qb: (b, 0, hb),
             (None, PAST, LAT_KV_HEADS * LANES), lambda b, hb, qb: (b, 0, hb))
    o = attend(q, k, v, hq=2 * DIFF_HEADS, group=2, cache=cache,
               diff=(diff_lambda[0], diff_g_sub[0][None, :]), lambda_init=lambda_init)
    x = post(l, (x,), o, diff_w_o[0])
    new_diff_k = k_new.reshape(BATCH, 1, SEQ, DIFF_HEADS, 2, DIFF_HD)
    new_diff_v = v_new.reshape(BATCH, 1, SEQ, DIFF_HEADS, 2 * DIFF_HD)

    l = 2
    q, k, v, k_new, v_new = _gqa_in(x, layer_mod(l), g_norm1[l][None, :], gqa_w_in[0].astype(BF16),
                                    gqa_g_qk[0], cos128, sin128)
    n_kv = GQA_KV_HEADS * GQA_HD
    group = GQA_Q_HEADS // GQA_KV_HEADS
    cache = (cache_gqa_k[:, 0].reshape(DEC_BATCH, PAST, n_kv),
             cache_gqa_v[:, 0].reshape(DEC_BATCH, PAST, n_kv),
             (None, PAST, LAT_KV_HEADS * GQA_HD), lambda b, hb, qb: (b, 0, hb),
             (None, PAST, LAT_KV_HEADS * GQA_HD), lambda b, hb, qb: (b, 0, hb))
    o = attend(q, k, v, hq=GQA_Q_HEADS, group=group, cache=cache, diff=None, lambda_init=0.0)
    x = post(l, (x,), o, gqa_w_o[0])
    new_gqa_k = k_new.reshape(BATCH, 1, SEQ, GQA_KV_HEADS, GQA_HD)
    new_gqa_v = v_new.reshape(BATCH, 1, SEQ, GQA_KV_HEADS, GQA_HD)

    l = 3
    gate, xr = _lru_in(x, layer_mod(l), g_norm1[l][None, :], lru_w_in[0].astype(BF16))
    wg = lru_w_gate[0]
    wg = jnp.concatenate([wg[:, 0], wg[:, 1]], axis=-1).astype(BF16)
    h0 = jnp.concatenate([jnp.zeros((BATCH, 2, D), F32), state_lru_h[:, 0]], axis=0)
    h0 = h0.reshape(BATCH + DEC_BATCH, 2, LRU_BLOCKS, LRU_BLK)
    conv_b = lru_conv_b[0][None, :]
    hf, st_f = _lru_scan(xr, lru_conv_w[0], conv_b, wg[0], lru_b_gate[0, 0], lru_lambda[0, 0][None, :],
                         h0[:, 0], reverse=False)
    y, st_b = _lru_scan(xr, lru_conv_w[0], conv_b, wg[1], lru_b_gate[0, 1], lru_lambda[0, 1][None, :],
                        h0[:, 1], reverse=True, hf=hf, gate=gate)
    y_ctx, y_lat = post(l, (x,), (y,), lru_w_out[0])
    new_lru_h = jnp.stack([st_f[:BATCH].reshape(BATCH, D), st_b[:BATCH].reshape(BATCH, D)],
                          axis=1)[:, None]

    y_prompt = y_ctx.reshape(BATCH, SEQ, D)
    y_sample = y_lat.reshape(DEC_BATCH, DEC_SEQ, D)
    return (y_prompt, y_sample, new_mla_ckv, new_mla_krope, new_diff_k, new_diff_v,
            new_gqa_k, new_gqa_v, new_lru_h)
```

```python
import functools
import math

import jax
import jax.numpy as jnp
from jax import lax
from jax.experimental import pallas as pl
from jax.experimental.pallas import tpu as pltpu

F32 = jnp.float32
BF16 = jnp.bfloat16

D = 1024
BATCH = 32
SEQ = 256
DEPTH = 4
DEC_BATCH = 2
DEC_SEQ = 4096
PAST = 256
GRID_W = 64
EPS = 1e-6
ROPE_THETA = 10000.0
FFN_H = 2816
N_CTX = BATCH * SEQ
N_LAT = DEC_BATCH * DEC_SEQ
T = N_CTX + N_LAT
N_GROUPS = 1 + DEC_BATCH

MLA_HEADS = 8
MLA_NOPE = 128
MLA_ROPE = 64
MLA_V = 128
MLA_Q_RANK = 384
MLA_KV_RANK = 256
MLA_DQ = 256
DIFF_HD = 64
DIFF_HEADS = 8
GQA_HD = 128
GQA_Q_HEADS = 8
GQA_KV_HEADS = 2
LRU_BLOCKS = 8
LRU_BLK = 128
LRU_C = 8.0

LANES = 128
SUBLANES = 8
VMEM_LIMIT = 56 * 1024 * 1024

TM = 512
FFN_CHUNK = 256
LRU_TB = 256
ATTN_TILES = 4
CTX_SEQS = 4
LAT_KV_HEADS = 1
ATTN_BK = 2048
ATTN_ROWS = 1024
LOG2E = 1.4426950408889634


def _cparams(sem):
    return pltpu.CompilerParams(dimension_semantics=sem, vmem_limit_bytes=VMEM_LIMIT)


def _group_of_block(i, rows_per_block):
    n_ctx_blocks = N_CTX // rows_per_block
    per = DEC_SEQ // rows_per_block
    return jnp.maximum(i - (n_ctx_blocks - per), 0) // per


def _rms(x, gain):
    y = x * lax.rsqrt(jnp.mean(x * x, axis=-1, keepdims=True) + EPS)
    return y * gain


PAIR_HD = LANES // 2
PAIR_HALF = PAIR_HD // 2


def _first_of_pair(shape):
    return (lax.broadcasted_iota(jnp.int32, shape, 1) % PAIR_HD) < PAIR_HALF


def _interleave(a):
    shape = a.shape
    a = a.reshape(shape[:-1] + (shape[-1] // LANES, 2, 2, PAIR_HALF))
    return jnp.swapaxes(a, -3, -2).reshape(shape)


def _rms_pair(x, gain):
    first = _first_of_pair(x.shape)
    sq = x * x
    s0 = jnp.sum(jnp.where(first, sq, 0.0), axis=-1, keepdims=True)
    s_all = jnp.sum(sq, axis=-1, keepdims=True)
    ms = jnp.where(first, s0, s_all - s0) * (1.0 / PAIR_HD)
    return x * lax.rsqrt(ms + EPS) * gain


def _rope128(x, c, s):
    return x * c + pltpu.roll(x, 64, 1) * s


def _is_ctx_block():
    return pl.program_id(0) < N_CTX // TM


def _norm_mod(x, g_ref, mod_ref, shift_row):
    y = _rms(x, g_ref[...])
    shift = mod_ref[shift_row:shift_row + 1, :]
    scale = mod_ref[shift_row + 1:shift_row + 2, :]
    return (y * (1.0 + scale) + shift).astype(BF16)


def _mod_kernel(cond_ref, w_ref, b_ref, o_ref):
    s = jax.nn.silu(cond_ref[...]).astype(BF16)
    w = w_ref[...].astype(BF16)
    o_ref[...] = jnp.dot(s, w, preferred_element_type=F32) + b_ref[...]


def _modulation(cond, w_mod, b_mod):
    tn = 1536
    return pl.pallas_call(
        _mod_kernel,
        grid=(DEPTH, 6 * D // tn),
        in_specs=[
            pl.BlockSpec((SUBLANES, D), lambda l, j: (0, 0)),
            pl.BlockSpec((None, D, tn), lambda l, j: (l, 0, j)),
            pl.BlockSpec((None, 1, tn), lambda l, j: (l, 0, j)),
        ],
        out_specs=pl.BlockSpec((None, SUBLANES, tn), lambda l, j: (l, 0, j)),
        out_shape=jax.ShapeDtypeStruct((DEPTH, SUBLANES, 6 * D), F32),
        compiler_params=_cparams(("arbitrary", "arbitrary")),
        name="adaln_mod",
    )(cond, w_mod, b_mod.reshape(DEPTH, 1, 6 * D))


def _row_spec(width):
    return pl.BlockSpec((TM, width), lambda i: (i, 0))


def _ctx_row_spec(width):
    return pl.BlockSpec((TM, width), lambda i: (jnp.minimum(i, N_CTX // TM - 1), 0))


def _full_spec(shape):
    return pl.BlockSpec(shape, lambda i: (0,) * len(shape))


def _mod_spec():
    return pl.BlockSpec((None, 6, D), lambda i: (_group_of_block(i, TM), 0, 0))


N_BLOCKS = T // TM
N_CTX_BLOCKS = N_CTX // TM


def _proj_block(i):
    return jnp.minimum(i, N_BLOCKS - 1)


def _epi_block(i):
    return jnp.maximum(i - 1, 0)


def _proj_row_spec(width):
    return pl.BlockSpec((TM, width), lambda i: (_proj_block(i), 0))


def _proj_mod_spec():
    return pl.BlockSpec((None, 6, D), lambda i: (_group_of_block(_proj_block(i), TM), 0, 0))


def _epi_row_spec(width):
    return pl.BlockSpec((TM, width), lambda i: (_epi_block(i), 0))


def _epi_heads_spec(heads, width):
    return pl.BlockSpec((heads, TM, width), lambda i: (0, _epi_block(i), 0))


def _epi_ctx_row_spec(width):
    return pl.BlockSpec((TM, width), lambda i: (jnp.minimum(_epi_block(i), N_CTX_BLOCKS - 1), 0))


def _epi_is_ctx():
    i = pl.program_id(0)
    return jnp.logical_and(i >= 1, i <= N_CTX_BLOCKS)


def _rope_spec():
    per = DEC_SEQ // TM

    def index(i):
        j = _epi_block(i)
        return (jnp.where(j < N_CTX_BLOCKS, 0, 1 + (j - N_CTX_BLOCKS) % per), 0)

    return pl.BlockSpec((TM, LANES), index)


def _proj_lat_row_spec(width):
    n_lat = N_BLOCKS - N_CTX_BLOCKS
    return pl.BlockSpec((TM, width), lambda i: (jnp.clip(i - N_CTX_BLOCKS, 0, n_lat - 1), 0))


def _two_stage(step, bufs_a, bufs_b):
    i = pl.program_id(0)

    @pl.when(i == 0)
    def _():
        for buf in bufs_b:
            buf[...] = jnp.zeros(buf.shape, buf.dtype)

    @pl.when(i % 2 == 0)
    def _():
        step(bufs_b, bufs_a)

    @pl.when(i % 2 == 1)
    def _():
        step(bufs_a, bufs_b)


def _gqa_in_kernel(x_ref, mod_ref, g_ref, w_ref, gqk_ref, cos_ref, sin_ref,
                   q_ref, k_ref, v_ref, ks_ref, vs_ref, proj_a, proj_b):
    n_q = GQA_Q_HEADS * GQA_HD
    n_kv = GQA_KV_HEADS * GQA_HD

    def step(prev, cur):
        h = _norm_mod(x_ref[...], g_ref, mod_ref, 0)
        cur[...] = jnp.dot(h, w_ref[...], preferred_element_type=F32)
        c = cos_ref[...]
        s = sin_ref[...]
        scale = LOG2E * GQA_HD ** -0.5
        for hd in range(GQA_Q_HEADS):
            q = _rms(prev[:, hd * GQA_HD:(hd + 1) * GQA_HD], gqk_ref[0:1, :])
            q_ref[hd] = (_rope128(q, c, s) * scale).astype(BF16)
        ks = []
        for hd in range(GQA_KV_HEADS):
            lo = n_q + hd * GQA_HD
            ks.append(_rms(prev[:, lo:lo + GQA_HD], gqk_ref[1:2, :]))
            k_ref[hd] = _rope128(ks[-1], c, s).astype(BF16)
            lo = n_q + n_kv + hd * GQA_HD
            v_ref[hd] = _with_ones(prev[:, lo:lo + GQA_HD].astype(BF16))

        @pl.when(_epi_is_ctx())
        def _():
            for hd in range(GQA_KV_HEADS):
                ks_ref[:, hd * GQA_HD:(hd + 1) * GQA_HD] = ks[hd]
            vs_ref[...] = prev[:, n_q + n_kv:]

    _two_stage(lambda prev, cur: step(prev[0], cur[0]), (proj_a,), (proj_b,))


def _proj_scratch(width):
    return [pltpu.VMEM((TM, width), F32), pltpu.VMEM((TM, width), F32)]


def _gqa_in(x, mod, g1, w_in, g_qk, cos, sin):
    n_kv = GQA_KV_HEADS * GQA_HD
    return pl.pallas_call(
        _gqa_in_kernel,
        grid=(N_BLOCKS + 1,),
        in_specs=[_proj_row_spec(D), _proj_mod_spec(), _full_spec((1, D)), _full_spec(w_in.shape),
                  _full_spec((2, GQA_HD)), _rope_spec(), _rope_spec()],
        out_specs=[_epi_heads_spec(GQA_Q_HEADS, GQA_HD), _epi_heads_spec(GQA_KV_HEADS, GQA_HD),
                   _epi_heads_spec(GQA_KV_HEADS, 2 * GQA_HD), _epi_ctx_row_spec(n_kv),
                   _epi_ctx_row_spec(n_kv)],
        scratch_shapes=_proj_scratch(w_in.shape[1]),
        out_shape=[jax.ShapeDtypeStruct((GQA_Q_HEADS, T, GQA_HD), BF16),
                   jax.ShapeDtypeStruct((GQA_KV_HEADS, T, GQA_HD), BF16),
                   jax.ShapeDtypeStruct((GQA_KV_HEADS, T, 2 * GQA_HD), BF16),
                   jax.ShapeDtypeStruct((N_CTX, n_kv), F32),
                   jax.ShapeDtypeStruct((N_CTX, n_kv), F32)],
        compiler_params=_cparams(("arbitrary",)),
        name="gqa_in",
    )(x, mod, g1, w_in, g_qk, cos, sin)


def _diff_in_kernel(x_ref, mod_ref, g_ref, w_ref, gqk_ref, cos_ref, sin_ref,
                    q_ref, k_ref, v_ref, ks_ref, vs_ref, proj_a, proj_b):
    n = DIFF_HEADS * 2 * DIFF_HD

    def step(prev, cur):
        h = _norm_mod(x_ref[...], g_ref, mod_ref, 0)
        cur[...] = jnp.dot(h, w_ref[...], preferred_element_type=F32)
        c = cos_ref[...]
        s = sin_ref[...]
        scale = LOG2E * DIFF_HD ** -0.5
        ks = []
        for hd in range(DIFF_HEADS):
            sl = slice(hd * LANES, (hd + 1) * LANES)
            q = _rope128(_rms_pair(prev[:, sl], gqk_ref[0:1, :]), c, s) * scale
            q_ref[hd] = q.astype(BF16)
            ks.append(_rms_pair(prev[:, n + hd * LANES:n + (hd + 1) * LANES], gqk_ref[1:2, :]))
            k_ref[hd] = _rope128(ks[-1], c, s).astype(BF16)
            v_ref[hd] = _with_ones(prev[:, 2 * n + hd * LANES:2 * n + (hd + 1) * LANES].astype(BF16))

        @pl.when(_epi_is_ctx())
        def _():
            for hd in range(DIFF_HEADS):
                ks_ref[:, hd * LANES:(hd + 1) * LANES] = ks[hd]
            vs_ref[...] = prev[:, 2 * n:]

    _two_stage(lambda prev, cur: step(prev[0], cur[0]), (proj_a,), (proj_b,))


def _diff_in(x, mod, g1, w_in, g_qk2, cos, sin):
    n = DIFF_HEADS * 2 * DIFF_HD
    return pl.pallas_call(
        _diff_in_kernel,
        grid=(N_BLOCKS + 1,),
        in_specs=[_proj_row_spec(D), _proj_mod_spec(), _full_spec((1, D)), _full_spec(w_in.shape),
                  _full_spec((2, LANES)), _rope_spec(), _rope_spec()],
        out_specs=[_epi_heads_spec(DIFF_HEADS, LANES), _epi_heads_spec(DIFF_HEADS, LANES),
                   _epi_heads_spec(DIFF_HEADS, 2 * LANES), _epi_ctx_row_spec(n), _epi_ctx_row_spec(n)],
        scratch_shapes=_proj_scratch(w_in.shape[1]),
        out_shape=[jax.ShapeDtypeStruct((DIFF_HEADS, T, LANES), BF16),
                   jax.ShapeDtypeStruct((DIFF_HEADS, T, LANES), BF16),
                   jax.ShapeDtypeStruct((DIFF_HEADS, T, 2 * LANES), BF16),
                   jax.ShapeDtypeStruct((N_CTX, n), F32),
                   jax.ShapeDtypeStruct((N_CTX, n), F32)],
        compiler_params=_cparams(("arbitrary",)),
        name="diff_in",
    )(x, mod, g1, w_in, g_qk2, cos, sin)


def _mla_keys_values(ckv_bf16, kr2, w_ukv_ref, gk_ref, k_ref, v_ref):
    kv = jnp.dot(ckv_bf16, w_ukv_ref[...], preferred_element_type=F32)
    low = _first_of_pair(kr2.shape)
    kr_low = jnp.where(low, kr2, 0.0).astype(BF16)
    kr_high = jnp.where(low, 0.0, kr2).astype(BF16)
    width = MLA_NOPE + MLA_V
    for hd in range(MLA_HEADS):
        k_nope = _rms(kv[:, hd * width:hd * width + MLA_NOPE], gk_ref[...])
        k_ref[hd, :, 0:MLA_NOPE] = k_nope.astype(BF16)
        k_ref[hd, :, MLA_NOPE:MLA_DQ] = kr_low if hd % 2 == 0 else kr_high
        v_ref[hd] = _with_ones(kv[:, hd * width + MLA_NOPE:(hd + 1) * width].astype(BF16))


def _mla_in_kernel(xc_ref, xl_ref, mod_ref, g_ref, w_ref, gcq_ref, gckv_ref, gqn_ref, gqr_ref,
                   gkn_ref, gkr_ref, wqn_ref, wqr_ref, wukv_ref, cos_ref, sin_ref,
                   q_ref, k_ref, v_ref, ckv_ref, kr_ref, proj_a, proj_b):
    def step(prev, cur):
        x = jnp.where(_is_ctx_block(), xc_ref[...], xl_ref[...])
        h = _norm_mod(x, g_ref, mod_ref, 0)
        cur[...] = jnp.dot(h, w_ref[...], preferred_element_type=F32)
        c = cos_ref[...]
        s = sin_ref[...]
        cq = _rms(prev[:, 0:MLA_Q_RANK], gcq_ref[...]).astype(BF16)
        ckv = _rms(prev[:, MLA_Q_RANK:MLA_Q_RANK + MLA_KV_RANK], gckv_ref[...])
        kr2 = _rms_pair(prev[:, MLA_Q_RANK + MLA_KV_RANK:], gkr_ref[...])
        _mla_keys_values(ckv.astype(BF16), _rope128(kr2, c, s), wukv_ref, gkn_ref, k_ref, v_ref)

        scale = LOG2E * (MLA_NOPE + MLA_ROPE) ** -0.5
        qn = jnp.dot(cq, wqn_ref[...], preferred_element_type=F32)
        qr = jnp.dot(cq, wqr_ref[...], preferred_element_type=F32)
        low = _first_of_pair((TM, LANES))
        for pair in range(MLA_HEADS // 2):
            r = _rms_pair(qr[:, pair * LANES:(pair + 1) * LANES], gqr_ref[...])
            r = _rope128(r, c, s) * scale
            q_ref[2 * pair, :, MLA_NOPE:MLA_DQ] = jnp.where(low, r, 0.0).astype(BF16)
            q_ref[2 * pair + 1, :, MLA_NOPE:MLA_DQ] = jnp.where(low, 0.0, r).astype(BF16)
        for hd in range(MLA_HEADS):
            q = _rms(qn[:, hd * MLA_NOPE:(hd + 1) * MLA_NOPE], gqn_ref[...]) * scale
            q_ref[hd, :, 0:MLA_NOPE] = q.astype(BF16)

        @pl.when(_epi_is_ctx())
        def _():
            ckv_ref[...] = ckv
            kr_ref[...] = jnp.concatenate([kr2[:, 0:MLA_ROPE // 2],
                                           kr2[:, MLA_ROPE:MLA_ROPE + MLA_ROPE // 2]], axis=1)

    _two_stage(lambda prev, cur: step(prev[0], cur[0]), (proj_a,), (proj_b,))


def _mla_in(x_ctx, x_lat, mod, g1, w_in, g_cq, g_ckv, gqn, gqr, gkn, gkr, wqn, wqr, wukv, cos, sin):
    n_lat_blocks = N_BLOCKS - N_CTX_BLOCKS
    return pl.pallas_call(
        _mla_in_kernel,
        grid=(N_BLOCKS + 1,),
        in_specs=[_ctx_row_spec(D),
                  pl.BlockSpec((TM, D), lambda i: (jnp.clip(i - N_CTX_BLOCKS, 0, n_lat_blocks - 1), 0)),
                  _proj_mod_spec(), _full_spec((1, D)), _full_spec(w_in.shape),
                  _full_spec((1, MLA_Q_RANK)), _full_spec((1, MLA_KV_RANK)),
                  _full_spec((1, LANES)), _full_spec((1, LANES)), _full_spec((1, LANES)),
                  _full_spec((1, LANES)), _full_spec(wqn.shape), _full_spec(wqr.shape),
                  _full_spec(wukv.shape), _rope_spec(), _rope_spec()],
        out_specs=[_epi_heads_spec(MLA_HEADS, MLA_DQ), _epi_heads_spec(MLA_HEADS, MLA_DQ),
                   _epi_heads_spec(MLA_HEADS, 2 * MLA_V), _epi_ctx_row_spec(MLA_KV_RANK),
                   _epi_ctx_row_spec(MLA_ROPE)],
        scratch_shapes=_proj_scratch(w_in.shape[1]),
        out_shape=[jax.ShapeDtypeStruct((MLA_HEADS, T, MLA_DQ), BF16),
                   jax.ShapeDtypeStruct((MLA_HEADS, T, MLA_DQ), BF16),
                   jax.ShapeDtypeStruct((MLA_HEADS, T, 2 * MLA_V), BF16),
                   jax.ShapeDtypeStruct((N_CTX, MLA_KV_RANK), F32),
                   jax.ShapeDtypeStruct((N_CTX, MLA_ROPE), F32)],
        compiler_params=_cparams(("arbitrary",)),
        name="mla_in",
    )(x_ctx, x_lat, mod, g1, w_in, g_cq, g_ckv, gqn, gqr, gkn, gkr, wqn, wqr, wukv, cos, sin)


def _mla_cache_kernel(ckv_ref, kr_ref, wukv_ref, gkn_ref, k_ref, v_ref):
    _mla_keys_values(ckv_ref[...].astype(BF16), kr_ref[...], wukv_ref, gkn_ref, k_ref, v_ref)


def _mla_cache(ckv, kr, wukv, gkn):
    rows = ckv.shape[0]
    return pl.pallas_call(
        _mla_cache_kernel,
        grid=(1,),
        in_specs=[_full_spec(ckv.shape), _full_spec(kr.shape), _full_spec(wukv.shape),
                  _full_spec((1, LANES))],
        out_specs=[_full_spec((MLA_HEADS, rows, MLA_DQ)), _full_spec((MLA_HEADS, rows, 2 * MLA_V))],
        out_shape=[jax.ShapeDtypeStruct((MLA_HEADS, rows, MLA_DQ), BF16),
                   jax.ShapeDtypeStruct((MLA_HEADS, rows, 2 * MLA_V), BF16)],
        compiler_params=_cparams(("arbitrary",)),
        name="mla_cache_kv",
    )(ckv, kr, wukv, gkn)


def _lru_in_kernel(x_ref, mod_ref, g_ref, w_ref, gate_ref, xr_ref):
    h = _norm_mod(x_ref[...], g_ref, mod_ref, 0)
    proj = jnp.dot(h, w_ref[...], preferred_element_type=F32)
    gate_ref[...] = proj[:, 0:D]
    xr_ref[...] = proj[:, D:2 * D]


def _lru_in(x, mod, g1, w_in):
    return pl.pallas_call(
        _lru_in_kernel,
        grid=(T // TM,),
        in_specs=[_row_spec(D), _mod_spec(), _full_spec((1, D)), _full_spec(w_in.shape)],
        out_specs=[_row_spec(D), _row_spec(D)],
        out_shape=[jax.ShapeDtypeStruct((T, D), F32), jax.ShapeDtypeStruct((T, D), F32)],
        compiler_params=_cparams(("arbitrary",)),
        name="lru_in",
    )(x, mod, g1, w_in)


def _with_ones(v):
    return jnp.concatenate([v, jnp.ones_like(v)], axis=-1)


def _softmax_pv(q, key_blocks, dv):
    nt = (((1,), (1,)), ((), ()))
    m = acc = None
    for k, v1 in key_blocks:
        s = lax.dot_general(q, k, nt, preferred_element_type=F32)
        mb = jnp.max(s, axis=-1, keepdims=True)
        if m is None:
            m = mb
            acc = jnp.dot(jnp.exp2(s - m).astype(BF16), v1, preferred_element_type=F32)
        else:
            m_new = jnp.maximum(m, mb)
            alpha = jnp.exp2(m - m_new)
            acc = alpha * acc + jnp.dot(jnp.exp2(s - m_new).astype(BF16), v1,
                                        preferred_element_type=F32)
            m = m_new
    return acc[:, 0:dv] / acc[:, dv:2 * dv]


def _attn_kernel(*refs, heads, group, seqs, sub, bk, has_cache, diff, dv, lambda_init):
    it = iter(refs)
    q_ref, kn_ref, vn_ref = next(it), next(it), next(it)
    kc_ref = vc_ref = lam_ref = gsub_ref = None
    if has_cache:
        kc_ref, vc_ref = next(it), next(it)
    if diff:
        lam_ref, gsub_ref = next(it), next(it)
    o_ref = next(it)

    tq = q_ref.shape[1] // seqs
    seq_len = kn_ref.shape[1] // seqs
    stacked = min(group, heads)
    def cached_block(kv):
        if not has_cache:
            return []
        if len(kc_ref.shape) == 2:
            wk = kc_ref.shape[1] // n_kv
            wv = vc_ref.shape[1] // n_kv
            kc, vc = kc_ref[:, kv * wk:(kv + 1) * wk], vc_ref[:, kv * wv:(kv + 1) * wv]
        else:
            kc, vc = kc_ref[kv], vc_ref[kv]
        vc = vc.astype(BF16)
        return [(kc.astype(BF16), vc if vc.shape[-1] == 2 * dv else _with_ones(vc))]

    n_kv = max(heads // group, 1)
    if diff:
        lam = lam_ref[...]
        lam_full = (jnp.exp(jnp.sum(lam[0:1] * lam[1:2], axis=-1, keepdims=True))
                    - jnp.exp(jnp.sum(lam[2:3] * lam[3:4], axis=-1, keepdims=True)) + lambda_init)

    for sq, kv in [(sq, kv) for sq in range(seqs) for kv in range(n_kv)]:
        def body(i, carry, sq=sq, kv=kv):
            rows = pl.ds(pl.multiple_of(sq * tq + i * sub, sub), sub)
            if diff:
                q2 = q_ref[kv, rows, :]
                first = _first_of_pair(q2.shape)
                zero = jnp.zeros_like(q2)
                q = jnp.concatenate([jnp.where(first, q2, zero), jnp.where(first, zero, q2)],
                                    axis=0)
            else:
                q = jnp.concatenate([q_ref[kv * stacked + g, rows, :] for g in range(stacked)],
                                    axis=0)
            keys = [slice(sq * seq_len + j * bk, sq * seq_len + (j + 1) * bk)
                    for j in range(seq_len // bk)]
            blocks = cached_block(kv) + [(kn_ref[kv, ks, :], vn_ref[kv, ks, :]) for ks in keys]
            o = _softmax_pv(q, blocks, dv)
            if diff:
                od = o[0:sub] - lam_full * o[sub:2 * sub]
                od = _rms(od, gsub_ref[...]) * (1.0 - lambda_init)
                o_ref[rows, kv * dv:(kv + 1) * dv] = od.astype(BF16)
            else:
                for g in range(stacked):
                    hd = kv * stacked + g
                    o_ref[rows, hd * dv:(hd + 1) * dv] = o[g * sub:(g + 1) * sub].astype(BF16)
            return carry
        lax.fori_loop(0, tq // sub, body, 0, unroll=True)


def _attention(q, kn, vn, *, seq_len, n_seq, row0, tq, sub, bk, heads_per_step, group, cache=None,
               diff=None, lambda_init=0.0, seqs=1):
    q_per_row = 2 if diff is not None else 1
    hq, _, dq = q.shape
    hq *= q_per_row
    hkv, _, dv1 = vn.shape
    dv = dv1 // 2
    kv_per_step = max(heads_per_step // group, 1)
    n_hblk = hq // heads_per_step
    n_qblk = seq_len // tq
    assert seqs == 1 or (n_qblk == 1 and cache is None and row0 % (seqs * seq_len) == 0)
    n_seq //= seqs
    seq0 = row0 // (seqs * seq_len)
    qblk0 = row0 // (seqs * tq)

    def q_map(b, hb, qb):
        return (hb, qblk0 + b * n_qblk + qb, 0)

    def kv_map(b, hb, qb):
        return ((hb * heads_per_step) // (group * kv_per_step), seq0 + b, 0)

    in_specs = [pl.BlockSpec((heads_per_step // q_per_row, seqs * tq, dq), q_map),
                pl.BlockSpec((kv_per_step, seqs * seq_len, dq), kv_map),
                pl.BlockSpec((kv_per_step, seqs * seq_len, dv1), kv_map)]
    args = [q, kn, vn]
    if cache is not None:
        kc, vc, kc_block, kc_map, vc_block, vc_map = cache
        in_specs += [pl.BlockSpec(kc_block, kc_map), pl.BlockSpec(vc_block, vc_map)]
        args += [kc, vc]
    out_heads = heads_per_step
    if diff is not None:
        lam, gsub = diff
        in_specs += [pl.BlockSpec(lam.shape, lambda b, hb, qb: (0, 0)),
                     pl.BlockSpec(gsub.shape, lambda b, hb, qb: (0, 0))]
        args += [lam, gsub]
        out_heads = heads_per_step // 2
    n_out = (hq // 2 if diff is not None else hq) * dv
    kernel = functools.partial(_attn_kernel, heads=heads_per_step, group=group, seqs=seqs, sub=sub,
                               bk=bk, has_cache=cache is not None, diff=diff is not None, dv=dv,
                               lambda_init=lambda_init)
    return pl.pallas_call(
        kernel,
        grid=(n_seq, n_hblk, n_qblk),
        in_specs=in_specs,
        out_specs=pl.BlockSpec((seqs * tq, out_heads * dv),
                               lambda b, hb, qb: (b * n_qblk + qb, hb)),
        out_shape=jax.ShapeDtypeStruct((n_seq * seqs * seq_len, n_out), BF16),
        compiler_params=_cparams(("arbitrary", "arbitrary", "arbitrary")),
        name="attention",
    )(*args)


def _lru_seq_pos(i):
    n_ctx_blocks = N_CTX // LRU_TB
    per_ctx = SEQ // LRU_TB
    per_lat = DEC_SEQ // LRU_TB
    pos = jnp.where(i < n_ctx_blocks, i % per_ctx, (i - n_ctx_blocks) % per_lat)
    length = jnp.where(i < n_ctx_blocks, per_ctx, per_lat)
    return pos == 0, pos == length - 1


def _lru_seq_of_block(i):
    n_ctx_blocks = N_CTX // LRU_TB
    return jnp.where(i < n_ctx_blocks, i // (SEQ // LRU_TB),
                     BATCH + (i - n_ctx_blocks) // (DEC_SEQ // LRU_TB))


LRU_PITCH = LRU_TB + SUBLANES


def _lru_scan_kernel(x_ref, prev_ref, next_ref, cw_ref, cb_ref, wg_ref, bg_ref, lam_ref, h0_ref,
                     *rest, reverse, combine):
    if combine:
        hf_ref, gate_ref, y_ref, st_ref, a_s, u_s, h_s, carry = rest
    else:
        y_ref, st_ref, a_s, u_s, h_s, carry = rest
    j = pl.program_id(0)
    i = (T // LRU_TB - 1 - j) if reverse else j
    first, last = _lru_seq_pos(i)
    starts = last if reverse else first
    ends = first if reverse else last

    x = x_ref[...]
    before = jnp.where(first, 0.0, prev_ref[SUBLANES - 1:SUBLANES, :])
    after = jnp.where(last, 0.0, next_ref[0:2, :])
    row = lax.broadcasted_iota(jnp.int32, (SUBLANES, D), 0)

    def shifted(k, fix):
        y = pltpu.roll(x, (-k) % LRU_TB, 0)
        if k < 0:
            return jnp.concatenate([fix(y[0:SUBLANES]), y[SUBLANES:]], axis=0)
        return jnp.concatenate([y[0:LRU_TB - SUBLANES], fix(y[LRU_TB - SUBLANES:])], axis=0)

    taps = [
        shifted(-1, lambda t: jnp.where(row == 0, before, t)),
        x,
        shifted(1, lambda t: jnp.where(row == SUBLANES - 1, after[0:1], t)),
        shifted(2, lambda t: jnp.where(row == SUBLANES - 2, after[0:1],
                                       jnp.where(row == SUBLANES - 1, after[1:2], t))),
    ]
    xr = taps[0] * cw_ref[0:1, :]
    for t in range(1, 4):
        xr = xr + taps[t] * cw_ref[t:t + 1, :]
    xr = xr + cb_ref[...]

    xr_b = xr.astype(BF16)
    lam = lam_ref[...]
    neg = -lam
    softplus = jnp.maximum(neg, 0.0) + jnp.log1p(jnp.exp(-jnp.abs(neg)))
    for n in range(LRU_BLOCKS):
        sl = slice(n * LRU_BLK, (n + 1) * LRU_BLK)
        g = jnp.dot(xr_b[:, sl], wg_ref[n], preferred_element_type=F32)
        r = jax.nn.sigmoid(g[:, 0:LRU_BLK] + bg_ref[0:1, sl])
        gi = jax.nn.sigmoid(g[:, LRU_BLK:] + bg_ref[1:2, sl])
        log_a = -LRU_C * r * softplus[:, sl]
        th = jnp.tanh(log_a)
        one_minus_a2 = -2.0 * th / (1.0 - th)
        rows_n = slice(n * LRU_PITCH, n * LRU_PITCH + LRU_TB)
        a_s[rows_n, :] = jnp.exp(log_a)
        root = jnp.where(one_minus_a2 > 0.0, one_minus_a2 * lax.rsqrt(one_minus_a2), 0.0)
        u_s[rows_n, :] = root * (gi * xr[:, sl])

    @pl.when(starts)
    def _():
        carry[...] = h0_ref[...]

    def body(t, h):
        tt = (LRU_TB - 1 - t) if reverse else t
        rows = pl.ds(tt, LRU_BLOCKS, stride=LRU_PITCH)
        h = a_s[rows, :] * h + u_s[rows, :]
        h_s[rows, :] = h
        return h

    h_end = lax.fori_loop(0, LRU_TB, body, carry[...], unroll=8)
    carry[...] = h_end

    for n in range(LRU_BLOCKS):
        sl = slice(n * LRU_BLK, (n + 1) * LRU_BLK)
        hs = h_s[n * LRU_PITCH:n * LRU_PITCH + LRU_TB, :]
        if combine:
            y_ref[:, sl] = ((hf_ref[:, sl] + hs) * jax.nn.gelu(gate_ref[:, sl])).astype(BF16)
        else:
            y_ref[:, sl] = hs

    @pl.when(ends)
    def _():
        st_ref[...] = h_end


def _lru_scan(xr, conv_w, conv_b, w_gate, b_gate, lam, h0, *, reverse, hf=None, gate=None):
    nb = T // LRU_TB
    hb = LRU_TB // SUBLANES
    n_halo = T // SUBLANES

    def blk(j):
        return (nb - 1 - j) if reverse else j

    in_specs = [
        pl.BlockSpec((LRU_TB, D), lambda j: (blk(j), 0)),
        pl.BlockSpec((SUBLANES, D), lambda j: (jnp.maximum(blk(j) * hb - 1, 0), 0)),
        pl.BlockSpec((SUBLANES, D), lambda j: (jnp.minimum((blk(j) + 1) * hb, n_halo - 1), 0)),
        pl.BlockSpec((4, D), lambda j: (0, 0)),
        pl.BlockSpec((1, D), lambda j: (0, 0)),
        pl.BlockSpec((LRU_BLOCKS, LRU_BLK, 2 * LRU_BLK), lambda j: (0, 0, 0)),
        pl.BlockSpec((2, D), lambda j: (0, 0)),
        pl.BlockSpec((1, D), lambda j: (0, 0)),
        pl.BlockSpec((None, LRU_BLOCKS, LRU_BLK), lambda j: (_lru_seq_of_block(blk(j)), 0, 0)),
    ]
    args = [xr, xr, xr, conv_w, conv_b, w_gate, b_gate, lam, h0]
    combine = hf is not None
    if combine:
        in_specs += [pl.BlockSpec((LRU_TB, D), lambda j: (blk(j), 0)),
                     pl.BlockSpec((LRU_TB, D), lambda j: (blk(j), 0))]
        args += [hf, gate]
    n_seq = BATCH + DEC_BATCH
    return pl.pallas_call(
        functools.partial(_lru_scan_kernel, reverse=reverse, combine=combine),
        grid=(nb,),
        in_specs=in_specs,
        out_specs=[pl.BlockSpec((LRU_TB, D), lambda j: (blk(j), 0)),
                   pl.BlockSpec((None, LRU_BLOCKS, LRU_BLK),
                                lambda j: (_lru_seq_of_block(blk(j)), 0, 0))],
        out_shape=[jax.ShapeDtypeStruct((T, D), BF16 if combine else F32),
                   jax.ShapeDtypeStruct((n_seq, LRU_BLOCKS, LRU_BLK), F32)],
        scratch_shapes=[pltpu.VMEM((LRU_BLOCKS * LRU_PITCH, LRU_BLK), F32),
                        pltpu.VMEM((LRU_BLOCKS * LRU_PITCH, LRU_BLK), F32),
                        pltpu.VMEM((LRU_BLOCKS * LRU_PITCH, LRU_BLK), F32),
                        pltpu.VMEM((LRU_BLOCKS, LRU_BLK), F32)],
        compiler_params=_cparams(("arbitrary",)),
        name="lru_scan_bwd" if reverse else "lru_scan_fwd",
    )(*args)


def _post_kernel(*refs, split_x, split_o, split_out):
    it = iter(refs)

    def rows(split):
        if split:
            return jnp.where(_is_ctx_block(), next(it)[...], next(it)[...])
        return next(it)[...]

    x = rows(split_x)
    o = rows(split_o)
    mod_ref, g2_ref, wo_ref, win_ref, wout_ref = (next(it) for _ in range(5))
    y_refs = list(it)
    x1 = x + mod_ref[2:3, :] * jnp.dot(o, wo_ref[...], preferred_element_type=F32)
    h = _rms(x1, g2_ref[...])
    h = (h * (1.0 + mod_ref[4:5, :]) + mod_ref[3:4, :]).astype(BF16)
    acc = jnp.zeros((TM, D), F32)
    for c in range(FFN_H // FFN_CHUNK):
        lo = c * FFN_CHUNK
        g = jnp.dot(h, win_ref[:, lo:lo + FFN_CHUNK], preferred_element_type=F32)
        u = jnp.dot(h, win_ref[:, FFN_H + lo:FFN_H + lo + FFN_CHUNK], preferred_element_type=F32)
        a = (jax.nn.silu(g) * u).astype(BF16)
        acc = acc + jnp.dot(a, wout_ref[lo:lo + FFN_CHUNK, :], preferred_element_type=F32)
    y = x1 + mod_ref[5:6, :] * acc
    if split_out:
        @pl.when(_is_ctx_block())
        def _():
            y_refs[0][...] = y

        @pl.when(jnp.logical_not(_is_ctx_block()))
        def _():
            y_refs[1][...] = y
    else:
        y_refs[0][...] = y


def _layer_weight_spec(shape, layer):
    return pl.BlockSpec((None,) + shape[1:], lambda i: (layer,) + (0,) * (len(shape) - 1),
                        pipeline_mode=pl.Buffered(1))


def _post(xs, os, mod, g2, w_o, w_in, w_out, layer, *, split_out):
    def row_specs(arrays):
        return [_ctx_row_spec(D), _proj_lat_row_spec(D)] if len(arrays) == 2 else [_row_spec(D)]

    if split_out:
        out_specs = [_ctx_row_spec(D), _proj_lat_row_spec(D)]
        out_shape = [jax.ShapeDtypeStruct((N_CTX, D), F32), jax.ShapeDtypeStruct((N_LAT, D), F32)]
    else:
        out_specs = [_row_spec(D)]
        out_shape = [jax.ShapeDtypeStruct((T, D), F32)]
    return pl.pallas_call(
        functools.partial(_post_kernel, split_x=len(xs) == 2, split_o=len(os) == 2,
                          split_out=split_out),
        grid=(N_BLOCKS,),
        in_specs=row_specs(xs) + row_specs(os) + [
            _mod_spec(), _full_spec((1, D)), _layer_weight_spec((1,) + w_o.shape, 0),
            _layer_weight_spec(w_in.shape, layer), _layer_weight_spec(w_out.shape, layer)],
        out_specs=out_specs,
        out_shape=out_shape,
        compiler_params=_cparams(("arbitrary",)),
        name="post_mixer_ffn",
    )(*xs, *os, mod, g2, w_o[None], w_in, w_out)


def _axial_tables(rot_dim):
    row = jnp.repeat(jnp.arange(DEC_SEQ // GRID_W), GRID_W).astype(F32)
    col = jnp.tile(jnp.arange(GRID_W), DEC_SEQ // GRID_W).astype(F32)
    n_freq = rot_dim // 4
    inv = ROPE_THETA ** (-jnp.arange(n_freq, dtype=F32) / n_freq)
    ang = jnp.concatenate([row[:, None] * inv, col[:, None] * inv], axis=-1)
    cos, sin = jnp.cos(ang), jnp.sin(ang)
    reps = LANES // rot_dim
    cos_t = jnp.tile(jnp.concatenate([cos, cos], axis=-1), (1, reps))
    sin_t = jnp.tile(jnp.concatenate([-sin, sin], axis=-1), (1, reps))
    cos_t = jnp.concatenate([jnp.ones((TM, LANES), F32), cos_t], axis=0)
    sin_t = jnp.concatenate([jnp.zeros((TM, LANES), F32), sin_t], axis=0)
    return cos_t, sin_t


def kernel(x_prompt, x_sample, cache_mla_ckv, cache_mla_krope, cache_diff_k, cache_diff_v, cache_gqa_k, cache_gqa_v, state_lru_h, c, c_ctx, w_mod, b_mod, g_norm1, g_norm2, w_ffn_in, w_ffn_out, mla_w_in, mla_g_cq, mla_g_ckv, mla_w_uq, mla_w_ukv, mla_g_qk, mla_w_o, diff_w_in, diff_g_qk, diff_lambda, diff_g_sub, diff_w_o, gqa_w_in, gqa_g_qk, gqa_w_o, lru_w_in, lru_conv_w, lru_conv_b, lru_w_gate, lru_b_gate, lru_lambda, lru_w_out):
    x_in = (x_prompt.reshape(N_CTX, D), x_sample.reshape(N_LAT, D))
    cond = jnp.concatenate([c_ctx[None, :], c, jnp.zeros((SUBLANES - N_GROUPS, D), F32)], axis=0)
    mod_all = _modulation(cond, w_mod, b_mod)
    cos128, sin128 = _axial_tables(GQA_HD)
    cos64, sin64 = _axial_tables(DIFF_HD)
    w_ffn_in_b = w_ffn_in.astype(BF16)
    w_ffn_out_b = w_ffn_out.astype(BF16)

    def layer_mod(l):
        return mod_all[l, :N_GROUPS].reshape(N_GROUPS, 6, D)

    def post(l, xs, os, w_o):
        out = _post(xs, os, layer_mod(l), g_norm2[l][None, :], w_o.astype(BF16),
                    w_ffn_in_b, w_ffn_out_b, l, split_out=l == DEPTH - 1)
        return out if l == DEPTH - 1 else out[0]

    def attend(q, k, v, *, hq, group, cache, diff, lambda_init):
        common = dict(group=group, diff=diff, lambda_init=lambda_init)
        sub = ATTN_ROWS // group
        o_ctx = _attention(q, k, v, seq_len=SEQ, n_seq=BATCH, row0=0, tq=SEQ, bk=SEQ,
                           sub=min(sub, SEQ), heads_per_step=hq, cache=None, seqs=CTX_SEQS,
                           **common)
        o_lat = _attention(q, k, v, seq_len=DEC_SEQ, n_seq=DEC_BATCH, row0=N_CTX, tq=ATTN_TILES * sub,
                           bk=ATTN_BK, sub=sub, heads_per_step=LAT_KV_HEADS * group, cache=cache,
                           **common)
        return o_ctx, o_lat

    l = 0
    w_in = mla_w_in[0]
    sin64i = _interleave(sin64)

    def twice(a):
        return _interleave(jnp.concatenate([a, a], axis=-1))

    n_lat = MLA_Q_RANK + MLA_KV_RANK
    w_in2 = jnp.concatenate([w_in[:, :n_lat], twice(w_in[:, n_lat:])], axis=1).astype(BF16)
    w_uq = mla_w_uq[0].reshape(MLA_Q_RANK, MLA_HEADS, MLA_NOPE + MLA_ROPE)
    wqn = w_uq[:, :, :MLA_NOPE].reshape(MLA_Q_RANK, MLA_HEADS * MLA_NOPE).astype(BF16)
    wqr = _interleave(w_uq[:, :, MLA_NOPE:].reshape(MLA_Q_RANK, MLA_HEADS * MLA_ROPE)).astype(BF16)
    wukv = mla_w_ukv[0].astype(BF16)
    gqk = mla_g_qk[0]
    gqn = gqk[0:1, :MLA_NOPE]
    gqr = twice(gqk[0:1, MLA_NOPE:])
    gkn = gqk[1:2, :MLA_NOPE]
    gkr = twice(gqk[1:2, MLA_NOPE:])
    q, k, v, ckv_new, kr_new = _mla_in(*x_in, layer_mod(l), g_norm1[l][None, :], w_in2,
                                       mla_g_cq[0][None, :], mla_g_ckv[0][None, :],
                                       gqn, gqr, gkn, gkr, wqn, wqr, wukv, cos64, sin64i)
    kc, vc = _mla_cache(cache_mla_ckv[:, 0].reshape(DEC_BATCH * PAST, MLA_KV_RANK),
                        twice(cache_mla_krope[:, 0].reshape(DEC_BATCH * PAST, MLA_ROPE)),
                        wukv, gkn)
    cache = (kc, vc,
             (LAT_KV_HEADS, PAST, MLA_DQ), lambda b, hb, qb: (hb, b, 0),
             (LAT_KV_HEADS, PAST, 2 * MLA_V), lambda b, hb, qb: (hb, b, 0))
    o = attend(q, k, v, hq=MLA_HEADS, group=1, cache=cache, diff=None, lambda_init=0.0)
    x = post(l, x_in, o, mla_w_o[0])
    new_mla_ckv = ckv_new.reshape(BATCH, 1, SEQ, MLA_KV_RANK)
    new_mla_krope = kr_new.reshape(BATCH, 1, SEQ, MLA_ROPE)

    l = 1
    lambda_init = 0.8 - 0.6 * math.exp(-0.3 * l)
    n_diff = DIFF_HEADS * 2 * DIFF_HD

    w_in = diff_w_in[0]
    w_in = jnp.concatenate([_interleave(w_in[:, :2 * n_diff]), w_in[:, 2 * n_diff:]], axis=1)
    q, k, v, k_new, v_new = _diff_in(x, layer_mod(l), g_norm1[l][None, :], w_in.astype(BF16),
                                     _interleave(jnp.tile(diff_g_qk[0], (1, 2))), cos64, sin64i)
    k_new = _interleave(k_new)
    cache = (_interleave(cache_diff_k[:, 0].reshape(DEC_BATCH, PAST, n_diff)),
             cache_diff_v[:, 0].reshape(DEC_BATCH, PAST, n_diff),
             (None, PAST, LAT_KV_HEADS * LANES), lambda b, hb, qb: (b, 0, hb),
             (None, PAST, LAT_KV_HEADS * LANES), lambda b, hb, qb: (b, 0, hb))
    o = attend(q, k, v, hq=2 * DIFF_HEADS, group=2, cache=cache,
               diff=(diff_lambda[0], diff_g_sub[0][None, :]), lambda_init=lambda_init)
    x = post(l, (x,), o, diff_w_o[0])
    new_diff_k = k_new.reshape(BATCH, 1, SEQ, DIFF_HEADS, 2, DIFF_HD)
    new_diff_v = v_new.reshape(BATCH, 1, SEQ, DIFF_HEADS, 2 * DIFF_HD)

    l = 2
    q, k, v, k_new, v_new = _gqa_in(x, layer_mod(l), g_norm1[l][None, :], gqa_w_in[0].astype(BF16),
                                    gqa_g_qk[0], cos128, sin128)
    n_kv = GQA_KV_HEADS * GQA_HD
    group = GQA_Q_HEADS // GQA_KV_HEADS
    cache = (cache_gqa_k[:, 0].reshape(DEC_BATCH, PAST, n_kv),
             cache_gqa_v[:, 0].reshape(DEC_BATCH, PAST, n_kv),
             (None, PAST, LAT_KV_HEADS * GQA_HD), lambda b, hb, qb: (b, 0, hb),
             (None, PAST, LAT_KV_HEADS * GQA_HD), lambda b, hb, qb: (b, 0, hb))
    o = attend(q, k, v, hq=GQA_Q_HEADS, group=group, cache=cache, diff=None, lambda_init=0.0)
    x = post(l, (x,), o, gqa_w_o[0])
    new_gqa_k = k_new.reshape(BATCH, 1, SEQ, GQA_KV_HEADS, GQA_HD)
    new_gqa_v = v_new.reshape(BATCH, 1, SEQ, GQA_KV_HEADS, GQA_HD)

    l = 3
    gate, xr = _lru_in(x, layer_mod(l), g_norm1[l][None, :], lru_w_in[0].astype(BF16))
    wg = lru_w_gate[0]
    wg = jnp.concatenate([wg[:, 0], wg[:, 1]], axis=-1).astype(BF16)
    h0 = jnp.concatenate([jnp.zeros((BATCH, 2, D), F32), state_lru_h[:, 0]], axis=0)
    h0 = h0.reshape(BATCH + DEC_BATCH, 2, LRU_BLOCKS, LRU_BLK)
    conv_b = lru_conv_b[0][None, :]
    hf, st_f = _lru_scan(xr, lru_conv_w[0], conv_b, wg[0], lru_b_gate[0, 0], lru_lambda[0, 0][None, :],
                         h0[:, 0], reverse=False)
    y, st_b = _lru_scan(xr, lru_conv_w[0], conv_b, wg[1], lru_b_gate[0, 1], lru_lambda[0, 1][None, :],
                        h0[:, 1], reverse=True, hf=hf, gate=gate)
    y_ctx, y_lat = post(l, (x,), (y,), lru_w_out[0])
    new_lru_h = jnp.stack([st_f[:BATCH].reshape(BATCH, D), st_b[:BATCH].reshape(BATCH, D)],
                          axis=1)[:, None]

    y_prompt = y_ctx.reshape(BATCH, SEQ, D)
    y_sample = y_lat.reshape(DEC_BATCH, DEC_SEQ, D)
    return (y_prompt, y_sample, new_mla_ckv, new_mla_krope, new_diff_k, new_diff_v,
            new_gqa_k, new_gqa_v, new_lru_h)
```

```python
import functools
import math

import jax
import jax.numpy as jnp
from jax import lax
from jax.experimental import pallas as pl
from jax.experimental.pallas import tpu as pltpu

F32 = jnp.float32
BF16 = jnp.bfloat16

D = 1024
BATCH = 32
SEQ = 256
DEPTH = 4
DEC_BATCH = 2
DEC_SEQ = 4096
PAST = 256
GRID_W = 64
EPS = 1e-6
ROPE_THETA = 10000.0
FFN_H = 2816
N_CTX = BATCH * SEQ
N_LAT = DEC_BATCH * DEC_SEQ
T = N_CTX + N_LAT
N_GROUPS = 1 + DEC_BATCH

MLA_HEADS = 8
MLA_NOPE = 128
MLA_ROPE = 64
MLA_V = 128
MLA_Q_RANK = 384
MLA_KV_RANK = 256
MLA_DQ = 256
DIFF_HD = 64
DIFF_HEADS = 8
GQA_HD = 128
GQA_Q_HEADS = 8
GQA_KV_HEADS = 2
LRU_BLOCKS = 8
LRU_BLK = 128
LRU_C = 8.0

LANES = 128
SUBLANES = 8
VMEM_LIMIT = 56 * 1024 * 1024

TM = 512
FFN_CHUNK = 256
LRU_TB = 256
ATTN_TILES = 8
CTX_SEQS = 4
LAT_KV_HEADS = 1
ATTN_BK = 1024
ATTN_ROWS = 512
LOG2E = 1.4426950408889634


def _cparams(sem):
    return pltpu.CompilerParams(dimension_semantics=sem, vmem_limit_bytes=VMEM_LIMIT)


def _group_of_block(i, rows_per_block):
    n_ctx_blocks = N_CTX // rows_per_block
    per = DEC_SEQ // rows_per_block
    return jnp.maximum(i - (n_ctx_blocks - per), 0) // per


def _rms(x, gain):
    y = x * lax.rsqrt(jnp.mean(x * x, axis=-1, keepdims=True) + EPS)
    return y * gain


PAIR_HD = LANES // 2
PAIR_HALF = PAIR_HD // 2


def _first_of_pair(shape):
    return (lax.broadcasted_iota(jnp.int32, shape, 1) % PAIR_HD) < PAIR_HALF


def _interleave(a):
    shape = a.shape
    a = a.reshape(shape[:-1] + (shape[-1] // LANES, 2, 2, PAIR_HALF))
    return jnp.swapaxes(a, -3, -2).reshape(shape)


def _rms_pair(x, gain):
    first = _first_of_pair(x.shape)
    sq = x * x
    s0 = jnp.sum(jnp.where(first, sq, 0.0), axis=-1, keepdims=True)
    s_all = jnp.sum(sq, axis=-1, keepdims=True)
    ms = jnp.where(first, s0, s_all - s0) * (1.0 / PAIR_HD)
    return x * lax.rsqrt(ms + EPS) * gain


def _rope128(x, c, s):
    return x * c + pltpu.roll(x, 64, 1) * s


def _is_ctx_block():
    return pl.program_id(0) < N_CTX // TM


def _norm_mod(x, g_ref, mod_ref, shift_row):
    y = _rms(x, g_ref[...])
    shift = mod_ref[shift_row:shift_row + 1, :]
    scale = mod_ref[shift_row + 1:shift_row + 2, :]
    return (y * (1.0 + scale) + shift).astype(BF16)


def _mod_kernel(cond_ref, w_ref, b_ref, o_ref):
    s = jax.nn.silu(cond_ref[...]).astype(BF16)
    w = w_ref[...].astype(BF16)
    o_ref[...] = jnp.dot(s, w, preferred_element_type=F32) + b_ref[...]


def _modulation(cond, w_mod, b_mod):
    tn = 1536
    return pl.pallas_call(
        _mod_kernel,
        grid=(DEPTH, 6 * D // tn),
        in_specs=[
            pl.BlockSpec((SUBLANES, D), lambda l, j: (0, 0)),
            pl.BlockSpec((None, D, tn), lambda l, j: (l, 0, j)),
            pl.BlockSpec((None, 1, tn), lambda l, j: (l, 0, j)),
        ],
        out_specs=pl.BlockSpec((None, SUBLANES, tn), lambda l, j: (l, 0, j)),
        out_shape=jax.ShapeDtypeStruct((DEPTH, SUBLANES, 6 * D), F32),
        compiler_params=_cparams(("arbitrary", "arbitrary")),
        name="adaln_mod",
    )(cond, w_mod, b_mod.reshape(DEPTH, 1, 6 * D))


def _row_spec(width):
    return pl.BlockSpec((TM, width), lambda i: (i, 0))


def _ctx_row_spec(width):
    return pl.BlockSpec((TM, width), lambda i: (jnp.minimum(i, N_CTX // TM - 1), 0))


def _full_spec(shape):
    return pl.BlockSpec(shape, lambda i: (0,) * len(shape))


def _mod_spec():
    return pl.BlockSpec((None, 6, D), lambda i: (_group_of_block(i, TM), 0, 0))


N_BLOCKS = T // TM
N_CTX_BLOCKS = N_CTX // TM


def _proj_block(i):
    return jnp.minimum(i, N_BLOCKS - 1)


def _epi_block(i):
    return jnp.maximum(i - 1, 0)


def _proj_row_spec(width):
    return pl.BlockSpec((TM, width), lambda i: (_proj_block(i), 0))


def _proj_mod_spec():
    return pl.BlockSpec((None, 6, D), lambda i: (_group_of_block(_proj_block(i), TM), 0, 0))


def _epi_row_spec(width):
    return pl.BlockSpec((TM, width), lambda i: (_epi_block(i), 0))


def _epi_heads_spec(heads, width):
    return pl.BlockSpec((heads, TM, width), lambda i: (0, _epi_block(i), 0))


def _epi_ctx_row_spec(width):
    return pl.BlockSpec((TM, width), lambda i: (jnp.minimum(_epi_block(i), N_CTX_BLOCKS - 1), 0))


def _epi_is_ctx():
    i = pl.program_id(0)
    return jnp.logical_and(i >= 1, i <= N_CTX_BLOCKS)


def _rope_spec():
    per = DEC_SEQ // TM

    def index(i):
        j = _epi_block(i)
        return (jnp.where(j < N_CTX_BLOCKS, 0, 1 + (j - N_CTX_BLOCKS) % per), 0)

    return pl.BlockSpec((TM, LANES), index)


def _proj_lat_row_spec(width):
    n_lat = N_BLOCKS - N_CTX_BLOCKS
    return pl.BlockSpec((TM, width), lambda i: (jnp.clip(i - N_CTX_BLOCKS, 0, n_lat - 1), 0))


def _two_stage(step, bufs_a, bufs_b):
    i = pl.program_id(0)

    @pl.when(i == 0)
    def _():
        for buf in bufs_b:
            buf[...] = jnp.zeros(buf.shape, buf.dtype)

    @pl.when(i % 2 == 0)
    def _():
        step(bufs_b, bufs_a)

    @pl.when(i % 2 == 1)
    def _():
        step(bufs_a, bufs_b)


def _gqa_in_kernel(x_ref, mod_ref, g_ref, w_ref, gqk_ref, cos_ref, sin_ref,
                   q_ref, k_ref, v_ref, ks_ref, vs_ref, proj_a, proj_b):
    n_q = GQA_Q_HEADS * GQA_HD
    n_kv = GQA_KV_HEADS * GQA_HD

    def step(prev, cur):
        h = _norm_mod(x_ref[...], g_ref, mod_ref, 0)
        cur[...] = jnp.dot(h, w_ref[...], preferred_element_type=F32)
        c = cos_ref[...]
        s = sin_ref[...]
        scale = LOG2E * GQA_HD ** -0.5
        for hd in range(GQA_Q_HEADS):
            q = _rms(prev[:, hd * GQA_HD:(hd + 1) * GQA_HD], gqk_ref[0:1, :])
            q_ref[hd] = (_rope128(q, c, s) * scale).astype(BF16)
        ks = []
        for hd in range(GQA_KV_HEADS):
            lo = n_q + hd * GQA_HD
            ks.append(_rms(prev[:, lo:lo + GQA_HD], gqk_ref[1:2, :]))
            k_ref[hd] = _rope128(ks[-1], c, s).astype(BF16)
            lo = n_q + n_kv + hd * GQA_HD
            v_ref[hd] = _with_ones(prev[:, lo:lo + GQA_HD].astype(BF16))

        @pl.when(_epi_is_ctx())
        def _():
            for hd in range(GQA_KV_HEADS):
                ks_ref[:, hd * GQA_HD:(hd + 1) * GQA_HD] = ks[hd]
            vs_ref[...] = prev[:, n_q + n_kv:]

    _two_stage(lambda prev, cur: step(prev[0], cur[0]), (proj_a,), (proj_b,))


def _proj_scratch(width):
    return [pltpu.VMEM((TM, width), F32), pltpu.VMEM((TM, width), F32)]


def _gqa_in(x, mod, g1, w_in, g_qk, cos, sin):
    n_kv = GQA_KV_HEADS * GQA_HD
    return pl.pallas_call(
        _gqa_in_kernel,
        grid=(N_BLOCKS + 1,),
        in_specs=[_proj_row_spec(D), _proj_mod_spec(), _full_spec((1, D)), _full_spec(w_in.shape),
                  _full_spec((2, GQA_HD)), _rope_spec(), _rope_spec()],
        out_specs=[_epi_heads_spec(GQA_Q_HEADS, GQA_HD), _epi_heads_spec(GQA_KV_HEADS, GQA_HD),
                   _epi_heads_spec(GQA_KV_HEADS, 2 * GQA_HD), _epi_ctx_row_spec(n_kv),
                   _epi_ctx_row_spec(n_kv)],
        scratch_shapes=_proj_scratch(w_in.shape[1]),
        out_shape=[jax.ShapeDtypeStruct((GQA_Q_HEADS, T, GQA_HD), BF16),
                   jax.ShapeDtypeStruct((GQA_KV_HEADS, T, GQA_HD), BF16),
                   jax.ShapeDtypeStruct((GQA_KV_HEADS, T, 2 * GQA_HD), BF16),
                   jax.ShapeDtypeStruct((N_CTX, n_kv), F32),
                   jax.ShapeDtypeStruct((N_CTX, n_kv), F32)],
        compiler_params=_cparams(("arbitrary",)),
        name="gqa_in",
    )(x, mod, g1, w_in, g_qk, cos, sin)


def _diff_in_kernel(x_ref, mod_ref, g_ref, w_ref, gqk_ref, cos_ref, sin_ref,
                    q_ref, k_ref, v_ref, ks_ref, vs_ref, proj_a, proj_b):
    n = DIFF_HEADS * 2 * DIFF_HD

    def step(prev, cur):
        h = _norm_mod(x_ref[...], g_ref, mod_ref, 0)
        cur[...] = jnp.dot(h, w_ref[...], preferred_element_type=F32)
        c = cos_ref[...]
        s = sin_ref[...]
        scale = LOG2E * DIFF_HD ** -0.5
        ks = []
        for hd in range(DIFF_HEADS):
            sl = slice(hd * LANES, (hd + 1) * LANES)
            q = _rope128(_rms_pair(prev[:, sl], gqk_ref[0:1, :]), c, s) * scale
            q_ref[hd] = q.astype(BF16)
            ks.append(_rms_pair(prev[:, n + hd * LANES:n + (hd + 1) * LANES], gqk_ref[1:2, :]))
            k_ref[hd] = _rope128(ks[-1], c, s).astype(BF16)
            v_ref[hd] = _with_ones(prev[:, 2 * n + hd * LANES:2 * n + (hd + 1) * LANES].astype(BF16))

        @pl.when(_epi_is_ctx())
        def _():
            for hd in range(DIFF_HEADS):
                ks_ref[:, hd * LANES:(hd + 1) * LANES] = ks[hd]
            vs_ref[...] = prev[:, 2 * n:]

    _two_stage(lambda prev, cur: step(prev[0], cur[0]), (proj_a,), (proj_b,))


def _diff_in(x, mod, g1, w_in, g_qk2, cos, sin):
    n = DIFF_HEADS * 2 * DIFF_HD
    return pl.pallas_call(
        _diff_in_kernel,
        grid=(N_BLOCKS + 1,),
        in_specs=[_proj_row_spec(D), _proj_mod_spec(), _full_spec((1, D)), _full_spec(w_in.shape),
                  _full_spec((2, LANES)), _rope_spec(), _rope_spec()],
        out_specs=[_epi_heads_spec(DIFF_HEADS, LANES), _epi_heads_spec(DIFF_HEADS, LANES),
                   _epi_heads_spec(DIFF_HEADS, 2 * LANES), _epi_ctx_row_spec(n), _epi_ctx_row_spec(n)],
        scratch_shapes=_proj_scratch(w_in.shape[1]),
        out_shape=[jax.ShapeDtypeStruct((DIFF_HEADS, T, LANES), BF16),
                   jax.ShapeDtypeStruct((DIFF_HEADS, T, LANES), BF16),
                   jax.ShapeDtypeStruct((DIFF_HEADS, T, 2 * LANES), BF16),
                   jax.ShapeDtypeStruct((N_CTX, n), F32),
                   jax.ShapeDtypeStruct((N_CTX, n), F32)],
        compiler_params=_cparams(("arbitrary",)),
        name="diff_in",
    )(x, mod, g1, w_in, g_qk2, cos, sin)


def _mla_keys_values(ckv_bf16, kr2, w_ukv_ref, gk_ref, k_ref, v_ref):
    kv = jnp.dot(ckv_bf16, w_ukv_ref[...], preferred_element_type=F32)
    low = _first_of_pair(kr2.shape)
    kr_low = jnp.where(low, kr2, 0.0).astype(BF16)
    kr_high = jnp.where(low, 0.0, kr2).astype(BF16)
    width = MLA_NOPE + MLA_V
    for hd in range(MLA_HEADS):
        k_nope = _rms(kv[:, hd * width:hd * width + MLA_NOPE], gk_ref[...])
        k_ref[hd, :, 0:MLA_NOPE] = k_nope.astype(BF16)
        k_ref[hd, :, MLA_NOPE:MLA_DQ] = kr_low if hd % 2 == 0 else kr_high
        v_ref[hd] = _with_ones(kv[:, hd * width + MLA_NOPE:(hd + 1) * width].astype(BF16))


def _mla_in_kernel(xc_ref, xl_ref, mod_ref, g_ref, w_ref, gcq_ref, gckv_ref, gqn_ref, gqr_ref,
                   gkn_ref, gkr_ref, wqn_ref, wqr_ref, wukv_ref, cos_ref, sin_ref,
                   q_ref, k_ref, v_ref, ckv_ref, kr_ref, proj_a, proj_b):
    def step(prev, cur):
        x = jnp.where(_is_ctx_block(), xc_ref[...], xl_ref[...])
        h = _norm_mod(x, g_ref, mod_ref, 0)
        cur[...] = jnp.dot(h, w_ref[...], preferred_element_type=F32)
        c = cos_ref[...]
        s = sin_ref[...]
        cq = _rms(prev[:, 0:MLA_Q_RANK], gcq_ref[...]).astype(BF16)
        ckv = _rms(prev[:, MLA_Q_RANK:MLA_Q_RANK + MLA_KV_RANK], gckv_ref[...])
        kr2 = _rms_pair(prev[:, MLA_Q_RANK + MLA_KV_RANK:], gkr_ref[...])
        _mla_keys_values(ckv.astype(BF16), _rope128(kr2, c, s), wukv_ref, gkn_ref, k_ref, v_ref)

        scale = LOG2E * (MLA_NOPE + MLA_ROPE) ** -0.5
        qn = jnp.dot(cq, wqn_ref[...], preferred_element_type=F32)
        qr = jnp.dot(cq, wqr_ref[...], preferred_element_type=F32)
        low = _first_of_pair((TM, LANES))
        for pair in range(MLA_HEADS // 2):
            r = _rms_pair(qr[:, pair * LANES:(pair + 1) * LANES], gqr_ref[...])
            r = _rope128(r, c, s) * scale
            q_ref[2 * pair, :, MLA_NOPE:MLA_DQ] = jnp.where(low, r, 0.0).astype(BF16)
            q_ref[2 * pair + 1, :, MLA_NOPE:MLA_DQ] = jnp.where(low, 0.0, r).astype(BF16)
        for hd in range(MLA_HEADS):
            q = _rms(qn[:, hd * MLA_NOPE:(hd + 1) * MLA_NOPE], gqn_ref[...]) * scale
            q_ref[hd, :, 0:MLA_NOPE] = q.astype(BF16)

        @pl.when(_epi_is_ctx())
        def _():
            ckv_ref[...] = ckv
            kr_ref[...] = jnp.concatenate([kr2[:, 0:MLA_ROPE // 2],
                                           kr2[:, MLA_ROPE:MLA_ROPE + MLA_ROPE // 2]], axis=1)

    _two_stage(lambda prev, cur: step(prev[0], cur[0]), (proj_a,), (proj_b,))


def _mla_in(x_ctx, x_lat, mod, g1, w_in, g_cq, g_ckv, gqn, gqr, gkn, gkr, wqn, wqr, wukv, cos, sin):
    n_lat_blocks = N_BLOCKS - N_CTX_BLOCKS
    return pl.pallas_call(
        _mla_in_kernel,
        grid=(N_BLOCKS + 1,),
        in_specs=[_ctx_row_spec(D),
                  pl.BlockSpec((TM, D), lambda i: (jnp.clip(i - N_CTX_BLOCKS, 0, n_lat_blocks - 1), 0)),
                  _proj_mod_spec(), _full_spec((1, D)), _full_spec(w_in.shape),
                  _full_spec((1, MLA_Q_RANK)), _full_spec((1, MLA_KV_RANK)),
                  _full_spec((1, LANES)), _full_spec((1, LANES)), _full_spec((1, LANES)),
                  _full_spec((1, LANES)), _full_spec(wqn.shape), _full_spec(wqr.shape),
                  _full_spec(wukv.shape), _rope_spec(), _rope_spec()],
        out_specs=[_epi_heads_spec(MLA_HEADS, MLA_DQ), _epi_heads_spec(MLA_HEADS, MLA_DQ),
                   _epi_heads_spec(MLA_HEADS, 2 * MLA_V), _epi_ctx_row_spec(MLA_KV_RANK),
                   _epi_ctx_row_spec(MLA_ROPE)],
        scratch_shapes=_proj_scratch(w_in.shape[1]),
        out_shape=[jax.ShapeDtypeStruct((MLA_HEADS, T, MLA_DQ), BF16),
                   jax.ShapeDtypeStruct((MLA_HEADS, T, MLA_DQ), BF16),
                   jax.ShapeDtypeStruct((MLA_HEADS, T, 2 * MLA_V), BF16),
                   jax.ShapeDtypeStruct((N_CTX, MLA_KV_RANK), F32),
                   jax.ShapeDtypeStruct((N_CTX, MLA_ROPE), F32)],
        compiler_params=_cparams(("arbitrary",)),
        name="mla_in",
    )(x_ctx, x_lat, mod, g1, w_in, g_cq, g_ckv, gqn, gqr, gkn, gkr, wqn, wqr, wukv, cos, sin)


def _mla_cache_kernel(ckv_ref, kr_ref, wukv_ref, gkn_ref, k_ref, v_ref):
    _mla_keys_values(ckv_ref[...].astype(BF16), kr_ref[...], wukv_ref, gkn_ref, k_ref, v_ref)


def _mla_cache(ckv, kr, wukv, gkn):
    rows = ckv.shape[0]
    return pl.pallas_call(
        _mla_cache_kernel,
        grid=(1,),
        in_specs=[_full_spec(ckv.shape), _full_spec(kr.shape), _full_spec(wukv.shape),
                  _full_spec((1, LANES))],
        out_specs=[_full_spec((MLA_HEADS, rows, MLA_DQ)), _full_spec((MLA_HEADS, rows, 2 * MLA_V))],
        out_shape=[jax.ShapeDtypeStruct((MLA_HEADS, rows, MLA_DQ), BF16),
                   jax.ShapeDtypeStruct((MLA_HEADS, rows, 2 * MLA_V), BF16)],
        compiler_params=_cparams(("arbitrary",)),
        name="mla_cache_kv",
    )(ckv, kr, wukv, gkn)


def _lru_in_kernel(x_ref, mod_ref, g_ref, w_ref, gate_ref, xr_ref):
    h = _norm_mod(x_ref[...], g_ref, mod_ref, 0)
    proj = jnp.dot(h, w_ref[...], preferred_element_type=F32)
    gate_ref[...] = proj[:, 0:D]
    xr_ref[...] = proj[:, D:2 * D]


def _lru_in(x, mod, g1, w_in):
    return pl.pallas_call(
        _lru_in_kernel,
        grid=(T // TM,),
        in_specs=[_row_spec(D), _mod_spec(), _full_spec((1, D)), _full_spec(w_in.shape)],
        out_specs=[_row_spec(D), _row_spec(D)],
        out_shape=[jax.ShapeDtypeStruct((T, D), F32), jax.ShapeDtypeStruct((T, D), F32)],
        compiler_params=_cparams(("arbitrary",)),
        name="lru_in",
    )(x, mod, g1, w_in)


def _with_ones(v):
    return jnp.concatenate([v, jnp.ones_like(v)], axis=-1)


def _softmax_pv(q, key_blocks, dv):
    nt = (((1,), (1,)), ((), ()))
    m = acc = None
    for k, v1 in key_blocks:
        s = lax.dot_general(q, k, nt, preferred_element_type=F32)
        mb = jnp.max(s, axis=-1, keepdims=True)
        if m is None:
            m = mb
            acc = jnp.dot(jnp.exp2(s - m).astype(BF16), v1, preferred_element_type=F32)
        else:
            m_new = jnp.maximum(m, mb)
            alpha = jnp.exp2(m - m_new)
            acc = alpha * acc + jnp.dot(jnp.exp2(s - m_new).astype(BF16), v1,
                                        preferred_element_type=F32)
            m = m_new
    return acc[:, 0:dv] / acc[:, dv:2 * dv]


def _attn_kernel(*refs, heads, group, seqs, sub, bk, has_cache, diff, dv, lambda_init):
    it = iter(refs)
    q_ref, kn_ref, vn_ref = next(it), next(it), next(it)
    kc_ref = vc_ref = lam_ref = gsub_ref = None
    if has_cache:
        kc_ref, vc_ref = next(it), next(it)
    if diff:
        lam_ref, gsub_ref = next(it), next(it)
    o_ref = next(it)

    tq = q_ref.shape[1] // seqs
    seq_len = kn_ref.shape[1] // seqs
    stacked = min(group, heads)
    def cached_block(kv):
        if not has_cache:
            return []
        if len(kc_ref.shape) == 2:
            wk = kc_ref.shape[1] // n_kv
            wv = vc_ref.shape[1] // n_kv
            kc, vc = kc_ref[:, kv * wk:(kv + 1) * wk], vc_ref[:, kv * wv:(kv + 1) * wv]
        else:
            kc, vc = kc_ref[kv], vc_ref[kv]
        vc = vc.astype(BF16)
        return [(kc.astype(BF16), vc if vc.shape[-1] == 2 * dv else _with_ones(vc))]

    n_kv = max(heads // group, 1)
    if diff:
        lam = lam_ref[...]
        lam_full = (jnp.exp(jnp.sum(lam[0:1] * lam[1:2], axis=-1, keepdims=True))
                    - jnp.exp(jnp.sum(lam[2:3] * lam[3:4], axis=-1, keepdims=True)) + lambda_init)

    for sq, kv in [(sq, kv) for sq in range(seqs) for kv in range(n_kv)]:
        def body(i, carry, sq=sq, kv=kv):
            rows = pl.ds(pl.multiple_of(sq * tq + i * sub, sub), sub)
            if diff:
                q2 = q_ref[kv, rows, :]
                first = _first_of_pair(q2.shape)
                zero = jnp.zeros_like(q2)
                q = jnp.concatenate([jnp.where(first, q2, zero), jnp.where(first, zero, q2)],
                                    axis=0)
            else:
                q = jnp.concatenate([q_ref[kv * stacked + g, rows, :] for g in range(stacked)],
                                    axis=0)
            keys = [slice(sq * seq_len + j * bk, sq * seq_len + (j + 1) * bk)
                    for j in range(seq_len // bk)]
            blocks = cached_block(kv) + [(kn_ref[kv, ks, :], vn_ref[kv, ks, :]) for ks in keys]
            o = _softmax_pv(q, blocks, dv)
            if diff:
                od = o[0:sub] - lam_full * o[sub:2 * sub]
                od = _rms(od, gsub_ref[...]) * (1.0 - lambda_init)
                o_ref[rows, kv * dv:(kv + 1) * dv] = od.astype(BF16)
            else:
                for g in range(stacked):
                    hd = kv * stacked + g
                    o_ref[rows, hd * dv:(hd + 1) * dv] = o[g * sub:(g + 1) * sub].astype(BF16)
            return carry
        lax.fori_loop(0, tq // sub, body, 0, unroll=True)


def _attention(q, kn, vn, *, seq_len, n_seq, row0, tq, sub, bk, heads_per_step, group, cache=None,
               diff=None, lambda_init=0.0, seqs=1):
    q_per_row = 2 if diff is not None else 1
    hq, _, dq = q.shape
    hq *= q_per_row
    hkv, _, dv1 = vn.shape
    dv = dv1 // 2
    kv_per_step = max(heads_per_step // group, 1)
    n_hblk = hq // heads_per_step
    n_qblk = seq_len // tq
    assert seqs == 1 or (n_qblk == 1 and cache is None and row0 % (seqs * seq_len) == 0)
    n_seq //= seqs
    seq0 = row0 // (seqs * seq_len)
    qblk0 = row0 // (seqs * tq)

    def q_map(b, hb, qb):
        return (hb, qblk0 + b * n_qblk + qb, 0)

    def kv_map(b, hb, qb):
        return ((hb * heads_per_step) // (group * kv_per_step), seq0 + b, 0)

    in_specs = [pl.BlockSpec((heads_per_step // q_per_row, seqs * tq, dq), q_map),
                pl.BlockSpec((kv_per_step, seqs * seq_len, dq), kv_map),
                pl.BlockSpec((kv_per_step, seqs * seq_len, dv1), kv_map)]
    args = [q, kn, vn]
    if cache is not None:
        kc, vc, kc_block, kc_map, vc_block, vc_map = cache
        in_specs += [pl.BlockSpec(kc_block, kc_map), pl.BlockSpec(vc_block, vc_map)]
        args += [kc, vc]
    out_heads = heads_per_step
    if diff is not None:
        lam, gsub = diff
        in_specs += [pl.BlockSpec(lam.shape, lambda b, hb, qb: (0, 0)),
                     pl.BlockSpec(gsub.shape, lambda b, hb, qb: (0, 0))]
        args += [lam, gsub]
        out_heads = heads_per_step // 2
    n_out = (hq // 2 if diff is not None else hq) * dv
    kernel = functools.partial(_attn_kernel, heads=heads_per_step, group=group, seqs=seqs, sub=sub,
                               bk=bk, has_cache=cache is not None, diff=diff is not None, dv=dv,
                               lambda_init=lambda_init)
    return pl.pallas_call(
        kernel,
        grid=(n_seq, n_hblk, n_qblk),
        in_specs=in_specs,
        out_specs=pl.BlockSpec((seqs * tq, out_heads * dv),
                               lambda b, hb, qb: (b * n_qblk + qb, hb)),
        out_shape=jax.ShapeDtypeStruct((n_seq * seqs * seq_len, n_out), BF16),
        compiler_params=_cparams(("arbitrary", "arbitrary", "arbitrary")),
        name="attention",
    )(*args)


def _lru_seq_pos(i):
    n_ctx_blocks = N_CTX // LRU_TB
    per_ctx = SEQ // LRU_TB
    per_lat = DEC_SEQ // LRU_TB
    pos = jnp.where(i < n_ctx_blocks, i % per_ctx, (i - n_ctx_blocks) % per_lat)
    length = jnp.where(i < n_ctx_blocks, per_ctx, per_lat)
    return pos == 0, pos == length - 1


def _lru_seq_of_block(i):
    n_ctx_blocks = N_CTX // LRU_TB
    return jnp.where(i < n_ctx_blocks, i // (SEQ // LRU_TB),
                     BATCH + (i - n_ctx_blocks) // (DEC_SEQ // LRU_TB))


LRU_PITCH = LRU_TB + SUBLANES


def _lru_scan_kernel(x_ref, prev_ref, next_ref, cw_ref, cb_ref, wg_ref, bg_ref, lam_ref, h0_ref,
                     *rest, reverse, combine):
    if combine:
        hf_ref, gate_ref, y_ref, st_ref, a_s, u_s, h_s, carry = rest
    else:
        y_ref, st_ref, a_s, u_s, h_s, carry = rest
    j = pl.program_id(0)
    i = (T // LRU_TB - 1 - j) if reverse else j
    first, last = _lru_seq_pos(i)
    starts = last if reverse else first
    ends = first if reverse else last

    x = x_ref[...]
    before = jnp.where(first, 0.0, prev_ref[SUBLANES - 1:SUBLANES, :])
    after = jnp.where(last, 0.0, next_ref[0:2, :])
    row = lax.broadcasted_iota(jnp.int32, (SUBLANES, D), 0)

    def shifted(k, fix):
        y = pltpu.roll(x, (-k) % LRU_TB, 0)
        if k < 0:
            return jnp.concatenate([fix(y[0:SUBLANES]), y[SUBLANES:]], axis=0)
        return jnp.concatenate([y[0:LRU_TB - SUBLANES], fix(y[LRU_TB - SUBLANES:])], axis=0)

    taps = [
        shifted(-1, lambda t: jnp.where(row == 0, before, t)),
        x,
        shifted(1, lambda t: jnp.where(row == SUBLANES - 1, after[0:1], t)),
        shifted(2, lambda t: jnp.where(row == SUBLANES - 2, after[0:1],
                                       jnp.where(row == SUBLANES - 1, after[1:2], t))),
    ]
    xr = taps[0] * cw_ref[0:1, :]
    for t in range(1, 4):
        xr = xr + taps[t] * cw_ref[t:t + 1, :]
    xr = xr + cb_ref[...]

    xr_b = xr.astype(BF16)
    lam = lam_ref[...]
    neg = -lam
    softplus = jnp.maximum(neg, 0.0) + jnp.log1p(jnp.exp(-jnp.abs(neg)))
    for n in range(LRU_BLOCKS):
        sl = slice(n * LRU_BLK, (n + 1) * LRU_BLK)
        g = jnp.dot(xr_b[:, sl], wg_ref[n], preferred_element_type=F32)
        r = jax.nn.sigmoid(g[:, 0:LRU_BLK] + bg_ref[0:1, sl])
        gi = jax.nn.sigmoid(g[:, LRU_BLK:] + bg_ref[1:2, sl])
        log_a = -LRU_C * r * softplus[:, sl]
        th = jnp.tanh(log_a)
        one_minus_a2 = -2.0 * th / (1.0 - th)
        rows_n = slice(n * LRU_PITCH, n * LRU_PITCH + LRU_TB)
        a_s[rows_n, :] = jnp.exp(log_a)
        root = jnp.where(one_minus_a2 > 0.0, one_minus_a2 * lax.rsqrt(one_minus_a2), 0.0)
        u_s[rows_n, :] = root * (gi * xr[:, sl])

    @pl.when(starts)
    def _():
        carry[...] = h0_ref[...]

    def body(t, h):
        tt = (LRU_TB - 1 - t) if reverse else t
        rows = pl.ds(tt, LRU_BLOCKS, stride=LRU_PITCH)
        h = a_s[rows, :] * h + u_s[rows, :]
        h_s[rows, :] = h
        return h

    h_end = lax.fori_loop(0, LRU_TB, body, carry[...], unroll=8)
    carry[...] = h_end

    for n in range(LRU_BLOCKS):
        sl = slice(n * LRU_BLK, (n + 1) * LRU_BLK)
        hs = h_s[n * LRU_PITCH:n * LRU_PITCH + LRU_TB, :]
        if combine:
            y_ref[:, sl] = ((hf_ref[:, sl] + hs) * jax.nn.gelu(gate_ref[:, sl])).astype(BF16)
        else:
            y_ref[:, sl] = hs

    @pl.when(ends)
    def _():
        st_ref[...] = h_end


def _lru_scan(xr, conv_w, conv_b, w_gate, b_gate, lam, h0, *, reverse, hf=None, gate=None):
    nb = T // LRU_TB
    hb = LRU_TB // SUBLANES
    n_halo = T // SUBLANES

    def blk(j):
        return (nb - 1 - j) if reverse else j

    in_specs = [
        pl.BlockSpec((LRU_TB, D), lambda j: (blk(j), 0)),
        pl.BlockSpec((SUBLANES, D), lambda j: (jnp.maximum(blk(j) * hb - 1, 0), 0)),
        pl.BlockSpec((SUBLANES, D), lambda j: (jnp.minimum((blk(j) + 1) * hb, n_halo - 1), 0)),
        pl.BlockSpec((4, D), lambda j: (0, 0)),
        pl.BlockSpec((1, D), lambda j: (0, 0)),
        pl.BlockSpec((LRU_BLOCKS, LRU_BLK, 2 * LRU_BLK), lambda j: (0, 0, 0)),
        pl.BlockSpec((2, D), lambda j: (0, 0)),
        pl.BlockSpec((1, D), lambda j: (0, 0)),
        pl.BlockSpec((None, LRU_BLOCKS, LRU_BLK), lambda j: (_lru_seq_of_block(blk(j)), 0, 0)),
    ]
    args = [xr, xr, xr, conv_w, conv_b, w_gate, b_gate, lam, h0]
    combine = hf is not None
    if combine:
        in_specs += [pl.BlockSpec((LRU_TB, D), lambda j: (blk(j), 0)),
                     pl.BlockSpec((LRU_TB, D), lambda j: (blk(j), 0))]
        args += [hf, gate]
    n_seq = BATCH + DEC_BATCH
    return pl.pallas_call(
        functools.partial(_lru_scan_kernel, reverse=reverse, combine=combine),
        grid=(nb,),
        in_specs=in_specs,
        out_specs=[pl.BlockSpec((LRU_TB, D), lambda j: (blk(j), 0)),
                   pl.BlockSpec((None, LRU_BLOCKS, LRU_BLK),
                                lambda j: (_lru_seq_of_block(blk(j)), 0, 0))],
        out_shape=[jax.ShapeDtypeStruct((T, D), BF16 if combine else F32),
                   jax.ShapeDtypeStruct((n_seq, LRU_BLOCKS, LRU_BLK), F32)],
        scratch_shapes=[pltpu.VMEM((LRU_BLOCKS * LRU_PITCH, LRU_BLK), F32),
                        pltpu.VMEM((LRU_BLOCKS * LRU_PITCH, LRU_BLK), F32),
                        pltpu.VMEM((LRU_BLOCKS * LRU_PITCH, LRU_BLK), F32),
                        pltpu.VMEM((LRU_BLOCKS, LRU_BLK), F32)],
        compiler_params=_cparams(("arbitrary",)),
        name="lru_scan_bwd" if reverse else "lru_scan_fwd",
    )(*args)


def _post_kernel(*refs, split_x, split_o, split_out):
    it = iter(refs)

    def rows(split):
        if split:
            return jnp.where(_is_ctx_block(), next(it)[...], next(it)[...])
        return next(it)[...]

    x = rows(split_x)
    o = rows(split_o)
    mod_ref, g2_ref, wo_ref, win_ref, wout_ref = (next(it) for _ in range(5))
    y_refs = list(it)
    x1 = x + mod_ref[2:3, :] * jnp.dot(o, wo_ref[...], preferred_element_type=F32)
    h = _rms(x1, g2_ref[...])
    h = (h * (1.0 + mod_ref[4:5, :]) + mod_ref[3:4, :]).astype(BF16)
    acc = jnp.zeros((TM, D), F32)
    for c in range(FFN_H // FFN_CHUNK):
        lo = c * FFN_CHUNK
        g = jnp.dot(h, win_ref[:, lo:lo + FFN_CHUNK], preferred_element_type=F32)
        u = jnp.dot(h, win_ref[:, FFN_H + lo:FFN_H + lo + FFN_CHUNK], preferred_element_type=F32)
        a = (jax.nn.silu(g) * u).astype(BF16)
        acc = acc + jnp.dot(a, wout_ref[lo:lo + FFN_CHUNK, :], preferred_element_type=F32)
    y = x1 + mod_ref[5:6, :] * acc
    if split_out:
        @pl.when(_is_ctx_block())
        def _():
            y_refs[0][...] = y

        @pl.when(jnp.logical_not(_is_ctx_block()))
        def _():
            y_refs[1][...] = y
    else:
        y_refs[0][...] = y


def _layer_weight_spec(shape, layer):
    return pl.BlockSpec((None,) + shape[1:], lambda i: (layer,) + (0,) * (len(shape) - 1),
                        pipeline_mode=pl.Buffered(1))


def _post(xs, os, mod, g2, w_o, w_in, w_out, layer, *, split_out):
    def row_specs(arrays):
        return [_ctx_row_spec(D), _proj_lat_row_spec(D)] if len(arrays) == 2 else [_row_spec(D)]

    if split_out:
        out_specs = [_ctx_row_spec(D), _proj_lat_row_spec(D)]
        out_shape = [jax.ShapeDtypeStruct((N_CTX, D), F32), jax.ShapeDtypeStruct((N_LAT, D), F32)]
    else:
        out_specs = [_row_spec(D)]
        out_shape = [jax.ShapeDtypeStruct((T, D), F32)]
    return pl.pallas_call(
        functools.partial(_post_kernel, split_x=len(xs) == 2, split_o=len(os) == 2,
                          split_out=split_out),
        grid=(N_BLOCKS,),
        in_specs=row_specs(xs) + row_specs(os) + [
            _mod_spec(), _full_spec((1, D)), _layer_weight_spec((1,) + w_o.shape, 0),
            _layer_weight_spec(w_in.shape, layer), _layer_weight_spec(w_out.shape, layer)],
        out_specs=out_specs,
        out_shape=out_shape,
        compiler_params=_cparams(("arbitrary",)),
        name="post_mixer_ffn",
    )(*xs, *os, mod, g2, w_o[None], w_in, w_out)


def _axial_tables(rot_dim):
    row = jnp.repeat(jnp.arange(DEC_SEQ // GRID_W), GRID_W).astype(F32)
    col = jnp.tile(jnp.arange(GRID_W), DEC_SEQ // GRID_W).astype(F32)
    n_freq = rot_dim // 4
    inv = ROPE_THETA ** (-jnp.arange(n_freq, dtype=F32) / n_freq)
    ang = jnp.concatenate([row[:, None] * inv, col[:, None] * inv], axis=-1)
    cos, sin = jnp.cos(ang), jnp.sin(ang)
    reps = LANES // rot_dim
    cos_t = jnp.tile(jnp.concatenate([cos, cos], axis=-1), (1, reps))
    sin_t = jnp.tile(jnp.concatenate([-sin, sin], axis=-1), (1, reps))
    cos_t = jnp.concatenate([jnp.ones((TM, LANES), F32), cos_t], axis=0)
    sin_t = jnp.concatenate([jnp.zeros((TM, LANES), F32), sin_t], axis=0)
    return cos_t, sin_t


def kernel(x_prompt, x_sample, cache_mla_ckv, cache_mla_krope, cache_diff_k, cache_diff_v, cache_gqa_k, cache_gqa_v, state_lru_h, c, c_ctx, w_mod, b_mod, g_norm1, g_norm2, w_ffn_in, w_ffn_out, mla_w_in, mla_g_cq, mla_g_ckv, mla_w_uq, mla_w_ukv, mla_g_qk, mla_w_o, diff_w_in, diff_g_qk, diff_lambda, diff_g_sub, diff_w_o, gqa_w_in, gqa_g_qk, gqa_w_o, lru_w_in, lru_conv_w, lru_conv_b, lru_w_gate, lru_b_gate, lru_lambda, lru_w_out):
    x_in = (x_prompt.reshape(N_CTX, D), x_sample.reshape(N_LAT, D))
    cond = jnp.concatenate([c_ctx[None, :], c, jnp.zeros((SUBLANES - N_GROUPS, D), F32)], axis=0)
    mod_all = _modulation(cond, w_mod, b_mod)
    cos128, sin128 = _axial_tables(GQA_HD)
    cos64, sin64 = _axial_tables(DIFF_HD)
    w_ffn_in_b = w_ffn_in.astype(BF16)
    w_ffn_out_b = w_ffn_out.astype(BF16)

    def layer_mod(l):
        return mod_all[l, :N_GROUPS].reshape(N_GROUPS, 6, D)

    def post(l, xs, os, w_o):
        out = _post(xs, os, layer_mod(l), g_norm2[l][None, :], w_o.astype(BF16),
                    w_ffn_in_b, w_ffn_out_b, l, split_out=l == DEPTH - 1)
        return out if l == DEPTH - 1 else out[0]

    def attend(q, k, v, *, hq, group, cache, diff, lambda_init):
        common = dict(group=group, diff=diff, lambda_init=lambda_init)
        sub = ATTN_ROWS // group
        o_ctx = _attention(q, k, v, seq_len=SEQ, n_seq=BATCH, row0=0, tq=SEQ, bk=SEQ,
                           sub=min(sub, SEQ), heads_per_step=hq, cache=None, seqs=CTX_SEQS,
                           **common)
        o_lat = _attention(q, k, v, seq_len=DEC_SEQ, n_seq=DEC_BATCH, row0=N_CTX, tq=ATTN_TILES * sub,
                           bk=ATTN_BK, sub=sub, heads_per_step=LAT_KV_HEADS * group, cache=cache,
                           **common)
        return o_ctx, o_lat

    l = 0
    w_in = mla_w_in[0]
    sin64i = _interleave(sin64)

    def twice(a):
        return _interleave(jnp.concatenate([a, a], axis=-1))

    n_lat = MLA_Q_RANK + MLA_KV_RANK
    w_in2 = jnp.concatenate([w_in[:, :n_lat], twice(w_in[:, n_lat:])], axis=1).astype(BF16)
    w_uq = mla_w_uq[0].reshape(MLA_Q_RANK, MLA_HEADS, MLA_NOPE + MLA_ROPE)
    wqn = w_uq[:, :, :MLA_NOPE].reshape(MLA_Q_RANK, MLA_HEADS * MLA_NOPE).astype(BF16)
    wqr = _interleave(w_uq[:, :, MLA_NOPE:].reshape(MLA_Q_RANK, MLA_HEADS * MLA_ROPE)).astype(BF16)
    wukv = mla_w_ukv[0].astype(BF16)
    gqk = mla_g_qk[0]
    gqn = gqk[0:1, :MLA_NOPE]
    gqr = twice(gqk[0:1, MLA_NOPE:])
    gkn = gqk[1:2, :MLA_NOPE]
    gkr = twice(gqk[1:2, MLA_NOPE:])
    q, k, v, ckv_new, kr_new = _mla_in(*x_in, layer_mod(l), g_norm1[l][None, :], w_in2,
                                       mla_g_cq[0][None, :], mla_g_ckv[0][None, :],
                                       gqn, gqr, gkn, gkr, wqn, wqr, wukv, cos64, sin64i)
    kc, vc = _mla_cache(cache_mla_ckv[:, 0].reshape(DEC_BATCH * PAST, MLA_KV_RANK),
                        twice(cache_mla_krope[:, 0].reshape(DEC_BATCH * PAST, MLA_ROPE)),
                        wukv, gkn)
    cache = (kc, vc,
             (LAT_KV_HEADS, PAST, MLA_DQ), lambda b, hb, qb: (hb, b, 0),
             (LAT_KV_HEADS, PAST, 2 * MLA_V), lambda b, hb, qb: (hb, b, 0))
    o = attend(q, k, v, hq=MLA_HEADS, group=1, cache=cache, diff=None, lambda_init=0.0)
    x = post(l, x_in, o, mla_w_o[0])
    new_mla_ckv = ckv_new.reshape(BATCH, 1, SEQ, MLA_KV_RANK)
    new_mla_krope = kr_new.reshape(BATCH, 1, SEQ, MLA_ROPE)

    l = 1
    lambda_init = 0.8 - 0.6 * math.exp(-0.3 * l)
    n_diff = DIFF_HEADS * 2 * DIFF_HD

    w_in = diff_w_in[0]
    w_in = jnp.concatenate([_interleave(w_in[:, :2 * n_diff]), w_in[:, 2 * n_diff:]], axis=1)
    q, k, v, k_new, v_new = _diff_in(x, layer_mod(l), g_norm1[l][None, :], w_in.astype(BF16),
                                     _interleave(jnp.tile(diff_g_qk[0], (1, 2))), cos64, sin64i)
    k_new = _interleave(k_new)
    cache = (_interleave(cache_diff_k[:, 0].reshape(DEC_BATCH, PAST, n_diff)),
             cache_diff_v[:, 0].reshape(DEC_BATCH, PAST, n_diff),
             (None, PAST, LAT_KV_HEADS * LANES), lambda b, hb, qb: (b, 0, hb),
             (None, PAST, LAT_KV_HEADS * LANES), lambda b, hb, qb: (b, 0, hb))
    o = attend(q, k, v, hq=2 * DIFF_HEADS, group=2, cache=cache,
               diff=(diff_lambda[0], diff_g_sub[0][None, :]), lambda_init=lambda_init)
    x = post(l, (x,), o, diff_w_o[0])
    new_diff_k = k_new.reshape(BATCH, 1, SEQ, DIFF_HEADS, 2, DIFF_HD)
    new_diff_v = v_new.reshape(BATCH, 1, SEQ, DIFF_HEADS, 2 * DIFF_HD)

    l = 2
    q, k, v, k_new, v_new = _gqa_in(x, layer_mod(l), g_norm1[l][None, :], gqa_w_in[0].astype(BF16),
                                    gqa_g_qk[0], cos128, sin128)
    n_kv = GQA_KV_HEADS * GQA_HD
    group = GQA_Q_HEADS // GQA_KV_HEADS
    cache = (cache_gqa_k[:, 0].reshape(DEC_BATCH, PAST, n_kv),
             cache_gqa_v[:, 0].reshape(DEC_BATCH, PAST, n_kv),
             (None, PAST, LAT_KV_HEADS * GQA_HD), lambda b, hb, qb: (b, 0, hb),
             (None, PAST, LAT_KV_HEADS * GQA_HD), lambda b, hb, qb: (b, 0, hb))
    o = attend(q, k, v, hq=GQA_Q_HEADS, group=group, cache=cache, diff=None, lambda_init=0.0)
    x = post(l, (x,), o, gqa_w_o[0])
    new_gqa_k = k_new.reshape(BATCH, 1, SEQ, GQA_KV_HEADS, GQA_HD)
    new_gqa_v = v_new.reshape(BATCH, 1, SEQ, GQA_KV_HEADS, GQA_HD)

    l = 3
    gate, xr = _lru_in(x, layer_mod(l), g_norm1[l][None, :], lru_w_in[0].astype(BF16))
    wg = lru_w_gate[0]
    wg = jnp.concatenate([wg[:, 0], wg[:, 1]], axis=-1).astype(BF16)
    h0 = jnp.concatenate([jnp.zeros((BATCH, 2, D), F32), state_lru_h[:, 0]], axis=0)
    h0 = h0.reshape(BATCH + DEC_BATCH, 2, LRU_BLOCKS, LRU_BLK)
    conv_b = lru_conv_b[0][None, :]
    hf, st_f = _lru_scan(xr, lru_conv_w[0], conv_b, wg[0], lru_b_gate[0, 0], lru_lambda[0, 0][None, :],
                         h0[:, 0], reverse=False)
    y, st_b = _lru_scan(xr, lru_conv_w[0], conv_b, wg[1], lru_b_gate[0, 1], lru_lambda[0, 1][None, :],
                        h0[:, 1], reverse=True, hf=hf, gate=gate)
    y_ctx, y_lat = post(l, (x,), (y,), lru_w_out[0])
    new_lru_h = jnp.stack([st_f[:BATCH].reshape(BATCH, D), st_b[:BATCH].reshape(BATCH, D)],
                          axis=1)[:, None]

    y_prompt = y_ctx.reshape(BATCH, SEQ, D)
    y_sample = y_lat.reshape(DEC_BATCH, DEC_SEQ, D)
    return (y_prompt, y_sample, new_mla_ckv, new_mla_krope, new_diff_k, new_diff_v,
            new_gqa_k, new_gqa_v, new_lru_h)
```
